```python
import math
import jax, jax.numpy as jnp
from jax import lax
import numpy as np

D_MODEL = 1024
BATCH = 4
SEQ = 4096
DEPTH = 1

N_HEADS = 4
HEAD_DIM = 64
V_DIM = 2 * HEAD_DIM
ATT_QK_WIDTH = N_HEADS * 2 * HEAD_DIM
ATT_V_WIDTH = N_HEADS * V_DIM
ROPE_THETA = 10000.0
Q_BLOCK = 128
CONV_WIDTH = D_MODEL // 2
CONV_K = 3
N_BRANCHES = 2
IN_PROJ_WIDTH = 3 * ATT_QK_WIDTH + 3 * CONV_WIDTH + N_BRANCHES * D_MODEL
N_GROUPS = 4
EXPERTS_PER_GROUP = 8
N_EXPERTS = N_GROUPS * EXPERTS_PER_GROUP
TOP_K = 2
EXPERT_FF = D_MODEL // 4
EPS = 1e-6
MASK_VALUE = -1e30

kernel_name = "hybrid_diffattn_shortconv_hmoe_block"


def rms_norm(x, g):
    xf = x.astype(jnp.float32)
    y = xf * lax.rsqrt(jnp.mean(xf * xf, axis=-1, keepdims=True) + EPS)
    return (y * g.astype(jnp.float32)).astype(x.dtype)


def rope(t, positions):
    inv_freq = ROPE_THETA ** (-jnp.arange(0, HEAD_DIM, 2, dtype=jnp.float32) / HEAD_DIM)
    ang = positions.astype(jnp.float32)[..., None] * inv_freq
    cos = jnp.cos(ang)[:, :, None, None, :].astype(t.dtype)
    sin = jnp.sin(ang)[:, :, None, None, :].astype(t.dtype)
    t1, t2 = jnp.split(t, 2, axis=-1)
    return jnp.concatenate([t1 * cos - t2 * sin, t2 * cos + t1 * sin], axis=-1)


def diff_attention(q, k, v, lam):
    B, S = q.shape[0], q.shape[1]
    n_blocks = S // Q_BLOCK
    scale = HEAD_DIM ** -0.5
    qb = q.reshape(B, n_blocks, Q_BLOCK, N_HEADS, 2, HEAD_DIM).transpose(1, 0, 2, 3, 4, 5)
    k_idx = jnp.arange(S)

    def one_block(args):
        q_blk, b_idx = args
        s = jnp.einsum('bqhmd,bkhmd->bhmqk', q_blk, k).astype(jnp.float32) * scale
        q_idx = b_idx * Q_BLOCK + jnp.arange(Q_BLOCK)
        causal = k_idx[None, :] <= q_idx[:, None]
        s = jnp.where(causal, s, MASK_VALUE)
        p = jax.nn.softmax(s, axis=-1)
        a = p[:, :, 0] - lam * p[:, :, 1]
        return jnp.einsum('bhqk,bkhv->bqhv', a.astype(v.dtype), v)

    out = lax.map(one_block, (qb, jnp.arange(n_blocks)))
    return out.transpose(1, 0, 2, 3, 4).reshape(B, S, N_HEADS, V_DIM)


def causal_short_conv(u, w):
    S = u.shape[1]
    up = jnp.pad(u, ((0, 0), (CONV_K - 1, 0), (0, 0)))
    return sum(w[j] * up[:, CONV_K - 1 - j: CONV_K - 1 - j + S] for j in range(CONV_K))


def hierarchical_moe(t, w_grp, b_grp, w_exp_r, b_exp_r, w_gate, w_up, w_down):
    T = t.shape[0]
    g_prob = jax.nn.softmax((t @ w_grp + b_grp).astype(jnp.float32), axis=-1)
    g_w, g_idx = lax.top_k(g_prob, 1)
    e_logits = (t @ w_exp_r + b_exp_r).astype(jnp.float32).reshape(T, N_GROUPS, EXPERTS_PER_GROUP)
    e_in = jnp.take_along_axis(e_logits, g_idx[:, :, None], axis=1)[:, 0]
    e_top, e_idx = lax.top_k(e_in, TOP_K)
    e_w = jax.nn.softmax(e_top, axis=-1) * g_w
    flat_idx = g_idx * EXPERTS_PER_GROUP + e_idx
    combine = jnp.einsum('tk,tke->te', e_w, jax.nn.one_hot(flat_idx, N_EXPERTS, dtype=jnp.float32))
    hg = jnp.einsum('td,edf->tef', t, w_gate)
    hu = jnp.einsum('td,edf->tef', t, w_up)
    act = jax.nn.silu(hg) * hu * combine[..., None].astype(t.dtype)
    return jnp.einsum('tef,efd->td', act, w_down)


def setup_inputs(seed: int = 0) -> dict:
    key = jax.random.key(seed)
    ks = jax.random.split(key, 26)
    f32 = jnp.float32
    nrm = lambda k, shape, s: jax.random.normal(k, shape, f32) * s
    L = DEPTH
    D = D_MODEL
    offsets = jax.random.randint(ks[2], (BATCH, 1), 0, 2048, dtype=jnp.int32)
    positions = offsets + jnp.arange(SEQ, dtype=jnp.int32)[None, :]
    return {
        "x": nrm(ks[0], (BATCH, SEQ, D), 1.0),
        "c": nrm(ks[1], (BATCH, D), 1.0),
        "positions": positions,
        "w_ada": nrm(ks[3], (L, D, 6 * D), 0.5 * D ** -0.5),
        "b_ada": nrm(ks[4], (L, 6 * D), 0.01),
        "norm_mix_g": 1.0 + nrm(ks[5], (L, D), 0.01),
        "w_in": nrm(ks[6], (L, D, IN_PROJ_WIDTH), D ** -0.5),
        "lambda_q1": nrm(ks[7], (L, HEAD_DIM), 0.1),
        "lambda_k1": nrm(ks[8], (L, HEAD_DIM), 0.1),
        "lambda_q2": nrm(ks[9], (L, HEAD_DIM), 0.1),
        "lambda_k2": nrm(ks[10], (L, HEAD_DIM), 0.1),
        "subln_g": 1.0 + nrm(ks[11], (L, V_DIM), 0.01),
        "conv_w": nrm(ks[12], (L, CONV_K, CONV_WIDTH), CONV_K ** -0.5),
        "w_up_att": nrm(ks[13], (L, ATT_V_WIDTH, D), ATT_V_WIDTH ** -0.5),
        "w_up_conv": nrm(ks[14], (L, CONV_WIDTH, D), CONV_WIDTH ** -0.5),
        "w_out": nrm(ks[15], (L, D, D), D ** -0.5),
        "norm_ffn_g": 1.0 + nrm(ks[16], (L, D), 0.01),
        "w_group_router": nrm(ks[17], (L, D, N_GROUPS), D ** -0.5),
        "b_group_router": nrm(ks[18], (L, N_GROUPS), 0.01),
        "w_expert_router": nrm(ks[19], (L, D, N_EXPERTS), D ** -0.5),
        "b_expert_router": nrm(ks[20], (L, N_EXPERTS), 0.01),
        "w_gate": nrm(ks[21], (L, N_EXPERTS, D, EXPERT_FF), D ** -0.5),
        "w_up": nrm(ks[22], (L, N_EXPERTS, D, EXPERT_FF), D ** -0.5),
        "w_down": nrm(ks[23], (L, N_EXPERTS, EXPERT_FF, D), EXPERT_FF ** -0.5),
        "final_norm_g": 1.0 + nrm(ks[24], (D,), 0.01),
    }


def reference(x, c, positions, w_ada, b_ada, norm_mix_g, w_in, lambda_q1, lambda_k1,
              lambda_q2, lambda_k2, subln_g, conv_w, w_up_att, w_up_conv, w_out,
              norm_ffn_g, w_group_router, b_group_router, w_expert_router,
              b_expert_router, w_gate, w_up, w_down, final_norm_g):
    B, S, D = x.shape
    split_pts = [ATT_QK_WIDTH, 2 * ATT_QK_WIDTH, 3 * ATT_QK_WIDTH,
                 3 * ATT_QK_WIDTH + CONV_WIDTH, 3 * ATT_QK_WIDTH + 2 * CONV_WIDTH,
                 3 * ATT_QK_WIDTH + 3 * CONV_WIDTH]
    c_act = jax.nn.silu(c)
    for l in range(DEPTH):
        lambda_init = 0.8 - 0.6 * math.exp(-0.3 * l)
        mod = c_act @ w_ada[l] + b_ada[l]
        sh_m, sc_m, g_m, sh_f, sc_f, g_f = [m[:, None, :] for m in jnp.split(mod, 6, axis=-1)]

        h = rms_norm(x, norm_mix_g[l]) * (1.0 + sc_m) + sh_m
        z = h @ w_in[l]
        q, k, v, cb, cc, cu, gate_logits = jnp.split(z, split_pts, axis=-1)

        q = rope(q.reshape(B, S, N_HEADS, 2, HEAD_DIM), positions)
        k = rope(k.reshape(B, S, N_HEADS, 2, HEAD_DIM), positions)
        v = v.reshape(B, S, N_HEADS, V_DIM)
        lam = (jnp.exp(jnp.sum(lambda_q1[l].astype(jnp.float32) * lambda_k1[l].astype(jnp.float32)))
               - jnp.exp(jnp.sum(lambda_q2[l].astype(jnp.float32) * lambda_k2[l].astype(jnp.float32)))
               + lambda_init)
        att = diff_attention(q, k, v, lam)
        att = (rms_norm(att, subln_g[l]) * (1.0 - lambda_init)).reshape(B, S, ATT_V_WIDTH)

        conv = cb * causal_short_conv(cc * cu, conv_w[l])

        g_att, g_conv = jnp.split(gate_logits, N_BRANCHES, axis=-1)
        merged = jax.nn.sigmoid(g_att) * (att @ w_up_att[l]) + jax.nn.sigmoid(g_conv) * (conv @ w_up_conv[l])
        x = x + g_m * (merged @ w_out[l])

        h2 = rms_norm(x, norm_ffn_g[l]) * (1.0 + sc_f) + sh_f
        y = hierarchical_moe(h2.reshape(B * S, D), w_group_router[l], b_group_router[l],
                             w_expert_router[l], b_expert_router[l],
                             w_gate[l], w_up[l], w_down[l])
        x = x + g_f * y.reshape(B, S, D)
    return rms_norm(x, final_norm_g)
```

```python
import functools
import math

import jax
import jax.numpy as jnp
from jax import lax
from jax.experimental import pallas as pl
from jax.experimental.pallas import tpu as pltpu

F32 = jnp.float32
BF16 = jnp.bfloat16
I32 = jnp.int32

EPS = 1e-6
MASK_VALUE = -1e30
ROPE_THETA = 10000.0

N_HEADS = 4
HEAD_DIM = 64
LANES = 128
N_GROUPS = 4
EXPERTS_PER_GROUP = 8
CONV_K = 3

VMEM_LIMIT = 56 * 1024 * 1024

TM_IN = 512
TQ = 256
TK = 512
TM_MIX = 512
TM_FFN = 512
TM_FIN = 512
TN_ADA = 1024


def _cparams(n_axes):
    return pltpu.CompilerParams(
        dimension_semantics=("arbitrary",) * n_axes, vmem_limit_bytes=VMEM_LIMIT)


def _ada_kernel(c_ref, w_ref, b_ref, lq1_ref, lk1_ref, lq2_ref, lk2_ref, mod_ref, lam_ref, *,
                lambda_init):
    c = c_ref[...]
    a = c * jax.nn.sigmoid(c)
    mod_ref[...] = jnp.dot(a, w_ref[...], preferred_element_type=F32,
                           precision=lax.Precision.HIGHEST) + b_ref[...]
    s1 = jnp.sum(lq1_ref[...] * lk1_ref[...], axis=-1, keepdims=True)
    s2 = jnp.sum(lq2_ref[...] * lk2_ref[...], axis=-1, keepdims=True)
    lam = jnp.exp(s1) - jnp.exp(s2) + lambda_init
    lam_ref[...] = jnp.broadcast_to(lam, lam_ref.shape)


def _ada(c8, w_ada, b_ada, lq1, lk1, lq2, lk2, lambda_init):
    d, n = w_ada.shape
    small = pl.BlockSpec((1, HEAD_DIM), lambda j: (0, 0))
    return pl.pallas_call(
        functools.partial(_ada_kernel, lambda_init=lambda_init),
        grid=(n // TN_ADA,),
        in_specs=[pl.BlockSpec((8, d), lambda j: (0, 0)),
                  pl.BlockSpec((d, TN_ADA), lambda j: (0, j)),
                  pl.BlockSpec((1, TN_ADA), lambda j: (0, j)),
                  small, small, small, small],
        out_specs=[pl.BlockSpec((8, TN_ADA), lambda j: (0, j)),
                   pl.BlockSpec((8, LANES), lambda j: (0, 0))],
        out_shape=[jax.ShapeDtypeStruct((8, n), F32), jax.ShapeDtypeStruct((8, LANES), F32)],
        compiler_params=_cparams(1),
        name="ada",
    )(c8, w_ada, b_ada, lq1, lk1, lq2, lk2)


def _inproj_kernel(x_ref, pos_ref, mod_ref, g_ref, invf_ref, w_ref, cw_ref, wupc_ref,
                   q_ref, k_ref, v_ref, sg_ref, mc_ref, carry_ref, *, d, qk_w, conv_w):
    tm = x_ref.shape[1]
    x = x_ref[0]
    ms = jnp.mean(x * x, axis=-1, keepdims=True)
    mod = mod_ref[0]
    sh = mod[:, 0:d]
    sc = mod[:, d:2 * d]
    h = (x * lax.rsqrt(ms + EPS) * g_ref[...]) * (1.0 + sc) + sh
    hb = h.astype(BF16)

    def proj(a, b):
        return jnp.dot(hb, w_ref[:, a:b], preferred_element_type=F32)

    lane = lax.broadcasted_iota(I32, (1, LANES), 1)
    first_half = (lane % HEAD_DIM) < (HEAD_DIM // 2)
    ang = pos_ref[0].astype(F32) * invf_ref[...]
    cos = jnp.cos(ang)
    sin = jnp.sin(ang)
    sin_signed = jnp.where(first_half, -sin, sin)

    def rope(t):
        partner = jnp.where(first_half, pltpu.roll(t, LANES - HEAD_DIM // 2, 1),
                            pltpu.roll(t, HEAD_DIM // 2, 1))
        return t * cos + partner * sin_signed

    zq = proj(0, qk_w)
    zk = proj(qk_w, 2 * qk_w)
    scale = HEAD_DIM ** -0.5
    for hd in range(qk_w // LANES):
        sl = slice(hd * LANES, (hd + 1) * LANES)
        q_ref[0, :, sl] = (rope(zq[:, sl]) * scale).astype(BF16)
        k_ref[0, :, sl] = rope(zk[:, sl]).astype(BF16)
    v_ref[0] = proj(2 * qk_w, 3 * qk_w).astype(BF16)

    o = 3 * qk_w
    cb = proj(o, o + conv_w)
    u = proj(o + conv_w, o + 2 * conv_w) * proj(o + 2 * conv_w, o + 3 * conv_w)

    @pl.when(pl.program_id(1) == 0)
    def _():
        carry_ref[...] = jnp.zeros_like(carry_ref)

    prev = carry_ref[...]
    row = lax.broadcasted_iota(I32, (tm, 1), 0)
    u1 = jnp.where(row == 0, prev[7:8, :], pltpu.roll(u, 1, 0))
    u2 = jnp.where(row == 0, prev[6:7, :],
                   jnp.where(row == 1, prev[7:8, :], pltpu.roll(u, 2, 0)))
    carry_ref[...] = u[tm - 8:tm, :]
    cw = cw_ref[...]
    conv = cb * (cw[0:1, :] * u + cw[1:2, :] * u1 + cw[2:3, :] * u2)
    convp = jnp.dot(conv.astype(BF16), wupc_ref[...], preferred_element_type=F32)

    o = o + 3 * conv_w
    sg_ref[0] = jax.nn.sigmoid(proj(o, o + d)).astype(BF16)
    mc_ref[0] = (jax.nn.sigmoid(proj(o + d, o + 2 * d)) * convp).astype(BF16)


def _inproj(x, pos3, mod3, norm_g, invf, w_in, conv_w, w_up_conv):
    b, s, d = x.shape
    qk_w = N_HEADS * 2 * HEAD_DIM
    cw = conv_w.shape[1]
    tm = TM_IN
    tok = lambda w: pl.BlockSpec((1, tm, w), lambda bi, si: (bi, si, 0))
    full = lambda a: pl.BlockSpec(a.shape, lambda bi, si: (0,) * a.ndim)
    return pl.pallas_call(
        functools.partial(_inproj_kernel, d=d, qk_w=qk_w, conv_w=cw),
        grid=(b, s // tm),
        in_specs=[tok(d), tok(1),
                  pl.BlockSpec((1, 1, mod3.shape[2]), lambda bi, si: (bi, 0, 0)),
                  full(norm_g), full(invf), full(w_in), full(conv_w), full(w_up_conv)],
        out_specs=[tok(qk_w), tok(qk_w), tok(qk_w), tok(d), tok(d)],
        out_shape=[jax.ShapeDtypeStruct((b, s, qk_w), BF16)] * 3
                  + [jax.ShapeDtypeStruct((b, s, d), BF16)] * 2,
        scratch_shapes=[pltpu.VMEM((8, cw), F32)],
        compiler_params=_cparams(2),
        name="inproj",
    )(x, pos3, mod3, norm_g, invf, w_in, conv_w, w_up_conv)


def _attn_kernel(lam_ref, q_ref, k_ref, v_ref, g_ref, o_ref, *, out_scale):
    tq = q_ref.shape[1]
    qi = pl.program_id(2)
    q = q_ref[0]
    lane = lax.broadcasted_iota(I32, (1, LANES), 1)
    zero = jnp.zeros_like(q)
    qq = jnp.concatenate([jnp.where(lane < HEAD_DIM, q, zero),
                          jnp.where(lane >= HEAD_DIM, q, zero)], axis=0)
    q_idx = qi * tq + lax.broadcasted_iota(I32, (2 * tq, 1), 0) % tq
    k_lane = lax.broadcasted_iota(I32, (1, TK), 1)

    def step(j, carry, masked):
        m, l, acc = carry
        start = pl.multiple_of(j * TK, TK)
        kt = k_ref[0, pl.ds(start, TK), :]
        vt = v_ref[0, pl.ds(start, TK), :]
        s = lax.dot_general(qq, kt, (((1,), (1,)), ((), ())), preferred_element_type=F32)
        if masked:
            s = jnp.where(j * TK + k_lane <= q_idx, s, MASK_VALUE)
        m_new = jnp.maximum(m, jnp.max(s, axis=1, keepdims=True))
        alpha = jnp.exp(m - m_new)
        p = jnp.exp(s - m_new)
        l = alpha * l + jnp.sum(p, axis=1, keepdims=True)
        acc = alpha * acc + jnp.dot(p.astype(BF16), vt, preferred_element_type=F32)
        return m_new, l, acc

    init = (jnp.full((2 * tq, 1), MASK_VALUE, F32), jnp.zeros((2 * tq, 1), F32),
            jnp.zeros((2 * tq, LANES), F32))
    n_full = (qi * tq) // TK
    carry = lax.fori_loop(0, n_full, lambda j, c: step(j, c, False), init)
    _, l, acc = step(n_full, carry, True)

    o = acc / l
    a = o[0:tq] - lam_ref[0:1, :] * o[tq:2 * tq]
    y = a * lax.rsqrt(jnp.mean(a * a, axis=-1, keepdims=True) + EPS) * g_ref[...]
    o_ref[0] = (y * out_scale).astype(BF16)


def _attn(lam, q, k, v, subln_g, out_scale):
    b, s, w = q.shape
    return pl.pallas_call(
        functools.partial(_attn_kernel, out_scale=out_scale),
        grid=(b, N_HEADS, s // TQ),
        in_specs=[pl.BlockSpec((8, LANES), lambda bi, hi, qi: (0, 0)),
                  pl.BlockSpec((1, TQ, LANES), lambda bi, hi, qi: (bi, qi, hi)),
                  pl.BlockSpec((1, s, LANES), lambda bi, hi, qi: (bi, 0, hi)),
                  pl.BlockSpec((1, s, LANES), lambda bi, hi, qi: (bi, 0, hi)),
                  pl.BlockSpec((1, LANES), lambda bi, hi, qi: (0, 0))],
        out_specs=pl.BlockSpec((1, TQ, LANES), lambda bi, hi, qi: (bi, qi, hi)),
        out_shape=jax.ShapeDtypeStruct((b, s, w), BF16),
        compiler_params=_cparams(3),
        name="attn",
    )(lam, q, k, v, subln_g)


def _mixout_kernel(att_ref, sg_ref, mc_ref, x_ref, mod_ref, wua_ref, wo_ref, g_ref, wrt_ref,
                   br_ref, x1_ref, pos_ref, cnt_ref, hs_ref,
                   h2c_ref, carry_ref, posv_ref, cntv_ref, poss_ref, cnts_ref,
                   sem_s, sem_r, *, d, cap):
    tm = x_ref.shape[0]
    i = pl.program_id(0)
    last = pl.num_programs(0) - 1

    @pl.when(i == 0)
    def _():
        carry_ref[...] = jnp.zeros_like(carry_ref)

    mod = mod_ref[0]
    g_m = mod[:, 2 * d:3 * d]
    sh_f = mod[:, 3 * d:4 * d]
    sc_f = mod[:, 4 * d:5 * d]

    attp = jnp.dot(att_ref[...], wua_ref[...], preferred_element_type=F32)
    merged = sg_ref[...].astype(F32) * attp + mc_ref[...].astype(F32)
    mo = jnp.dot(merged.astype(BF16), wo_ref[...], preferred_element_type=F32)
    x1 = x_ref[...] + g_m * mo
    x1_ref[...] = x1
    ms = jnp.mean(x1 * x1, axis=-1, keepdims=True)
    h2 = (x1 * lax.rsqrt(ms + EPS) * g_ref[...]) * (1.0 + sc_f) + sh_f

    lt = lax.dot_general(wrt_ref[...], h2, (((1,), (1,)), ((), ())),
                         preferred_element_type=F32,
                         precision=lax.Precision.HIGHEST) + br_ref[...]
    ridx = lax.broadcasted_iota(I32, (8, 1), 0)
    n_e = N_GROUPS * EXPERTS_PER_GROUP
    is_grp = ridx < N_GROUPS
    gl = jnp.where(is_grp, lt[n_e:n_e + 8, :], MASK_VALUE)
    gmax = jnp.max(gl, axis=0, keepdims=True)
    gsum = jnp.sum(jnp.where(is_grp, jnp.exp(gl - gmax), 0.0), axis=0, keepdims=True)
    g_w = 1.0 / gsum
    gidx = jnp.min(jnp.where(gl == gmax, ridx, 8), axis=0, keepdims=True)
    e_in = jnp.zeros((8, tm), F32)
    for g in range(N_GROUPS):
        e_in = jnp.where(gidx == g, lt[g * 8:(g + 1) * 8, :], e_in)
    t1 = jnp.max(e_in, axis=0, keepdims=True)
    i1 = jnp.min(jnp.where(e_in == t1, ridx, 8), axis=0, keepdims=True)
    e_in2 = jnp.where(ridx == i1, MASK_VALUE, e_in)
    t2 = jnp.max(e_in2, axis=0, keepdims=True)
    i2 = jnp.min(jnp.where(e_in2 == t2, ridx, 8), axis=0, keepdims=True)
    e2 = jnp.exp(t2 - t1)
    den = 1.0 + e2
    w1 = (1.0 / den) * g_w
    w2 = (e2 / den) * g_w
    cw_t = jnp.where(ridx == i1, w1, jnp.where(ridx == i2, w2, 0.0))
    cw_col = jnp.concatenate([cw_t, jnp.zeros((LANES - 8, tm), F32)], axis=0).T

    onehot = jnp.where(ridx == gidx, 1.0, 0.0)
    tl = lax.broadcasted_iota(I32, (1, tm), 1)
    incl = onehot
    shift = 1
    while shift < tm:
        incl = incl + jnp.where(tl >= shift, pltpu.roll(incl, shift, 1), 0.0)
        shift *= 2
    run = carry_ref[...][:, 0:1]
    base = ridx.astype(F32) * float(cap)
    pos = jnp.sum(onehot * (base + run + incl - onehot), axis=0, keepdims=True).astype(I32)
    new_run = run + incl[:, tm - 1:tm]
    carry_ref[...] = jnp.broadcast_to(new_run, carry_ref.shape)
    cnt_i = jnp.broadcast_to(new_run, cnt_ref.shape).astype(I32)
    cnt_ref[...] = cnt_i
    pos_ref[0] = pos
    posv_ref[...] = pos

    h2c_ref[:, 0:d] = h2
    h2c_ref[:, d:d + LANES] = cw_col

    cp = pltpu.make_async_copy(posv_ref.at[0], poss_ref, sem_s)
    cp.start()
    cp.wait()

    def row_copy(r, p):
        return pltpu.make_async_copy(h2c_ref.at[pl.ds(r, 1), :], hs_ref.at[pl.ds(p, 1), :], sem_r)

    def issue(r, c):
        row_copy(r, poss_ref[r]).start()
        return c

    lax.fori_loop(0, tm, issue, 0, unroll=8)
    pltpu.make_async_copy(h2c_ref, hs_ref.at[pl.ds(0, tm), :], sem_r).wait()

    @pl.when(i == last)
    def _():
        cntv_ref[...] = cnt_i
        cc = pltpu.make_async_copy(cntv_ref, cnts_ref, sem_s)
        cc.start()
        cc.wait()
        h2c_ref[0:8, :] = jnp.zeros((8, h2c_ref.shape[1]), F32)

        def zissue(r, c):
            for g in range(N_GROUPS):
                row_copy(0, g * cap + cnts_ref[g, 0] + r).start()
            return c

        lax.fori_loop(0, tm, zissue, 0, unroll=2)
        for g in range(N_GROUPS):
            pltpu.make_async_copy(h2c_ref, hs_ref.at[pl.ds(0, tm), :], sem_r).wait()


def _mixout(att, sg, mc, x, mod3, w_up_att, w_out, norm_g, wrt, br, tiles_per_batch, cap):
    t, d = x.shape
    tm = TM_MIX
    dc = d + LANES
    tok = lambda w: pl.BlockSpec((tm, w), lambda i: (i, 0))
    full = lambda a: pl.BlockSpec(a.shape, lambda i: (0,) * a.ndim)
    return pl.pallas_call(
        functools.partial(_mixout_kernel, d=d, cap=cap),
        grid=(t // tm,),
        in_specs=[tok(att.shape[1]), tok(d), tok(d), tok(d),
                  pl.BlockSpec((1, 1, mod3.shape[2]), lambda i: (i // tiles_per_batch, 0, 0)),
                  full(w_up_att), full(w_out), full(norm_g), full(wrt), full(br)],
        out_specs=[tok(d),
                   pl.BlockSpec((1, 1, tm), lambda i: (i, 0, 0)),
                   pl.BlockSpec((8, LANES), lambda i: (0, 0)),
                   pl.BlockSpec(memory_space=pl.ANY)],
        out_shape=[jax.ShapeDtypeStruct((t, d), F32),
                   jax.ShapeDtypeStruct((t // tm, 1, tm), I32),
                   jax.ShapeDtypeStruct((8, LANES), I32),
                   jax.ShapeDtypeStruct((N_GROUPS * cap, dc), F32)],
        scratch_shapes=[pltpu.VMEM((tm, dc), F32), pltpu.VMEM((8, LANES), F32),
                        pltpu.VMEM((1, tm), I32), pltpu.VMEM((8, LANES), I32),
                        pltpu.SMEM((tm,), I32), pltpu.SMEM((8, LANES), I32),
                        pltpu.SemaphoreType.DMA, pltpu.SemaphoreType.DMA],
        compiler_params=_cparams(1),
        name="mixout",
    )(att, sg, mc, x, mod3, w_up_att, w_out, norm_g, wrt, br)


def _ffn_tile(step, cnt_ref, tm):
    log_tm = tm.bit_length() - 1
    ends = []
    tot = jnp.int32(0)
    for g in range(N_GROUPS):
        tot = tot + lax.shift_right_logical(cnt_ref[g] + (tm - 1), log_tm)
        ends.append(tot)
    sc = jnp.minimum(step, tot - 1)
    g = sum((sc >= e).astype(I32) for e in ends[:-1])
    start = jnp.where(g == 0, 0, jnp.where(g == 1, ends[0], jnp.where(g == 2, ends[1], ends[2])))
    return g, sc - start, step < tot


def _ffn_kernel(cnt_ref, hs_ref, wg_ref, wu_ref, wd_ref, y_ref, *, d):
    tm = hs_ref.shape[0]
    _, _, real = _ffn_tile(pl.program_id(0), cnt_ref, tm)

    @pl.when(real)
    def _():
        h = hs_ref[:, 0:d].astype(BF16)
        cw = hs_ref[:, d:d + LANES]
        acc = jnp.zeros((tm, d), F32)
        for e in range(EXPERTS_PER_GROUP):
            hg = jnp.dot(h, wg_ref[e], preferred_element_type=F32)
            hu = jnp.dot(h, wu_ref[e], preferred_element_type=F32)
            a = hg * jax.nn.sigmoid(hg) * hu * cw[:, e:e + 1]
            acc = acc + jnp.dot(a.astype(BF16), wd_ref[e], preferred_element_type=F32)
        y_ref[...] = acc


def _ffn(cnt, hs, w_gate, w_up, w_down, cap):
    rows, dc = hs.shape
    d = dc - LANES
    tm = TM_FFN
    blocks_per_group = cap // tm
    n_steps = (blocks_per_group - 1) + N_GROUPS

    def row_map(s, c):
        g, blk, _ = _ffn_tile(s, c, tm)
        return (g * blocks_per_group + blk, 0)

    def w_map(s, c):
        g, _, _ = _ffn_tile(s, c, tm)
        return (g, 0, 0)

    ff = w_gate.shape[2]
    gs = pltpu.PrefetchScalarGridSpec(
        num_scalar_prefetch=1,
        grid=(n_steps,),
        in_specs=[pl.BlockSpec((tm, dc), row_map),
                  pl.BlockSpec((EXPERTS_PER_GROUP, d, ff), w_map),
                  pl.BlockSpec((EXPERTS_PER_GROUP, d, ff), w_map),
                  pl.BlockSpec((EXPERTS_PER_GROUP, ff, d), w_map)],
        out_specs=pl.BlockSpec((tm, d), row_map),
    )
    return pl.pallas_call(
        functools.partial(_ffn_kernel, d=d),
        grid_spec=gs,
        out_shape=jax.ShapeDtypeStruct((rows, d), F32),
        compiler_params=_cparams(1),
        name="ffn",
    )(cnt, hs, w_gate, w_up, w_down)


def _final_kernel(pos_ref, ys_ref, x1_ref, mod_ref, g_ref, o_ref, ybuf_ref, sem, *, d, normalize):
    tm = x1_ref.shape[0]
    base = pl.program_id(0) * tm

    def issue(r, c):
        pltpu.make_async_copy(ys_ref.at[pl.ds(pos_ref[base + r], 1), :],
                              ybuf_ref.at[pl.ds(r, 1), :], sem).start()
        return c

    lax.fori_loop(0, tm, issue, 0, unroll=8)
    pltpu.make_async_copy(ys_ref.at[pl.ds(0, tm), :], ybuf_ref, sem).wait()

    g_f = mod_ref[0][:, 5 * d:6 * d]
    x2 = x1_ref[...] + g_f * ybuf_ref[...]
    if normalize:
        x2 = x2 * lax.rsqrt(jnp.mean(x2 * x2, axis=-1, keepdims=True) + EPS) * g_ref[...]
    o_ref[...] = x2


def _final(pos, ys, x1, mod3, final_g, tiles_per_batch, normalize):
    t, d = x1.shape
    tm = TM_FIN
    gs = pltpu.PrefetchScalarGridSpec(
        num_scalar_prefetch=1,
        grid=(t // tm,),
        in_specs=[pl.BlockSpec(memory_space=pl.ANY),
                  pl.BlockSpec((tm, d), lambda i, p: (i, 0)),
                  pl.BlockSpec((1, 1, mod3.shape[2]), lambda i, p: (i // tiles_per_batch, 0, 0)),
                  pl.BlockSpec((1, d), lambda i, p: (0, 0))],
        out_specs=pl.BlockSpec((tm, d), lambda i, p: (i, 0)),
        scratch_shapes=[pltpu.VMEM((tm, d), F32), pltpu.SemaphoreType.DMA],
    )
    return pl.pallas_call(
        functools.partial(_final_kernel, d=d, normalize=normalize),
        grid_spec=gs,
        out_shape=jax.ShapeDtypeStruct((t, d), F32),
        compiler_params=_cparams(1),
        name="final",
    )(pos, ys, x1, mod3, final_g)


def kernel(x, c, positions, w_ada, b_ada, norm_mix_g, w_in, lambda_q1, lambda_k1, lambda_q2,
           lambda_k2, subln_g, conv_w, w_up_att, w_up_conv, w_out, norm_ffn_g, w_group_router,
           b_group_router, w_expert_router, b_expert_router, w_gate, w_up, w_down, final_norm_g):
    b, s, d = x.shape
    depth = w_ada.shape[0]
    t = b * s
    assert s % TM_IN == 0 and s % TK == 0 and TK % TQ == 0 and s % TM_MIX == 0
    assert TM_MIX == TM_FIN and t % TM_FFN == 0
    cap = t + TM_FFN
    n_e = N_GROUPS * EXPERTS_PER_GROUP

    c8 = jnp.concatenate([c, jnp.zeros((8 - b, d), F32)], axis=0)
    pos3 = positions.reshape(b, s, 1)
    inv_freq = ROPE_THETA ** (-jnp.arange(0, HEAD_DIM, 2, dtype=F32) / HEAD_DIM)
    invf = jnp.tile(inv_freq, LANES // (HEAD_DIM // 2)).reshape(1, LANES)

    xf = x
    for l in range(depth):
        lambda_init = 0.8 - 0.6 * math.exp(-0.3 * l)
        row = lambda a: a[l].reshape(1, -1)
        mod8, lam = _ada(c8, w_ada[l], row(b_ada), row(lambda_q1), row(lambda_k1),
                         row(lambda_q2), row(lambda_k2), lambda_init)
        mod3 = mod8[0:b].reshape(b, 1, 6 * d)

        q, k, v, sg, mc = _inproj(xf.reshape(b, s, d), pos3, mod3, row(norm_mix_g), invf,
                                  w_in[l].astype(BF16), conv_w[l], w_up_conv[l].astype(BF16))
        att = _attn(lam, q, k, v, row(subln_g), 1.0 - lambda_init)

        wrt = jnp.concatenate([w_expert_router[l].T, w_group_router[l].T,
                               jnp.zeros((LANES - n_e - N_GROUPS, d), F32)], axis=0)
        br = jnp.concatenate([b_expert_router[l], b_group_router[l],
                              jnp.zeros((LANES - n_e - N_GROUPS,), F32)]).reshape(LANES, 1)
        x1, pos, cnt, hs = _mixout(att.reshape(t, -1), sg.reshape(t, d), mc.reshape(t, d),
                                   xf.reshape(t, d), mod3, w_up_att[l].astype(BF16),
                                   w_out[l].astype(BF16), row(norm_ffn_g), wrt, br,
                                   s // TM_MIX, cap)
        ys = _ffn(cnt[0:N_GROUPS, 0], hs, w_gate[l].astype(BF16), w_up[l].astype(BF16),
                  w_down[l].astype(BF16), cap)
        xf = _final(pos.reshape(t), ys, x1, mod3, final_norm_g.reshape(1, d), s // TM_FIN,
                    l == depth - 1)
    return xf.reshape(b, s, d)
```

```python
import functools
import math

import jax
import jax.numpy as jnp
from jax import lax
from jax.experimental import pallas as pl
from jax.experimental.pallas import tpu as pltpu

F32 = jnp.float32
BF16 = jnp.bfloat16
I32 = jnp.int32

EPS = 1e-6
MASK_VALUE = -1e30
ROPE_THETA = 10000.0

N_HEADS = 4
HEAD_DIM = 64
LANES = 128
N_GROUPS = 4
EXPERTS_PER_GROUP = 8
CONV_K = 3

VMEM_LIMIT = 56 * 1024 * 1024

TM_IN = 512
TQ = 512
TK = 512
TM_MIX = 512
TM_FFN = 512
TM_FIN = 512
TN_ADA = 1024


def _cparams(n_axes):
    return pltpu.CompilerParams(
        dimension_semantics=("arbitrary",) * n_axes, vmem_limit_bytes=VMEM_LIMIT)


def _ada_kernel(c_ref, w_ref, b_ref, lq1_ref, lk1_ref, lq2_ref, lk2_ref, mod_ref, lam_ref, *,
                lambda_init):
    c = c_ref[...]
    a = c * jax.nn.sigmoid(c)
    mod_ref[...] = jnp.dot(a, w_ref[...], preferred_element_type=F32,
                           precision=lax.Precision.HIGHEST) + b_ref[...]
    s1 = jnp.sum(lq1_ref[...] * lk1_ref[...], axis=-1, keepdims=True)
    s2 = jnp.sum(lq2_ref[...] * lk2_ref[...], axis=-1, keepdims=True)
    lam = jnp.exp(s1) - jnp.exp(s2) + lambda_init
    lam_ref[...] = jnp.broadcast_to(lam, lam_ref.shape)


def _ada(c8, w_ada, b_ada, lq1, lk1, lq2, lk2, lambda_init):
    d, n = w_ada.shape
    small = pl.BlockSpec((1, HEAD_DIM), lambda j: (0, 0))
    return pl.pallas_call(
        functools.partial(_ada_kernel, lambda_init=lambda_init),
        grid=(n // TN_ADA,),
        in_specs=[pl.BlockSpec((8, d), lambda j: (0, 0)),
                  pl.BlockSpec((d, TN_ADA), lambda j: (0, j)),
                  pl.BlockSpec((1, TN_ADA), lambda j: (0, j)),
                  small, small, small, small],
        out_specs=[pl.BlockSpec((8, TN_ADA), lambda j: (0, j)),
                   pl.BlockSpec((8, LANES), lambda j: (0, 0))],
        out_shape=[jax.ShapeDtypeStruct((8, n), F32), jax.ShapeDtypeStruct((8, LANES), F32)],
        compiler_params=_cparams(1),
        name="ada",
    )(c8, w_ada, b_ada, lq1, lk1, lq2, lk2)


def _inproj_kernel(x_ref, pos_ref, mod_ref, g_ref, invf_ref, w_ref, cw_ref, wupc_ref,
                   qt_ref, k_ref, vt_ref, sg_ref, mc_ref, carry_ref, *, d, qk_w, conv_w):
    tm = x_ref.shape[1]
    x = x_ref[0]
    ms = jnp.mean(x * x, axis=-1, keepdims=True)
    mod = mod_ref[0]
    sh = mod[:, 0:d]
    sc = mod[:, d:2 * d]
    h = (x * lax.rsqrt(ms + EPS) * g_ref[...]) * (1.0 + sc) + sh
    hb = h.astype(BF16)

    def proj(a, b):
        return jnp.dot(hb, w_ref[:, a:b], preferred_element_type=F32)

    lane = lax.broadcasted_iota(I32, (1, LANES), 1)
    first_half = (lane % HEAD_DIM) < (HEAD_DIM // 2)
    ang = pos_ref[0].astype(F32) * invf_ref[...]
    cos = jnp.cos(ang)
    sin = jnp.sin(ang)
    sin_signed = jnp.where(first_half, -sin, sin)

    def rope(t):
        partner = jnp.where(first_half, pltpu.roll(t, LANES - HEAD_DIM // 2, 1),
                            pltpu.roll(t, HEAD_DIM // 2, 1))
        return t * cos + partner * sin_signed

    zq = proj(0, qk_w)
    zk = proj(qk_w, 2 * qk_w)
    scale = HEAD_DIM ** -0.5 * math.log2(math.e)
    zv = proj(2 * qk_w, 3 * qk_w)
    for hd in range(qk_w // LANES):
        sl = slice(hd * LANES, (hd + 1) * LANES)
        qt_ref[0, hd] = (rope(zq[:, sl]) * scale).T.astype(BF16)
        k_ref[0, :, sl] = rope(zk[:, sl]).astype(BF16)
        for ck in range(tm // TK):
            vt_ref[0, hd, ck] = zv[ck * TK:(ck + 1) * TK, sl].T.astype(BF16)

    o = 3 * qk_w
    cb = proj(o, o + conv_w)
    u = proj(o + conv_w, o + 2 * conv_w) * proj(o + 2 * conv_w, o + 3 * conv_w)

    @pl.when(pl.program_id(1) == 0)
    def _():
        carry_ref[...] = jnp.zeros_like(carry_ref)

    prev = carry_ref[...]
    row = lax.broadcasted_iota(I32, (tm, 1), 0)
    u1 = jnp.where(row == 0, prev[7:8, :], pltpu.roll(u, 1, 0))
    u2 = jnp.where(row == 0, prev[6:7, :],
                   jnp.where(row == 1, prev[7:8, :], pltpu.roll(u, 2, 0)))
    carry_ref[...] = u[tm - 8:tm, :]
    cw = cw_ref[...]
    conv = cb * (cw[0:1, :] * u + cw[1:2, :] * u1 + cw[2:3, :] * u2)
    convp = jnp.dot(conv.astype(BF16), wupc_ref[...], preferred_element_type=F32)

    o = o + 3 * conv_w
    sg_ref[0] = jax.nn.sigmoid(proj(o, o + d)).astype(BF16)
    mc_ref[0] = (jax.nn.sigmoid(proj(o + d, o + 2 * d)) * convp).astype(BF16)


def _inproj(x, pos3, mod3, norm_g, invf, w_in, conv_w, w_up_conv):
    b, s, d = x.shape
    qk_w = N_HEADS * 2 * HEAD_DIM
    cw = conv_w.shape[1]
    tm = TM_IN
    tok = lambda w: pl.BlockSpec((1, tm, w), lambda bi, si: (bi, si, 0))
    full = lambda a: pl.BlockSpec(a.shape, lambda bi, si: (0,) * a.ndim)
    return pl.pallas_call(
        functools.partial(_inproj_kernel, d=d, qk_w=qk_w, conv_w=cw),
        grid=(b, s // tm),
        in_specs=[tok(d), tok(1),
                  pl.BlockSpec((1, 1, mod3.shape[2]), lambda bi, si: (bi, 0, 0)),
                  full(norm_g), full(invf), full(w_in), full(conv_w), full(w_up_conv)],
        out_specs=[pl.BlockSpec((1, N_HEADS, LANES, tm), lambda bi, si: (bi, 0, 0, si)),
                   tok(qk_w),
                   pl.BlockSpec((1, N_HEADS, tm // TK, LANES, TK),
                                lambda bi, si: (bi, 0, si, 0, 0)),
                   tok(d), tok(d)],
        out_shape=[jax.ShapeDtypeStruct((b, N_HEADS, LANES, s), BF16),
                   jax.ShapeDtypeStruct((b, s, qk_w), BF16),
                   jax.ShapeDtypeStruct((b, N_HEADS, s // TK, LANES, TK), BF16),
                   jax.ShapeDtypeStruct((b, s, d), BF16),
                   jax.ShapeDtypeStruct((b, s, d), BF16)],
        scratch_shapes=[pltpu.VMEM((8, cw), F32)],
        compiler_params=_cparams(2),
        name="inproj",
    )(x, pos3, mod3, norm_g, invf, w_in, conv_w, w_up_conv)


def _attn_kernel(lam_ref, qt_ref, k_ref, vt_ref, g_ref, o_ref, s_ref, *, out_scale):
    tq = qt_ref.shape[3]
    qi = pl.program_id(2)
    qt = qt_ref[0, 0]
    feat = lax.broadcasted_iota(I32, (LANES, 1), 0)
    zero = jnp.zeros_like(qt)
    qqt = jnp.concatenate([jnp.where(feat < HEAD_DIM, qt, zero),
                           jnp.where(feat >= HEAD_DIM, qt, zero)], axis=1)
    q_idx = qi * tq + lax.broadcasted_iota(I32, (1, tq), 1)
    k_sub = lax.broadcasted_iota(I32, (TK, 1), 0)

    def scores(j):
        start = pl.multiple_of(j * TK, TK)
        kt = k_ref[0, pl.ds(start, TK), :]
        col_max = []
        for half in range(2):
            lanes = slice(half * tq, (half + 1) * tq)
            s = jnp.dot(kt, qqt[:, lanes], preferred_element_type=F32)
            s_ref[:, lanes] = s
            col_max.append(jnp.max(s, axis=0, keepdims=True))
        return tuple(col_max)

    def softmax_pv(j, col_max, carry, masked):
        out = []
        for half, (m, l, acc) in enumerate(carry):
            s = s_ref[:, half * tq:(half + 1) * tq]
            if masked:
                s = jnp.where(j * TK + k_sub <= q_idx, s, MASK_VALUE)
                cm = jnp.max(s, axis=0, keepdims=True)
            else:
                cm = col_max[half]
            m_new = jnp.maximum(m, cm)
            alpha = jnp.exp2(m - m_new)
            p = jnp.exp2(s - m_new)
            l = alpha * l + jnp.sum(p, axis=0, keepdims=True)
            acc = alpha * acc + jnp.dot(vt_ref[0, 0, j], p.astype(BF16),
                                        preferred_element_type=F32)
            out.append((m_new, l, acc))
        return tuple(out)

    def body(j, sc):
        col_max, carry = sc
        carry = softmax_pv(j, col_max, carry, False)
        return scores(j + 1), carry

    one = (jnp.full((1, tq), MASK_VALUE, F32), jnp.zeros((1, tq), F32),
           jnp.zeros((LANES, tq), F32))
    _, carry = lax.fori_loop(0, qi, body, (scores(0), (one, one)))
    (_, l1, acc1), (_, l2, acc2) = softmax_pv(qi, None, carry, True)

    a = acc1 / l1 - lam_ref[0:1, 0:1] * (acc2 / l2)
    y = a * lax.rsqrt(jnp.mean(a * a, axis=0, keepdims=True) + EPS) * g_ref[...]
    o_ref[0] = (y * out_scale).T.astype(BF16)


def _attn(lam, qt, k, vt, subln_g_col, out_scale):
    b, s, w = k.shape
    return pl.pallas_call(
        functools.partial(_attn_kernel, out_scale=out_scale),
        grid=(b, N_HEADS, s // TQ),
        in_specs=[pl.BlockSpec((8, LANES), lambda bi, hi, qi: (0, 0)),
                  pl.BlockSpec((1, 1, LANES, TQ), lambda bi, hi, qi: (bi, hi, 0, qi)),
                  pl.BlockSpec((1, s, LANES), lambda bi, hi, qi: (bi, 0, hi)),
                  pl.BlockSpec((1, 1, s // TK, LANES, TK), lambda bi, hi, qi: (bi, hi, 0, 0, 0)),
                  pl.BlockSpec((LANES, 1), lambda bi, hi, qi: (0, 0))],
        out_specs=pl.BlockSpec((1, TQ, LANES), lambda bi, hi, qi: (bi, qi, hi)),
        out_shape=jax.ShapeDtypeStruct((b, s, w), BF16),
        scratch_shapes=[pltpu.VMEM((TK, 2 * TQ), F32)],
        compiler_params=_cparams(3),
        name="attn",
    )(lam, qt, k, vt, subln_g_col)


def _mixout_kernel(att_ref, sg_ref, mc_ref, x_ref, mod_ref, wua_ref, wo_ref, g_ref, wrt_ref,
                   br_ref, x1_ref, pos_ref, cnt_ref, hs_ref,
                   h2c_ref, carry_ref, posv_ref, cntv_ref, poss_ref, cnts_ref,
                   sem_s, sem_r, *, d, cap):
    tm = x_ref.shape[0]
    i = pl.program_id(0)
    last = pl.num_programs(0) - 1

    @pl.when(i == 0)
    def _():
        carry_ref[...] = jnp.zeros_like(carry_ref)

    mod = mod_ref[0]
    g_m = mod[:, 2 * d:3 * d]
    sh_f = mod[:, 3 * d:4 * d]
    sc_f = mod[:, 4 * d:5 * d]

    attp = jnp.dot(att_ref[...], wua_ref[...], preferred_element_type=F32)
    merged = sg_ref[...].astype(F32) * attp + mc_ref[...].astype(F32)
    mo = jnp.dot(merged.astype(BF16), wo_ref[...], preferred_element_type=F32)
    x1 = x_ref[...] + g_m * mo
    x1_ref[...] = x1
    ms = jnp.mean(x1 * x1, axis=-1, keepdims=True)
    h2 = (x1 * lax.rsqrt(ms + EPS) * g_ref[...]) * (1.0 + sc_f) + sh_f

    lt = lax.dot_general(wrt_ref[...], h2, (((1,), (1,)), ((), ())),
                         preferred_element_type=F32,
                         precision=lax.Precision.HIGHEST) + br_ref[...]
    ridx = lax.broadcasted_iota(I32, (8, 1), 0)
    n_e = N_GROUPS * EXPERTS_PER_GROUP
    is_grp = ridx < N_GROUPS
    gl = jnp.where(is_grp, lt[n_e:n_e + 8, :], MASK_VALUE)
    gmax = jnp.max(gl, axis=0, keepdims=True)
    gsum = jnp.sum(jnp.where(is_grp, jnp.exp(gl - gmax), 0.0), axis=0, keepdims=True)
    g_w = 1.0 / gsum
    gidx = jnp.min(jnp.where(gl == gmax, ridx, 8), axis=0, keepdims=True)
    e_in = jnp.zeros((8, tm), F32)
    for g in range(N_GROUPS):
        e_in = jnp.where(gidx == g, lt[g * 8:(g + 1) * 8, :], e_in)
    t1 = jnp.max(e_in, axis=0, keepdims=True)
    i1 = jnp.min(jnp.where(e_in == t1, ridx, 8), axis=0, keepdims=True)
    e_in2 = jnp.where(ridx == i1, MASK_VALUE, e_in)
    t2 = jnp.max(e_in2, axis=0, keepdims=True)
    i2 = jnp.min(jnp.where(e_in2 == t2, ridx, 8), axis=0, keepdims=True)
    e2 = jnp.exp(t2 - t1)
    den = 1.0 + e2
    w1 = (1.0 / den) * g_w
    w2 = (e2 / den) * g_w
    cw_t = jnp.where(ridx == i1, w1, jnp.where(ridx == i2, w2, 0.0))
    cw_col = jnp.concatenate([cw_t, jnp.zeros((LANES - 8, tm), F32)], axis=0).T

    onehot = jnp.where(ridx == gidx, 1.0, 0.0)
    tl = lax.broadcasted_iota(I32, (1, tm), 1)
    incl = onehot
    shift = 1
    while shift < tm:
        incl = incl + jnp.where(tl >= shift, pltpu.roll(incl, shift, 1), 0.0)
        shift *= 2
    run = carry_ref[...][:, 0:1]
    base = ridx.astype(F32) * float(cap)
    pos = jnp.sum(onehot * (base + run + incl - onehot), axis=0, keepdims=True).astype(I32)
    new_run = run + incl[:, tm - 1:tm]
    carry_ref[...] = jnp.broadcast_to(new_run, carry_ref.shape)
    cnt_i = jnp.broadcast_to(new_run, cnt_ref.shape).astype(I32)
    cnt_ref[...] = cnt_i
    pos_ref[0] = pos
    posv_ref[...] = pos

    h2c_ref[:, 0:d] = h2
    h2c_ref[:, d:d + LANES] = cw_col

    cp = pltpu.make_async_copy(posv_ref.at[0], poss_ref, sem_s)
    cp.start()
    cp.wait()

    def row_copy(r, p):
        return pltpu.make_async_copy(h2c_ref.at[pl.ds(r, 1), :], hs_ref.at[pl.ds(p, 1), :], sem_r)

    def issue(r, c):
        row_copy(r, poss_ref[r]).start()
        return c

    lax.fori_loop(0, tm, issue, 0, unroll=8)
    pltpu.make_async_copy(h2c_ref, hs_ref.at[pl.ds(0, tm), :], sem_r).wait()

    @pl.when(i == last)
    def _():
        cntv_ref[...] = cnt_i
        cc = pltpu.make_async_copy(cntv_ref, cnts_ref, sem_s)
        cc.start()
        cc.wait()
        h2c_ref[0:8, :] = jnp.zeros((8, h2c_ref.shape[1]), F32)

        def zissue(r, c):
            for g in range(N_GROUPS):
                row_copy(0, g * cap + cnts_ref[g, 0] + r).start()
            return c

        lax.fori_loop(0, tm, zissue, 0, unroll=2)
        for g in range(N_GROUPS):
            pltpu.make_async_copy(h2c_ref, hs_ref.at[pl.ds(0, tm), :], sem_r).wait()


def _mixout(att, sg, mc, x, mod3, w_up_att, w_out, norm_g, wrt, br, tiles_per_batch, cap):
    t, d = x.shape
    tm = TM_MIX
    dc = d + LANES
    tok = lambda w: pl.BlockSpec((tm, w), lambda i: (i, 0))
    full = lambda a: pl.BlockSpec(a.shape, lambda i: (0,) * a.ndim)
    return pl.pallas_call(
        functools.partial(_mixout_kernel, d=d, cap=cap),
        grid=(t // tm,),
        in_specs=[tok(att.shape[1]), tok(d), tok(d), tok(d),
                  pl.BlockSpec((1, 1, mod3.shape[2]), lambda i: (i // tiles_per_batch, 0, 0)),
                  full(w_up_att), full(w_out), full(norm_g), full(wrt), full(br)],
        out_specs=[tok(d),
                   pl.BlockSpec((1, 1, tm), lambda i: (i, 0, 0)),
                   pl.BlockSpec((8, LANES), lambda i: (0, 0)),
                   pl.BlockSpec(memory_space=pl.ANY)],
        out_shape=[jax.ShapeDtypeStruct((t, d), F32),
                   jax.ShapeDtypeStruct((t // tm, 1, tm), I32),
                   jax.ShapeDtypeStruct((8, LANES), I32),
                   jax.ShapeDtypeStruct((N_GROUPS * cap, dc), F32)],
        scratch_shapes=[pltpu.VMEM((tm, dc), F32), pltpu.VMEM((8, LANES), F32),
                        pltpu.VMEM((1, tm), I32), pltpu.VMEM((8, LANES), I32),
                        pltpu.SMEM((tm,), I32), pltpu.SMEM((8, LANES), I32),
                        pltpu.SemaphoreType.DMA, pltpu.SemaphoreType.DMA],
        compiler_params=_cparams(1),
        name="mixout",
    )(att, sg, mc, x, mod3, w_up_att, w_out, norm_g, wrt, br)


def _ffn_tile(step, cnt_ref, tm):
    log_tm = tm.bit_length() - 1
    ends = []
    tot = jnp.int32(0)
    for g in range(N_GROUPS):
        tot = tot + lax.shift_right_logical(cnt_ref[g] + (tm - 1), log_tm)
        ends.append(tot)
    sc = jnp.minimum(step, tot - 1)
    g = sum((sc >= e).astype(I32) for e in ends[:-1])
    start = jnp.where(g == 0, 0, jnp.where(g == 1, ends[0], jnp.where(g == 2, ends[1], ends[2])))
    return g, sc - start, step < tot


def _ffn_kernel(cnt_ref, hs_ref, wg_ref, wu_ref, wd_ref, y_ref, *, d):
    tm = hs_ref.shape[0]
    _, _, real = _ffn_tile(pl.program_id(0), cnt_ref, tm)

    @pl.when(real)
    def _():
        h = hs_ref[:, 0:d].astype(BF16)
        cw = hs_ref[:, d:d + LANES]
        acc = jnp.zeros((tm, d), F32)
        for e in range(EXPERTS_PER_GROUP):
            hg = jnp.dot(h, wg_ref[e], preferred_element_type=F32)
            hu = jnp.dot(h, wu_ref[e], preferred_element_type=F32)
            a = hg * jax.nn.sigmoid(hg) * hu * cw[:, e:e + 1]
            acc = acc + jnp.dot(a.astype(BF16), wd_ref[e], preferred_element_type=F32)
        y_ref[...] = acc


def _ffn(cnt, hs, w_gate, w_up, w_down, cap):
    rows, dc = hs.shape
    d = dc - LANES
    tm = TM_FFN
    blocks_per_group = cap // tm
    n_steps = (blocks_per_group - 1) + N_GROUPS

    def row_map(s, c):
        g, blk, _ = _ffn_tile(s, c, tm)
        return (g * blocks_per_group + blk, 0)

    def w_map(s, c):
        g, _, _ = _ffn_tile(s, c, tm)
        return (g, 0, 0)

    ff = w_gate.shape[2]
    gs = pltpu.PrefetchScalarGridSpec(
        num_scalar_prefetch=1,
        grid=(n_steps,),
        in_specs=[pl.BlockSpec((tm, dc), row_map),
                  pl.BlockSpec((EXPERTS_PER_GROUP, d, ff), w_map),
                  pl.BlockSpec((EXPERTS_PER_GROUP, d, ff), w_map),
                  pl.BlockSpec((EXPERTS_PER_GROUP, ff, d), w_map)],
        out_specs=pl.BlockSpec((tm, d), row_map),
    )
    return pl.pallas_call(
        functools.partial(_ffn_kernel, d=d),
        grid_spec=gs,
        out_shape=jax.ShapeDtypeStruct((rows, d), F32),
        compiler_params=_cparams(1),
        name="ffn",
    )(cnt, hs, w_gate, w_up, w_down)


def _final_kernel(pos_ref, ys_ref, x1_ref, mod_ref, g_ref, o_ref, ybuf_ref, sem, *, d, normalize):
    tm = x1_ref.shape[0]
    base = pl.program_id(0) * tm

    def issue(r, c):
        pltpu.make_async_copy(ys_ref.at[pl.ds(pos_ref[base + r], 1), :],
                              ybuf_ref.at[pl.ds(r, 1), :], sem).start()
        return c

    lax.fori_loop(0, tm, issue, 0, unroll=8)
    pltpu.make_async_copy(ys_ref.at[pl.ds(0, tm), :], ybuf_ref, sem).wait()

    g_f = mod_ref[0][:, 5 * d:6 * d]
    x2 = x1_ref[...] + g_f * ybuf_ref[...]
    if normalize:
        x2 = x2 * lax.rsqrt(jnp.mean(x2 * x2, axis=-1, keepdims=True) + EPS) * g_ref[...]
    o_ref[...] = x2


def _final(pos, ys, x1, mod3, final_g, tiles_per_batch, normalize):
    t, d = x1.shape
    tm = TM_FIN
    gs = pltpu.PrefetchScalarGridSpec(
        num_scalar_prefetch=1,
        grid=(t // tm,),
        in_specs=[pl.BlockSpec(memory_space=pl.ANY),
                  pl.BlockSpec((tm, d), lambda i, p: (i, 0)),
                  pl.BlockSpec((1, 1, mod3.shape[2]), lambda i, p: (i // tiles_per_batch, 0, 0)),
                  pl.BlockSpec((1, d), lambda i, p: (0, 0))],
        out_specs=pl.BlockSpec((tm, d), lambda i, p: (i, 0)),
        scratch_shapes=[pltpu.VMEM((tm, d), F32), pltpu.SemaphoreType.DMA],
    )
    return pl.pallas_call(
        functools.partial(_final_kernel, d=d, normalize=normalize),
        grid_spec=gs,
        out_shape=jax.ShapeDtypeStruct((t, d), F32),
        compiler_params=_cparams(1),
        name="final",
    )(pos, ys, x1, mod3, final_g)


def kernel(x, c, positions, w_ada, b_ada, norm_mix_g, w_in, lambda_q1, lambda_k1, lambda_q2,
           lambda_k2, subln_g, conv_w, w_up_att, w_up_conv, w_out, norm_ffn_g, w_group_router,
           b_group_router, w_expert_router, b_expert_router, w_gate, w_up, w_down, final_norm_g):
    b, s, d = x.shape
    depth = w_ada.shape[0]
    t = b * s
    assert s % TM_IN == 0 and s % TQ == 0 and TQ == TK and TM_IN % TK == 0
    assert s % TM_MIX == 0
    assert TM_MIX == TM_FIN and t % TM_FFN == 0
    cap = t + TM_FFN
    n_e = N_GROUPS * EXPERTS_PER_GROUP

    c8 = jnp.concatenate([c, jnp.zeros((8 - b, d), F32)], axis=0)
    pos3 = positions.reshape(b, s, 1)
    inv_freq = ROPE_THETA ** (-jnp.arange(0, HEAD_DIM, 2, dtype=F32) / HEAD_DIM)
    invf = jnp.tile(inv_freq, LANES // (HEAD_DIM // 2)).reshape(1, LANES)

    xf = x
    for l in range(depth):
        lambda_init = 0.8 - 0.6 * math.exp(-0.3 * l)
        row = lambda a: a[l].reshape(1, -1)
        mod8, lam = _ada(c8, w_ada[l], row(b_ada), row(lambda_q1), row(lambda_k1),
                         row(lambda_q2), row(lambda_k2), lambda_init)
        mod3 = mod8[0:b].reshape(b, 1, 6 * d)

        qt, k, vt, sg, mc = _inproj(xf.reshape(b, s, d), pos3, mod3, row(norm_mix_g), invf,
                                    w_in[l].astype(BF16), conv_w[l], w_up_conv[l].astype(BF16))
        att = _attn(lam, qt, k, vt, subln_g[l].reshape(-1, 1), 1.0 - lambda_init)

        wrt = jnp.concatenate([w_expert_router[l].T, w_group_router[l].T,
                               jnp.zeros((LANES - n_e - N_GROUPS, d), F32)], axis=0)
        br = jnp.concatenate([b_expert_router[l], b_group_router[l],
                              jnp.zeros((LANES - n_e - N_GROUPS,), F32)]).reshape(LANES, 1)
        x1, pos, cnt, hs = _mixout(att.reshape(t, -1), sg.reshape(t, d), mc.reshape(t, d),
                                   xf.reshape(t, d), mod3, w_up_att[l].astype(BF16),
                                   w_out[l].astype(BF16), row(norm_ffn_g), wrt, br,
                                   s // TM_MIX, cap)
        ys = _ffn(cnt[0:N_GROUPS, 0], hs, w_gate[l].astype(BF16), w_up[l].astype(BF16),
                  w_down[l].astype(BF16), cap)
        xf = _final(pos.reshape(t), ys, x1, mod3, final_norm_g.reshape(1, d), s // TM_FIN,
                    l == depth - 1)
    return xf.reshape(b, s, d)
```

```python
import functools
import math

import jax
import jax.numpy as jnp
from jax import lax
from jax.experimental import pallas as pl
from jax.experimental.pallas import tpu as pltpu

F32 = jnp.float32
BF16 = jnp.bfloat16
I32 = jnp.int32

EPS = 1e-6
MASK_VALUE = -1e30
ROPE_THETA = 10000.0

N_HEADS = 4
HEAD_DIM = 64
LANES = 128
N_GROUPS = 4
EXPERTS_PER_GROUP = 8
CONV_K = 3
ROUTER_ROWS = 48

VMEM_LIMIT = 56 * 1024 * 1024

TM_IN = 512
TQ = 512
TK = 512
TM_MIX = 512
TM_FFN = 512
TM_FIN = 512
TN_ADA = 1024


def _cparams(n_axes):
    return pltpu.CompilerParams(
        dimension_semantics=("arbitrary",) * n_axes, vmem_limit_bytes=VMEM_LIMIT)


def _ada_kernel(c_ref, w_ref, b_ref, lq1_ref, lk1_ref, lq2_ref, lk2_ref, mod_ref, lam_ref, *,
                lambda_init):
    c = c_ref[...]
    a = c * jax.nn.sigmoid(c)
    mod_ref[...] = jnp.dot(a, w_ref[...], preferred_element_type=F32,
                           precision=lax.Precision.HIGHEST) + b_ref[...]
    s1 = jnp.sum(lq1_ref[...] * lk1_ref[...], axis=-1, keepdims=True)
    s2 = jnp.sum(lq2_ref[...] * lk2_ref[...], axis=-1, keepdims=True)
    lam = jnp.exp(s1) - jnp.exp(s2) + lambda_init
    lam_ref[...] = jnp.broadcast_to(lam, lam_ref.shape)


def _ada(c8, w_ada, b_ada, lq1, lk1, lq2, lk2, lambda_init):
    d, n = w_ada.shape
    small = pl.BlockSpec((1, HEAD_DIM), lambda j: (0, 0))
    return pl.pallas_call(
        functools.partial(_ada_kernel, lambda_init=lambda_init),
        grid=(n // TN_ADA,),
        in_specs=[pl.BlockSpec((8, d), lambda j: (0, 0)),
                  pl.BlockSpec((d, TN_ADA), lambda j: (0, j)),
                  pl.BlockSpec((1, TN_ADA), lambda j: (0, j)),
                  small, small, small, small],
        out_specs=[pl.BlockSpec((8, TN_ADA), lambda j: (0, j)),
                   pl.BlockSpec((8, LANES), lambda j: (0, 0))],
        out_shape=[jax.ShapeDtypeStruct((8, n), F32), jax.ShapeDtypeStruct((8, LANES), F32)],
        compiler_params=_cparams(1),
        name="ada",
    )(c8, w_ada, b_ada, lq1, lk1, lq2, lk2)


def _inproj_kernel(x_ref, pos_ref, mod_ref, g_ref, invf_ref, w_ref, cw_ref, wupc_ref,
                   qt_ref, k_ref, vt_ref, sg_ref, mc_ref, carry_ref, *, d, qk_w, conv_w):
    tm = x_ref.shape[1]
    x = x_ref[0]
    ms = jnp.mean(x * x, axis=-1, keepdims=True)
    mod = mod_ref[0]
    sh = mod[:, 0:d]
    sc = mod[:, d:2 * d]
    h = (x * lax.rsqrt(ms + EPS) * g_ref[...]) * (1.0 + sc) + sh
    hb = h.astype(BF16)

    def proj(a, b):
        return jnp.dot(hb, w_ref[:, a:b], preferred_element_type=F32)

    lane = lax.broadcasted_iota(I32, (1, LANES), 1)
    first_half = (lane % HEAD_DIM) < (HEAD_DIM // 2)
    ang = pos_ref[0].astype(F32) * invf_ref[...]
    cos = jnp.cos(ang)
    sin = jnp.sin(ang)
    sin_signed = jnp.where(first_half, -sin, sin)

    def rope(t):
        partner = jnp.where(first_half, pltpu.roll(t, LANES - HEAD_DIM // 2, 1),
                            pltpu.roll(t, HEAD_DIM // 2, 1))
        return t * cos + partner * sin_signed

    zq = proj(0, qk_w)
    zk = proj(qk_w, 2 * qk_w)
    scale = HEAD_DIM ** -0.5 * math.log2(math.e)
    zv = proj(2 * qk_w, 3 * qk_w)
    for hd in range(qk_w // LANES):
        sl = slice(hd * LANES, (hd + 1) * LANES)
        qt_ref[0, hd] = (rope(zq[:, sl]) * scale).T.astype(BF16)
        k_ref[0, :, sl] = rope(zk[:, sl]).astype(BF16)
        for ck in range(tm // TK):
            vt_ref[0, hd, ck] = zv[ck * TK:(ck + 1) * TK, sl].T.astype(BF16)

    o = 3 * qk_w
    cb = proj(o, o + conv_w)
    u = proj(o + conv_w, o + 2 * conv_w) * proj(o + 2 * conv_w, o + 3 * conv_w)

    @pl.when(pl.program_id(1) == 0)
    def _():
        carry_ref[...] = jnp.zeros_like(carry_ref)

    prev = carry_ref[...]
    row = lax.broadcasted_iota(I32, (tm, 1), 0)
    u1 = jnp.where(row == 0, prev[7:8, :], pltpu.roll(u, 1, 0))
    u2 = jnp.where(row == 0, prev[6:7, :],
                   jnp.where(row == 1, prev[7:8, :], pltpu.roll(u, 2, 0)))
    carry_ref[...] = u[tm - 8:tm, :]
    cw = cw_ref[...]
    conv = cb * (cw[0:1, :] * u + cw[1:2, :] * u1 + cw[2:3, :] * u2)
    convp = jnp.dot(conv.astype(BF16), wupc_ref[...], preferred_element_type=F32)

    o = o + 3 * conv_w
    sg_ref[0] = jax.nn.sigmoid(proj(o, o + d)).astype(BF16)
    mc_ref[0] = (jax.nn.sigmoid(proj(o + d, o + 2 * d)) * convp).astype(BF16)


def _inproj(x, pos3, mod3, norm_g, invf, w_in, conv_w, w_up_conv):
    b, s, d = x.shape
    qk_w = N_HEADS * 2 * HEAD_DIM
    cw = conv_w.shape[1]
    tm = TM_IN
    tok = lambda w: pl.BlockSpec((1, tm, w), lambda bi, si: (bi, si, 0))
    full = lambda a: pl.BlockSpec(a.shape, lambda bi, si: (0,) * a.ndim)
    return pl.pallas_call(
        functools.partial(_inproj_kernel, d=d, qk_w=qk_w, conv_w=cw),
        grid=(b, s // tm),
        in_specs=[tok(d), tok(1),
                  pl.BlockSpec((1, 1, mod3.shape[2]), lambda bi, si: (bi, 0, 0)),
                  full(norm_g), full(invf), full(w_in), full(conv_w), full(w_up_conv)],
        out_specs=[pl.BlockSpec((1, N_HEADS, LANES, tm), lambda bi, si: (bi, 0, 0, si)),
                   tok(qk_w),
                   pl.BlockSpec((1, N_HEADS, tm // TK, LANES, TK),
                                lambda bi, si: (bi, 0, si, 0, 0)),
                   tok(d), tok(d)],
        out_shape=[jax.ShapeDtypeStruct((b, N_HEADS, LANES, s), BF16),
                   jax.ShapeDtypeStruct((b, s, qk_w), BF16),
                   jax.ShapeDtypeStruct((b, N_HEADS, s // TK, LANES, TK), BF16),
                   jax.ShapeDtypeStruct((b, s, d), BF16),
                   jax.ShapeDtypeStruct((b, s, d), BF16)],
        scratch_shapes=[pltpu.VMEM((8, cw), F32)],
        compiler_params=_cparams(2),
        name="inproj",
    )(x, pos3, mod3, norm_g, invf, w_in, conv_w, w_up_conv)


def _attn_kernel(lam_ref, qt_ref, k_ref, vt_ref, g_ref, o_ref, s_ref, *, out_scale):
    tq = qt_ref.shape[3]
    qi = pl.program_id(2)
    qt = qt_ref[0, 0]
    feat = lax.broadcasted_iota(I32, (LANES, 1), 0)
    zero = jnp.zeros_like(qt)
    qqt = jnp.concatenate([jnp.where(feat < HEAD_DIM, qt, zero),
                           jnp.where(feat >= HEAD_DIM, qt, zero)], axis=1)
    q_idx = qi * tq + lax.broadcasted_iota(I32, (1, tq), 1)
    k_sub = lax.broadcasted_iota(I32, (TK, 1), 0)

    def scores(j):
        start = pl.multiple_of(j * TK, TK)
        kt = k_ref[0, pl.ds(start, TK), :]
        col_max = []
        for half in range(2):
            lanes = slice(half * tq, (half + 1) * tq)
            s = jnp.dot(kt, qqt[:, lanes], preferred_element_type=F32)
            s_ref[:, lanes] = s
            col_max.append(jnp.max(s, axis=0, keepdims=True))
        return tuple(col_max)

    def softmax_pv(j, col_max, carry, masked):
        out = []
        for half, (m, l, acc) in enumerate(carry):
            s = s_ref[:, half * tq:(half + 1) * tq]
            if masked:
                s = jnp.where(j * TK + k_sub <= q_idx, s, MASK_VALUE)
                cm = jnp.max(s, axis=0, keepdims=True)
            else:
                cm = col_max[half]
            m_new = jnp.maximum(m, cm)
            alpha = jnp.exp2(m - m_new)
            p = jnp.exp2(s - m_new)
            l = alpha * l + jnp.sum(p, axis=0, keepdims=True)
            acc = alpha * acc + jnp.dot(vt_ref[0, 0, j], p.astype(BF16),
                                        preferred_element_type=F32)
            out.append((m_new, l, acc))
        return tuple(out)

    def body(j, sc):
        col_max, carry = sc
        carry = softmax_pv(j, col_max, carry, False)
        return scores(j + 1), carry

    one = (jnp.full((1, tq), MASK_VALUE, F32), jnp.zeros((1, tq), F32),
           jnp.zeros((LANES, tq), F32))
    _, carry = lax.fori_loop(0, qi, body, (scores(0), (one, one)))
    (_, l1, acc1), (_, l2, acc2) = softmax_pv(qi, None, carry, True)

    a = acc1 / l1 - lam_ref[0:1, 0:1] * (acc2 / l2)
    y = a * lax.rsqrt(jnp.mean(a * a, axis=0, keepdims=True) + EPS) * g_ref[...]
    o_ref[0] = (y * out_scale).T.astype(BF16)


def _attn(lam, qt, k, vt, subln_g_col, out_scale):
    b, s, w = k.shape
    return pl.pallas_call(
        functools.partial(_attn_kernel, out_scale=out_scale),
        grid=(b, N_HEADS, s // TQ),
        in_specs=[pl.BlockSpec((8, LANES), lambda bi, hi, qi: (0, 0)),
                  pl.BlockSpec((1, 1, LANES, TQ), lambda bi, hi, qi: (bi, hi, 0, qi)),
                  pl.BlockSpec((1, s, LANES), lambda bi, hi, qi: (bi, 0, hi)),
                  pl.BlockSpec((1, 1, s // TK, LANES, TK), lambda bi, hi, qi: (bi, hi, 0, 0, 0)),
                  pl.BlockSpec((LANES, 1), lambda bi, hi, qi: (0, 0))],
        out_specs=pl.BlockSpec((1, TQ, LANES), lambda bi, hi, qi: (bi, qi, hi)),
        out_shape=jax.ShapeDtypeStruct((b, s, w), BF16),
        scratch_shapes=[pltpu.VMEM((TK, 2 * TQ), F32)],
        compiler_params=_cparams(3),
        name="attn",
    )(lam, qt, k, vt, subln_g_col)


def _mixout_kernel(att_ref, sg_ref, mc_ref, x_ref, mod_ref, wua_ref, wo_ref, g_ref, wrt_ref,
                   br_ref, x1_ref, pos_ref, cnt_ref, hs_ref,
                   h2c_ref, carry_ref, posv_ref, cntv_ref, poss_ref, cnts_ref,
                   sem_s, sem_r, *, d, cap):
    tm = x_ref.shape[0]
    i = pl.program_id(0)
    last = pl.num_programs(0) - 1

    @pl.when(i == 0)
    def _():
        carry_ref[...] = jnp.zeros_like(carry_ref)

    mod = mod_ref[0]
    g_m = mod[:, 2 * d:3 * d]
    sh_f = mod[:, 3 * d:4 * d]
    sc_f = mod[:, 4 * d:5 * d]

    attp = jnp.dot(att_ref[...], wua_ref[...], preferred_element_type=F32)
    merged = sg_ref[...].astype(F32) * attp + mc_ref[...].astype(F32)
    mo = jnp.dot(merged.astype(BF16), wo_ref[...], preferred_element_type=F32)
    x1 = x_ref[...] + g_m * mo
    x1_ref[...] = x1
    ms = jnp.mean(x1 * x1, axis=-1, keepdims=True)
    h2 = (x1 * lax.rsqrt(ms + EPS) * g_ref[...]) * (1.0 + sc_f) + sh_f

    def split(a):
        hi = a.astype(BF16)
        return hi, (a - hi.astype(F32)).astype(BF16)

    def dot_t(a, b):
        return lax.dot_general(a, b, (((1,), (1,)), ((), ())), preferred_element_type=F32)

    w_hi, w_lo = split(wrt_ref[...])
    h_hi, h_lo = split(h2)
    lt = dot_t(w_hi, h_hi) + (dot_t(w_hi, h_lo) + dot_t(w_lo, h_hi)) + br_ref[...]
    ridx = lax.broadcasted_iota(I32, (8, 1), 0)
    n_e = N_GROUPS * EXPERTS_PER_GROUP
    is_grp = ridx < N_GROUPS
    gl = jnp.where(is_grp, lt[n_e:n_e + 8, :], MASK_VALUE)
    gmax = jnp.max(gl, axis=0, keepdims=True)
    gsum = jnp.sum(jnp.where(is_grp, jnp.exp(gl - gmax), 0.0), axis=0, keepdims=True)
    g_w = 1.0 / gsum
    gidx = jnp.min(jnp.where(gl == gmax, ridx, 8), axis=0, keepdims=True)
    e_in = jnp.zeros((8, tm), F32)
    for g in range(N_GROUPS):
        e_in = jnp.where(gidx == g, lt[g * 8:(g + 1) * 8, :], e_in)
    t1 = jnp.max(e_in, axis=0, keepdims=True)
    i1 = jnp.min(jnp.where(e_in == t1, ridx, 8), axis=0, keepdims=True)
    e_in2 = jnp.where(ridx == i1, MASK_VALUE, e_in)
    t2 = jnp.max(e_in2, axis=0, keepdims=True)
    i2 = jnp.min(jnp.where(e_in2 == t2, ridx, 8), axis=0, keepdims=True)
    e2 = jnp.exp(t2 - t1)
    den = 1.0 + e2
    w1 = (1.0 / den) * g_w
    w2 = (e2 / den) * g_w
    cw_t = jnp.where(ridx == i1, w1, jnp.where(ridx == i2, w2, 0.0))
    cw_col = jnp.concatenate([cw_t, jnp.zeros((LANES - 8, tm), F32)], axis=0).T

    onehot = jnp.where(ridx == gidx, 1.0, 0.0)
    tl = lax.broadcasted_iota(I32, (1, tm), 1)
    incl = onehot
    shift = 1
    while shift < tm:
        incl = incl + jnp.where(tl >= shift, pltpu.roll(incl, shift, 1), 0.0)
        shift *= 2
    run = carry_ref[...][:, 0:1]
    base = ridx.astype(F32) * float(cap)
    pos = jnp.sum(onehot * (base + run + incl - onehot), axis=0, keepdims=True).astype(I32)
    new_run = run + incl[:, tm - 1:tm]
    carry_ref[...] = jnp.broadcast_to(new_run, carry_ref.shape)
    cnt_i = jnp.broadcast_to(new_run, cnt_ref.shape).astype(I32)
    cnt_ref[...] = cnt_i
    pos_ref[0] = pos
    posv_ref[...] = pos
    cp = pltpu.make_async_copy(posv_ref.at[0], poss_ref, sem_s)
    cp.start()

    def row_copy(r, p):
        return pltpu.make_async_copy(h2c_ref.at[pl.ds(r, 1), :], hs_ref.at[pl.ds(p, 1), :], sem_r)

    def wait_tile():
        pltpu.make_async_copy(h2c_ref, hs_ref.at[pl.ds(0, tm), :], sem_r).wait()

    @pl.when(i > 0)
    def _():
        wait_tile()

    h2c_ref[:, 0:d] = h2
    h2c_ref[:, d:d + LANES] = cw_col
    cp.wait()

    def issue(r, c):
        row_copy(r, poss_ref[r]).start()
        return c

    lax.fori_loop(0, tm, issue, 0, unroll=8)

    @pl.when(i == last)
    def _():
        wait_tile()
        cntv_ref[...] = cnt_i
        cc = pltpu.make_async_copy(cntv_ref, cnts_ref, sem_s)
        cc.start()
        cc.wait()
        h2c_ref[0:8, :] = jnp.zeros((8, h2c_ref.shape[1]), F32)

        def zissue(r, c):
            for g in range(N_GROUPS):
                row_copy(0, g * cap + cnts_ref[g, 0] + r).start()
            return c

        lax.fori_loop(0, tm, zissue, 0, unroll=2)
        for g in range(N_GROUPS):
            wait_tile()


def _mixout(att, sg, mc, x, mod3, w_up_att, w_out, norm_g, wrt, br, tiles_per_batch, cap):
    t, d = x.shape
    tm = TM_MIX
    dc = d + LANES
    tok = lambda w: pl.BlockSpec((tm, w), lambda i: (i, 0))
    full = lambda a: pl.BlockSpec(a.shape, lambda i: (0,) * a.ndim)
    return pl.pallas_call(
        functools.partial(_mixout_kernel, d=d, cap=cap),
        grid=(t // tm,),
        in_specs=[tok(att.shape[1]), tok(d), tok(d), tok(d),
                  pl.BlockSpec((1, 1, mod3.shape[2]), lambda i: (i // tiles_per_batch, 0, 0)),
                  full(w_up_att), full(w_out), full(norm_g), full(wrt), full(br)],
        out_specs=[tok(d),
                   pl.BlockSpec((1, 1, tm), lambda i: (i, 0, 0)),
                   pl.BlockSpec((8, LANES), lambda i: (0, 0)),
                   pl.BlockSpec(memory_space=pl.ANY)],
        out_shape=[jax.ShapeDtypeStruct((t, d), F32),
                   jax.ShapeDtypeStruct((t // tm, 1, tm), I32),
                   jax.ShapeDtypeStruct((8, LANES), I32),
                   jax.ShapeDtypeStruct((N_GROUPS * cap, dc), F32)],
        scratch_shapes=[pltpu.VMEM((tm, dc), F32), pltpu.VMEM((8, LANES), F32),
                        pltpu.VMEM((1, tm), I32), pltpu.VMEM((8, LANES), I32),
                        pltpu.SMEM((tm,), I32), pltpu.SMEM((8, LANES), I32),
                        pltpu.SemaphoreType.DMA, pltpu.SemaphoreType.DMA],
        compiler_params=_cparams(1),
        name="mixout",
    )(att, sg, mc, x, mod3, w_up_att, w_out, norm_g, wrt, br)


def _ffn_tile(step, cnt_ref, tm):
    log_tm = tm.bit_length() - 1
    ends = []
    tot = jnp.int32(0)
    for g in range(N_GROUPS):
        tot = tot + lax.shift_right_logical(cnt_ref[g] + (tm - 1), log_tm)
        ends.append(tot)
    sc = jnp.maximum(jnp.minimum(step, tot - 1), 0)
    g = sum((sc >= e).astype(I32) for e in ends[:-1])
    start = jnp.where(g == 0, 0, jnp.where(g == 1, ends[0], jnp.where(g == 2, ends[1], ends[2])))
    return g, sc - start, step < tot


def _ffn_kernel(cnt_ref, hs_ref, wg_ref, wu_ref, wd_ref, y_ref, *, d):
    tm = hs_ref.shape[0]
    _, _, real = _ffn_tile(pl.program_id(0), cnt_ref, tm)

    @pl.when(real)
    def _():
        h = hs_ref[:, 0:d].astype(BF16)
        cw = hs_ref[:, d:d + LANES]
        acc = jnp.zeros((tm, d), F32)
        for e in range(EXPERTS_PER_GROUP):
            hg = jnp.dot(h, wg_ref[e], preferred_element_type=F32)
            hu = jnp.dot(h, wu_ref[e], preferred_element_type=F32)
            a = hg * jax.nn.sigmoid(hg) * hu * cw[:, e:e + 1]
            acc = acc + jnp.dot(a.astype(BF16), wd_ref[e], preferred_element_type=F32)
        y_ref[...] = acc


def _ffn(cnt, hs, w_gate, w_up, w_down, cap):
    rows, dc = hs.shape
    d = dc - LANES
    tm = TM_FFN
    blocks_per_group = cap // tm
    n_steps = (blocks_per_group - 1) + N_GROUPS

    def row_map(s, c):
        g, blk, _ = _ffn_tile(s, c, tm)
        return (g * blocks_per_group + blk, 0)

    def w_map(s, c):
        g, _, _ = _ffn_tile(s, c, tm)
        return (g, 0, 0)

    ff = w_gate.shape[2]
    gs = pltpu.PrefetchScalarGridSpec(
        num_scalar_prefetch=1,
        grid=(n_steps,),
        in_specs=[pl.BlockSpec((tm, dc), row_map),
                  pl.BlockSpec((EXPERTS_PER_GROUP, d, ff), w_map),
                  pl.BlockSpec((EXPERTS_PER_GROUP, d, ff), w_map),
                  pl.BlockSpec((EXPERTS_PER_GROUP, ff, d), w_map)],
        out_specs=pl.BlockSpec((tm, d), row_map),
    )
    return pl.pallas_call(
        functools.partial(_ffn_kernel, d=d),
        grid_spec=gs,
        out_shape=jax.ShapeDtypeStruct((rows, d), F32),
        compiler_params=_cparams(1),
        name="ffn",
    )(cnt, hs, w_gate, w_up, w_down)


def _final_kernel(pos_ref, ys_ref, x1_ref, mod_ref, g_ref, o_ref, ybuf_ref, sem, *, d, normalize):
    tm = x1_ref.shape[0]
    i = pl.program_id(0)

    def gather_tile(tile):
        slot = tile & 1

        def issue(r, c):
            pltpu.make_async_copy(ys_ref.at[pl.ds(pos_ref[tile * tm + r], 1), :],
                                  ybuf_ref.at[slot, pl.ds(r, 1), :], sem.at[slot]).start()
            return c

        lax.fori_loop(0, tm, issue, 0, unroll=8)

    @pl.when(i == 0)
    def _():
        gather_tile(i)

    @pl.when(i + 1 < pl.num_programs(0))
    def _():
        gather_tile(i + 1)

    slot = i & 1
    pltpu.make_async_copy(ys_ref.at[pl.ds(0, tm), :], ybuf_ref.at[slot], sem.at[slot]).wait()

    g_f = mod_ref[0][:, 5 * d:6 * d]
    x2 = x1_ref[...] + g_f * ybuf_ref[slot]
    if normalize:
        x2 = x2 * lax.rsqrt(jnp.mean(x2 * x2, axis=-1, keepdims=True) + EPS) * g_ref[...]
    o_ref[...] = x2


def _final(pos, ys, x1, mod3, final_g, tiles_per_batch, normalize):
    t, d = x1.shape
    tm = TM_FIN
    gs = pltpu.PrefetchScalarGridSpec(
        num_scalar_prefetch=1,
        grid=(t // tm,),
        in_specs=[pl.BlockSpec(memory_space=pl.ANY),
                  pl.BlockSpec((tm, d), lambda i, p: (i, 0)),
                  pl.BlockSpec((1, 1, mod3.shape[2]), lambda i, p: (i // tiles_per_batch, 0, 0)),
                  pl.BlockSpec((1, d), lambda i, p: (0, 0))],
        out_specs=pl.BlockSpec((tm, d), lambda i, p: (i, 0)),
        scratch_shapes=[pltpu.VMEM((2, tm, d), F32), pltpu.SemaphoreType.DMA((2,))],
    )
    return pl.pallas_call(
        functools.partial(_final_kernel, d=d, normalize=normalize),
        grid_spec=gs,
        out_shape=jax.ShapeDtypeStruct((t, d), F32),
        compiler_params=_cparams(1),
        name="final",
    )(pos, ys, x1, mod3, final_g)


def kernel(x, c, positions, w_ada, b_ada, norm_mix_g, w_in, lambda_q1, lambda_k1, lambda_q2,
           lambda_k2, subln_g, conv_w, w_up_att, w_up_conv, w_out, norm_ffn_g, w_group_router,
           b_group_router, w_expert_router, b_expert_router, w_gate, w_up, w_down, final_norm_g):
    b, s, d = x.shape
    depth = w_ada.shape[0]
    t = b * s
    assert s % TM_IN == 0 and s % TQ == 0 and TQ == TK and TM_IN % TK == 0
    assert s % TM_MIX == 0
    assert TM_MIX == TM_FIN and t % TM_FFN == 0
    cap = t + TM_FFN
    n_e = N_GROUPS * EXPERTS_PER_GROUP

    c8 = jnp.concatenate([c, jnp.zeros((8 - b, d), F32)], axis=0)
    pos3 = positions.reshape(b, s, 1)
    inv_freq = ROPE_THETA ** (-jnp.arange(0, HEAD_DIM, 2, dtype=F32) / HEAD_DIM)
    invf = jnp.tile(inv_freq, LANES // (HEAD_DIM // 2)).reshape(1, LANES)

    xf = x
    for l in range(depth):
        lambda_init = 0.8 - 0.6 * math.exp(-0.3 * l)
        row = lambda a: a[l].reshape(1, -1)
        mod8, lam = _ada(c8, w_ada[l], row(b_ada), row(lambda_q1), row(lambda_k1),
                         row(lambda_q2), row(lambda_k2), lambda_init)
        mod3 = mod8[0:b].reshape(b, 1, 6 * d)

        qt, k, vt, sg, mc = _inproj(xf.reshape(b, s, d), pos3, mod3, row(norm_mix_g), invf,
                                    w_in[l].astype(BF16), conv_w[l], w_up_conv[l].astype(BF16))
        att = _attn(lam, qt, k, vt, subln_g[l].reshape(-1, 1), 1.0 - lambda_init)

        pad = ROUTER_ROWS - n_e - N_GROUPS
        wrt = jnp.concatenate([w_expert_router[l].T, w_group_router[l].T,
                               jnp.zeros((pad, d), F32)], axis=0)
        br = jnp.concatenate([b_expert_router[l], b_group_router[l],
                              jnp.zeros((pad,), F32)]).reshape(ROUTER_ROWS, 1)
        x1, pos, cnt, hs = _mixout(att.reshape(t, -1), sg.reshape(t, d), mc.reshape(t, d),
                                   xf.reshape(t, d), mod3, w_up_att[l].astype(BF16),
                                   w_out[l].astype(BF16), row(norm_ffn_g), wrt, br,
                                   s // TM_MIX, cap)
        ys = _ffn(cnt[0:N_GROUPS, 0], hs, w_gate[l].astype(BF16), w_up[l].astype(BF16),
                  w_down[l].astype(BF16), cap)
        xf = _final(pos.reshape(t), ys, x1, mod3, final_norm_g.reshape(1, d), s // TM_FIN,
                    l == depth - 1)
    return xf.reshape(b, s, d)
```

```python
import functools
import math

import jax
import jax.numpy as jnp
from jax import lax
from jax.experimental import pallas as pl
from jax.experimental.pallas import tpu as pltpu

F32 = jnp.float32
BF16 = jnp.bfloat16
I32 = jnp.int32

EPS = 1e-6
MASK_VALUE = -1e30
ROPE_THETA = 10000.0

N_HEADS = 4
HEAD_DIM = 64
LANES = 128
N_GROUPS = 4
EXPERTS_PER_GROUP = 8
CONV_K = 3
ROUTER_ROWS = 48

VMEM_LIMIT = 56 * 1024 * 1024

TM_IN = 512
TQ = 512
TK = 512
TM_MIX = 512
TM_FFN = 512
TM_FIN = 512
TN_ADA = 1024


def _cparams(n_axes):
    return pltpu.CompilerParams(
        dimension_semantics=("arbitrary",) * n_axes, vmem_limit_bytes=VMEM_LIMIT)


def _ada_kernel(c_ref, w_ref, b_ref, lq1_ref, lk1_ref, lq2_ref, lk2_ref, mod_ref, lam_ref, *,
                lambda_init):
    c = c_ref[...]
    a = c * jax.nn.sigmoid(c)
    mod_ref[...] = jnp.dot(a, w_ref[...], preferred_element_type=F32,
                           precision=lax.Precision.HIGHEST) + b_ref[...]
    s1 = jnp.sum(lq1_ref[...] * lk1_ref[...], axis=-1, keepdims=True)
    s2 = jnp.sum(lq2_ref[...] * lk2_ref[...], axis=-1, keepdims=True)
    lam = jnp.exp(s1) - jnp.exp(s2) + lambda_init
    lam_ref[...] = jnp.broadcast_to(lam, lam_ref.shape)


def _ada(c8, w_ada, b_ada, lq1, lk1, lq2, lk2, lambda_init):
    d, n = w_ada.shape
    small = pl.BlockSpec((1, HEAD_DIM), lambda j: (0, 0))
    return pl.pallas_call(
        functools.partial(_ada_kernel, lambda_init=lambda_init),
        grid=(n // TN_ADA,),
        in_specs=[pl.BlockSpec((8, d), lambda j: (0, 0)),
                  pl.BlockSpec((d, TN_ADA), lambda j: (0, j)),
                  pl.BlockSpec((1, TN_ADA), lambda j: (0, j)),
                  small, small, small, small],
        out_specs=[pl.BlockSpec((8, TN_ADA), lambda j: (0, j)),
                   pl.BlockSpec((8, LANES), lambda j: (0, 0))],
        out_shape=[jax.ShapeDtypeStruct((8, n), F32), jax.ShapeDtypeStruct((8, LANES), F32)],
        compiler_params=_cparams(1),
        name="ada",
    )(c8, w_ada, b_ada, lq1, lk1, lq2, lk2)


def _inproj_kernel(x_ref, pos_ref, mod_ref, g_ref, invf_ref, w_ref, cw_ref, wupc_ref,
                   qt_ref, k_ref, vt_ref, sg_ref, mc_ref, carry_ref, *, d, qk_w, conv_w):
    tm = x_ref.shape[1]
    x = x_ref[0]
    ms = jnp.mean(x * x, axis=-1, keepdims=True)
    mod = mod_ref[0]
    sh = mod[:, 0:d]
    sc = mod[:, d:2 * d]
    h = (x * lax.rsqrt(ms + EPS) * g_ref[...]) * (1.0 + sc) + sh
    hb = h.astype(BF16)

    def proj(a, b):
        return jnp.dot(hb, w_ref[:, a:b], preferred_element_type=F32)

    lane = lax.broadcasted_iota(I32, (1, LANES), 1)
    first_half = (lane % HEAD_DIM) < (HEAD_DIM // 2)
    ang = pos_ref[0].astype(F32) * invf_ref[...]
    cos = jnp.cos(ang)
    sin = jnp.sin(ang)
    sin_signed = jnp.where(first_half, -sin, sin)

    def rope(t):
        partner = jnp.where(first_half, pltpu.roll(t, LANES - HEAD_DIM // 2, 1),
                            pltpu.roll(t, HEAD_DIM // 2, 1))
        return t * cos + partner * sin_signed

    zq = proj(0, qk_w)
    zk = proj(qk_w, 2 * qk_w)
    scale = HEAD_DIM ** -0.5 * math.log2(math.e)
    zv = proj(2 * qk_w, 3 * qk_w)
    for hd in range(qk_w // LANES):
        sl = slice(hd * LANES, (hd + 1) * LANES)
        qt_ref[0, hd] = (rope(zq[:, sl]) * scale).T.astype(BF16)
        k_ref[0, :, sl] = rope(zk[:, sl]).astype(BF16)
        for ck in range(tm // TK):
            vt_ref[0, hd, ck] = zv[ck * TK:(ck + 1) * TK, sl].T.astype(BF16)

    o = 3 * qk_w
    cb = proj(o, o + conv_w)
    u = proj(o + conv_w, o + 2 * conv_w) * proj(o + 2 * conv_w, o + 3 * conv_w)

    @pl.when(pl.program_id(1) == 0)
    def _():
        carry_ref[...] = jnp.zeros_like(carry_ref)

    prev = carry_ref[...]
    row = lax.broadcasted_iota(I32, (tm, 1), 0)
    u1 = jnp.where(row == 0, prev[7:8, :], pltpu.roll(u, 1, 0))
    u2 = jnp.where(row == 0, prev[6:7, :],
                   jnp.where(row == 1, prev[7:8, :], pltpu.roll(u, 2, 0)))
    carry_ref[...] = u[tm - 8:tm, :]
    cw = cw_ref[...]
    conv = cb * (cw[0:1, :] * u + cw[1:2, :] * u1 + cw[2:3, :] * u2)
    convp = jnp.dot(conv.astype(BF16), wupc_ref[...], preferred_element_type=F32)

    o = o + 3 * conv_w
    sg_ref[0] = jax.nn.sigmoid(proj(o, o + d)).astype(BF16)
    mc_ref[0] = (jax.nn.sigmoid(proj(o + d, o + 2 * d)) * convp).astype(BF16)


def _inproj(x, pos3, mod3, norm_g, invf, w_in, conv_w, w_up_conv):
    b, s, d = x.shape
    qk_w = N_HEADS * 2 * HEAD_DIM
    cw = conv_w.shape[1]
    tm = TM_IN
    tok = lambda w: pl.BlockSpec((1, tm, w), lambda bi, si: (bi, si, 0))
    full = lambda a: pl.BlockSpec(a.shape, lambda bi, si: (0,) * a.ndim)
    return pl.pallas_call(
        functools.partial(_inproj_kernel, d=d, qk_w=qk_w, conv_w=cw),
        grid=(b, s // tm),
        in_specs=[tok(d), tok(1),
                  pl.BlockSpec((1, 1, mod3.shape[2]), lambda bi, si: (bi, 0, 0)),
                  full(norm_g), full(invf), full(w_in), full(conv_w), full(w_up_conv)],
        out_specs=[pl.BlockSpec((1, N_HEADS, LANES, tm), lambda bi, si: (bi, 0, 0, si)),
                   tok(qk_w),
                   pl.BlockSpec((1, N_HEADS, tm // TK, LANES, TK),
                                lambda bi, si: (bi, 0, si, 0, 0)),
                   tok(d), tok(d)],
        out_shape=[jax.ShapeDtypeStruct((b, N_HEADS, LANES, s), BF16),
                   jax.ShapeDtypeStruct((b, s, qk_w), BF16),
                   jax.ShapeDtypeStruct((b, N_HEADS, s // TK, LANES, TK), BF16),
                   jax.ShapeDtypeStruct((b, s, d), BF16),
                   jax.ShapeDtypeStruct((b, s, d), BF16)],
        scratch_shapes=[pltpu.VMEM((8, cw), F32)],
        compiler_params=_cparams(2),
        name="inproj",
    )(x, pos3, mod3, norm_g, invf, w_in, conv_w, w_up_conv)


def _attn_kernel(lam_ref, qt_ref, k_ref, vt_ref, g_ref, o_ref, s_ref, *, out_scale):
    tq = qt_ref.shape[3]
    qi = pl.program_id(2)
    qt = qt_ref[0, 0]
    feat = lax.broadcasted_iota(I32, (LANES, 1), 0)
    zero = jnp.zeros_like(qt)
    qqt = jnp.concatenate([jnp.where(feat < HEAD_DIM, qt, zero),
                           jnp.where(feat >= HEAD_DIM, qt, zero)], axis=1)
    q_idx = qi * tq + lax.broadcasted_iota(I32, (1, tq), 1)
    k_sub = lax.broadcasted_iota(I32, (TK, 1), 0)

    def scores(j):
        start = pl.multiple_of(j * TK, TK)
        kt = k_ref[0, pl.ds(start, TK), :]
        col_max = []
        for half in range(2):
            lanes = slice(half * tq, (half + 1) * tq)
            s = jnp.dot(kt, qqt[:, lanes], preferred_element_type=F32)
            s_ref[:, lanes] = s
            col_max.append(jnp.max(s, axis=0, keepdims=True))
        return tuple(col_max)

    def softmax_pv(j, col_max, carry, masked):
        out = []
        for half, (m, l, acc) in enumerate(carry):
            s = s_ref[:, half * tq:(half + 1) * tq]
            if masked:
                s = jnp.where(j * TK + k_sub <= q_idx, s, MASK_VALUE)
                cm = jnp.max(s, axis=0, keepdims=True)
            else:
                cm = col_max[half]
            m_new = jnp.maximum(m, cm)
            alpha = jnp.exp2(m - m_new)
            p = jnp.exp2(s - m_new)
            l = alpha * l + jnp.sum(p, axis=0, keepdims=True)
            acc = alpha * acc + jnp.dot(vt_ref[0, 0, j], p.astype(BF16),
                                        preferred_element_type=F32)
            out.append((m_new, l, acc))
        return tuple(out)

    def body(j, sc):
        col_max, carry = sc
        carry = softmax_pv(j, col_max, carry, False)
        return scores(j + 1), carry

    one = (jnp.full((1, tq), MASK_VALUE, F32), jnp.zeros((1, tq), F32),
           jnp.zeros((LANES, tq), F32))
    _, carry = lax.fori_loop(0, qi, body, (scores(0), (one, one)))
    (_, l1, acc1), (_, l2, acc2) = softmax_pv(qi, None, carry, True)

    a = acc1 / l1 - lam_ref[0:1, 0:1] * (acc2 / l2)
    y = a * lax.rsqrt(jnp.mean(a * a, axis=0, keepdims=True) + EPS) * g_ref[...]
    o_ref[0] = (y * out_scale).T.astype(BF16)


def _attn(lam, qt, k, vt, subln_g_col, out_scale):
    b, s, w = k.shape
    return pl.pallas_call(
        functools.partial(_attn_kernel, out_scale=out_scale),
        grid=(b, N_HEADS, s // TQ),
        in_specs=[pl.BlockSpec((8, LANES), lambda bi, hi, qi: (0, 0)),
                  pl.BlockSpec((1, 1, LANES, TQ), lambda bi, hi, qi: (bi, hi, 0, qi)),
                  pl.BlockSpec((1, s, LANES), lambda bi, hi, qi: (bi, 0, hi)),
                  pl.BlockSpec((1, 1, s // TK, LANES, TK), lambda bi, hi, qi: (bi, hi, 0, 0, 0)),
                  pl.BlockSpec((LANES, 1), lambda bi, hi, qi: (0, 0))],
        out_specs=pl.BlockSpec((1, TQ, LANES), lambda bi, hi, qi: (bi, qi, hi)),
        out_shape=jax.ShapeDtypeStruct((b, s, w), BF16),
        scratch_shapes=[pltpu.VMEM((TK, 2 * TQ), F32)],
        compiler_params=_cparams(3),
        name="attn",
    )(lam, qt, k, vt, subln_g_col)


def _mixout_kernel(att_ref, sg_ref, mc_ref, x_ref, mod_ref, wua_ref, wo_ref, g_ref, wrt_ref,
                   br_ref, x1_ref, pos_ref, cnt_ref, hs_ref,
                   h2c_ref, carry_ref, posv_ref, cntv_ref, poss_ref, cnts_ref,
                   sem_s, sem_r, *, d, cap):
    tm = x_ref.shape[0]
    i = pl.program_id(0)
    last = pl.num_programs(0) - 1

    @pl.when(i == 0)
    def _():
        carry_ref[...] = jnp.zeros_like(carry_ref)

    mod = mod_ref[0]
    g_m = mod[:, 2 * d:3 * d]
    sh_f = mod[:, 3 * d:4 * d]
    sc_f = mod[:, 4 * d:5 * d]

    attp = jnp.dot(att_ref[...], wua_ref[...], preferred_element_type=F32)
    merged = sg_ref[...].astype(F32) * attp + mc_ref[...].astype(F32)
    mo = jnp.dot(merged.astype(BF16), wo_ref[...], preferred_element_type=F32)
    x1 = x_ref[...] + g_m * mo
    x1_ref[...] = x1
    ms = jnp.mean(x1 * x1, axis=-1, keepdims=True)
    h2 = (x1 * lax.rsqrt(ms + EPS) * g_ref[...]) * (1.0 + sc_f) + sh_f

    def split(a):
        hi = a.astype(BF16)
        return hi, (a - hi.astype(F32)).astype(BF16)

    def dot_t(a, b):
        return lax.dot_general(a, b, (((1,), (1,)), ((), ())), preferred_element_type=F32)

    w_hi, w_lo = split(wrt_ref[...])
    h_hi, h_lo = split(h2)
    lt = dot_t(w_hi, h_hi) + (dot_t(w_hi, h_lo) + dot_t(w_lo, h_hi)) + br_ref[...]
    ridx = lax.broadcasted_iota(I32, (8, 1), 0)
    n_e = N_GROUPS * EXPERTS_PER_GROUP
    is_grp = ridx < N_GROUPS
    gl = jnp.where(is_grp, lt[n_e:n_e + 8, :], MASK_VALUE)
    gmax = jnp.max(gl, axis=0, keepdims=True)
    gsum = jnp.sum(jnp.where(is_grp, jnp.exp(gl - gmax), 0.0), axis=0, keepdims=True)
    g_w = 1.0 / gsum
    gidx = jnp.min(jnp.where(gl == gmax, ridx, 8), axis=0, keepdims=True)
    e_in = jnp.zeros((8, tm), F32)
    for g in range(N_GROUPS):
        e_in = jnp.where(gidx == g, lt[g * 8:(g + 1) * 8, :], e_in)
    t1 = jnp.max(e_in, axis=0, keepdims=True)
    i1 = jnp.min(jnp.where(e_in == t1, ridx, 8), axis=0, keepdims=True)
    e_in2 = jnp.where(ridx == i1, MASK_VALUE, e_in)
    t2 = jnp.max(e_in2, axis=0, keepdims=True)
    i2 = jnp.min(jnp.where(e_in2 == t2, ridx, 8), axis=0, keepdims=True)
    e2 = jnp.exp(t2 - t1)
    den = 1.0 + e2
    w1 = (1.0 / den) * g_w
    w2 = (e2 / den) * g_w
    cw_t = jnp.where(ridx == i1, w1, jnp.where(ridx == i2, w2, 0.0))
    cw_col = jnp.concatenate([cw_t, jnp.zeros((LANES - 8, tm), F32)], axis=0).T

    onehot = jnp.where(ridx == gidx, 1.0, 0.0)
    tl = lax.broadcasted_iota(I32, (1, tm), 1)
    incl = onehot
    shift = 1
    while shift < tm:
        incl = incl + jnp.where(tl >= shift, pltpu.roll(incl, shift, 1), 0.0)
        shift *= 2
    run = carry_ref[...][:, 0:1]
    base = ridx.astype(F32) * float(cap)
    pos = jnp.sum(onehot * (base + run + incl - onehot), axis=0, keepdims=True).astype(I32)
    new_run = run + incl[:, tm - 1:tm]
    carry_ref[...] = jnp.broadcast_to(new_run, carry_ref.shape)
    cnt_i = jnp.broadcast_to(new_run, cnt_ref.shape).astype(I32)
    cnt_ref[...] = cnt_i
    pos_ref[0] = pos
    posv_ref[...] = pos
    cp = pltpu.make_async_copy(posv_ref.at[0], poss_ref, sem_s)
    cp.start()

    def row_copy(r, p):
        return pltpu.make_async_copy(h2c_ref.at[pl.ds(r, 1), :], hs_ref.at[pl.ds(p, 1), :], sem_r)

    def wait_tile():
        pltpu.make_async_copy(h2c_ref, hs_ref.at[pl.ds(0, tm), :], sem_r).wait()

    @pl.when(i > 0)
    def _():
        wait_tile()

    h2c_ref[:, 0:d] = h2
    h2c_ref[:, d:d + LANES] = cw_col
    cp.wait()

    for r in range(tm):
        row_copy(r, poss_ref[r]).start()

    @pl.when(i == last)
    def _():
        wait_tile()
        cntv_ref[...] = cnt_i
        cc = pltpu.make_async_copy(cntv_ref, cnts_ref, sem_s)
        cc.start()
        cc.wait()
        h2c_ref[0:8, :] = jnp.zeros((8, h2c_ref.shape[1]), F32)

        def zissue(r, c):
            for g in range(N_GROUPS):
                row_copy(0, g * cap + cnts_ref[g, 0] + r).start()
            return c

        lax.fori_loop(0, tm, zissue, 0, unroll=2)
        for g in range(N_GROUPS):
            wait_tile()


def _mixout(att, sg, mc, x, mod3, w_up_att, w_out, norm_g, wrt, br, tiles_per_batch, cap):
    t, d = x.shape
    tm = TM_MIX
    dc = d + LANES
    tok = lambda w: pl.BlockSpec((tm, w), lambda i: (i, 0))
    full = lambda a: pl.BlockSpec(a.shape, lambda i: (0,) * a.ndim)
    return pl.pallas_call(
        functools.partial(_mixout_kernel, d=d, cap=cap),
        grid=(t // tm,),
        in_specs=[tok(att.shape[1]), tok(d), tok(d), tok(d),
                  pl.BlockSpec((1, 1, mod3.shape[2]), lambda i: (i // tiles_per_batch, 0, 0)),
                  full(w_up_att), full(w_out), full(norm_g), full(wrt), full(br)],
        out_specs=[tok(d),
                   pl.BlockSpec((1, 1, tm), lambda i: (i, 0, 0)),
                   pl.BlockSpec((8, LANES), lambda i: (0, 0)),
                   pl.BlockSpec(memory_space=pl.ANY)],
        out_shape=[jax.ShapeDtypeStruct((t, d), F32),
                   jax.ShapeDtypeStruct((t // tm, 1, tm), I32),
                   jax.ShapeDtypeStruct((8, LANES), I32),
                   jax.ShapeDtypeStruct((N_GROUPS * cap, dc), F32)],
        scratch_shapes=[pltpu.VMEM((tm, dc), F32), pltpu.VMEM((8, LANES), F32),
                        pltpu.VMEM((1, tm), I32), pltpu.VMEM((8, LANES), I32),
                        pltpu.SMEM((tm,), I32), pltpu.SMEM((8, LANES), I32),
                        pltpu.SemaphoreType.DMA, pltpu.SemaphoreType.DMA],
        compiler_params=_cparams(1),
        name="mixout",
    )(att, sg, mc, x, mod3, w_up_att, w_out, norm_g, wrt, br)


def _ffn_tile(step, cnt_ref, tm):
    log_tm = tm.bit_length() - 1
    ends = []
    tot = jnp.int32(0)
    for g in range(N_GROUPS):
        tot = tot + lax.shift_right_logical(cnt_ref[g] + (tm - 1), log_tm)
        ends.append(tot)
    sc = jnp.maximum(jnp.minimum(step, tot - 1), 0)
    g = sum((sc >= e).astype(I32) for e in ends[:-1])
    start = jnp.where(g == 0, 0, jnp.where(g == 1, ends[0], jnp.where(g == 2, ends[1], ends[2])))
    return g, sc - start, step < tot


def _ffn_kernel(cnt_ref, hs_ref, wg_ref, wu_ref, wd_ref, y_ref, *, d):
    tm = hs_ref.shape[0]
    _, _, real = _ffn_tile(pl.program_id(0), cnt_ref, tm)

    @pl.when(real)
    def _():
        h = hs_ref[:, 0:d].astype(BF16)
        cw = hs_ref[:, d:d + LANES]
        acc = jnp.zeros((tm, d), F32)
        for e in range(EXPERTS_PER_GROUP):
            hg = jnp.dot(h, wg_ref[e], preferred_element_type=F32)
            hu = jnp.dot(h, wu_ref[e], preferred_element_type=F32)
            a = hg * jax.nn.sigmoid(hg) * hu * cw[:, e:e + 1]
            acc = acc + jnp.dot(a.astype(BF16), wd_ref[e], preferred_element_type=F32)
        y_ref[...] = acc


def _ffn(cnt, hs, w_gate, w_up, w_down, cap):
    rows, dc = hs.shape
    d = dc - LANES
    tm = TM_FFN
    blocks_per_group = cap // tm
    n_steps = (blocks_per_group - 1) + N_GROUPS

    def row_map(s, c):
        g, blk, _ = _ffn_tile(s, c, tm)
        return (g * blocks_per_group + blk, 0)

    def w_map(s, c):
        g, _, _ = _ffn_tile(s, c, tm)
        return (g, 0, 0)

    ff = w_gate.shape[2]
    gs = pltpu.PrefetchScalarGridSpec(
        num_scalar_prefetch=1,
        grid=(n_steps,),
        in_specs=[pl.BlockSpec((tm, dc), row_map),
                  pl.BlockSpec((EXPERTS_PER_GROUP, d, ff), w_map),
                  pl.BlockSpec((EXPERTS_PER_GROUP, d, ff), w_map),
                  pl.BlockSpec((EXPERTS_PER_GROUP, ff, d), w_map)],
        out_specs=pl.BlockSpec((tm, d), row_map),
    )
    return pl.pallas_call(
        functools.partial(_ffn_kernel, d=d),
        grid_spec=gs,
        out_shape=jax.ShapeDtypeStruct((rows, d), F32),
        compiler_params=_cparams(1),
        name="ffn",
    )(cnt, hs, w_gate, w_up, w_down)


def _final_kernel(pos_ref, ys_ref, x1_ref, mod_ref, g_ref, o_ref, ybuf_ref, sem, *, d, normalize):
    tm = x1_ref.shape[0]
    i = pl.program_id(0)

    def gather_tile(tile):
        slot = tile & 1
        base = tile * tm
        for r in range(tm):
            pltpu.make_async_copy(ys_ref.at[pl.ds(pos_ref[base + r], 1), :],
                                  ybuf_ref.at[slot, pl.ds(r, 1), :], sem.at[slot]).start()

    @pl.when(i == 0)
    def _():
        gather_tile(i)

    @pl.when(i + 1 < pl.num_programs(0))
    def _():
        gather_tile(i + 1)

    slot = i & 1
    pltpu.make_async_copy(ys_ref.at[pl.ds(0, tm), :], ybuf_ref.at[slot], sem.at[slot]).wait()

    g_f = mod_ref[0][:, 5 * d:6 * d]
    x2 = x1_ref[...] + g_f * ybuf_ref[slot]
    if normalize:
        x2 = x2 * lax.rsqrt(jnp.mean(x2 * x2, axis=-1, keepdims=True) + EPS) * g_ref[...]
    o_ref[...] = x2


def _final(pos, ys, x1, mod3, final_g, tiles_per_batch, normalize):
    t, d = x1.shape
    tm = TM_FIN
    gs = pltpu.PrefetchScalarGridSpec(
        num_scalar_prefetch=1,
        grid=(t // tm,),
        in_specs=[pl.BlockSpec(memory_space=pl.ANY),
                  pl.BlockSpec((tm, d), lambda i, p: (i, 0)),
                  pl.BlockSpec((1, 1, mod3.shape[2]), lambda i, p: (i // tiles_per_batch, 0, 0)),
                  pl.BlockSpec((1, d), lambda i, p: (0, 0))],
        out_specs=pl.BlockSpec((tm, d), lambda i, p: (i, 0)),
        scratch_shapes=[pltpu.VMEM((2, tm, d), F32), pltpu.SemaphoreType.DMA((2,))],
    )
    return pl.pallas_call(
        functools.partial(_final_kernel, d=d, normalize=normalize),
        grid_spec=gs,
        out_shape=jax.ShapeDtypeStruct((t, d), F32),
        compiler_params=_cparams(1),
        name="final",
    )(pos, ys, x1, mod3, final_g)


def kernel(x, c, positions, w_ada, b_ada, norm_mix_g, w_in, lambda_q1, lambda_k1, lambda_q2,
           lambda_k2, subln_g, conv_w, w_up_att, w_up_conv, w_out, norm_ffn_g, w_group_router,
           b_group_router, w_expert_router, b_expert_router, w_gate, w_up, w_down, final_norm_g):
    b, s, d = x.shape
    depth = w_ada.shape[0]
    t = b * s
    assert s % TM_IN == 0 and s % TQ == 0 and TQ == TK and TM_IN % TK == 0
    assert s % TM_MIX == 0
    assert TM_MIX == TM_FIN and t % TM_FFN == 0
    cap = t + TM_FFN
    n_e = N_GROUPS * EXPERTS_PER_GROUP

    c8 = jnp.concatenate([c, jnp.zeros((8 - b, d), F32)], axis=0)
    pos3 = positions.reshape(b, s, 1)
    inv_freq = ROPE_THETA ** (-jnp.arange(0, HEAD_DIM, 2, dtype=F32) / HEAD_DIM)
    invf = jnp.tile(inv_freq, LANES // (HEAD_DIM // 2)).reshape(1, LANES)

    xf = x
    for l in range(depth):
        lambda_init = 0.8 - 0.6 * math.exp(-0.3 * l)
        row = lambda a: a[l].reshape(1, -1)
        mod8, lam = _ada(c8, w_ada[l], row(b_ada), row(lambda_q1), row(lambda_k1),
                         row(lambda_q2), row(lambda_k2), lambda_init)
        mod3 = mod8[0:b].reshape(b, 1, 6 * d)

        qt, k, vt, sg, mc = _inproj(xf.reshape(b, s, d), pos3, mod3, row(norm_mix_g), invf,
                                    w_in[l].astype(BF16), conv_w[l], w_up_conv[l].astype(BF16))
        att = _attn(lam, qt, k, vt, subln_g[l].reshape(-1, 1), 1.0 - lambda_init)

        pad = ROUTER_ROWS - n_e - N_GROUPS
        wrt = jnp.concatenate([w_expert_router[l].T, w_group_router[l].T,
                               jnp.zeros((pad, d), F32)], axis=0)
        br = jnp.concatenate([b_expert_router[l], b_group_router[l],
                              jnp.zeros((pad,), F32)]).reshape(ROUTER_ROWS, 1)
        x1, pos, cnt, hs = _mixout(att.reshape(t, -1), sg.reshape(t, d), mc.reshape(t, d),
                                   xf.reshape(t, d), mod3, w_up_att[l].astype(BF16),
                                   w_out[l].astype(BF16), row(norm_ffn_g), wrt, br,
                                   s // TM_MIX, cap)
        ys = _ffn(cnt[0:N_GROUPS, 0], hs, w_gate[l].astype(BF16), w_up[l].astype(BF16),
                  w_down[l].astype(BF16), cap)
        xf = _final(pos.reshape(t), ys, x1, mod3, final_norm_g.reshape(1, d), s // TM_FIN,
                    l == depth - 1)
    return xf.reshape(b, s, d)
```

```python
import functools
import math

import jax
import jax.numpy as jnp
from jax import lax
from jax.experimental import pallas as pl
from jax.experimental.pallas import tpu as pltpu

F32 = jnp.float32
BF16 = jnp.bfloat16
I32 = jnp.int32

EPS = 1e-6
MASK_VALUE = -1e30
ROPE_THETA = 10000.0

N_HEADS = 4
HEAD_DIM = 64
LANES = 128
N_GROUPS = 4
EXPERTS_PER_GROUP = 8
CONV_K = 3
ROUTER_ROWS = 48

VMEM_LIMIT = 56 * 1024 * 1024

TM_IN = 512
TQ = 512
TK = 512
ATTN_HEADS_PER_STEP = 1
TM_MIX = 512
TM_FFN = 512
TM_FIN = 512
TN_ADA = 1024


def _cparams(n_axes):
    return pltpu.CompilerParams(
        dimension_semantics=("arbitrary",) * n_axes, vmem_limit_bytes=VMEM_LIMIT)


def _ada_kernel(c_ref, w_ref, b_ref, lq1_ref, lk1_ref, lq2_ref, lk2_ref, mod_ref, lam_ref, *,
                lambda_init):
    c = c_ref[...]
    a = c * jax.nn.sigmoid(c)
    mod_ref[...] = jnp.dot(a, w_ref[...], preferred_element_type=F32,
                           precision=lax.Precision.HIGHEST) + b_ref[...]
    s1 = jnp.sum(lq1_ref[...] * lk1_ref[...], axis=-1, keepdims=True)
    s2 = jnp.sum(lq2_ref[...] * lk2_ref[...], axis=-1, keepdims=True)
    lam = jnp.exp(s1) - jnp.exp(s2) + lambda_init
    lam_ref[...] = jnp.broadcast_to(lam, lam_ref.shape)


def _ada(c8, w_ada, b_ada, lq1, lk1, lq2, lk2, lambda_init):
    d, n = w_ada.shape
    small = pl.BlockSpec((1, HEAD_DIM), lambda j: (0, 0))
    return pl.pallas_call(
        functools.partial(_ada_kernel, lambda_init=lambda_init),
        grid=(n // TN_ADA,),
        in_specs=[pl.BlockSpec((8, d), lambda j: (0, 0)),
                  pl.BlockSpec((d, TN_ADA), lambda j: (0, j)),
                  pl.BlockSpec((1, TN_ADA), lambda j: (0, j)),
                  small, small, small, small],
        out_specs=[pl.BlockSpec((8, TN_ADA), lambda j: (0, j)),
                   pl.BlockSpec((8, LANES), lambda j: (0, 0))],
        out_shape=[jax.ShapeDtypeStruct((8, n), F32), jax.ShapeDtypeStruct((8, LANES), F32)],
        compiler_params=_cparams(1),
        name="ada",
    )(c8, w_ada, b_ada, lq1, lk1, lq2, lk2)


def _inproj_kernel(x_ref, pos_ref, mod_ref, g_ref, invf_ref, w_ref, cw_ref, wupc_ref,
                   qt_ref, k_ref, vt_ref, sg_ref, mc_ref, carry_ref, *, d, qk_w, conv_w):
    tm = x_ref.shape[1]
    x = x_ref[0]
    ms = jnp.mean(x * x, axis=-1, keepdims=True)
    mod = mod_ref[0]
    sh = mod[:, 0:d]
    sc = mod[:, d:2 * d]
    h = (x * lax.rsqrt(ms + EPS) * g_ref[...]) * (1.0 + sc) + sh
    hb = h.astype(BF16)

    def proj(a, b):
        return jnp.dot(hb, w_ref[:, a:b], preferred_element_type=F32)

    lane = lax.broadcasted_iota(I32, (1, LANES), 1)
    first_half = (lane % HEAD_DIM) < (HEAD_DIM // 2)
    ang = pos_ref[0].astype(F32) * invf_ref[...]
    cos = jnp.cos(ang)
    sin = jnp.sin(ang)
    sin_signed = jnp.where(first_half, -sin, sin)

    def rope(t):
        partner = jnp.where(first_half, pltpu.roll(t, LANES - HEAD_DIM // 2, 1),
                            pltpu.roll(t, HEAD_DIM // 2, 1))
        return t * cos + partner * sin_signed

    zq = proj(0, qk_w)
    zk = proj(qk_w, 2 * qk_w)
    scale = HEAD_DIM ** -0.5 * math.log2(math.e)
    zv = proj(2 * qk_w, 3 * qk_w)
    for hd in range(qk_w // LANES):
        sl = slice(hd * LANES, (hd + 1) * LANES)
        rq = rope(zq[:, sl]) * scale
        qt_ref[0, hd, 0] = rq.T.astype(BF16)
        k_ref[0, :, sl] = rope(zk[:, sl]).astype(BF16)
        for ck in range(tm // TK):
            vt_ref[0, hd, ck] = zv[ck * TK:(ck + 1) * TK, sl].T.astype(BF16)

    o = 3 * qk_w
    cb = proj(o, o + conv_w)
    u = proj(o + conv_w, o + 2 * conv_w) * proj(o + 2 * conv_w, o + 3 * conv_w)

    @pl.when(pl.program_id(1) == 0)
    def _():
        carry_ref[...] = jnp.zeros_like(carry_ref)

    prev = carry_ref[...]
    row = lax.broadcasted_iota(I32, (tm, 1), 0)
    u1 = jnp.where(row == 0, prev[7:8, :], pltpu.roll(u, 1, 0))
    u2 = jnp.where(row == 0, prev[6:7, :],
                   jnp.where(row == 1, prev[7:8, :], pltpu.roll(u, 2, 0)))
    carry_ref[...] = u[tm - 8:tm, :]
    cw = cw_ref[...]
    conv = cb * (cw[0:1, :] * u + cw[1:2, :] * u1 + cw[2:3, :] * u2)
    convp = jnp.dot(conv.astype(BF16), wupc_ref[...], preferred_element_type=F32)

    o = o + 3 * conv_w
    sg_ref[0] = jax.nn.sigmoid(proj(o, o + d)).astype(BF16)
    mc_ref[0] = (jax.nn.sigmoid(proj(o + d, o + 2 * d)) * convp).astype(BF16)


def _inproj(x, pos3, mod3, norm_g, invf, w_in, conv_w, w_up_conv):
    b, s, d = x.shape
    qk_w = N_HEADS * 2 * HEAD_DIM
    cw = conv_w.shape[1]
    tm = TM_IN
    tok = lambda w: pl.BlockSpec((1, tm, w), lambda bi, si: (bi, si, 0))
    full = lambda a: pl.BlockSpec(a.shape, lambda bi, si: (0,) * a.ndim)
    return pl.pallas_call(
        functools.partial(_inproj_kernel, d=d, qk_w=qk_w, conv_w=cw),
        grid=(b, s // tm),
        in_specs=[tok(d), tok(1),
                  pl.BlockSpec((1, 1, mod3.shape[2]), lambda bi, si: (bi, 0, 0)),
                  full(norm_g), full(invf), full(w_in), full(conv_w), full(w_up_conv)],
        out_specs=[pl.BlockSpec((1, N_HEADS, 1, LANES, tm),
                                lambda bi, si: (bi, 0, si // (TQ // tm), 0, si % (TQ // tm))),
                   tok(qk_w),
                   pl.BlockSpec((1, N_HEADS, tm // TK, LANES, TK),
                                lambda bi, si: (bi, 0, si, 0, 0)),
                   tok(d), tok(d)],
        out_shape=[jax.ShapeDtypeStruct((b, N_HEADS, s // TQ, LANES, TQ), BF16),
                   jax.ShapeDtypeStruct((b, s, qk_w), BF16),
                   jax.ShapeDtypeStruct((b, N_HEADS, s // TK, LANES, TK), BF16),
                   jax.ShapeDtypeStruct((b, s, d), BF16),
                   jax.ShapeDtypeStruct((b, s, d), BF16)],
        scratch_shapes=[pltpu.VMEM((8, cw), F32)],
        compiler_params=_cparams(2),
        name="inproj",
    )(x, pos3, mod3, norm_g, invf, w_in, conv_w, w_up_conv)


def _attn_kernel(lam_ref, qt_ref, k_ref, vt_ref, g_ref, o_ref, s_ref, p_ref, qq_ref, bias_ref, *,
                 out_scale):
    hps, n_q, tq = qt_ref.shape[1], qt_ref.shape[2], qt_ref.shape[4]
    feat = lax.broadcasted_iota(I32, (LANES, 1), 0)
    halves = [slice(h * tq, (h + 1) * tq) for h in range(2)]
    streams = [(h, half) for h in range(hps) for half in halves]

    k_sub = lax.broadcasted_iota(I32, (TK, 1), 0)
    q_lane = lax.broadcasted_iota(I32, (1, tq), 1)
    bias_ref[...] = jnp.where(k_sub <= q_lane, 0.0, MASK_VALUE)

    def load_queries(qi):
        for h in range(hps):
            qt = qt_ref[0, h, qi]
            zero = jnp.zeros_like(qt)
            qq_ref[h, :, halves[0]] = jnp.where(feat < HEAD_DIM, qt, zero)
            qq_ref[h, :, halves[1]] = jnp.where(feat >= HEAD_DIM, qt, zero)

    def scores(j):
        start = pl.multiple_of(j * TK, TK)
        col_max = []
        for h, lanes in streams:
            kt = k_ref[0, pl.ds(start, TK), h * LANES:(h + 1) * LANES]
            s = jnp.dot(kt, qq_ref[h, :, lanes], preferred_element_type=F32)
            s_ref[h, :, lanes] = s
            col_max.append(jnp.max(s, axis=0, keepdims=True))
        return tuple(col_max)

    def values(j, state):
        return tuple((m, l, alpha * acc + jnp.dot(vt_ref[0, h, j], p_ref[h, :, lanes],
                                                  preferred_element_type=F32), alpha)
                     for (h, lanes), (m, l, acc, alpha) in zip(streams, state))

    def softmax(col_max, state, diagonal):
        out = []
        for i, ((h, lanes), (m, l, acc, _)) in enumerate(zip(streams, state)):
            s = s_ref[h, :, lanes]
            if diagonal:
                s = s + bias_ref[...]
                cm = jnp.max(s, axis=0, keepdims=True)
            else:
                cm = col_max[i]
            m_new = jnp.maximum(m, cm)
            alpha = jnp.exp2(m - m_new)
            p = jnp.exp2(s - m_new)
            p_ref[h, :, lanes] = p.astype(BF16)
            out.append((m_new, alpha * l + jnp.sum(p, axis=0, keepdims=True), acc, alpha))
        return tuple(out)

    def finish(qi, last_tile, state):
        state = values(last_tile, state)
        for h in range(hps):
            (_, l1, acc1, _), (_, l2, acc2, _) = state[2 * h], state[2 * h + 1]
            a = acc1 / l1 - lam_ref[0:1, 0:1] * (acc2 / l2)
            y = a * lax.rsqrt(jnp.mean(a * a, axis=0, keepdims=True) + EPS) * g_ref[...]
            o_ref[0, pl.ds(pl.multiple_of(qi * tq, tq), tq), h * LANES:(h + 1) * LANES] = (
                (y * out_scale).T.astype(BF16))

    def next_diagonal(qi):
        nxt = jnp.minimum(qi + 1, n_q - 1)
        load_queries(nxt)
        scores(nxt)

    one = (jnp.full((1, tq), MASK_VALUE, F32), jnp.zeros((1, tq), F32),
           jnp.zeros((LANES, tq), F32), jnp.ones((1, tq), F32))
    init = tuple(one for _ in streams)

    load_queries(0)
    scores(0)
    state = softmax(None, init, True)
    next_diagonal(0)
    finish(0, 0, state)

    def query_tile(qi, carry):
        state = softmax(None, init, True)
        col_max = scores(0)

        def step(t, c):
            col_max, state = c
            state = values(jnp.where(t == 1, qi, t - 2), state)
            state = softmax(col_max, state, False)
            return scores(t), state

        col_max, state = lax.fori_loop(1, qi, step, (col_max, state))
        state = values(jnp.where(qi == 1, qi, qi - 2), state)
        state = softmax(col_max, state, False)
        next_diagonal(qi)
        finish(qi, qi - 1, state)
        return carry

    lax.fori_loop(1, n_q, query_tile, 0)


def _attn(lam, qt, k, vt, subln_g_col, out_scale):
    b, s, w = k.shape
    hps = ATTN_HEADS_PER_STEP
    return pl.pallas_call(
        functools.partial(_attn_kernel, out_scale=out_scale),
        grid=(b, N_HEADS // hps),
        in_specs=[pl.BlockSpec((8, LANES), lambda bi, hi: (0, 0)),
                  pl.BlockSpec((1, hps, s // TQ, LANES, TQ), lambda bi, hi: (bi, hi, 0, 0, 0)),
                  pl.BlockSpec((1, s, hps * LANES), lambda bi, hi: (bi, 0, hi)),
                  pl.BlockSpec((1, hps, s // TK, LANES, TK), lambda bi, hi: (bi, hi, 0, 0, 0)),
                  pl.BlockSpec((LANES, 1), lambda bi, hi: (0, 0))],
        out_specs=pl.BlockSpec((1, s, hps * LANES), lambda bi, hi: (bi, 0, hi)),
        out_shape=jax.ShapeDtypeStruct((b, s, w), BF16),
        scratch_shapes=[pltpu.VMEM((hps, TK, 2 * TQ), F32), pltpu.VMEM((hps, TK, 2 * TQ), BF16),
                        pltpu.VMEM((hps, LANES, 2 * TQ), BF16), pltpu.VMEM((TK, TQ), F32)],
        compiler_params=_cparams(2),
        name="attn",
    )(lam, qt, k, vt, subln_g_col)


def _mixout_kernel(att_ref, sg_ref, mc_ref, x_ref, mod_ref, wua_ref, wo_ref, g_ref, wrt_ref,
                   br_ref, x1_ref, pos_ref, cnt_ref, hs_ref,
                   h2c_ref, carry_ref, posv_ref, cntv_ref, poss_ref, cnts_ref,
                   sem_s, sem_r, *, d, cap):
    tm = x_ref.shape[0]
    i = pl.program_id(0)
    last = pl.num_programs(0) - 1

    @pl.when(i == 0)
    def _():
        carry_ref[...] = jnp.zeros_like(carry_ref)

    mod = mod_ref[0]
    g_m = mod[:, 2 * d:3 * d]
    sh_f = mod[:, 3 * d:4 * d]
    sc_f = mod[:, 4 * d:5 * d]

    attp = jnp.dot(att_ref[...], wua_ref[...], preferred_element_type=F32)
    merged = sg_ref[...].astype(F32) * attp + mc_ref[...].astype(F32)
    mo = jnp.dot(merged.astype(BF16), wo_ref[...], preferred_element_type=F32)
    x1 = x_ref[...] + g_m * mo
    x1_ref[...] = x1
    ms = jnp.mean(x1 * x1, axis=-1, keepdims=True)
    h2 = (x1 * lax.rsqrt(ms + EPS) * g_ref[...]) * (1.0 + sc_f) + sh_f

    def split(a):
        hi = a.astype(BF16)
        return hi, (a - hi.astype(F32)).astype(BF16)

    def dot_t(a, b):
        return lax.dot_general(a, b, (((1,), (1,)), ((), ())), preferred_element_type=F32)

    w_hi, w_lo = split(wrt_ref[...])
    h_hi, h_lo = split(h2)
    lt = dot_t(w_hi, h_hi) + (dot_t(w_hi, h_lo) + dot_t(w_lo, h_hi)) + br_ref[...]
    ridx = lax.broadcasted_iota(I32, (8, 1), 0)
    n_e = N_GROUPS * EXPERTS_PER_GROUP
    is_grp = ridx < N_GROUPS
    gl = jnp.where(is_grp, lt[n_e:n_e + 8, :], MASK_VALUE)
    gmax = jnp.max(gl, axis=0, keepdims=True)
    gsum = jnp.sum(jnp.where(is_grp, jnp.exp(gl - gmax), 0.0), axis=0, keepdims=True)
    g_w = 1.0 / gsum
    gidx = jnp.min(jnp.where(gl == gmax, ridx, 8), axis=0, keepdims=True)
    e_in = jnp.zeros((8, tm), F32)
    for g in range(N_GROUPS):
        e_in = jnp.where(gidx == g, lt[g * 8:(g + 1) * 8, :], e_in)
    t1 = jnp.max(e_in, axis=0, keepdims=True)
    i1 = jnp.min(jnp.where(e_in == t1, ridx, 8), axis=0, keepdims=True)
    e_in2 = jnp.where(ridx == i1, MASK_VALUE, e_in)
    t2 = jnp.max(e_in2, axis=0, keepdims=True)
    i2 = jnp.min(jnp.where(e_in2 == t2, ridx, 8), axis=0, keepdims=True)
    e2 = jnp.exp(t2 - t1)
    den = 1.0 + e2
    w1 = (1.0 / den) * g_w
    w2 = (e2 / den) * g_w
    cw_t = jnp.where(ridx == i1, w1, jnp.where(ridx == i2, w2, 0.0))
    cw_col = jnp.concatenate([cw_t, jnp.zeros((LANES - 8, tm), F32)], axis=0).T

    onehot = jnp.where(ridx == gidx, 1.0, 0.0)
    tl = lax.broadcasted_iota(I32, (1, tm), 1)
    incl = onehot
    shift = 1
    while shift < tm:
        incl = incl + jnp.where(tl >= shift, pltpu.roll(incl, shift, 1), 0.0)
        shift *= 2
    run = carry_ref[...][:, 0:1]
    base = ridx.astype(F32) * float(cap)
    pos = jnp.sum(onehot * (base + run + incl - onehot), axis=0, keepdims=True).astype(I32)
    new_run = run + incl[:, tm - 1:tm]
    carry_ref[...] = jnp.broadcast_to(new_run, carry_ref.shape)
    cnt_i = jnp.broadcast_to(new_run, cnt_ref.shape).astype(I32)
    cnt_ref[...] = cnt_i
    pos_ref[0] = pos
    posv_ref[...] = pos
    cp = pltpu.make_async_copy(posv_ref.at[0], poss_ref, sem_s)
    cp.start()

    def row_copy(r, p):
        return pltpu.make_async_copy(h2c_ref.at[pl.ds(r, 1), :], hs_ref.at[pl.ds(p, 1), :], sem_r)

    def wait_tile():
        pltpu.make_async_copy(h2c_ref, hs_ref.at[pl.ds(0, tm), :], sem_r).wait()

    @pl.when(i > 0)
    def _():
        wait_tile()

    h2c_ref[:, 0:d] = h2
    h2c_ref[:, d:d + LANES] = cw_col
    cp.wait()

    for r in range(tm):
        row_copy(r, poss_ref[r]).start()

    @pl.when(i == last)
    def _():
        wait_tile()
        cntv_ref[...] = cnt_i
        cc = pltpu.make_async_copy(cntv_ref, cnts_ref, sem_s)
        cc.start()
        cc.wait()
        h2c_ref[0:8, :] = jnp.zeros((8, h2c_ref.shape[1]), F32)

        def zissue(r, c):
            for g in range(N_GROUPS):
                row_copy(0, g * cap + cnts_ref[g, 0] + r).start()
            return c

        lax.fori_loop(0, tm, zissue, 0, unroll=2)
        for g in range(N_GROUPS):
            wait_tile()


def _mixout(att, sg, mc, x, mod3, w_up_att, w_out, norm_g, wrt, br, tiles_per_batch, cap):
    t, d = x.shape
    tm = TM_MIX
    dc = d + LANES
    tok = lambda w: pl.BlockSpec((tm, w), lambda i: (i, 0))
    full = lambda a: pl.BlockSpec(a.shape, lambda i: (0,) * a.ndim)
    return pl.pallas_call(
        functools.partial(_mixout_kernel, d=d, cap=cap),
        grid=(t // tm,),
        in_specs=[tok(att.shape[1]), tok(d), tok(d), tok(d),
                  pl.BlockSpec((1, 1, mod3.shape[2]), lambda i: (i // tiles_per_batch, 0, 0)),
                  full(w_up_att), full(w_out), full(norm_g), full(wrt), full(br)],
        out_specs=[tok(d),
                   pl.BlockSpec((1, 1, tm), lambda i: (i, 0, 0)),
                   pl.BlockSpec((8, LANES), lambda i: (0, 0)),
                   pl.BlockSpec(memory_space=pl.ANY)],
        out_shape=[jax.ShapeDtypeStruct((t, d), F32),
                   jax.ShapeDtypeStruct((t // tm, 1, tm), I32),
                   jax.ShapeDtypeStruct((8, LANES), I32),
                   jax.ShapeDtypeStruct((N_GROUPS * cap, dc), F32)],
        scratch_shapes=[pltpu.VMEM((tm, dc), F32), pltpu.VMEM((8, LANES), F32),
                        pltpu.VMEM((1, tm), I32), pltpu.VMEM((8, LANES), I32),
                        pltpu.SMEM((tm,), I32), pltpu.SMEM((8, LANES), I32),
                        pltpu.SemaphoreType.DMA, pltpu.SemaphoreType.DMA],
        compiler_params=_cparams(1),
        name="mixout",
    )(att, sg, mc, x, mod3, w_up_att, w_out, norm_g, wrt, br)


def _ffn_tile(step, cnt_ref, tm):
    log_tm = tm.bit_length() - 1
    ends = []
    tot = jnp.int32(0)
    for g in range(N_GROUPS):
        tot = tot + lax.shift_right_logical(cnt_ref[g] + (tm - 1), log_tm)
        ends.append(tot)
    sc = jnp.maximum(jnp.minimum(step, tot - 1), 0)
    g = sum((sc >= e).astype(I32) for e in ends[:-1])
    start = jnp.where(g == 0, 0, jnp.where(g == 1, ends[0], jnp.where(g == 2, ends[1], ends[2])))
    return g, sc - start, step < tot


def _ffn_kernel(cnt_ref, hs_ref, wg_ref, wu_ref, wd_ref, y_ref, *, d):
    tm = hs_ref.shape[0]
    _, _, real = _ffn_tile(pl.program_id(0), cnt_ref, tm)

    @pl.when(real)
    def _():
        h = hs_ref[:, 0:d].astype(BF16)
        cw = hs_ref[:, d:d + LANES]
        acc = jnp.zeros((tm, d), F32)
        for e in range(EXPERTS_PER_GROUP):
            hg = jnp.dot(h, wg_ref[e], preferred_element_type=F32)
            hu = jnp.dot(h, wu_ref[e], preferred_element_type=F32)
            a = hg * jax.nn.sigmoid(hg) * hu * cw[:, e:e + 1]
            acc = acc + jnp.dot(a.astype(BF16), wd_ref[e], preferred_element_type=F32)
        y_ref[...] = acc


def _ffn(cnt, hs, w_gate, w_up, w_down, cap):
    rows, dc = hs.shape
    d = dc - LANES
    tm = TM_FFN
    blocks_per_group = cap // tm
    n_steps = (blocks_per_group - 1) + N_GROUPS

    def row_map(s, c):
        g, blk, _ = _ffn_tile(s, c, tm)
        return (g * blocks_per_group + blk, 0)

    def w_map(s, c):
        g, _, _ = _ffn_tile(s, c, tm)
        return (g, 0, 0)

    ff = w_gate.shape[2]
    gs = pltpu.PrefetchScalarGridSpec(
        num_scalar_prefetch=1,
        grid=(n_steps,),
        in_specs=[pl.BlockSpec((tm, dc), row_map),
                  pl.BlockSpec((EXPERTS_PER_GROUP, d, ff), w_map),
                  pl.BlockSpec((EXPERTS_PER_GROUP, d, ff), w_map),
                  pl.BlockSpec((EXPERTS_PER_GROUP, ff, d), w_map)],
        out_specs=pl.BlockSpec((tm, d), row_map),
    )
    return pl.pallas_call(
        functools.partial(_ffn_kernel, d=d),
        grid_spec=gs,
        out_shape=jax.ShapeDtypeStruct((rows, d), F32),
        compiler_params=_cparams(1),
        name="ffn",
    )(cnt, hs, w_gate, w_up, w_down)


def _final_kernel(pos_ref, ys_ref, x1_ref, mod_ref, g_ref, o_ref, ybuf_ref, sem, *, d, normalize):
    tm = x1_ref.shape[0]
    i = pl.program_id(0)

    def gather_tile(tile):
        slot = tile & 1
        base = tile * tm
        for r in range(tm):
            pltpu.make_async_copy(ys_ref.at[pl.ds(pos_ref[base + r], 1), :],
                                  ybuf_ref.at[slot, pl.ds(r, 1), :], sem.at[slot]).start()

    @pl.when(i == 0)
    def _():
        gather_tile(i)

    @pl.when(i + 1 < pl.num_programs(0))
    def _():
        gather_tile(i + 1)

    slot = i & 1
    pltpu.make_async_copy(ys_ref.at[pl.ds(0, tm), :], ybuf_ref.at[slot], sem.at[slot]).wait()

    g_f = mod_ref[0][:, 5 * d:6 * d]
    x2 = x1_ref[...] + g_f * ybuf_ref[slot]
    if normalize:
        x2 = x2 * lax.rsqrt(jnp.mean(x2 * x2, axis=-1, keepdims=True) + EPS) * g_ref[...]
    o_ref[...] = x2


def _final(pos, ys, x1, mod3, final_g, tiles_per_batch, normalize):
    t, d = x1.shape
    tm = TM_FIN
    gs = pltpu.PrefetchScalarGridSpec(
        num_scalar_prefetch=1,
        grid=(t // tm,),
        in_specs=[pl.BlockSpec(memory_space=pl.ANY),
                  pl.BlockSpec((tm, d), lambda i, p: (i, 0)),
                  pl.BlockSpec((1, 1, mod3.shape[2]), lambda i, p: (i // tiles_per_batch, 0, 0)),
                  pl.BlockSpec((1, d), lambda i, p: (0, 0))],
        out_specs=pl.BlockSpec((tm, d), lambda i, p: (i, 0)),
        scratch_shapes=[pltpu.VMEM((2, tm, d), F32), pltpu.SemaphoreType.DMA((2,))],
    )
    return pl.pallas_call(
        functools.partial(_final_kernel, d=d, normalize=normalize),
        grid_spec=gs,
        out_shape=jax.ShapeDtypeStruct((t, d), F32),
        compiler_params=_cparams(1),
        name="final",
    )(pos, ys, x1, mod3, final_g)


def kernel(x, c, positions, w_ada, b_ada, norm_mix_g, w_in, lambda_q1, lambda_k1, lambda_q2,
           lambda_k2, subln_g, conv_w, w_up_att, w_up_conv, w_out, norm_ffn_g, w_group_router,
           b_group_router, w_expert_router, b_expert_router, w_gate, w_up, w_down, final_norm_g):
    b, s, d = x.shape
    depth = w_ada.shape[0]
    t = b * s
    assert s % TQ == 0 and TQ % TM_IN == 0 and TQ == TK and TM_IN % TK == 0
    assert N_HEADS % ATTN_HEADS_PER_STEP == 0
    assert s % TM_MIX == 0
    assert TM_MIX == TM_FIN and t % TM_FFN == 0
    cap = t + TM_FFN
    n_e = N_GROUPS * EXPERTS_PER_GROUP

    c8 = jnp.concatenate([c, jnp.zeros((8 - b, d), F32)], axis=0)
    pos3 = positions.reshape(b, s, 1)
    inv_freq = ROPE_THETA ** (-jnp.arange(0, HEAD_DIM, 2, dtype=F32) / HEAD_DIM)
    invf = jnp.tile(inv_freq, LANES // (HEAD_DIM // 2)).reshape(1, LANES)

    xf = x
    for l in range(depth):
        lambda_init = 0.8 - 0.6 * math.exp(-0.3 * l)
        row = lambda a: a[l].reshape(1, -1)
        mod8, lam = _ada(c8, w_ada[l], row(b_ada), row(lambda_q1), row(lambda_k1),
                         row(lambda_q2), row(lambda_k2), lambda_init)
        mod3 = mod8[0:b].reshape(b, 1, 6 * d)

        qt, k, vt, sg, mc = _inproj(xf.reshape(b, s, d), pos3, mod3, row(norm_mix_g), invf,
                                    w_in[l].astype(BF16), conv_w[l], w_up_conv[l].astype(BF16))
        att = _attn(lam, qt, k, vt, subln_g[l].reshape(-1, 1), 1.0 - lambda_init)

        pad = ROUTER_ROWS - n_e - N_GROUPS
        wrt = jnp.concatenate([w_expert_router[l].T, w_group_router[l].T,
                               jnp.zeros((pad, d), F32)], axis=0)
        br = jnp.concatenate([b_expert_router[l], b_group_router[l],
                              jnp.zeros((pad,), F32)]).reshape(ROUTER_ROWS, 1)
        x1, pos, cnt, hs = _mixout(att.reshape(t, -1), sg.reshape(t, d), mc.reshape(t, d),
                                   xf.reshape(t, d), mod3, w_up_att[l].astype(BF16),
                                   w_out[l].astype(BF16), row(norm_ffn_g), wrt, br,
                                   s // TM_MIX, cap)
        ys = _ffn(cnt[0:N_GROUPS, 0], hs, w_gate[l].astype(BF16), w_up[l].astype(BF16),
                  w_down[l].astype(BF16), cap)
        xf = _final(pos.reshape(t), ys, x1, mod3, final_norm_g.reshape(1, d), s // TM_FIN,
                    l == depth - 1)
    return xf.reshape(b, s, d)
```

```python
import functools
import math

import jax
import jax.numpy as jnp
from jax import lax
from jax.experimental import pallas as pl
from jax.experimental.pallas import tpu as pltpu

F32 = jnp.float32
BF16 = jnp.bfloat16
I32 = jnp.int32

EPS = 1e-6
MASK_VALUE = -1e30
ROPE_THETA = 10000.0

N_HEADS = 4
HEAD_DIM = 64
LANES = 128
N_GROUPS = 4
EXPERTS_PER_GROUP = 8
CONV_K = 3
ROUTER_ROWS = 48

VMEM_LIMIT = 56 * 1024 * 1024

TM_IN = 512
TQ = 512
TK = 512
ATTN_HEADS_PER_STEP = 1
TM_MIX = 512
TM_FFN = 512
TM_FIN = 512
TN_ADA = 1024


def _cparams(n_axes):
    return pltpu.CompilerParams(
        dimension_semantics=("arbitrary",) * n_axes, vmem_limit_bytes=VMEM_LIMIT)


def _ada_kernel(c_ref, w_ref, b_ref, lq1_ref, lk1_ref, lq2_ref, lk2_ref, mod_ref, lam_ref, *,
                lambda_init):
    c = c_ref[...]
    a = c * jax.nn.sigmoid(c)
    mod_ref[...] = jnp.dot(a.astype(BF16), w_ref[...].astype(BF16),
                           preferred_element_type=F32) + b_ref[...]
    s1 = jnp.sum(lq1_ref[...] * lk1_ref[...], axis=-1, keepdims=True)
    s2 = jnp.sum(lq2_ref[...] * lk2_ref[...], axis=-1, keepdims=True)
    lam = jnp.exp(s1) - jnp.exp(s2) + lambda_init
    lam_ref[...] = jnp.broadcast_to(lam, lam_ref.shape)


def _ada(c8, w_ada, b_ada, lq1, lk1, lq2, lk2, lambda_init):
    d, n = w_ada.shape
    small = pl.BlockSpec((1, HEAD_DIM), lambda j: (0, 0))
    return pl.pallas_call(
        functools.partial(_ada_kernel, lambda_init=lambda_init),
        grid=(n // TN_ADA,),
        in_specs=[pl.BlockSpec((8, d), lambda j: (0, 0)),
                  pl.BlockSpec((d, TN_ADA), lambda j: (0, j)),
                  pl.BlockSpec((1, TN_ADA), lambda j: (0, j)),
                  small, small, small, small],
        out_specs=[pl.BlockSpec((8, TN_ADA), lambda j: (0, j)),
                   pl.BlockSpec((8, LANES), lambda j: (0, 0))],
        out_shape=[jax.ShapeDtypeStruct((8, n), F32), jax.ShapeDtypeStruct((8, LANES), F32)],
        compiler_params=_cparams(1),
        name="ada",
    )(c8, w_ada, b_ada, lq1, lk1, lq2, lk2)


def _inproj_kernel(x_ref, pos_ref, mod_ref, g_ref, invf_ref, w_ref, cw_ref, wupc_ref,
                   qt_ref, k_ref, vt_ref, sg_ref, mc_ref, carry_ref, *, d, qk_w, conv_w):
    tm = x_ref.shape[1]
    x = x_ref[0]
    ms = jnp.mean(x * x, axis=-1, keepdims=True)
    mod = mod_ref[0]
    sh = mod[:, 0:d]
    sc = mod[:, d:2 * d]
    h = (x * lax.rsqrt(ms + EPS) * g_ref[...]) * (1.0 + sc) + sh
    hb = h.astype(BF16)

    def proj(a, b):
        return jnp.dot(hb, w_ref[:, a:b], preferred_element_type=F32)

    lane = lax.broadcasted_iota(I32, (1, LANES), 1)
    first_half = (lane % HEAD_DIM) < (HEAD_DIM // 2)
    ang = pos_ref[0].astype(F32) * invf_ref[...]
    cos = jnp.cos(ang)
    sin = jnp.sin(ang)
    sin_signed = jnp.where(first_half, -sin, sin)

    def rope(t):
        partner = jnp.where(first_half, pltpu.roll(t, LANES - HEAD_DIM // 2, 1),
                            pltpu.roll(t, HEAD_DIM // 2, 1))
        return t * cos + partner * sin_signed

    zq = proj(0, qk_w)
    zk = proj(qk_w, 2 * qk_w)
    scale = HEAD_DIM ** -0.5 * math.log2(math.e)
    zv = proj(2 * qk_w, 3 * qk_w)
    for hd in range(qk_w // LANES):
        sl = slice(hd * LANES, (hd + 1) * LANES)
        rq = rope(zq[:, sl]) * scale
        qt_ref[0, hd, 0] = rq.T.astype(BF16)
        k_ref[0, :, sl] = rope(zk[:, sl]).astype(BF16)
        for ck in range(tm // TK):
            vt_ref[0, hd, ck] = zv[ck * TK:(ck + 1) * TK, sl].T.astype(BF16)

    o = 3 * qk_w
    cb = proj(o, o + conv_w)
    u = proj(o + conv_w, o + 2 * conv_w) * proj(o + 2 * conv_w, o + 3 * conv_w)

    @pl.when(pl.program_id(1) == 0)
    def _():
        carry_ref[...] = jnp.zeros_like(carry_ref)

    prev = carry_ref[...]
    row = lax.broadcasted_iota(I32, (tm, 1), 0)
    u1 = jnp.where(row == 0, prev[7:8, :], pltpu.roll(u, 1, 0))
    u2 = jnp.where(row == 0, prev[6:7, :],
                   jnp.where(row == 1, prev[7:8, :], pltpu.roll(u, 2, 0)))
    carry_ref[...] = u[tm - 8:tm, :]
    cw = cw_ref[...]
    conv = cb * (cw[0:1, :] * u + cw[1:2, :] * u1 + cw[2:3, :] * u2)
    convp = jnp.dot(conv.astype(BF16), wupc_ref[...], preferred_element_type=F32)

    o = o + 3 * conv_w
    sg_ref[0] = jax.nn.sigmoid(proj(o, o + d)).astype(BF16)
    mc_ref[0] = (jax.nn.sigmoid(proj(o + d, o + 2 * d)) * convp).astype(BF16)


def _inproj(x, pos3, mod3, norm_g, invf, w_in, conv_w, w_up_conv):
    b, s, d = x.shape
    qk_w = N_HEADS * 2 * HEAD_DIM
    cw = conv_w.shape[1]
    tm = TM_IN
    tok = lambda w: pl.BlockSpec((1, tm, w), lambda bi, si: (bi, si, 0))
    full = lambda a: pl.BlockSpec(a.shape, lambda bi, si: (0,) * a.ndim)
    return pl.pallas_call(
        functools.partial(_inproj_kernel, d=d, qk_w=qk_w, conv_w=cw),
        grid=(b, s // tm),
        in_specs=[tok(d), tok(1),
                  pl.BlockSpec((1, 1, mod3.shape[2]), lambda bi, si: (bi, 0, 0)),
                  full(norm_g), full(invf), full(w_in), full(conv_w), full(w_up_conv)],
        out_specs=[pl.BlockSpec((1, N_HEADS, 1, LANES, tm),
                                lambda bi, si: (bi, 0, si // (TQ // tm), 0, si % (TQ // tm))),
                   tok(qk_w),
                   pl.BlockSpec((1, N_HEADS, tm // TK, LANES, TK),
                                lambda bi, si: (bi, 0, si, 0, 0)),
                   tok(d), tok(d)],
        out_shape=[jax.ShapeDtypeStruct((b, N_HEADS, s // TQ, LANES, TQ), BF16),
                   jax.ShapeDtypeStruct((b, s, qk_w), BF16),
                   jax.ShapeDtypeStruct((b, N_HEADS, s // TK, LANES, TK), BF16),
                   jax.ShapeDtypeStruct((b, s, d), BF16),
                   jax.ShapeDtypeStruct((b, s, d), BF16)],
        scratch_shapes=[pltpu.VMEM((8, cw), F32)],
        compiler_params=_cparams(2),
        name="inproj",
    )(x, pos3, mod3, norm_g, invf, w_in, conv_w, w_up_conv)


def _attn_kernel(lam_ref, qt_ref, k_ref, vt_ref, g_ref, o_ref, s_ref, p_ref, qq_ref, bias_ref, *,
                 out_scale):
    hps, n_q, tq = qt_ref.shape[1], qt_ref.shape[2], qt_ref.shape[4]
    feat = lax.broadcasted_iota(I32, (LANES, 1), 0)
    halves = [slice(h * tq, (h + 1) * tq) for h in range(2)]
    streams = [(h, half) for h in range(hps) for half in halves]

    k_sub = lax.broadcasted_iota(I32, (TK, 1), 0)
    q_lane = lax.broadcasted_iota(I32, (1, tq), 1)
    bias_ref[...] = jnp.where(k_sub <= q_lane, 0.0, MASK_VALUE)

    def load_queries(qi):
        for h in range(hps):
            qt = qt_ref[0, h, qi]
            zero = jnp.zeros_like(qt)
            qq_ref[h, :, halves[0]] = jnp.where(feat < HEAD_DIM, qt, zero)
            qq_ref[h, :, halves[1]] = jnp.where(feat >= HEAD_DIM, qt, zero)

    def scores(j):
        start = pl.multiple_of(j * TK, TK)
        col_max = []
        for h, lanes in streams:
            kt = k_ref[0, pl.ds(start, TK), h * LANES:(h + 1) * LANES]
            s = jnp.dot(kt, qq_ref[h, :, lanes], preferred_element_type=F32)
            s_ref[h, :, lanes] = s
            col_max.append(jnp.max(s, axis=0, keepdims=True))
        return tuple(col_max)

    def values(j, state):
        return tuple((m, l, alpha * acc + jnp.dot(vt_ref[0, h, j], p_ref[h, :, lanes],
                                                  preferred_element_type=F32), alpha)
                     for (h, lanes), (m, l, acc, alpha) in zip(streams, state))

    def softmax(col_max, state, diagonal):
        out = []
        for i, ((h, lanes), (m, l, acc, _)) in enumerate(zip(streams, state)):
            s = s_ref[h, :, lanes]
            if diagonal:
                s = s + bias_ref[...]
                cm = jnp.max(s, axis=0, keepdims=True)
            else:
                cm = col_max[i]
            m_new = jnp.maximum(m, cm)
            alpha = jnp.exp2(m - m_new)
            p = jnp.exp2(s - m_new)
            p_ref[h, :, lanes] = p.astype(BF16)
            out.append((m_new, alpha * l + jnp.sum(p, axis=0, keepdims=True), acc, alpha))
        return tuple(out)

    def finish(qi, last_tile, state):
        state = values(last_tile, state)
        row0 = qi * tq if isinstance(qi, int) else pl.multiple_of(qi * tq, tq)
        for h in range(hps):
            (_, l1, acc1, _), (_, l2, acc2, _) = state[2 * h], state[2 * h + 1]
            a = acc1 / l1 - lam_ref[0:1, 0:1] * (acc2 / l2)
            y = a * lax.rsqrt(jnp.mean(a * a, axis=0, keepdims=True) + EPS) * g_ref[...]
            o_ref[0, pl.ds(row0, tq), h * LANES:(h + 1) * LANES] = (
                (y * out_scale).T.astype(BF16))

    def next_diagonal(qi):
        nxt = jnp.minimum(qi + 1, n_q - 1)
        load_queries(nxt)
        scores(nxt)

    one = (jnp.full((1, tq), MASK_VALUE, F32), jnp.zeros((1, tq), F32),
           jnp.zeros((LANES, tq), F32), jnp.ones((1, tq), F32))
    init = tuple(one for _ in streams)

    load_queries(0)
    scores(0)
    pending = softmax(None, init, True)
    next_diagonal(0)

    def query_tile(qi, pending):
        finish(qi - 1, jnp.maximum(qi - 2, 0), pending)
        state = softmax(None, init, True)
        col_max = scores(0)

        def step(t, c):
            col_max, state = c
            state = values(jnp.where(t == 1, qi, t - 2), state)
            state = softmax(col_max, state, False)
            return scores(t), state

        col_max, state = lax.fori_loop(1, qi, step, (col_max, state))
        state = values(jnp.where(qi == 1, qi, qi - 2), state)
        state = softmax(col_max, state, False)
        next_diagonal(qi)
        return state

    pending = lax.fori_loop(1, n_q, query_tile, pending)
    finish(n_q - 1, max(n_q - 2, 0), pending)


def _attn(lam, qt, k, vt, subln_g_col, out_scale):
    b, s, w = k.shape
    hps = ATTN_HEADS_PER_STEP
    return pl.pallas_call(
        functools.partial(_attn_kernel, out_scale=out_scale),
        grid=(b, N_HEADS // hps),
        in_specs=[pl.BlockSpec((8, LANES), lambda bi, hi: (0, 0)),
                  pl.BlockSpec((1, hps, s // TQ, LANES, TQ), lambda bi, hi: (bi, hi, 0, 0, 0)),
                  pl.BlockSpec((1, s, hps * LANES), lambda bi, hi: (bi, 0, hi)),
                  pl.BlockSpec((1, hps, s // TK, LANES, TK), lambda bi, hi: (bi, hi, 0, 0, 0)),
                  pl.BlockSpec((LANES, 1), lambda bi, hi: (0, 0))],
        out_specs=pl.BlockSpec((1, s, hps * LANES), lambda bi, hi: (bi, 0, hi)),
        out_shape=jax.ShapeDtypeStruct((b, s, w), BF16),
        scratch_shapes=[pltpu.VMEM((hps, TK, 2 * TQ), F32), pltpu.VMEM((hps, TK, 2 * TQ), BF16),
                        pltpu.VMEM((hps, LANES, 2 * TQ), BF16), pltpu.VMEM((TK, TQ), F32)],
        compiler_params=_cparams(2),
        name="attn",
    )(lam, qt, k, vt, subln_g_col)


def _mixout_kernel(att_ref, sg_ref, mc_ref, x_ref, mod_ref, wua_ref, wo_ref, g_ref, wrt_ref,
                   br_ref, x1_ref, pos_ref, cnt_ref, hs_ref,
                   h2c_ref, carry_ref, posv_ref, cntv_ref, poss_ref, cnts_ref,
                   sem_s, sem_r, *, d, cap):
    tm = x_ref.shape[0]
    i = pl.program_id(0)
    last = pl.num_programs(0) - 1

    @pl.when(i == 0)
    def _():
        carry_ref[...] = jnp.zeros_like(carry_ref)

    mod = mod_ref[0]
    g_m = mod[:, 2 * d:3 * d]
    sh_f = mod[:, 3 * d:4 * d]
    sc_f = mod[:, 4 * d:5 * d]

    attp = jnp.dot(att_ref[...], wua_ref[...], preferred_element_type=F32)
    merged = sg_ref[...].astype(F32) * attp + mc_ref[...].astype(F32)
    mo = jnp.dot(merged.astype(BF16), wo_ref[...], preferred_element_type=F32)
    x1 = x_ref[...] + g_m * mo
    x1_ref[...] = x1
    ms = jnp.mean(x1 * x1, axis=-1, keepdims=True)
    h2 = (x1 * lax.rsqrt(ms + EPS) * g_ref[...]) * (1.0 + sc_f) + sh_f

    def split(a):
        hi = a.astype(BF16)
        return hi, (a - hi.astype(F32)).astype(BF16)

    def dot_t(a, b):
        return lax.dot_general(a, b, (((1,), (1,)), ((), ())), preferred_element_type=F32)

    w_hi, w_lo = split(wrt_ref[...])
    h_hi, h_lo = split(h2)
    by_hi = dot_t(jnp.concatenate([w_hi, w_lo], axis=0), h_hi)
    lt = (by_hi[0:ROUTER_ROWS] + (dot_t(w_hi, h_lo) + by_hi[ROUTER_ROWS:2 * ROUTER_ROWS])
          + br_ref[...])
    ridx = lax.broadcasted_iota(I32, (8, 1), 0)
    n_e = N_GROUPS * EXPERTS_PER_GROUP
    is_grp = ridx < N_GROUPS
    gl = jnp.where(is_grp, lt[n_e:n_e + 8, :], MASK_VALUE)
    gmax = jnp.max(gl, axis=0, keepdims=True)
    gsum = jnp.sum(jnp.where(is_grp, jnp.exp(gl - gmax), 0.0), axis=0, keepdims=True)
    g_w = 1.0 / gsum
    gidx = jnp.min(jnp.where(gl == gmax, ridx, 8), axis=0, keepdims=True)
    e_in = jnp.zeros((8, tm), F32)
    for g in range(N_GROUPS):
        e_in = jnp.where(gidx == g, lt[g * 8:(g + 1) * 8, :], e_in)
    t1 = jnp.max(e_in, axis=0, keepdims=True)
    i1 = jnp.min(jnp.where(e_in == t1, ridx, 8), axis=0, keepdims=True)
    e_in2 = jnp.where(ridx == i1, MASK_VALUE, e_in)
    t2 = jnp.max(e_in2, axis=0, keepdims=True)
    i2 = jnp.min(jnp.where(e_in2 == t2, ridx, 8), axis=0, keepdims=True)
    e2 = jnp.exp(t2 - t1)
    den = 1.0 + e2
    w1 = (1.0 / den) * g_w
    w2 = (e2 / den) * g_w
    cw_t = jnp.where(ridx == i1, w1, jnp.where(ridx == i2, w2, 0.0))
    cw_col = jnp.concatenate([cw_t, jnp.zeros((LANES - 8, tm), F32)], axis=0).T

    onehot = jnp.where(ridx == gidx, 1.0, 0.0)
    tl = lax.broadcasted_iota(I32, (1, tm), 1)
    incl = onehot
    shift = 1
    while shift < tm:
        incl = incl + jnp.where(tl >= shift, pltpu.roll(incl, shift, 1), 0.0)
        shift *= 2
    run = carry_ref[...][:, 0:1]
    base = ridx.astype(F32) * float(cap)
    pos = jnp.sum(onehot * (base + run + incl - onehot), axis=0, keepdims=True).astype(I32)
    new_run = run + incl[:, tm - 1:tm]
    carry_ref[...] = jnp.broadcast_to(new_run, carry_ref.shape)
    cnt_i = jnp.broadcast_to(new_run, cnt_ref.shape).astype(I32)
    cnt_ref[...] = cnt_i
    pos_ref[0] = pos
    posv_ref[...] = pos
    cp = pltpu.make_async_copy(posv_ref.at[0], poss_ref, sem_s)
    cp.start()

    def row_copy(r, p):
        return pltpu.make_async_copy(h2c_ref.at[pl.ds(r, 1), :], hs_ref.at[pl.ds(p, 1), :], sem_r)

    def wait_tile():
        pltpu.make_async_copy(h2c_ref, hs_ref.at[pl.ds(0, tm), :], sem_r).wait()

    @pl.when(i > 0)
    def _():
        wait_tile()

    h2c_ref[:, 0:d] = h2
    h2c_ref[:, d:d + LANES] = cw_col
    cp.wait()

    for r in range(tm):
        row_copy(r, poss_ref[r]).start()

    @pl.when(i == last)
    def _():
        wait_tile()
        cntv_ref[...] = cnt_i
        cc = pltpu.make_async_copy(cntv_ref, cnts_ref, sem_s)
        cc.start()
        cc.wait()
        h2c_ref[0:8, :] = jnp.zeros((8, h2c_ref.shape[1]), F32)

        def zissue(r, c):
            for g in range(N_GROUPS):
                row_copy(0, g * cap + cnts_ref[g, 0] + r).start()
            return c

        lax.fori_loop(0, tm, zissue, 0, unroll=2)
        for g in range(N_GROUPS):
            wait_tile()


def _mixout(att, sg, mc, x, mod3, w_up_att, w_out, norm_g, wrt, br, tiles_per_batch, cap):
    t, d = x.shape
    tm = TM_MIX
    dc = d + LANES
    tok = lambda w: pl.BlockSpec((tm, w), lambda i: (i, 0))
    full = lambda a: pl.BlockSpec(a.shape, lambda i: (0,) * a.ndim)
    return pl.pallas_call(
        functools.partial(_mixout_kernel, d=d, cap=cap),
        grid=(t // tm,),
        in_specs=[tok(att.shape[1]), tok(d), tok(d), tok(d),
                  pl.BlockSpec((1, 1, mod3.shape[2]), lambda i: (i // tiles_per_batch, 0, 0)),
                  full(w_up_att), full(w_out), full(norm_g), full(wrt), full(br)],
        out_specs=[tok(d),
                   pl.BlockSpec((1, 1, tm), lambda i: (i, 0, 0)),
                   pl.BlockSpec((8, LANES), lambda i: (0, 0)),
                   pl.BlockSpec(memory_space=pl.ANY)],
        out_shape=[jax.ShapeDtypeStruct((t, d), F32),
                   jax.ShapeDtypeStruct((t // tm, 1, tm), I32),
                   jax.ShapeDtypeStruct((8, LANES), I32),
                   jax.ShapeDtypeStruct((N_GROUPS * cap, dc), F32)],
        scratch_shapes=[pltpu.VMEM((tm, dc), F32), pltpu.VMEM((8, LANES), F32),
                        pltpu.VMEM((1, tm), I32), pltpu.VMEM((8, LANES), I32),
                        pltpu.SMEM((tm,), I32), pltpu.SMEM((8, LANES), I32),
                        pltpu.SemaphoreType.DMA, pltpu.SemaphoreType.DMA],
        compiler_params=_cparams(1),
        name="mixout",
    )(att, sg, mc, x, mod3, w_up_att, w_out, norm_g, wrt, br)


def _ffn_tile(step, cnt_ref, tm):
    log_tm = tm.bit_length() - 1
    ends = []
    tot = jnp.int32(0)
    for g in range(N_GROUPS):
        tot = tot + lax.shift_right_logical(cnt_ref[g] + (tm - 1), log_tm)
        ends.append(tot)
    sc = jnp.maximum(jnp.minimum(step, tot - 1), 0)
    g = sum((sc >= e).astype(I32) for e in ends[:-1])
    start = jnp.where(g == 0, 0, jnp.where(g == 1, ends[0], jnp.where(g == 2, ends[1], ends[2])))
    return g, sc - start, step < tot


def _ffn_kernel(cnt_ref, hs_ref, wg_ref, wu_ref, wd_ref, y_ref, *, d):
    tm = hs_ref.shape[0]
    _, _, real = _ffn_tile(pl.program_id(0), cnt_ref, tm)

    @pl.when(real)
    def _():
        h = hs_ref[:, 0:d].astype(BF16)
        cw = hs_ref[:, d:d + LANES]
        acc = jnp.zeros((tm, d), F32)
        for e in range(EXPERTS_PER_GROUP):
            hg = jnp.dot(h, wg_ref[e], preferred_element_type=F32)
            hu = jnp.dot(h, wu_ref[e], preferred_element_type=F32)
            a = hg * jax.nn.sigmoid(hg) * hu * cw[:, e:e + 1]
            acc = acc + jnp.dot(a.astype(BF16), wd_ref[e], preferred_element_type=F32)
        y_ref[...] = acc


def _ffn(cnt, hs, w_gate, w_up, w_down, cap):
    rows, dc = hs.shape
    d = dc - LANES
    tm = TM_FFN
    blocks_per_group = cap // tm
    n_steps = (blocks_per_group - 1) + N_GROUPS

    def row_map(s, c):
        g, blk, _ = _ffn_tile(s, c, tm)
        return (g * blocks_per_group + blk, 0)

    def w_map(s, c):
        g, _, _ = _ffn_tile(s, c, tm)
        return (g, 0, 0)

    ff = w_gate.shape[2]
    gs = pltpu.PrefetchScalarGridSpec(
        num_scalar_prefetch=1,
        grid=(n_steps,),
        in_specs=[pl.BlockSpec((tm, dc), row_map),
                  pl.BlockSpec((EXPERTS_PER_GROUP, d, ff), w_map),
                  pl.BlockSpec((EXPERTS_PER_GROUP, d, ff), w_map),
                  pl.BlockSpec((EXPERTS_PER_GROUP, ff, d), w_map)],
        out_specs=pl.BlockSpec((tm, d), row_map),
    )
    return pl.pallas_call(
        functools.partial(_ffn_kernel, d=d),
        grid_spec=gs,
        out_shape=jax.ShapeDtypeStruct((rows, d), F32),
        compiler_params=_cparams(1),
        name="ffn",
    )(cnt, hs, w_gate, w_up, w_down)


def _final_kernel(pos_ref, ys_ref, x1_ref, mod_ref, g_ref, o_ref, ybuf_ref, sem, *, d, normalize):
    tm = x1_ref.shape[0]
    i = pl.program_id(0)

    def gather_tile(tile):
        slot = tile & 1
        base = tile * tm
        for r in range(tm):
            pltpu.make_async_copy(ys_ref.at[pl.ds(pos_ref[base + r], 1), :],
                                  ybuf_ref.at[slot, pl.ds(r, 1), :], sem.at[slot]).start()

    @pl.when(i == 0)
    def _():
        gather_tile(i)

    @pl.when(i + 1 < pl.num_programs(0))
    def _():
        gather_tile(i + 1)

    slot = i & 1
    pltpu.make_async_copy(ys_ref.at[pl.ds(0, tm), :], ybuf_ref.at[slot], sem.at[slot]).wait()

    g_f = mod_ref[0][:, 5 * d:6 * d]
    x2 = x1_ref[...] + g_f * ybuf_ref[slot]
    if normalize:
        x2 = x2 * lax.rsqrt(jnp.mean(x2 * x2, axis=-1, keepdims=True) + EPS) * g_ref[...]
    o_ref[...] = x2


def _final(pos, ys, x1, mod3, final_g, tiles_per_batch, normalize):
    t, d = x1.shape
    tm = TM_FIN
    gs = pltpu.PrefetchScalarGridSpec(
        num_scalar_prefetch=1,
        grid=(t // tm,),
        in_specs=[pl.BlockSpec(memory_space=pl.ANY),
                  pl.BlockSpec((tm, d), lambda i, p: (i, 0)),
                  pl.BlockSpec((1, 1, mod3.shape[2]), lambda i, p: (i // tiles_per_batch, 0, 0)),
                  pl.BlockSpec((1, d), lambda i, p: (0, 0))],
        out_specs=pl.BlockSpec((tm, d), lambda i, p: (i, 0)),
        scratch_shapes=[pltpu.VMEM((2, tm, d), F32), pltpu.SemaphoreType.DMA((2,))],
    )
    return pl.pallas_call(
        functools.partial(_final_kernel, d=d, normalize=normalize),
        grid_spec=gs,
        out_shape=jax.ShapeDtypeStruct((t, d), F32),
        compiler_params=_cparams(1),
        name="final",
    )(pos, ys, x1, mod3, final_g)


def kernel(x, c, positions, w_ada, b_ada, norm_mix_g, w_in, lambda_q1, lambda_k1, lambda_q2,
           lambda_k2, subln_g, conv_w, w_up_att, w_up_conv, w_out, norm_ffn_g, w_group_router,
           b_group_router, w_expert_router, b_expert_router, w_gate, w_up, w_down, final_norm_g):
    b, s, d = x.shape
    depth = w_ada.shape[0]
    t = b * s
    assert s % TQ == 0 and TQ % TM_IN == 0 and TQ == TK and TM_IN % TK == 0
    assert N_HEADS % ATTN_HEADS_PER_STEP == 0
    assert s % TM_MIX == 0
    assert TM_MIX == TM_FIN and t % TM_FFN == 0
    cap = t + TM_FFN
    n_e = N_GROUPS * EXPERTS_PER_GROUP

    c8 = jnp.concatenate([c, jnp.zeros((8 - b, d), F32)], axis=0)
    pos3 = positions.reshape(b, s, 1)
    inv_freq = ROPE_THETA ** (-jnp.arange(0, HEAD_DIM, 2, dtype=F32) / HEAD_DIM)
    invf = jnp.tile(inv_freq, LANES // (HEAD_DIM // 2)).reshape(1, LANES)

    xf = x
    for l in range(depth):
        lambda_init = 0.8 - 0.6 * math.exp(-0.3 * l)
        row = lambda a: a[l].reshape(1, -1)
        mod8, lam = _ada(c8, w_ada[l], row(b_ada), row(lambda_q1), row(lambda_k1),
                         row(lambda_q2), row(lambda_k2), lambda_init)
        mod3 = mod8[0:b].reshape(b, 1, 6 * d)

        qt, k, vt, sg, mc = _inproj(xf.reshape(b, s, d), pos3, mod3, row(norm_mix_g), invf,
                                    w_in[l].astype(BF16), conv_w[l], w_up_conv[l].astype(BF16))
        att = _attn(lam, qt, k, vt, subln_g[l].reshape(-1, 1), 1.0 - lambda_init)

        pad = ROUTER_ROWS - n_e - N_GROUPS
        wrt = jnp.concatenate([w_expert_router[l].T, w_group_router[l].T,
                               jnp.zeros((pad, d), F32)], axis=0)
        br = jnp.concatenate([b_expert_router[l], b_group_router[l],
                              jnp.zeros((pad,), F32)]).reshape(ROUTER_ROWS, 1)
        x1, pos, cnt, hs = _mixout(att.reshape(t, -1), sg.reshape(t, d), mc.reshape(t, d),
                                   xf.reshape(t, d), mod3, w_up_att[l].astype(BF16),
                                   w_out[l].astype(BF16), row(norm_ffn_g), wrt, br,
                                   s // TM_MIX, cap)
        ys = _ffn(cnt[0:N_GROUPS, 0], hs, w_gate[l].astype(BF16), w_up[l].astype(BF16),
                  w_down[l].astype(BF16), cap)
        xf = _final(pos.reshape(t), ys, x1, mod3, final_norm_g.reshape(1, d), s // TM_FIN,
                    l == depth - 1)
    return xf.reshape(b, s, d)
```

```python
import functools
import math

import jax
import jax.numpy as jnp
from jax import lax
from jax.experimental import pallas as pl
from jax.experimental.pallas import tpu as pltpu

F32 = jnp.float32
BF16 = jnp.bfloat16
I32 = jnp.int32

EPS = 1e-6
MASK_VALUE = -1e30
ROPE_THETA = 10000.0

N_HEADS = 4
HEAD_DIM = 64
LANES = 128
N_GROUPS = 4
EXPERTS_PER_GROUP = 8
CONV_K = 3
ROUTER_ROWS = 48

VMEM_LIMIT = 56 * 1024 * 1024

TM_IN = 1024
TQ = 512
TK = 512
ATTN_HEADS_PER_STEP = 1
TM_MIX = 512
TM_FFN = 512
TM_FIN = 512
TN_ADA = 1024


def _cparams(n_axes):
    return pltpu.CompilerParams(
        dimension_semantics=("arbitrary",) * n_axes, vmem_limit_bytes=VMEM_LIMIT)


def _ada_kernel(c_ref, w_ref, b_ref, lq1_ref, lk1_ref, lq2_ref, lk2_ref, mod_ref, lam_ref, *,
                lambda_init):
    c = c_ref[...]
    a = c * jax.nn.sigmoid(c)
    mod_ref[...] = jnp.dot(a.astype(BF16), w_ref[...].astype(BF16),
                           preferred_element_type=F32) + b_ref[...]
    s1 = jnp.sum(lq1_ref[...] * lk1_ref[...], axis=-1, keepdims=True)
    s2 = jnp.sum(lq2_ref[...] * lk2_ref[...], axis=-1, keepdims=True)
    lam = jnp.exp(s1) - jnp.exp(s2) + lambda_init
    lam_ref[...] = jnp.broadcast_to(lam, lam_ref.shape)


def _ada(c8, w_ada, b_ada, lq1, lk1, lq2, lk2, lambda_init):
    d, n = w_ada.shape
    small = pl.BlockSpec((1, HEAD_DIM), lambda j: (0, 0))
    return pl.pallas_call(
        functools.partial(_ada_kernel, lambda_init=lambda_init),
        grid=(n // TN_ADA,),
        in_specs=[pl.BlockSpec((8, d), lambda j: (0, 0)),
                  pl.BlockSpec((d, TN_ADA), lambda j: (0, j)),
                  pl.BlockSpec((1, TN_ADA), lambda j: (0, j)),
                  small, small, small, small],
        out_specs=[pl.BlockSpec((8, TN_ADA), lambda j: (0, j)),
                   pl.BlockSpec((8, LANES), lambda j: (0, 0))],
        out_shape=[jax.ShapeDtypeStruct((8, n), F32), jax.ShapeDtypeStruct((8, LANES), F32)],
        compiler_params=_cparams(1),
        name="ada",
    )(c8, w_ada, b_ada, lq1, lk1, lq2, lk2)


def _inproj_kernel(x_ref, pos_ref, mod_ref, g_ref, invf_ref, w_ref, cw_ref, wupc_ref,
                   qt_ref, k_ref, vt_ref, sg_ref, mc_ref, carry_ref, *, d, qk_w, conv_w):
    tm = x_ref.shape[1]
    x = x_ref[0]
    ms = jnp.mean(x * x, axis=-1, keepdims=True)
    mod = mod_ref[0]
    sh = mod[:, 0:d]
    sc = mod[:, d:2 * d]
    h = (x * lax.rsqrt(ms + EPS) * g_ref[...]) * (1.0 + sc) + sh
    hb = h.astype(BF16)

    def proj(a, b):
        return jnp.dot(hb, w_ref[:, a:b], preferred_element_type=F32)

    lane = lax.broadcasted_iota(I32, (1, LANES), 1)
    first_half = (lane % HEAD_DIM) < (HEAD_DIM // 2)
    ang = pos_ref[0].astype(F32) * invf_ref[...]
    cos = jnp.cos(ang)
    sin = jnp.sin(ang)
    sin_signed = jnp.where(first_half, -sin, sin)

    def rope(t):
        partner = jnp.where(first_half, pltpu.roll(t, LANES - HEAD_DIM // 2, 1),
                            pltpu.roll(t, HEAD_DIM // 2, 1))
        return t * cos + partner * sin_signed

    zq = proj(0, qk_w)
    zk = proj(qk_w, 2 * qk_w)
    scale = HEAD_DIM ** -0.5 * math.log2(math.e)
    zv = proj(2 * qk_w, 3 * qk_w)
    for hd in range(qk_w // LANES):
        sl = slice(hd * LANES, (hd + 1) * LANES)
        rq = rope(zq[:, sl]) * scale
        qw = qt_ref.shape[4]
        for ck in range(tm // qw):
            qt_ref[0, hd, ck] = rq[ck * qw:(ck + 1) * qw, :].T.astype(BF16)
        k_ref[0, :, sl] = rope(zk[:, sl]).astype(BF16)
        for ck in range(tm // TK):
            vt_ref[0, hd, ck] = zv[ck * TK:(ck + 1) * TK, sl].T.astype(BF16)

    o = 3 * qk_w
    cb = proj(o, o + conv_w)
    u = proj(o + conv_w, o + 2 * conv_w) * proj(o + 2 * conv_w, o + 3 * conv_w)

    @pl.when(pl.program_id(1) == 0)
    def _():
        carry_ref[...] = jnp.zeros_like(carry_ref)

    prev = carry_ref[...]
    row = lax.broadcasted_iota(I32, (tm, 1), 0)
    u1 = jnp.where(row == 0, prev[7:8, :], pltpu.roll(u, 1, 0))
    u2 = jnp.where(row == 0, prev[6:7, :],
                   jnp.where(row == 1, prev[7:8, :], pltpu.roll(u, 2, 0)))
    carry_ref[...] = u[tm - 8:tm, :]
    cw = cw_ref[...]
    conv = cb * (cw[0:1, :] * u + cw[1:2, :] * u1 + cw[2:3, :] * u2)
    convp = jnp.dot(conv.astype(BF16), wupc_ref[...], preferred_element_type=F32)

    o = o + 3 * conv_w
    sg_ref[0] = jax.nn.sigmoid(proj(o, o + d)).astype(BF16)
    mc_ref[0] = (jax.nn.sigmoid(proj(o + d, o + 2 * d)) * convp).astype(BF16)


def _inproj(x, pos3, mod3, norm_g, invf, w_in, conv_w, w_up_conv):
    b, s, d = x.shape
    qk_w = N_HEADS * 2 * HEAD_DIM
    cw = conv_w.shape[1]
    tm = TM_IN
    if tm >= TQ:
        qt_spec = pl.BlockSpec((1, N_HEADS, tm // TQ, LANES, TQ), lambda bi, si: (bi, 0, si, 0, 0))
    else:
        qt_spec = pl.BlockSpec((1, N_HEADS, 1, LANES, tm),
                               lambda bi, si: (bi, 0, si // (TQ // tm), 0, si % (TQ // tm)))
    tok = lambda w: pl.BlockSpec((1, tm, w), lambda bi, si: (bi, si, 0))
    full = lambda a: pl.BlockSpec(a.shape, lambda bi, si: (0,) * a.ndim)
    return pl.pallas_call(
        functools.partial(_inproj_kernel, d=d, qk_w=qk_w, conv_w=cw),
        grid=(b, s // tm),
        in_specs=[tok(d), tok(1),
                  pl.BlockSpec((1, 1, mod3.shape[2]), lambda bi, si: (bi, 0, 0)),
                  full(norm_g), full(invf), full(w_in), full(conv_w), full(w_up_conv)],
        out_specs=[qt_spec,
                   tok(qk_w),
                   pl.BlockSpec((1, N_HEADS, tm // TK, LANES, TK),
                                lambda bi, si: (bi, 0, si, 0, 0)),
                   tok(d), tok(d)],
        out_shape=[jax.ShapeDtypeStruct((b, N_HEADS, s // TQ, LANES, TQ), BF16),
                   jax.ShapeDtypeStruct((b, s, qk_w), BF16),
                   jax.ShapeDtypeStruct((b, N_HEADS, s // TK, LANES, TK), BF16),
                   jax.ShapeDtypeStruct((b, s, d), BF16),
                   jax.ShapeDtypeStruct((b, s, d), BF16)],
        scratch_shapes=[pltpu.VMEM((8, cw), F32)],
        compiler_params=_cparams(2),
        name="inproj",
    )(x, pos3, mod3, norm_g, invf, w_in, conv_w, w_up_conv)


def _attn_kernel(lam_ref, qt_ref, k_ref, vt_ref, g_ref, o_ref, s_ref, p_ref, qq_ref, bias_ref, *,
                 out_scale):
    hps, n_q, tq = qt_ref.shape[1], qt_ref.shape[2], qt_ref.shape[4]
    feat = lax.broadcasted_iota(I32, (LANES, 1), 0)
    halves = [slice(h * tq, (h + 1) * tq) for h in range(2)]
    streams = [(h, half) for h in range(hps) for half in halves]

    k_sub = lax.broadcasted_iota(I32, (TK, 1), 0)
    q_lane = lax.broadcasted_iota(I32, (1, tq), 1)
    bias_ref[...] = jnp.where(k_sub <= q_lane, 0.0, MASK_VALUE)

    def load_queries(qi):
        for h in range(hps):
            qt = qt_ref[0, h, qi]
            zero = jnp.zeros_like(qt)
            qq_ref[h, :, halves[0]] = jnp.where(feat < HEAD_DIM, qt, zero)
            qq_ref[h, :, halves[1]] = jnp.where(feat >= HEAD_DIM, qt, zero)

    def scores(j):
        start = j * TK
        col_max = []
        for h, lanes in streams:
            kt = k_ref[0, pl.ds(start, TK), h * LANES:(h + 1) * LANES]
            s = jnp.dot(kt, qq_ref[h, :, lanes], preferred_element_type=F32)
            s_ref[h, :, lanes] = s
            col_max.append(jnp.max(s, axis=0, keepdims=True))
        return tuple(col_max)

    def values(j, state):
        return tuple((m, l, alpha * acc + jnp.dot(vt_ref[0, h, j], p_ref[h, :, lanes],
                                                  preferred_element_type=F32), alpha)
                     for (h, lanes), (m, l, acc, alpha) in zip(streams, state))

    def softmax(col_max, state, diagonal):
        out = []
        for i, ((h, lanes), (m, l, acc, _)) in enumerate(zip(streams, state)):
            s = s_ref[h, :, lanes]
            if diagonal:
                s = s + bias_ref[...]
                cm = jnp.max(s, axis=0, keepdims=True)
            else:
                cm = col_max[i]
            m_new = jnp.maximum(m, cm)
            alpha = jnp.exp2(m - m_new)
            p = jnp.exp2(s - m_new)
            p_ref[h, :, lanes] = p.astype(BF16)
            out.append((m_new, alpha * l + jnp.sum(p, axis=0, keepdims=True), acc, alpha))
        return tuple(out)

    def finish(qi, last_tile, state):
        state = values(last_tile, state)
        row0 = qi * tq
        for h in range(hps):
            (_, l1, acc1, _), (_, l2, acc2, _) = state[2 * h], state[2 * h + 1]
            a = acc1 / l1 - lam_ref[0:1, 0:1] * (acc2 / l2)
            y = a * lax.rsqrt(jnp.mean(a * a, axis=0, keepdims=True) + EPS) * g_ref[...]
            o_ref[0, pl.ds(row0, tq), h * LANES:(h + 1) * LANES] = (
                (y * out_scale).T.astype(BF16))

    def next_diagonal(qi):
        if qi + 1 < n_q:
            load_queries(qi + 1)
            scores(qi + 1)

    one = (jnp.full((1, tq), MASK_VALUE, F32), jnp.zeros((1, tq), F32),
           jnp.zeros((LANES, tq), F32), jnp.ones((1, tq), F32))
    init = tuple(one for _ in streams)

    load_queries(0)
    scores(0)
    pending = softmax(None, init, True)
    next_diagonal(0)

    for qi in range(1, n_q):
        finish(qi - 1, max(qi - 2, 0), pending)
        state = softmax(None, init, True)
        col_max = scores(0)
        for t in range(1, qi + 1):
            state = values(qi if t == 1 else t - 2, state)
            state = softmax(col_max, state, False)
            if t < qi:
                col_max = scores(t)
            else:
                next_diagonal(qi)
        pending = state
    finish(n_q - 1, max(n_q - 2, 0), pending)


def _attn(lam, qt, k, vt, subln_g_col, out_scale):
    b, s, w = k.shape
    hps = ATTN_HEADS_PER_STEP
    return pl.pallas_call(
        functools.partial(_attn_kernel, out_scale=out_scale),
        grid=(b, N_HEADS // hps),
        in_specs=[pl.BlockSpec((8, LANES), lambda bi, hi: (0, 0)),
                  pl.BlockSpec((1, hps, s // TQ, LANES, TQ), lambda bi, hi: (bi, hi, 0, 0, 0)),
                  pl.BlockSpec((1, s, hps * LANES), lambda bi, hi: (bi, 0, hi)),
                  pl.BlockSpec((1, hps, s // TK, LANES, TK), lambda bi, hi: (bi, hi, 0, 0, 0)),
                  pl.BlockSpec((LANES, 1), lambda bi, hi: (0, 0))],
        out_specs=pl.BlockSpec((1, s, hps * LANES), lambda bi, hi: (bi, 0, hi)),
        out_shape=jax.ShapeDtypeStruct((b, s, w), BF16),
        scratch_shapes=[pltpu.VMEM((hps, TK, 2 * TQ), F32), pltpu.VMEM((hps, TK, 2 * TQ), BF16),
                        pltpu.VMEM((hps, LANES, 2 * TQ), BF16), pltpu.VMEM((TK, TQ), F32)],
        compiler_params=_cparams(2),
        name="attn",
    )(lam, qt, k, vt, subln_g_col)


def _mixout_kernel(att_ref, sg_ref, mc_ref, x_ref, mod_ref, wua_ref, wo_ref, g_ref, wrt_ref,
                   br_ref, x1_ref, pos_ref, cnt_ref, hs_ref,
                   h2c_ref, carry_ref, posv_ref, cntv_ref, poss_ref, cnts_ref,
                   sem_s, sem_r, *, d, cap):
    tm = x_ref.shape[0]
    i = pl.program_id(0)
    last = pl.num_programs(0) - 1

    @pl.when(i == 0)
    def _():
        carry_ref[...] = jnp.zeros_like(carry_ref)

    mod = mod_ref[0]
    g_m = mod[:, 2 * d:3 * d]
    sh_f = mod[:, 3 * d:4 * d]
    sc_f = mod[:, 4 * d:5 * d]

    attp = jnp.dot(att_ref[...], wua_ref[...], preferred_element_type=F32)
    merged = sg_ref[...].astype(F32) * attp + mc_ref[...].astype(F32)
    mo = jnp.dot(merged.astype(BF16), wo_ref[...], preferred_element_type=F32)
    x1 = x_ref[...] + g_m * mo
    x1_ref[...] = x1
    ms = jnp.mean(x1 * x1, axis=-1, keepdims=True)
    h2 = (x1 * lax.rsqrt(ms + EPS) * g_ref[...]) * (1.0 + sc_f) + sh_f

    def split(a):
        hi = a.astype(BF16)
        return hi, (a - hi.astype(F32)).astype(BF16)

    def dot_t(a, b):
        return lax.dot_general(a, b, (((1,), (1,)), ((), ())), preferred_element_type=F32)

    w_hi, w_lo = split(wrt_ref[...])
    h_hi, h_lo = split(h2)
    by_hi = dot_t(jnp.concatenate([w_hi, w_lo], axis=0), h_hi)
    lt = (by_hi[0:ROUTER_ROWS] + (dot_t(w_hi, h_lo) + by_hi[ROUTER_ROWS:2 * ROUTER_ROWS])
          + br_ref[...])
    ridx = lax.broadcasted_iota(I32, (8, 1), 0)
    n_e = N_GROUPS * EXPERTS_PER_GROUP
    is_grp = ridx < N_GROUPS
    gl = jnp.where(is_grp, lt[n_e:n_e + 8, :], MASK_VALUE)
    gmax = jnp.max(gl, axis=0, keepdims=True)
    gsum = jnp.sum(jnp.where(is_grp, jnp.exp(gl - gmax), 0.0), axis=0, keepdims=True)
    g_w = 1.0 / gsum
    gidx = jnp.min(jnp.where(gl == gmax, ridx, 8), axis=0, keepdims=True)
    e_in = jnp.zeros((8, tm), F32)
    for g in range(N_GROUPS):
        e_in = jnp.where(gidx == g, lt[g * 8:(g + 1) * 8, :], e_in)
    t1 = jnp.max(e_in, axis=0, keepdims=True)
    i1 = jnp.min(jnp.where(e_in == t1, ridx, 8), axis=0, keepdims=True)
    e_in2 = jnp.where(ridx == i1, MASK_VALUE, e_in)
    t2 = jnp.max(e_in2, axis=0, keepdims=True)
    i2 = jnp.min(jnp.where(e_in2 == t2, ridx, 8), axis=0, keepdims=True)
    e2 = jnp.exp(t2 - t1)
    den = 1.0 + e2
    w1 = (1.0 / den) * g_w
    w2 = (e2 / den) * g_w
    cw_t = jnp.where(ridx == i1, w1, jnp.where(ridx == i2, w2, 0.0))
    cw_col = jnp.concatenate([cw_t, jnp.zeros((LANES - 8, tm), F32)], axis=0).T

    onehot = jnp.where(ridx == gidx, 1.0, 0.0)
    tl = lax.broadcasted_iota(I32, (1, tm), 1)
    incl = onehot
    shift = 1
    while shift < tm:
        incl = incl + jnp.where(tl >= shift, pltpu.roll(incl, shift, 1), 0.0)
        shift *= 2
    run = carry_ref[...][:, 0:1]
    base = ridx.astype(F32) * float(cap)
    pos = jnp.sum(onehot * (base + run + incl - onehot), axis=0, keepdims=True).astype(I32)
    new_run = run + incl[:, tm - 1:tm]
    carry_ref[...] = jnp.broadcast_to(new_run, carry_ref.shape)
    cnt_i = jnp.broadcast_to(new_run, cnt_ref.shape).astype(I32)
    cnt_ref[...] = cnt_i
    pos_ref[0] = pos
    posv_ref[...] = pos
    cp = pltpu.make_async_copy(posv_ref.at[0], poss_ref, sem_s)
    cp.start()

    def row_copy(r, p):
        return pltpu.make_async_copy(h2c_ref.at[pl.ds(r, 1), :], hs_ref.at[pl.ds(p, 1), :], sem_r)

    def wait_tile():
        pltpu.make_async_copy(h2c_ref, hs_ref.at[pl.ds(0, tm), :], sem_r).wait()

    @pl.when(i > 0)
    def _():
        wait_tile()

    h2c_ref[:, 0:d] = h2
    h2c_ref[:, d:d + LANES] = cw_col
    cp.wait()

    for r in range(tm):
        row_copy(r, poss_ref[r]).start()

    @pl.when(i == last)
    def _():
        wait_tile()
        cntv_ref[...] = cnt_i
        cc = pltpu.make_async_copy(cntv_ref, cnts_ref, sem_s)
        cc.start()
        cc.wait()
        h2c_ref[0:8, :] = jnp.zeros((8, h2c_ref.shape[1]), F32)

        def zissue(r, c):
            for g in range(N_GROUPS):
                row_copy(0, g * cap + cnts_ref[g, 0] + r).start()
            return c

        lax.fori_loop(0, tm, zissue, 0, unroll=2)
        for g in range(N_GROUPS):
            wait_tile()


def _mixout(att, sg, mc, x, mod3, w_up_att, w_out, norm_g, wrt, br, tiles_per_batch, cap):
    t, d = x.shape
    tm = TM_MIX
    dc = d + LANES
    tok = lambda w: pl.BlockSpec((tm, w), lambda i: (i, 0))
    full = lambda a: pl.BlockSpec(a.shape, lambda i: (0,) * a.ndim)
    return pl.pallas_call(
        functools.partial(_mixout_kernel, d=d, cap=cap),
        grid=(t // tm,),
        in_specs=[tok(att.shape[1]), tok(d), tok(d), tok(d),
                  pl.BlockSpec((1, 1, mod3.shape[2]), lambda i: (i // tiles_per_batch, 0, 0)),
                  full(w_up_att), full(w_out), full(norm_g), full(wrt), full(br)],
        out_specs=[tok(d),
                   pl.BlockSpec((1, 1, tm), lambda i: (i, 0, 0)),
                   pl.BlockSpec((8, LANES), lambda i: (0, 0)),
                   pl.BlockSpec(memory_space=pl.ANY)],
        out_shape=[jax.ShapeDtypeStruct((t, d), F32),
                   jax.ShapeDtypeStruct((t // tm, 1, tm), I32),
                   jax.ShapeDtypeStruct((8, LANES), I32),
                   jax.ShapeDtypeStruct((N_GROUPS * cap, dc), F32)],
        scratch_shapes=[pltpu.VMEM((tm, dc), F32), pltpu.VMEM((8, LANES), F32),
                        pltpu.VMEM((1, tm), I32), pltpu.VMEM((8, LANES), I32),
                        pltpu.SMEM((tm,), I32), pltpu.SMEM((8, LANES), I32),
                        pltpu.SemaphoreType.DMA, pltpu.SemaphoreType.DMA],
        compiler_params=_cparams(1),
        name="mixout",
    )(att, sg, mc, x, mod3, w_up_att, w_out, norm_g, wrt, br)


def _ffn_tile(step, cnt_ref, tm):
    log_tm = tm.bit_length() - 1
    ends = []
    tot = jnp.int32(0)
    for g in range(N_GROUPS):
        tot = tot + lax.shift_right_logical(cnt_ref[g] + (tm - 1), log_tm)
        ends.append(tot)
    sc = jnp.maximum(jnp.minimum(step, tot - 1), 0)
    g = sum((sc >= e).astype(I32) for e in ends[:-1])
    start = jnp.where(g == 0, 0, jnp.where(g == 1, ends[0], jnp.where(g == 2, ends[1], ends[2])))
    return g, sc - start, step < tot


def _ffn_kernel(cnt_ref, hs_ref, wg_ref, wu_ref, wd_ref, y_ref, *, d):
    tm = hs_ref.shape[0]
    _, _, real = _ffn_tile(pl.program_id(0), cnt_ref, tm)

    @pl.when(real)
    def _():
        h = hs_ref[:, 0:d].astype(BF16)
        cw = hs_ref[:, d:d + LANES]
        acc = jnp.zeros((tm, d), F32)
        for e in range(EXPERTS_PER_GROUP):
            hg = jnp.dot(h, wg_ref[e], preferred_element_type=F32)
            hu = jnp.dot(h, wu_ref[e], preferred_element_type=F32)
            a = hg * jax.nn.sigmoid(hg) * hu * cw[:, e:e + 1]
            acc = acc + jnp.dot(a.astype(BF16), wd_ref[e], preferred_element_type=F32)
        y_ref[...] = acc


def _ffn(cnt, hs, w_gate, w_up, w_down, cap):
    rows, dc = hs.shape
    d = dc - LANES
    tm = TM_FFN
    blocks_per_group = cap // tm
    n_steps = (blocks_per_group - 1) + N_GROUPS

    def row_map(s, c):
        g, blk, _ = _ffn_tile(s, c, tm)
        return (g * blocks_per_group + blk, 0)

    def w_map(s, c):
        g, _, _ = _ffn_tile(s, c, tm)
        return (g, 0, 0)

    ff = w_gate.shape[2]
    gs = pltpu.PrefetchScalarGridSpec(
        num_scalar_prefetch=1,
        grid=(n_steps,),
        in_specs=[pl.BlockSpec((tm, dc), row_map),
                  pl.BlockSpec((EXPERTS_PER_GROUP, d, ff), w_map),
                  pl.BlockSpec((EXPERTS_PER_GROUP, d, ff), w_map),
                  pl.BlockSpec((EXPERTS_PER_GROUP, ff, d), w_map)],
        out_specs=pl.BlockSpec((tm, d), row_map),
    )
    return pl.pallas_call(
        functools.partial(_ffn_kernel, d=d),
        grid_spec=gs,
        out_shape=jax.ShapeDtypeStruct((rows, d), F32),
        compiler_params=_cparams(1),
        name="ffn",
    )(cnt, hs, w_gate, w_up, w_down)


def _final_kernel(pos_ref, ys_ref, x1_ref, mod_ref, g_ref, o_ref, ybuf_ref, sem, *, d, normalize):
    tm = x1_ref.shape[0]
    i = pl.program_id(0)

    def gather_tile(tile):
        slot = tile & 1
        base = tile * tm
        for r in range(tm):
            pltpu.make_async_copy(ys_ref.at[pl.ds(pos_ref[base + r], 1), :],
                                  ybuf_ref.at[slot, pl.ds(r, 1), :], sem.at[slot]).start()

    @pl.when(i == 0)
    def _():
        gather_tile(i)

    @pl.when(i + 1 < pl.num_programs(0))
    def _():
        gather_tile(i + 1)

    slot = i & 1
    pltpu.make_async_copy(ys_ref.at[pl.ds(0, tm), :], ybuf_ref.at[slot], sem.at[slot]).wait()

    g_f = mod_ref[0][:, 5 * d:6 * d]
    x2 = x1_ref[...] + g_f * ybuf_ref[slot]
    if normalize:
        x2 = x2 * lax.rsqrt(jnp.mean(x2 * x2, axis=-1, keepdims=True) + EPS) * g_ref[...]
    o_ref[...] = x2


def _final(pos, ys, x1, mod3, final_g, tiles_per_batch, normalize):
    t, d = x1.shape
    tm = TM_FIN
    gs = pltpu.PrefetchScalarGridSpec(
        num_scalar_prefetch=1,
        grid=(t // tm,),
        in_specs=[pl.BlockSpec(memory_space=pl.ANY),
                  pl.BlockSpec((tm, d), lambda i, p: (i, 0)),
                  pl.BlockSpec((1, 1, mod3.shape[2]), lambda i, p: (i // tiles_per_batch, 0, 0)),
                  pl.BlockSpec((1, d), lambda i, p: (0, 0))],
        out_specs=pl.BlockSpec((tm, d), lambda i, p: (i, 0)),
        scratch_shapes=[pltpu.VMEM((2, tm, d), F32), pltpu.SemaphoreType.DMA((2,))],
    )
    return pl.pallas_call(
        functools.partial(_final_kernel, d=d, normalize=normalize),
        grid_spec=gs,
        out_shape=jax.ShapeDtypeStruct((t, d), F32),
        compiler_params=_cparams(1),
        name="final",
    )(pos, ys, x1, mod3, final_g)


def kernel(x, c, positions, w_ada, b_ada, norm_mix_g, w_in, lambda_q1, lambda_k1, lambda_q2,
           lambda_k2, subln_g, conv_w, w_up_att, w_up_conv, w_out, norm_ffn_g, w_group_router,
           b_group_router, w_expert_router, b_expert_router, w_gate, w_up, w_down, final_norm_g):
    b, s, d = x.shape
    depth = w_ada.shape[0]
    t = b * s
    assert s % TQ == 0 and s % TM_IN == 0 and TQ == TK and TM_IN % TK == 0
    assert TQ % TM_IN == 0 or TM_IN % TQ == 0
    assert N_HEADS % ATTN_HEADS_PER_STEP == 0
    assert s % TM_MIX == 0
    assert TM_MIX == TM_FIN and t % TM_FFN == 0
    cap = t + TM_FFN
    n_e = N_GROUPS * EXPERTS_PER_GROUP

    c8 = jnp.concatenate([c, jnp.zeros((8 - b, d), F32)], axis=0)
    pos3 = positions.reshape(b, s, 1)
    inv_freq = ROPE_THETA ** (-jnp.arange(0, HEAD_DIM, 2, dtype=F32) / HEAD_DIM)
    invf = jnp.tile(inv_freq, LANES // (HEAD_DIM // 2)).reshape(1, LANES)

    xf = x
    for l in range(depth):
        lambda_init = 0.8 - 0.6 * math.exp(-0.3 * l)
        row = lambda a: a[l].reshape(1, -1)
        mod8, lam = _ada(c8, w_ada[l], row(b_ada), row(lambda_q1), row(lambda_k1),
                         row(lambda_q2), row(lambda_k2), lambda_init)
        mod3 = mod8[0:b].reshape(b, 1, 6 * d)

        qt, k, vt, sg, mc = _inproj(xf.reshape(b, s, d), pos3, mod3, row(norm_mix_g), invf,
                                    w_in[l].astype(BF16), conv_w[l], w_up_conv[l].astype(BF16))
        att = _attn(lam, qt, k, vt, subln_g[l].reshape(-1, 1), 1.0 - lambda_init)

        pad = ROUTER_ROWS - n_e - N_GROUPS
        wrt = jnp.concatenate([w_expert_router[l].T, w_group_router[l].T,
                               jnp.zeros((pad, d), F32)], axis=0)
        br = jnp.concatenate([b_expert_router[l], b_group_router[l],
                              jnp.zeros((pad,), F32)]).reshape(ROUTER_ROWS, 1)
        x1, pos, cnt, hs = _mixout(att.reshape(t, -1), sg.reshape(t, d), mc.reshape(t, d),
                                   xf.reshape(t, d), mod3, w_up_att[l].astype(BF16),
                                   w_out[l].astype(BF16), row(norm_ffn_g), wrt, br,
                                   s // TM_MIX, cap)
        ys = _ffn(cnt[0:N_GROUPS, 0], hs, w_gate[l].astype(BF16), w_up[l].astype(BF16),
                  w_down[l].astype(BF16), cap)
        xf = _final(pos.reshape(t), ys, x1, mod3, final_norm_g.reshape(1, d), s // TM_FIN,
                    l == depth - 1)
    return xf.reshape(b, s, d)
```

```python
import functools
import math

import jax
import jax.numpy as jnp
from jax import lax
from jax.experimental import pallas as pl
from jax.experimental.pallas import tpu as pltpu

F32 = jnp.float32
BF16 = jnp.bfloat16
I32 = jnp.int32

EPS = 1e-6
MASK_VALUE = -1e30
ROPE_THETA = 10000.0

N_HEADS = 4
HEAD_DIM = 64
LANES = 128
N_GROUPS = 4
EXPERTS_PER_GROUP = 8
CONV_K = 3
ROUTER_ROWS = 48

VMEM_LIMIT = 56 * 1024 * 1024

TM_IN = 1024
TQ = 512
TK = 512
ATTN_HEADS_PER_STEP = 1
VT_ROWS = LANES + 16
TM_MIX = 512
TM_FFN = 512
TM_FIN = 512
TN_ADA = 1024


def _cparams(n_axes):
    return pltpu.CompilerParams(
        dimension_semantics=("arbitrary",) * n_axes, vmem_limit_bytes=VMEM_LIMIT)


def _ada_kernel(c_ref, w_ref, b_ref, lq1_ref, lk1_ref, lq2_ref, lk2_ref, mod_ref, lam_ref, *,
                lambda_init):
    c = c_ref[...]
    a = c * jax.nn.sigmoid(c)
    mod_ref[...] = jnp.dot(a.astype(BF16), w_ref[...].astype(BF16),
                           preferred_element_type=F32) + b_ref[...]
    s1 = jnp.sum(lq1_ref[...] * lk1_ref[...], axis=-1, keepdims=True)
    s2 = jnp.sum(lq2_ref[...] * lk2_ref[...], axis=-1, keepdims=True)
    lam = jnp.exp(s1) - jnp.exp(s2) + lambda_init
    lam_ref[...] = jnp.broadcast_to(lam, lam_ref.shape)


def _ada(c8, w_ada, b_ada, lq1, lk1, lq2, lk2, lambda_init):
    d, n = w_ada.shape
    small = pl.BlockSpec((1, HEAD_DIM), lambda j: (0, 0))
    return pl.pallas_call(
        functools.partial(_ada_kernel, lambda_init=lambda_init),
        grid=(n // TN_ADA,),
        in_specs=[pl.BlockSpec((8, d), lambda j: (0, 0)),
                  pl.BlockSpec((d, TN_ADA), lambda j: (0, j)),
                  pl.BlockSpec((1, TN_ADA), lambda j: (0, j)),
                  small, small, small, small],
        out_specs=[pl.BlockSpec((8, TN_ADA), lambda j: (0, j)),
                   pl.BlockSpec((8, LANES), lambda j: (0, 0))],
        out_shape=[jax.ShapeDtypeStruct((8, n), F32), jax.ShapeDtypeStruct((8, LANES), F32)],
        compiler_params=_cparams(1),
        name="ada",
    )(c8, w_ada, b_ada, lq1, lk1, lq2, lk2)


def _inproj_kernel(x_ref, pos_ref, mod_ref, g_ref, invf_ref, w_ref, cw_ref, wupc_ref,
                   qt_ref, k_ref, vt_ref, sg_ref, mc_ref, carry_ref, *, d, qk_w, conv_w):
    tm = x_ref.shape[1]
    x = x_ref[0]
    ms = jnp.mean(x * x, axis=-1, keepdims=True)
    mod = mod_ref[0]
    sh = mod[:, 0:d]
    sc = mod[:, d:2 * d]
    h = (x * lax.rsqrt(ms + EPS) * g_ref[...]) * (1.0 + sc) + sh
    hb = h.astype(BF16)

    def proj(a, b):
        return jnp.dot(hb, w_ref[:, a:b], preferred_element_type=F32)

    lane = lax.broadcasted_iota(I32, (1, LANES), 1)
    first_half = (lane % HEAD_DIM) < (HEAD_DIM // 2)
    ang = pos_ref[0].astype(F32) * invf_ref[...]
    cos = jnp.cos(ang)
    sin = jnp.sin(ang)
    sin_signed = jnp.where(first_half, -sin, sin)

    def rope(t):
        partner = jnp.where(first_half, pltpu.roll(t, LANES - HEAD_DIM // 2, 1),
                            pltpu.roll(t, HEAD_DIM // 2, 1))
        return t * cos + partner * sin_signed

    zq = proj(0, qk_w)
    zk = proj(qk_w, 2 * qk_w)
    scale = HEAD_DIM ** -0.5 * math.log2(math.e)
    zv = proj(2 * qk_w, 3 * qk_w)
    for hd in range(qk_w // LANES):
        sl = slice(hd * LANES, (hd + 1) * LANES)
        rq = rope(zq[:, sl]) * scale
        qw = qt_ref.shape[4]
        for ck in range(tm // qw):
            qt_ref[0, hd, ck] = rq[ck * qw:(ck + 1) * qw, :].T.astype(BF16)
        k_ref[0, :, sl] = rope(zk[:, sl]).astype(BF16)
        for ck in range(tm // TK):
            vt_ref[0, hd, ck, 0:LANES, :] = zv[ck * TK:(ck + 1) * TK, sl].T.astype(BF16)
            vt_ref[0, hd, ck, LANES:VT_ROWS, :] = jnp.ones((VT_ROWS - LANES, TK), BF16)

    o = 3 * qk_w
    cb = proj(o, o + conv_w)
    u = proj(o + conv_w, o + 2 * conv_w) * proj(o + 2 * conv_w, o + 3 * conv_w)

    @pl.when(pl.program_id(1) == 0)
    def _():
        carry_ref[...] = jnp.zeros_like(carry_ref)

    prev = carry_ref[...]
    row = lax.broadcasted_iota(I32, (tm, 1), 0)
    u1 = jnp.where(row == 0, prev[7:8, :], pltpu.roll(u, 1, 0))
    u2 = jnp.where(row == 0, prev[6:7, :],
                   jnp.where(row == 1, prev[7:8, :], pltpu.roll(u, 2, 0)))
    carry_ref[...] = u[tm - 8:tm, :]
    cw = cw_ref[...]
    conv = cb * (cw[0:1, :] * u + cw[1:2, :] * u1 + cw[2:3, :] * u2)
    convp = jnp.dot(conv.astype(BF16), wupc_ref[...], preferred_element_type=F32)

    o = o + 3 * conv_w
    sg_ref[0] = jax.nn.sigmoid(proj(o, o + d)).astype(BF16)
    mc_ref[0] = (jax.nn.sigmoid(proj(o + d, o + 2 * d)) * convp).astype(BF16)


def _inproj(x, pos3, mod3, norm_g, invf, w_in, conv_w, w_up_conv):
    b, s, d = x.shape
    qk_w = N_HEADS * 2 * HEAD_DIM
    cw = conv_w.shape[1]
    tm = TM_IN
    if tm >= TQ:
        qt_spec = pl.BlockSpec((1, N_HEADS, tm // TQ, LANES, TQ), lambda bi, si: (bi, 0, si, 0, 0))
    else:
        qt_spec = pl.BlockSpec((1, N_HEADS, 1, LANES, tm),
                               lambda bi, si: (bi, 0, si // (TQ // tm), 0, si % (TQ // tm)))
    tok = lambda w: pl.BlockSpec((1, tm, w), lambda bi, si: (bi, si, 0))
    full = lambda a: pl.BlockSpec(a.shape, lambda bi, si: (0,) * a.ndim)
    return pl.pallas_call(
        functools.partial(_inproj_kernel, d=d, qk_w=qk_w, conv_w=cw),
        grid=(b, s // tm),
        in_specs=[tok(d), tok(1),
                  pl.BlockSpec((1, 1, mod3.shape[2]), lambda bi, si: (bi, 0, 0)),
                  full(norm_g), full(invf), full(w_in), full(conv_w), full(w_up_conv)],
        out_specs=[qt_spec,
                   tok(qk_w),
                   pl.BlockSpec((1, N_HEADS, tm // TK, VT_ROWS, TK),
                                lambda bi, si: (bi, 0, si, 0, 0)),
                   tok(d), tok(d)],
        out_shape=[jax.ShapeDtypeStruct((b, N_HEADS, s // TQ, LANES, TQ), BF16),
                   jax.ShapeDtypeStruct((b, s, qk_w), BF16),
                   jax.ShapeDtypeStruct((b, N_HEADS, s // TK, VT_ROWS, TK), BF16),
                   jax.ShapeDtypeStruct((b, s, d), BF16),
                   jax.ShapeDtypeStruct((b, s, d), BF16)],
        scratch_shapes=[pltpu.VMEM((8, cw), F32)],
        compiler_params=_cparams(2),
        name="inproj",
    )(x, pos3, mod3, norm_g, invf, w_in, conv_w, w_up_conv)


def _attn_kernel(lam_ref, qt_ref, k_ref, vt_ref, g_ref, o_ref, s_ref, p_ref, qq_ref, bias_ref, *,
                 out_scale):
    hps, n_q, tq = qt_ref.shape[1], qt_ref.shape[2], qt_ref.shape[4]
    feat = lax.broadcasted_iota(I32, (LANES, 1), 0)
    halves = [slice(h * tq, (h + 1) * tq) for h in range(2)]
    streams = [(h, half) for h in range(hps) for half in halves]

    k_sub = lax.broadcasted_iota(I32, (TK, 1), 0)
    q_lane = lax.broadcasted_iota(I32, (1, tq), 1)
    bias_ref[...] = jnp.where(k_sub <= q_lane, 0.0, MASK_VALUE)

    def load_queries(qi):
        for h in range(hps):
            qt = qt_ref[0, h, qi]
            zero = jnp.zeros_like(qt)
            qq_ref[h, :, halves[0]] = jnp.where(feat < HEAD_DIM, qt, zero)
            qq_ref[h, :, halves[1]] = jnp.where(feat >= HEAD_DIM, qt, zero)

    def scores(j):
        start = j * TK
        col_max = []
        for h, lanes in streams:
            kt = k_ref[0, pl.ds(start, TK), h * LANES:(h + 1) * LANES]
            s_ref[h, :, lanes] = jnp.dot(kt, qq_ref[h, :, lanes], preferred_element_type=F32)
        return None

    def values(j, state):
        return tuple((m, alpha * acc + jnp.dot(vt_ref[0, h, j], p_ref[h, :, lanes],
                                               preferred_element_type=F32), alpha)
                     for (h, lanes), (m, acc, alpha) in zip(streams, state))

    def softmax(col_max, state, diagonal):
        out = []
        for i, ((h, lanes), (m, acc, _)) in enumerate(zip(streams, state)):
            load = ((lambda: s_ref[h, :, lanes] + bias_ref[...]) if diagonal
                    else (lambda: s_ref[h, :, lanes]))
            m_new = jnp.maximum(m, jnp.max(load(), axis=0, keepdims=True))
            p_ref[h, :, lanes] = jnp.exp2(load() - m_new).astype(BF16)
            out.append((m_new, acc, jnp.exp2(m - m_new)))
        return tuple(out)

    def finish(qi, last_tile, state):
        state = values(last_tile, state)
        row0 = qi * tq
        for h in range(hps):
            (_, acc1, _), (_, acc2, _) = state[2 * h], state[2 * h + 1]
            o1 = acc1[0:LANES] / acc1[LANES:LANES + 1]
            o2 = acc2[0:LANES] / acc2[LANES:LANES + 1]
            a = o1 - lam_ref[0:1, 0:1] * o2
            y = a * lax.rsqrt(jnp.mean(a * a, axis=0, keepdims=True) + EPS) * g_ref[...]
            o_ref[0, pl.ds(row0, tq), h * LANES:(h + 1) * LANES] = (
                (y * out_scale).T.astype(BF16))

    def next_diagonal(qi):
        if qi + 1 < n_q:
            load_queries(qi + 1)
            scores(qi + 1)

    one = (jnp.full((1, tq), MASK_VALUE, F32), jnp.zeros((VT_ROWS, tq), F32),
           jnp.ones((1, tq), F32))
    init = tuple(one for _ in streams)

    load_queries(0)
    scores(0)
    pending = softmax(None, init, True)
    next_diagonal(0)

    for qi in range(1, n_q):
        finish(qi - 1, max(qi - 2, 0), pending)
        state = softmax(None, init, True)
        col_max = scores(0)
        for t in range(1, qi + 1):
            state = values(qi if t == 1 else t - 2, state)
            state = softmax(col_max, state, False)
            if t < qi:
                col_max = scores(t)
            else:
                next_diagonal(qi)
        pending = state
    finish(n_q - 1, max(n_q - 2, 0), pending)


def _attn(lam, qt, k, vt, subln_g_col, out_scale):
    b, s, w = k.shape
    hps = ATTN_HEADS_PER_STEP
    return pl.pallas_call(
        functools.partial(_attn_kernel, out_scale=out_scale),
        grid=(b, N_HEADS // hps),
        in_specs=[pl.BlockSpec((8, LANES), lambda bi, hi: (0, 0)),
                  pl.BlockSpec((1, hps, s // TQ, LANES, TQ), lambda bi, hi: (bi, hi, 0, 0, 0)),
                  pl.BlockSpec((1, s, hps * LANES), lambda bi, hi: (bi, 0, hi)),
                  pl.BlockSpec((1, hps, s // TK, VT_ROWS, TK), lambda bi, hi: (bi, hi, 0, 0, 0)),
                  pl.BlockSpec((LANES, 1), lambda bi, hi: (0, 0))],
        out_specs=pl.BlockSpec((1, s, hps * LANES), lambda bi, hi: (bi, 0, hi)),
        out_shape=jax.ShapeDtypeStruct((b, s, w), BF16),
        scratch_shapes=[pltpu.VMEM((hps, TK, 2 * TQ), F32), pltpu.VMEM((hps, TK, 2 * TQ), BF16),
                        pltpu.VMEM((hps, LANES, 2 * TQ), BF16), pltpu.VMEM((TK, TQ), F32)],
        compiler_params=_cparams(2),
        name="attn",
    )(lam, qt, k, vt, subln_g_col)


def _mixout_kernel(att_ref, sg_ref, mc_ref, x_ref, mod_ref, wua_ref, wo_ref, g_ref, wrt_ref,
                   br_ref, x1_ref, pos_ref, cnt_ref, hs_ref,
                   h2c_ref, tri_ref, carry_ref, posv_ref, cntv_ref, poss_ref, cnts_ref,
                   sem_s, sem_r, *, d, cap):
    tm = x_ref.shape[0]
    i = pl.program_id(0)
    last = pl.num_programs(0) - 1

    @pl.when(i == 0)
    def _():
        carry_ref[...] = jnp.zeros_like(carry_ref)
        earlier = (lax.broadcasted_iota(I32, (tm, 1), 0)
                   <= lax.broadcasted_iota(I32, (1, tm), 1))
        tri_ref[...] = jnp.where(earlier, 1.0, 0.0).astype(BF16)

    mod = mod_ref[0]
    g_m = mod[:, 2 * d:3 * d]
    sh_f = mod[:, 3 * d:4 * d]
    sc_f = mod[:, 4 * d:5 * d]

    attp = jnp.dot(att_ref[...], wua_ref[...], preferred_element_type=F32)
    merged = sg_ref[...].astype(F32) * attp + mc_ref[...].astype(F32)
    mo = jnp.dot(merged.astype(BF16), wo_ref[...], preferred_element_type=F32)
    x1 = x_ref[...] + g_m * mo
    x1_ref[...] = x1
    ms = jnp.mean(x1 * x1, axis=-1, keepdims=True)
    h2 = (x1 * lax.rsqrt(ms + EPS) * g_ref[...]) * (1.0 + sc_f) + sh_f

    def split(a):
        hi = a.astype(BF16)
        return hi, (a - hi.astype(F32)).astype(BF16)

    def dot_t(a, b):
        return lax.dot_general(a, b, (((1,), (1,)), ((), ())), preferred_element_type=F32)

    w_hi, w_lo = split(wrt_ref[...])
    h_hi, h_lo = split(h2)
    by_hi = dot_t(jnp.concatenate([w_hi, w_lo], axis=0), h_hi)
    lt = (by_hi[0:ROUTER_ROWS] + (dot_t(w_hi, h_lo) + by_hi[ROUTER_ROWS:2 * ROUTER_ROWS])
          + br_ref[...])
    ridx = lax.broadcasted_iota(I32, (8, 1), 0)
    n_e = N_GROUPS * EXPERTS_PER_GROUP
    is_grp = ridx < N_GROUPS
    gl = jnp.where(is_grp, lt[n_e:n_e + 8, :], MASK_VALUE)
    gmax = jnp.max(gl, axis=0, keepdims=True)
    gsum = jnp.sum(jnp.where(is_grp, jnp.exp(gl - gmax), 0.0), axis=0, keepdims=True)
    g_w = 1.0 / gsum
    gidx = jnp.min(jnp.where(gl == gmax, ridx, 8), axis=0, keepdims=True)
    e_in = jnp.zeros((8, tm), F32)
    for g in range(N_GROUPS):
        e_in = jnp.where(gidx == g, lt[g * 8:(g + 1) * 8, :], e_in)
    t1 = jnp.max(e_in, axis=0, keepdims=True)
    i1 = jnp.min(jnp.where(e_in == t1, ridx, 8), axis=0, keepdims=True)
    e_in2 = jnp.where(ridx == i1, MASK_VALUE, e_in)
    t2 = jnp.max(e_in2, axis=0, keepdims=True)
    i2 = jnp.min(jnp.where(e_in2 == t2, ridx, 8), axis=0, keepdims=True)
    e2 = jnp.exp(t2 - t1)
    den = 1.0 + e2
    w1 = (1.0 / den) * g_w
    w2 = (e2 / den) * g_w
    cw_t = jnp.where(ridx == i1, w1, jnp.where(ridx == i2, w2, 0.0))
    cw_col = jnp.concatenate([cw_t, jnp.zeros((LANES - 8, tm), F32)], axis=0).T

    onehot = jnp.where(ridx == gidx, 1.0, 0.0)
    oh16 = jnp.concatenate([onehot, jnp.zeros_like(onehot)], axis=0).astype(BF16)
    incl = jnp.dot(oh16, tri_ref[...], preferred_element_type=F32)[0:8, :]
    run = carry_ref[...][:, 0:1]
    base = ridx.astype(F32) * float(cap)
    pos = jnp.sum(onehot * (base + run + incl - onehot), axis=0, keepdims=True).astype(I32)
    new_run = run + incl[:, tm - 1:tm]
    carry_ref[...] = jnp.broadcast_to(new_run, carry_ref.shape)
    cnt_i = jnp.broadcast_to(new_run, cnt_ref.shape).astype(I32)
    cnt_ref[...] = cnt_i
    pos_ref[0] = pos
    posv_ref[...] = pos
    cp = pltpu.make_async_copy(posv_ref.at[0], poss_ref, sem_s)
    cp.start()

    def row_copy(r, p):
        return pltpu.make_async_copy(h2c_ref.at[pl.ds(r, 1), :], hs_ref.at[pl.ds(p, 1), :], sem_r)

    def wait_tile():
        pltpu.make_async_copy(h2c_ref, hs_ref.at[pl.ds(0, tm), :], sem_r).wait()

    @pl.when(i > 0)
    def _():
        wait_tile()

    h2c_ref[:, 0:d] = h2
    h2c_ref[:, d:d + LANES] = cw_col
    cp.wait()

    for r in range(tm):
        row_copy(r, poss_ref[r]).start()

    @pl.when(i == last)
    def _():
        wait_tile()
        cntv_ref[...] = cnt_i
        cc = pltpu.make_async_copy(cntv_ref, cnts_ref, sem_s)
        cc.start()
        cc.wait()
        h2c_ref[0:8, :] = jnp.zeros((8, h2c_ref.shape[1]), F32)

        def zissue(r, c):
            for g in range(N_GROUPS):
                row_copy(0, g * cap + cnts_ref[g, 0] + r).start()
            return c

        lax.fori_loop(0, tm, zissue, 0, unroll=2)
        for g in range(N_GROUPS):
            wait_tile()


def _mixout(att, sg, mc, x, mod3, w_up_att, w_out, norm_g, wrt, br, tiles_per_batch, cap):
    t, d = x.shape
    tm = TM_MIX
    dc = d + LANES
    tok = lambda w: pl.BlockSpec((tm, w), lambda i: (i, 0))
    full = lambda a: pl.BlockSpec(a.shape, lambda i: (0,) * a.ndim)
    return pl.pallas_call(
        functools.partial(_mixout_kernel, d=d, cap=cap),
        grid=(t // tm,),
        in_specs=[tok(att.shape[1]), tok(d), tok(d), tok(d),
                  pl.BlockSpec((1, 1, mod3.shape[2]), lambda i: (i // tiles_per_batch, 0, 0)),
                  full(w_up_att), full(w_out), full(norm_g), full(wrt), full(br)],
        out_specs=[tok(d),
                   pl.BlockSpec((1, 1, tm), lambda i: (i, 0, 0)),
                   pl.BlockSpec((8, LANES), lambda i: (0, 0)),
                   pl.BlockSpec(memory_space=pl.ANY)],
        out_shape=[jax.ShapeDtypeStruct((t, d), F32),
                   jax.ShapeDtypeStruct((t // tm, 1, tm), I32),
                   jax.ShapeDtypeStruct((8, LANES), I32),
                   jax.ShapeDtypeStruct((N_GROUPS * cap, dc), F32)],
        scratch_shapes=[pltpu.VMEM((tm, dc), F32), pltpu.VMEM((tm, tm), BF16),
                        pltpu.VMEM((8, LANES), F32),
                        pltpu.VMEM((1, tm), I32), pltpu.VMEM((8, LANES), I32),
                        pltpu.SMEM((tm,), I32), pltpu.SMEM((8, LANES), I32),
                        pltpu.SemaphoreType.DMA, pltpu.SemaphoreType.DMA],
        compiler_params=_cparams(1),
        name="mixout",
    )(att, sg, mc, x, mod3, w_up_att, w_out, norm_g, wrt, br)


def _ffn_tile(step, cnt_ref, tm):
    log_tm = tm.bit_length() - 1
    ends = []
    tot = jnp.int32(0)
    for g in range(N_GROUPS):
        tot = tot + lax.shift_right_logical(cnt_ref[g] + (tm - 1), log_tm)
        ends.append(tot)
    sc = jnp.maximum(jnp.minimum(step, tot - 1), 0)
    g = sum((sc >= e).astype(I32) for e in ends[:-1])
    start = jnp.where(g == 0, 0, jnp.where(g == 1, ends[0], jnp.where(g == 2, ends[1], ends[2])))
    return g, sc - start, step < tot


def _ffn_kernel(cnt_ref, hs_ref, wg_ref, wu_ref, wd_ref, y_ref, *, d):
    tm = hs_ref.shape[0]
    _, _, real = _ffn_tile(pl.program_id(0), cnt_ref, tm)

    @pl.when(real)
    def _():
        h = hs_ref[:, 0:d].astype(BF16)
        cw = hs_ref[:, d:d + LANES]
        acc = jnp.zeros((tm, d), F32)
        for e in range(EXPERTS_PER_GROUP):
            hg = jnp.dot(h, wg_ref[e], preferred_element_type=F32)
            hu = jnp.dot(h, wu_ref[e], preferred_element_type=F32)
            a = hg * jax.nn.sigmoid(hg) * hu * cw[:, e:e + 1]
            acc = acc + jnp.dot(a.astype(BF16), wd_ref[e], preferred_element_type=F32)
        y_ref[...] = acc


def _ffn(cnt, hs, w_gate, w_up, w_down, cap):
    rows, dc = hs.shape
    d = dc - LANES
    tm = TM_FFN
    blocks_per_group = cap // tm
    n_steps = (blocks_per_group - 1) + N_GROUPS

    def row_map(s, c):
        g, blk, _ = _ffn_tile(s, c, tm)
        return (g * blocks_per_group + blk, 0)

    def w_map(s, c):
        g, _, _ = _ffn_tile(s, c, tm)
        return (g, 0, 0)

    ff = w_gate.shape[2]
    gs = pltpu.PrefetchScalarGridSpec(
        num_scalar_prefetch=1,
        grid=(n_steps,),
        in_specs=[pl.BlockSpec((tm, dc), row_map),
                  pl.BlockSpec((EXPERTS_PER_GROUP, d, ff), w_map),
                  pl.BlockSpec((EXPERTS_PER_GROUP, d, ff), w_map),
                  pl.BlockSpec((EXPERTS_PER_GROUP, ff, d), w_map)],
        out_specs=pl.BlockSpec((tm, d), row_map),
    )
    return pl.pallas_call(
        functools.partial(_ffn_kernel, d=d),
        grid_spec=gs,
        out_shape=jax.ShapeDtypeStruct((rows, d), F32),
        compiler_params=_cparams(1),
        name="ffn",
    )(cnt, hs, w_gate, w_up, w_down)


def _final_kernel(pos_ref, ys_ref, x1_ref, mod_ref, g_ref, o_ref, ybuf_ref, sem, *, d, normalize):
    tm = x1_ref.shape[0]
    i = pl.program_id(0)

    def gather_tile(tile):
        slot = tile & 1
        base = tile * tm
        for r in range(tm):
            pltpu.make_async_copy(ys_ref.at[pl.ds(pos_ref[base + r], 1), :],
                                  ybuf_ref.at[slot, pl.ds(r, 1), :], sem.at[slot]).start()

    @pl.when(i == 0)
    def _():
        gather_tile(i)

    @pl.when(i + 1 < pl.num_programs(0))
    def _():
        gather_tile(i + 1)

    slot = i & 1
    pltpu.make_async_copy(ys_ref.at[pl.ds(0, tm), :], ybuf_ref.at[slot], sem.at[slot]).wait()

    g_f = mod_ref[0][:, 5 * d:6 * d]
    x2 = x1_ref[...] + g_f * ybuf_ref[slot]
    if normalize:
        x2 = x2 * lax.rsqrt(jnp.mean(x2 * x2, axis=-1, keepdims=True) + EPS) * g_ref[...]
    o_ref[...] = x2


def _final(pos, ys, x1, mod3, final_g, tiles_per_batch, normalize):
    t, d = x1.shape
    tm = TM_FIN
    gs = pltpu.PrefetchScalarGridSpec(
        num_scalar_prefetch=1,
        grid=(t // tm,),
        in_specs=[pl.BlockSpec(memory_space=pl.ANY),
                  pl.BlockSpec((tm, d), lambda i, p: (i, 0)),
                  pl.BlockSpec((1, 1, mod3.shape[2]), lambda i, p: (i // tiles_per_batch, 0, 0)),
                  pl.BlockSpec((1, d), lambda i, p: (0, 0))],
        out_specs=pl.BlockSpec((tm, d), lambda i, p: (i, 0)),
        scratch_shapes=[pltpu.VMEM((2, tm, d), F32), pltpu.SemaphoreType.DMA((2,))],
    )
    return pl.pallas_call(
        functools.partial(_final_kernel, d=d, normalize=normalize),
        grid_spec=gs,
        out_shape=jax.ShapeDtypeStruct((t, d), F32),
        compiler_params=_cparams(1),
        name="final",
    )(pos, ys, x1, mod3, final_g)


def kernel(x, c, positions, w_ada, b_ada, norm_mix_g, w_in, lambda_q1, lambda_k1, lambda_q2,
           lambda_k2, subln_g, conv_w, w_up_att, w_up_conv, w_out, norm_ffn_g, w_group_router,
           b_group_router, w_expert_router, b_expert_router, w_gate, w_up, w_down, final_norm_g):
    b, s, d = x.shape
    depth = w_ada.shape[0]
    t = b * s
    assert s % TQ == 0 and s % TM_IN == 0 and TQ == TK and TM_IN % TK == 0
    assert TQ % TM_IN == 0 or TM_IN % TQ == 0
    assert N_HEADS % ATTN_HEADS_PER_STEP == 0
    assert s % TM_MIX == 0
    assert TM_MIX == TM_FIN and t % TM_FFN == 0
    cap = t + TM_FFN
    n_e = N_GROUPS * EXPERTS_PER_GROUP

    c8 = jnp.concatenate([c, jnp.zeros((8 - b, d), F32)], axis=0)
    pos3 = positions.reshape(b, s, 1)
    inv_freq = ROPE_THETA ** (-jnp.arange(0, HEAD_DIM, 2, dtype=F32) / HEAD_DIM)
    invf = jnp.tile(inv_freq, LANES // (HEAD_DIM // 2)).reshape(1, LANES)

    xf = x
    for l in range(depth):
        lambda_init = 0.8 - 0.6 * math.exp(-0.3 * l)
        row = lambda a: a[l].reshape(1, -1)
        mod8, lam = _ada(c8, w_ada[l], row(b_ada), row(lambda_q1), row(lambda_k1),
                         row(lambda_q2), row(lambda_k2), lambda_init)
        mod3 = mod8[0:b].reshape(b, 1, 6 * d)

        qt, k, vt, sg, mc = _inproj(xf.reshape(b, s, d), pos3, mod3, row(norm_mix_g), invf,
                                    w_in[l].astype(BF16), conv_w[l], w_up_conv[l].astype(BF16))
        att = _attn(lam, qt, k, vt, subln_g[l].reshape(-1, 1), 1.0 - lambda_init)

        pad = ROUTER_ROWS - n_e - N_GROUPS
        wrt = jnp.concatenate([w_expert_router[l].T, w_group_router[l].T,
                               jnp.zeros((pad, d), F32)], axis=0)
        br = jnp.concatenate([b_expert_router[l], b_group_router[l],
                              jnp.zeros((pad,), F32)]).reshape(ROUTER_ROWS, 1)
        x1, pos, cnt, hs = _mixout(att.reshape(t, -1), sg.reshape(t, d), mc.reshape(t, d),
                                   xf.reshape(t, d), mod3, w_up_att[l].astype(BF16),
                                   w_out[l].astype(BF16), row(norm_ffn_g), wrt, br,
                                   s // TM_MIX, cap)
        ys = _ffn(cnt[0:N_GROUPS, 0], hs, w_gate[l].astype(BF16), w_up[l].astype(BF16),
                  w_down[l].astype(BF16), cap)
        xf = _final(pos.reshape(t), ys, x1, mod3, final_norm_g.reshape(1, d), s // TM_FIN,
                    l == depth - 1)
    return xf.reshape(b, s, d)
```

```python
import functools
import math

import jax
import jax.numpy as jnp
from jax import lax
from jax.experimental import pallas as pl
from jax.experimental.pallas import tpu as pltpu

F32 = jnp.float32
BF16 = jnp.bfloat16
I32 = jnp.int32

EPS = 1e-6
MASK_VALUE = -1e30
ROPE_THETA = 10000.0

N_HEADS = 4
HEAD_DIM = 64
LANES = 128
N_GROUPS = 4
EXPERTS_PER_GROUP = 8
N_PAIRS = EXPERTS_PER_GROUP * (EXPERTS_PER_GROUP - 1) // 2
N_BUCKETS = N_GROUPS * N_PAIRS
CONV_K = 3
ROUTER_ROWS = 48

VMEM_LIMIT = 56 * 1024 * 1024

TM_IN = 1024
TQ = 512
TK = 512
ATTN_HEADS_PER_STEP = 1
VT_ROWS = LANES + 16
TM_MIX = 512
TM_FFN = 128
TM_FIN = 512
TN_ADA = 1024


def _cparams(n_axes):
    return pltpu.CompilerParams(
        dimension_semantics=("arbitrary",) * n_axes, vmem_limit_bytes=VMEM_LIMIT)


def _ada_kernel(c_ref, w_ref, b_ref, lq1_ref, lk1_ref, lq2_ref, lk2_ref, mod_ref, lam_ref, *,
                lambda_init):
    c = c_ref[...]
    a = c * jax.nn.sigmoid(c)
    mod_ref[...] = jnp.dot(a.astype(BF16), w_ref[...].astype(BF16),
                           preferred_element_type=F32) + b_ref[...]
    s1 = jnp.sum(lq1_ref[...] * lk1_ref[...], axis=-1, keepdims=True)
    s2 = jnp.sum(lq2_ref[...] * lk2_ref[...], axis=-1, keepdims=True)
    lam = jnp.exp(s1) - jnp.exp(s2) + lambda_init
    lam_ref[...] = jnp.broadcast_to(lam, lam_ref.shape)


def _ada(c8, w_ada, b_ada, lq1, lk1, lq2, lk2, lambda_init):
    d, n = w_ada.shape
    small = pl.BlockSpec((1, HEAD_DIM), lambda j: (0, 0))
    return pl.pallas_call(
        functools.partial(_ada_kernel, lambda_init=lambda_init),
        grid=(n // TN_ADA,),
        in_specs=[pl.BlockSpec((8, d), lambda j: (0, 0)),
                  pl.BlockSpec((d, TN_ADA), lambda j: (0, j)),
                  pl.BlockSpec((1, TN_ADA), lambda j: (0, j)),
                  small, small, small, small],
        out_specs=[pl.BlockSpec((8, TN_ADA), lambda j: (0, j)),
                   pl.BlockSpec((8, LANES), lambda j: (0, 0))],
        out_shape=[jax.ShapeDtypeStruct((8, n), F32), jax.ShapeDtypeStruct((8, LANES), F32)],
        compiler_params=_cparams(1),
        name="ada",
    )(c8, w_ada, b_ada, lq1, lk1, lq2, lk2)


def _inproj_kernel(x_ref, pos_ref, mod_ref, g_ref, invf_ref, w_ref, cw_ref, wupc_ref,
                   qt_ref, k_ref, vt_ref, sg_ref, mc_ref, carry_ref, *, d, qk_w, conv_w):
    tm = x_ref.shape[1]
    x = x_ref[0]
    ms = jnp.mean(x * x, axis=-1, keepdims=True)
    mod = mod_ref[0]
    sh = mod[:, 0:d]
    sc = mod[:, d:2 * d]
    h = (x * lax.rsqrt(ms + EPS) * g_ref[...]) * (1.0 + sc) + sh
    hb = h.astype(BF16)

    def proj(a, b):
        return jnp.dot(hb, w_ref[:, a:b], preferred_element_type=F32)

    lane = lax.broadcasted_iota(I32, (1, LANES), 1)
    first_half = (lane % HEAD_DIM) < (HEAD_DIM // 2)
    ang = pos_ref[0].astype(F32) * invf_ref[...]
    cos = jnp.cos(ang)
    sin = jnp.sin(ang)
    sin_signed = jnp.where(first_half, -sin, sin)

    def rope(t):
        partner = jnp.where(first_half, pltpu.roll(t, LANES - HEAD_DIM // 2, 1),
                            pltpu.roll(t, HEAD_DIM // 2, 1))
        return t * cos + partner * sin_signed

    zq = proj(0, qk_w)
    zk = proj(qk_w, 2 * qk_w)
    scale = HEAD_DIM ** -0.5 * math.log2(math.e)
    zv = proj(2 * qk_w, 3 * qk_w)
    for hd in range(qk_w // LANES):
        sl = slice(hd * LANES, (hd + 1) * LANES)
        rq = rope(zq[:, sl]) * scale
        qw = qt_ref.shape[4]
        for ck in range(tm // qw):
            qt_ref[0, hd, ck] = rq[ck * qw:(ck + 1) * qw, :].T.astype(BF16)
        k_ref[0, :, sl] = rope(zk[:, sl]).astype(BF16)
        for ck in range(tm // TK):
            vt_ref[0, hd, ck, 0:LANES, :] = zv[ck * TK:(ck + 1) * TK, sl].T.astype(BF16)
            vt_ref[0, hd, ck, LANES:VT_ROWS, :] = jnp.ones((VT_ROWS - LANES, TK), BF16)

    o = 3 * qk_w
    cb = proj(o, o + conv_w)
    u = proj(o + conv_w, o + 2 * conv_w) * proj(o + 2 * conv_w, o + 3 * conv_w)

    @pl.when(pl.program_id(1) == 0)
    def _():
        carry_ref[...] = jnp.zeros_like(carry_ref)

    prev = carry_ref[...]
    row = lax.broadcasted_iota(I32, (tm, 1), 0)
    u1 = jnp.where(row == 0, prev[7:8, :], pltpu.roll(u, 1, 0))
    u2 = jnp.where(row == 0, prev[6:7, :],
                   jnp.where(row == 1, prev[7:8, :], pltpu.roll(u, 2, 0)))
    carry_ref[...] = u[tm - 8:tm, :]
    cw = cw_ref[...]
    conv = cb * (cw[0:1, :] * u + cw[1:2, :] * u1 + cw[2:3, :] * u2)
    convp = jnp.dot(conv.astype(BF16), wupc_ref[...], preferred_element_type=F32)

    o = o + 3 * conv_w
    sg_ref[0] = jax.nn.sigmoid(proj(o, o + d)).astype(BF16)
    mc_ref[0] = (jax.nn.sigmoid(proj(o + d, o + 2 * d)) * convp).astype(BF16)


def _inproj(x, pos3, mod3, norm_g, invf, w_in, conv_w, w_up_conv):
    b, s, d = x.shape
    qk_w = N_HEADS * 2 * HEAD_DIM
    cw = conv_w.shape[1]
    tm = TM_IN
    if tm >= TQ:
        qt_spec = pl.BlockSpec((1, N_HEADS, tm // TQ, LANES, TQ), lambda bi, si: (bi, 0, si, 0, 0))
    else:
        qt_spec = pl.BlockSpec((1, N_HEADS, 1, LANES, tm),
                               lambda bi, si: (bi, 0, si // (TQ // tm), 0, si % (TQ // tm)))
    tok = lambda w: pl.BlockSpec((1, tm, w), lambda bi, si: (bi, si, 0))
    full = lambda a: pl.BlockSpec(a.shape, lambda bi, si: (0,) * a.ndim)
    return pl.pallas_call(
        functools.partial(_inproj_kernel, d=d, qk_w=qk_w, conv_w=cw),
        grid=(b, s // tm),
        in_specs=[tok(d), tok(1),
                  pl.BlockSpec((1, 1, mod3.shape[2]), lambda bi, si: (bi, 0, 0)),
                  full(norm_g), full(invf), full(w_in), full(conv_w), full(w_up_conv)],
        out_specs=[qt_spec,
                   tok(qk_w),
                   pl.BlockSpec((1, N_HEADS, tm // TK, VT_ROWS, TK),
                                lambda bi, si: (bi, 0, si, 0, 0)),
                   tok(d), tok(d)],
        out_shape=[jax.ShapeDtypeStruct((b, N_HEADS, s // TQ, LANES, TQ), BF16),
                   jax.ShapeDtypeStruct((b, s, qk_w), BF16),
                   jax.ShapeDtypeStruct((b, N_HEADS, s // TK, VT_ROWS, TK), BF16),
                   jax.ShapeDtypeStruct((b, s, d), BF16),
                   jax.ShapeDtypeStruct((b, s, d), BF16)],
        scratch_shapes=[pltpu.VMEM((8, cw), F32)],
        compiler_params=_cparams(2),
        name="inproj",
    )(x, pos3, mod3, norm_g, invf, w_in, conv_w, w_up_conv)


def _attn_kernel(lam_ref, qt_ref, k_ref, vt_ref, g_ref, o_ref, s_ref, p_ref, qq_ref, bias_ref, *,
                 out_scale):
    hps, n_q, tq = qt_ref.shape[1], qt_ref.shape[2], qt_ref.shape[4]
    feat = lax.broadcasted_iota(I32, (LANES, 1), 0)
    halves = [slice(h * tq, (h + 1) * tq) for h in range(2)]
    streams = [(h, half) for h in range(hps) for half in halves]

    k_sub = lax.broadcasted_iota(I32, (TK, 1), 0)
    q_lane = lax.broadcasted_iota(I32, (1, tq), 1)
    bias_ref[...] = jnp.where(k_sub <= q_lane, 0.0, MASK_VALUE)

    def load_queries(qi):
        for h in range(hps):
            qt = qt_ref[0, h, qi]
            zero = jnp.zeros_like(qt)
            qq_ref[h, :, halves[0]] = jnp.where(feat < HEAD_DIM, qt, zero)
            qq_ref[h, :, halves[1]] = jnp.where(feat >= HEAD_DIM, qt, zero)

    def scores(j):
        start = j * TK
        col_max = []
        for h, lanes in streams:
            kt = k_ref[0, pl.ds(start, TK), h * LANES:(h + 1) * LANES]
            s_ref[h, :, lanes] = jnp.dot(kt, qq_ref[h, :, lanes], preferred_element_type=F32)
        return None

    def values(j, state):
        return tuple((m, alpha * acc + jnp.dot(vt_ref[0, h, j], p_ref[h, :, lanes],
                                               preferred_element_type=F32), alpha)
                     for (h, lanes), (m, acc, alpha) in zip(streams, state))

    def softmax(col_max, state, diagonal):
        out = []
        for i, ((h, lanes), (m, acc, _)) in enumerate(zip(streams, state)):
            load = ((lambda: s_ref[h, :, lanes] + bias_ref[...]) if diagonal
                    else (lambda: s_ref[h, :, lanes]))
            m_new = jnp.maximum(m, jnp.max(load(), axis=0, keepdims=True))
            p_ref[h, :, lanes] = jnp.exp2(load() - m_new).astype(BF16)
            out.append((m_new, acc, jnp.exp2(m - m_new)))
        return tuple(out)

    def finish(qi, last_tile, state):
        state = values(last_tile, state)
        row0 = qi * tq
        for h in range(hps):
            (_, acc1, _), (_, acc2, _) = state[2 * h], state[2 * h + 1]
            o1 = acc1[0:LANES] / acc1[LANES:LANES + 1]
            o2 = acc2[0:LANES] / acc2[LANES:LANES + 1]
            a = o1 - lam_ref[0:1, 0:1] * o2
            y = a * lax.rsqrt(jnp.mean(a * a, axis=0, keepdims=True) + EPS) * g_ref[...]
            o_ref[0, pl.ds(row0, tq), h * LANES:(h + 1) * LANES] = (
                (y * out_scale).T.astype(BF16))

    def next_diagonal(qi):
        if qi + 1 < n_q:
            load_queries(qi + 1)
            scores(qi + 1)

    one = (jnp.full((1, tq), MASK_VALUE, F32), jnp.zeros((VT_ROWS, tq), F32),
           jnp.ones((1, tq), F32))
    init = tuple(one for _ in streams)

    load_queries(0)
    scores(0)
    pending = softmax(None, init, True)
    next_diagonal(0)

    for qi in range(1, n_q):
        finish(qi - 1, max(qi - 2, 0), pending)
        state = softmax(None, init, True)
        col_max = scores(0)
        for t in range(1, qi + 1):
            state = values(qi if t == 1 else t - 2, state)
            state = softmax(col_max, state, False)
            if t < qi:
                col_max = scores(t)
            else:
                next_diagonal(qi)
        pending = state
    finish(n_q - 1, max(n_q - 2, 0), pending)


def _attn(lam, qt, k, vt, subln_g_col, out_scale):
    b, s, w = k.shape
    hps = ATTN_HEADS_PER_STEP
    return pl.pallas_call(
        functools.partial(_attn_kernel, out_scale=out_scale),
        grid=(b, N_HEADS // hps),
        in_specs=[pl.BlockSpec((8, LANES), lambda bi, hi: (0, 0)),
                  pl.BlockSpec((1, hps, s // TQ, LANES, TQ), lambda bi, hi: (bi, hi, 0, 0, 0)),
                  pl.BlockSpec((1, s, hps * LANES), lambda bi, hi: (bi, 0, hi)),
                  pl.BlockSpec((1, hps, s // TK, VT_ROWS, TK), lambda bi, hi: (bi, hi, 0, 0, 0)),
                  pl.BlockSpec((LANES, 1), lambda bi, hi: (0, 0))],
        out_specs=pl.BlockSpec((1, s, hps * LANES), lambda bi, hi: (bi, 0, hi)),
        out_shape=jax.ShapeDtypeStruct((b, s, w), BF16),
        scratch_shapes=[pltpu.VMEM((hps, TK, 2 * TQ), F32), pltpu.VMEM((hps, TK, 2 * TQ), BF16),
                        pltpu.VMEM((hps, LANES, 2 * TQ), BF16), pltpu.VMEM((TK, TQ), F32)],
        compiler_params=_cparams(2),
        name="attn",
    )(lam, qt, k, vt, subln_g_col)


def _mixout_kernel(att_ref, sg_ref, mc_ref, x_ref, mod_ref, wua_ref, wo_ref, g_ref, wrt_ref,
                   br_ref, x1_ref, h2c_ref, bkt_ref, rank_ref, cnt_ref, tri_ref, carry_ref, *, d):
    tm = x_ref.shape[0]
    i = pl.program_id(0)

    @pl.when(i == 0)
    def _():
        carry_ref[...] = jnp.zeros_like(carry_ref)
        earlier = (lax.broadcasted_iota(I32, (tm, 1), 0)
                   <= lax.broadcasted_iota(I32, (1, tm), 1))
        tri_ref[...] = jnp.where(earlier, 1.0, 0.0).astype(BF16)

    mod = mod_ref[0]
    g_m = mod[:, 2 * d:3 * d]
    sh_f = mod[:, 3 * d:4 * d]
    sc_f = mod[:, 4 * d:5 * d]

    attp = jnp.dot(att_ref[...], wua_ref[...], preferred_element_type=F32)
    merged = sg_ref[...].astype(F32) * attp + mc_ref[...].astype(F32)
    mo = jnp.dot(merged.astype(BF16), wo_ref[...], preferred_element_type=F32)
    x1 = x_ref[...] + g_m * mo
    x1_ref[...] = x1
    ms = jnp.mean(x1 * x1, axis=-1, keepdims=True)
    h2 = (x1 * lax.rsqrt(ms + EPS) * g_ref[...]) * (1.0 + sc_f) + sh_f

    def split(a):
        hi = a.astype(BF16)
        return hi, (a - hi.astype(F32)).astype(BF16)

    def dot_t(a, b):
        return lax.dot_general(a, b, (((1,), (1,)), ((), ())), preferred_element_type=F32)

    w_hi, w_lo = split(wrt_ref[...])
    h_hi, h_lo = split(h2)
    by_hi = dot_t(jnp.concatenate([w_hi, w_lo], axis=0), h_hi)
    lt = (by_hi[0:ROUTER_ROWS] + (dot_t(w_hi, h_lo) + by_hi[ROUTER_ROWS:2 * ROUTER_ROWS])
          + br_ref[...])
    ridx = lax.broadcasted_iota(I32, (8, 1), 0)
    n_e = N_GROUPS * EXPERTS_PER_GROUP
    is_grp = ridx < N_GROUPS
    gl = jnp.where(is_grp, lt[n_e:n_e + 8, :], MASK_VALUE)
    gmax = jnp.max(gl, axis=0, keepdims=True)
    gsum = jnp.sum(jnp.where(is_grp, jnp.exp(gl - gmax), 0.0), axis=0, keepdims=True)
    g_w = 1.0 / gsum
    gidx = jnp.min(jnp.where(gl == gmax, ridx, 8), axis=0, keepdims=True)
    e_in = jnp.zeros((8, tm), F32)
    for g in range(N_GROUPS):
        e_in = jnp.where(gidx == g, lt[g * 8:(g + 1) * 8, :], e_in)
    t1 = jnp.max(e_in, axis=0, keepdims=True)
    i1 = jnp.min(jnp.where(e_in == t1, ridx, 8), axis=0, keepdims=True)
    e_in2 = jnp.where(ridx == i1, MASK_VALUE, e_in)
    t2 = jnp.max(e_in2, axis=0, keepdims=True)
    i2 = jnp.min(jnp.where(e_in2 == t2, ridx, 8), axis=0, keepdims=True)
    e2 = jnp.exp(t2 - t1)
    den = 1.0 + e2
    w1 = (1.0 / den) * g_w
    w2 = (e2 / den) * g_w
    first_lower = i1 < i2
    e_lo = jnp.minimum(i1, i2)
    e_hi = jnp.maximum(i1, i2)
    cw_t = jnp.where(ridx == 0, jnp.where(first_lower, w1, w2),
                     jnp.where(ridx == 1, jnp.where(first_lower, w2, w1), 0.0))
    cw_col = jnp.concatenate([cw_t, jnp.zeros((LANES - 8, tm), F32)], axis=0).T
    pair = (lax.shift_right_logical(e_lo * (2 * EXPERTS_PER_GROUP - 1 - e_lo), 1)
            + (e_hi - e_lo - 1))
    bucket = gidx * N_PAIRS + pair

    bidx = lax.broadcasted_iota(I32, (LANES, 1), 0)
    onehot = jnp.where(bidx == bucket, 1.0, 0.0)
    incl = jnp.dot(onehot.astype(BF16), tri_ref[...], preferred_element_type=F32)
    run = carry_ref[...][:, 0:1]
    rank = jnp.sum(onehot * (run + incl - 1.0), axis=0, keepdims=True).astype(I32)
    new_run = run + incl[:, tm - 1:tm]
    carry_ref[...] = jnp.broadcast_to(new_run, carry_ref.shape)
    cnt_ref[...] = jnp.broadcast_to(new_run, cnt_ref.shape).astype(I32)
    bkt_ref[0] = bucket
    rank_ref[0] = rank
    h2c_ref[:, 0:d] = h2
    h2c_ref[:, d:d + LANES] = cw_col


def _mixout(att, sg, mc, x, mod3, w_up_att, w_out, norm_g, wrt, br, tiles_per_batch):
    t, d = x.shape
    tm = TM_MIX
    dc = d + LANES
    assert N_BUCKETS <= LANES
    tok = lambda w: pl.BlockSpec((tm, w), lambda i: (i, 0))
    full = lambda a: pl.BlockSpec(a.shape, lambda i: (0,) * a.ndim)
    lane_row = pl.BlockSpec((1, 1, tm), lambda i: (i, 0, 0))
    return pl.pallas_call(
        functools.partial(_mixout_kernel, d=d),
        grid=(t // tm,),
        in_specs=[tok(att.shape[1]), tok(d), tok(d), tok(d),
                  pl.BlockSpec((1, 1, mod3.shape[2]), lambda i: (i // tiles_per_batch, 0, 0)),
                  full(w_up_att), full(w_out), full(norm_g), full(wrt), full(br)],
        out_specs=[tok(d), tok(dc), lane_row, lane_row,
                   pl.BlockSpec((LANES, LANES), lambda i: (0, 0))],
        out_shape=[jax.ShapeDtypeStruct((t, d), F32),
                   jax.ShapeDtypeStruct((t, dc), F32),
                   jax.ShapeDtypeStruct((t // tm, 1, tm), I32),
                   jax.ShapeDtypeStruct((t // tm, 1, tm), I32),
                   jax.ShapeDtypeStruct((LANES, LANES), I32)],
        scratch_shapes=[pltpu.VMEM((tm, tm), BF16), pltpu.VMEM((LANES, LANES), F32)],
        compiler_params=_cparams(1),
        name="mixout",
    )(att, sg, mc, x, mod3, w_up_att, w_out, norm_g, wrt, br)


def _pair_tables():
    lo = [a for a in range(EXPERTS_PER_GROUP) for _ in range(a + 1, EXPERTS_PER_GROUP)]
    hi = [b for a in range(EXPERTS_PER_GROUP) for b in range(a + 1, EXPERTS_PER_GROUP)]
    return jnp.array(lo, I32), jnp.array(hi, I32)


def _routing_tables(cnt, bkt, rank, tmf, n_steps):
    c = cnt[:N_BUCKETS, 0]
    n_tile = (c + (tmf - 1)) // tmf
    tile_end = jnp.cumsum(n_tile)
    total = tile_end[-1]
    pos = ((tile_end - n_tile) * tmf)[bkt] + rank
    step = jnp.minimum(jnp.arange(n_steps, dtype=I32), total - 1)
    b_of = jnp.minimum(jnp.searchsorted(tile_end, step, side="right"), N_BUCKETS - 1).astype(I32)
    pair_lo, pair_hi = _pair_tables()
    first = (b_of // N_PAIRS) * EXPERTS_PER_GROUP
    ea = first + pair_lo[b_of % N_PAIRS]
    eb = first + pair_hi[b_of % N_PAIRS]
    last_tile = jnp.where(n_tile > 0, tile_end - 1, -1)
    return (pos.astype(I32), ea.astype(I32), eb.astype(I32), total.reshape(1).astype(I32),
            last_tile.astype(I32))


def _scatter_kernel(pos_ref, zt_ref, h2c_ref, hs_ref, zero_ref, sem, zsem, *, tmf):
    tm = h2c_ref.shape[0]
    i = pl.program_id(0)

    @pl.when(i == 0)
    def _():
        zero_ref[...] = jnp.zeros_like(zero_ref)

        def zero_tile(b):
            start = pl.multiple_of(zt_ref[b] * tmf, tmf)
            return pltpu.make_async_copy(zero_ref, hs_ref.at[pl.ds(start, tmf), :], zsem)

        for b in range(N_BUCKETS):
            @pl.when(zt_ref[b] >= 0)
            def _():
                zero_tile(b).start()
        for b in range(N_BUCKETS):
            @pl.when(zt_ref[b] >= 0)
            def _():
                zero_tile(b).wait()

    base = i * tm
    for r in range(tm):
        pltpu.make_async_copy(h2c_ref.at[pl.ds(r, 1), :],
                              hs_ref.at[pl.ds(pos_ref[base + r], 1), :], sem).start()
    pltpu.make_async_copy(h2c_ref, hs_ref.at[pl.ds(0, tm), :], sem).wait()


def _scatter(pos, last_tile, h2c, rows, tmf):
    t, dc = h2c.shape
    tm = TM_MIX
    gs = pltpu.PrefetchScalarGridSpec(
        num_scalar_prefetch=2,
        grid=(t // tm,),
        in_specs=[pl.BlockSpec((tm, dc), lambda i, p, z: (i, 0))],
        out_specs=pl.BlockSpec(memory_space=pl.ANY),
        scratch_shapes=[pltpu.VMEM((tmf, dc), F32), pltpu.SemaphoreType.DMA,
                        pltpu.SemaphoreType.DMA],
    )
    return pl.pallas_call(
        functools.partial(_scatter_kernel, tmf=tmf),
        grid_spec=gs,
        out_shape=jax.ShapeDtypeStruct((rows, dc), F32),
        compiler_params=_cparams(1),
        name="scatter",
    )(pos, last_tile, h2c)


def _ffn_kernel(ea_ref, eb_ref, tot_ref, hs_ref, wga_ref, wua_ref, wda_ref, wgb_ref, wub_ref,
                wdb_ref, y_ref, *, d):
    @pl.when(pl.program_id(0) < tot_ref[0])
    def _():
        h = hs_ref[:, 0:d].astype(BF16)
        cw = hs_ref[:, d:d + LANES]
        acc = None
        for lane, (wg, wu, wd) in enumerate(((wga_ref, wua_ref, wda_ref),
                                             (wgb_ref, wub_ref, wdb_ref))):
            hg = jnp.dot(h, wg[0], preferred_element_type=F32)
            hu = jnp.dot(h, wu[0], preferred_element_type=F32)
            a = hg * jax.nn.sigmoid(hg) * hu * cw[:, lane:lane + 1]
            y = jnp.dot(a.astype(BF16), wd[0], preferred_element_type=F32)
            acc = y if acc is None else acc + y
        y_ref[...] = acc


def _ffn(ea, eb, total, hs, w_gate, w_up, w_down, tmf):
    rows, dc = hs.shape
    d = dc - LANES
    ff = w_gate.shape[2]
    row_map = lambda s, a, b, n: (jnp.minimum(s, n[0] - 1), 0)
    a_map = lambda s, a, b, n: (a[s], 0, 0)
    b_map = lambda s, a, b, n: (b[s], 0, 0)
    gs = pltpu.PrefetchScalarGridSpec(
        num_scalar_prefetch=3,
        grid=(rows // tmf,),
        in_specs=[pl.BlockSpec((tmf, dc), row_map),
                  pl.BlockSpec((1, d, ff), a_map), pl.BlockSpec((1, d, ff), a_map),
                  pl.BlockSpec((1, ff, d), a_map),
                  pl.BlockSpec((1, d, ff), b_map), pl.BlockSpec((1, d, ff), b_map),
                  pl.BlockSpec((1, ff, d), b_map)],
        out_specs=pl.BlockSpec((tmf, d), row_map),
    )
    return pl.pallas_call(
        functools.partial(_ffn_kernel, d=d),
        grid_spec=gs,
        out_shape=jax.ShapeDtypeStruct((rows, d), F32),
        compiler_params=_cparams(1),
        name="ffn",
    )(ea, eb, total, hs, w_gate, w_up, w_down, w_gate, w_up, w_down)


def _final_kernel(pos_ref, ys_ref, x1_ref, mod_ref, g_ref, o_ref, ybuf_ref, sem, *, d, normalize):
    tm = x1_ref.shape[0]
    i = pl.program_id(0)

    def gather_tile(tile):
        slot = tile & 1
        base = tile * tm
        for r in range(tm):
            pltpu.make_async_copy(ys_ref.at[pl.ds(pos_ref[base + r], 1), :],
                                  ybuf_ref.at[slot, pl.ds(r, 1), :], sem.at[slot]).start()

    @pl.when(i == 0)
    def _():
        gather_tile(i)

    @pl.when(i + 1 < pl.num_programs(0))
    def _():
        gather_tile(i + 1)

    slot = i & 1
    pltpu.make_async_copy(ys_ref.at[pl.ds(0, tm), :], ybuf_ref.at[slot], sem.at[slot]).wait()

    g_f = mod_ref[0][:, 5 * d:6 * d]
    x2 = x1_ref[...] + g_f * ybuf_ref[slot]
    if normalize:
        x2 = x2 * lax.rsqrt(jnp.mean(x2 * x2, axis=-1, keepdims=True) + EPS) * g_ref[...]
    o_ref[...] = x2


def _final(pos, ys, x1, mod3, final_g, tiles_per_batch, normalize):
    t, d = x1.shape
    tm = TM_FIN
    gs = pltpu.PrefetchScalarGridSpec(
        num_scalar_prefetch=1,
        grid=(t // tm,),
        in_specs=[pl.BlockSpec(memory_space=pl.ANY),
                  pl.BlockSpec((tm, d), lambda i, p: (i, 0)),
                  pl.BlockSpec((1, 1, mod3.shape[2]), lambda i, p: (i // tiles_per_batch, 0, 0)),
                  pl.BlockSpec((1, d), lambda i, p: (0, 0))],
        out_specs=pl.BlockSpec((tm, d), lambda i, p: (i, 0)),
        scratch_shapes=[pltpu.VMEM((2, tm, d), F32), pltpu.SemaphoreType.DMA((2,))],
    )
    return pl.pallas_call(
        functools.partial(_final_kernel, d=d, normalize=normalize),
        grid_spec=gs,
        out_shape=jax.ShapeDtypeStruct((t, d), F32),
        compiler_params=_cparams(1),
        name="final",
    )(pos, ys, x1, mod3, final_g)


def kernel(x, c, positions, w_ada, b_ada, norm_mix_g, w_in, lambda_q1, lambda_k1, lambda_q2,
           lambda_k2, subln_g, conv_w, w_up_att, w_up_conv, w_out, norm_ffn_g, w_group_router,
           b_group_router, w_expert_router, b_expert_router, w_gate, w_up, w_down, final_norm_g):
    b, s, d = x.shape
    depth = w_ada.shape[0]
    t = b * s
    assert s % TQ == 0 and s % TM_IN == 0 and TQ == TK and TM_IN % TK == 0
    assert TQ % TM_IN == 0 or TM_IN % TQ == 0
    assert N_HEADS % ATTN_HEADS_PER_STEP == 0
    assert s % TM_MIX == 0
    assert TM_MIX == TM_FIN and t % TM_FFN == 0
    n_ffn_steps = t // TM_FFN + N_BUCKETS
    n_e = N_GROUPS * EXPERTS_PER_GROUP

    c8 = jnp.concatenate([c, jnp.zeros((8 - b, d), F32)], axis=0)
    pos3 = positions.reshape(b, s, 1)
    inv_freq = ROPE_THETA ** (-jnp.arange(0, HEAD_DIM, 2, dtype=F32) / HEAD_DIM)
    invf = jnp.tile(inv_freq, LANES // (HEAD_DIM // 2)).reshape(1, LANES)

    xf = x
    for l in range(depth):
        lambda_init = 0.8 - 0.6 * math.exp(-0.3 * l)
        row = lambda a: a[l].reshape(1, -1)
        mod8, lam = _ada(c8, w_ada[l], row(b_ada), row(lambda_q1), row(lambda_k1),
                         row(lambda_q2), row(lambda_k2), lambda_init)
        mod3 = mod8[0:b].reshape(b, 1, 6 * d)

        qt, k, vt, sg, mc = _inproj(xf.reshape(b, s, d), pos3, mod3, row(norm_mix_g), invf,
                                    w_in[l].astype(BF16), conv_w[l], w_up_conv[l].astype(BF16))
        att = _attn(lam, qt, k, vt, subln_g[l].reshape(-1, 1), 1.0 - lambda_init)

        pad = ROUTER_ROWS - n_e - N_GROUPS
        wrt = jnp.concatenate([w_expert_router[l].T, w_group_router[l].T,
                               jnp.zeros((pad, d), F32)], axis=0)
        br = jnp.concatenate([b_expert_router[l], b_group_router[l],
                              jnp.zeros((pad,), F32)]).reshape(ROUTER_ROWS, 1)
        x1, h2c, bkt, rank, cnt = _mixout(att.reshape(t, -1), sg.reshape(t, d), mc.reshape(t, d),
                                          xf.reshape(t, d), mod3, w_up_att[l].astype(BF16),
                                          w_out[l].astype(BF16), row(norm_ffn_g), wrt, br,
                                          s // TM_MIX)
        pos, ea, eb, total, last_tile = _routing_tables(cnt, bkt.reshape(t), rank.reshape(t),
                                                        TM_FFN, n_ffn_steps)
        hs = _scatter(pos, last_tile, h2c, n_ffn_steps * TM_FFN, TM_FFN)
        ys = _ffn(ea, eb, total, hs, w_gate[l].astype(BF16), w_up[l].astype(BF16),
                  w_down[l].astype(BF16), TM_FFN)
        xf = _final(pos, ys, x1, mod3, final_norm_g.reshape(1, d), s // TM_FIN, l == depth - 1)
    return xf.reshape(b, s, d)
```

```python
import functools
import math

import jax
import jax.numpy as jnp
from jax import lax
from jax.experimental import pallas as pl
from jax.experimental.pallas import tpu as pltpu

F32 = jnp.float32
BF16 = jnp.bfloat16
I32 = jnp.int32

EPS = 1e-6
MASK_VALUE = -1e30
ROPE_THETA = 10000.0

N_HEADS = 4
HEAD_DIM = 64
LANES = 128
N_GROUPS = 4
EXPERTS_PER_GROUP = 8
N_PAIRS = EXPERTS_PER_GROUP * (EXPERTS_PER_GROUP - 1) // 2
N_BUCKETS = N_GROUPS * N_PAIRS
CONV_K = 3
ROUTER_ROWS = 48

VMEM_LIMIT = 56 * 1024 * 1024

TM_IN = 1024
TQ = 512
TK = 512
ATTN_HEADS_PER_STEP = 1
VT_ROWS = LANES + 16
TM_MIX = 512
TM_FFN = 256
TM_FIN = 512
TN_ADA = 1024


def _cparams(n_axes):
    return pltpu.CompilerParams(
        dimension_semantics=("arbitrary",) * n_axes, vmem_limit_bytes=VMEM_LIMIT)


def _ada_kernel(c_ref, w_ref, b_ref, lq1_ref, lk1_ref, lq2_ref, lk2_ref, mod_ref, lam_ref, *,
                lambda_init):
    c = c_ref[...]
    a = c * jax.nn.sigmoid(c)
    mod_ref[...] = jnp.dot(a.astype(BF16), w_ref[...].astype(BF16),
                           preferred_element_type=F32) + b_ref[...]
    s1 = jnp.sum(lq1_ref[...] * lk1_ref[...], axis=-1, keepdims=True)
    s2 = jnp.sum(lq2_ref[...] * lk2_ref[...], axis=-1, keepdims=True)
    lam = jnp.exp(s1) - jnp.exp(s2) + lambda_init
    lam_ref[...] = jnp.broadcast_to(lam, lam_ref.shape)


def _ada(c8, w_ada, b_ada, lq1, lk1, lq2, lk2, lambda_init):
    d, n = w_ada.shape
    small = pl.BlockSpec((1, HEAD_DIM), lambda j: (0, 0))
    return pl.pallas_call(
        functools.partial(_ada_kernel, lambda_init=lambda_init),
        grid=(n // TN_ADA,),
        in_specs=[pl.BlockSpec((8, d), lambda j: (0, 0)),
                  pl.BlockSpec((d, TN_ADA), lambda j: (0, j)),
                  pl.BlockSpec((1, TN_ADA), lambda j: (0, j)),
                  small, small, small, small],
        out_specs=[pl.BlockSpec((8, TN_ADA), lambda j: (0, j)),
                   pl.BlockSpec((8, LANES), lambda j: (0, 0))],
        out_shape=[jax.ShapeDtypeStruct((8, n), F32), jax.ShapeDtypeStruct((8, LANES), F32)],
        compiler_params=_cparams(1),
        name="ada",
    )(c8, w_ada, b_ada, lq1, lk1, lq2, lk2)


def _inproj_kernel(x_ref, pos_ref, mod_ref, g_ref, invf_ref, w_ref, cw_ref, wupc_ref,
                   qt_ref, k_ref, vt_ref, sg_ref, mc_ref, carry_ref, *, d, qk_w, conv_w):
    tm = x_ref.shape[1]
    x = x_ref[0]
    ms = jnp.mean(x * x, axis=-1, keepdims=True)
    mod = mod_ref[0]
    sh = mod[:, 0:d]
    sc = mod[:, d:2 * d]
    h = (x * lax.rsqrt(ms + EPS) * g_ref[...]) * (1.0 + sc) + sh
    hb = h.astype(BF16)

    def proj(a, b):
        return jnp.dot(hb, w_ref[:, a:b], preferred_element_type=F32)

    lane = lax.broadcasted_iota(I32, (1, LANES), 1)
    first_half = (lane % HEAD_DIM) < (HEAD_DIM // 2)
    ang = pos_ref[0].astype(F32) * invf_ref[...]
    cos = jnp.cos(ang)
    sin = jnp.sin(ang)
    sin_signed = jnp.where(first_half, -sin, sin)

    def rope(t):
        partner = jnp.where(first_half, pltpu.roll(t, LANES - HEAD_DIM // 2, 1),
                            pltpu.roll(t, HEAD_DIM // 2, 1))
        return t * cos + partner * sin_signed

    zq = proj(0, qk_w)
    zk = proj(qk_w, 2 * qk_w)
    scale = HEAD_DIM ** -0.5 * math.log2(math.e)
    zv = proj(2 * qk_w, 3 * qk_w)
    for hd in range(qk_w // LANES):
        sl = slice(hd * LANES, (hd + 1) * LANES)
        rq = rope(zq[:, sl]) * scale
        qw = qt_ref.shape[4]
        for ck in range(tm // qw):
            qt_ref[0, hd, ck] = rq[ck * qw:(ck + 1) * qw, :].T.astype(BF16)
        k_ref[0, :, sl] = rope(zk[:, sl]).astype(BF16)
        for ck in range(tm // TK):
            vt_ref[0, hd, ck, 0:LANES, :] = zv[ck * TK:(ck + 1) * TK, sl].T.astype(BF16)
            vt_ref[0, hd, ck, LANES:VT_ROWS, :] = jnp.ones((VT_ROWS - LANES, TK), BF16)

    o = 3 * qk_w
    cb = proj(o, o + conv_w)
    u = proj(o + conv_w, o + 2 * conv_w) * proj(o + 2 * conv_w, o + 3 * conv_w)

    @pl.when(pl.program_id(1) == 0)
    def _():
        carry_ref[...] = jnp.zeros_like(carry_ref)

    prev = carry_ref[...]
    row = lax.broadcasted_iota(I32, (tm, 1), 0)
    u1 = jnp.where(row == 0, prev[7:8, :], pltpu.roll(u, 1, 0))
    u2 = jnp.where(row == 0, prev[6:7, :],
                   jnp.where(row == 1, prev[7:8, :], pltpu.roll(u, 2, 0)))
    carry_ref[...] = u[tm - 8:tm, :]
    cw = cw_ref[...]
    conv = cb * (cw[0:1, :] * u + cw[1:2, :] * u1 + cw[2:3, :] * u2)
    convp = jnp.dot(conv.astype(BF16), wupc_ref[...], preferred_element_type=F32)

    o = o + 3 * conv_w
    sg_ref[0] = jax.nn.sigmoid(proj(o, o + d)).astype(BF16)
    mc_ref[0] = (jax.nn.sigmoid(proj(o + d, o + 2 * d)) * convp).astype(BF16)


def _inproj(x, pos3, mod3, norm_g, invf, w_in, conv_w, w_up_conv):
    b, s, d = x.shape
    qk_w = N_HEADS * 2 * HEAD_DIM
    cw = conv_w.shape[1]
    tm = TM_IN
    if tm >= TQ:
        qt_spec = pl.BlockSpec((1, N_HEADS, tm // TQ, LANES, TQ), lambda bi, si: (bi, 0, si, 0, 0))
    else:
        qt_spec = pl.BlockSpec((1, N_HEADS, 1, LANES, tm),
                               lambda bi, si: (bi, 0, si // (TQ // tm), 0, si % (TQ // tm)))
    tok = lambda w: pl.BlockSpec((1, tm, w), lambda bi, si: (bi, si, 0))
    full = lambda a: pl.BlockSpec(a.shape, lambda bi, si: (0,) * a.ndim)
    return pl.pallas_call(
        functools.partial(_inproj_kernel, d=d, qk_w=qk_w, conv_w=cw),
        grid=(b, s // tm),
        in_specs=[tok(d), tok(1),
                  pl.BlockSpec((1, 1, mod3.shape[2]), lambda bi, si: (bi, 0, 0)),
                  full(norm_g), full(invf), full(w_in), full(conv_w), full(w_up_conv)],
        out_specs=[qt_spec,
                   tok(qk_w),
                   pl.BlockSpec((1, N_HEADS, tm // TK, VT_ROWS, TK),
                                lambda bi, si: (bi, 0, si, 0, 0)),
                   tok(d), tok(d)],
        out_shape=[jax.ShapeDtypeStruct((b, N_HEADS, s // TQ, LANES, TQ), BF16),
                   jax.ShapeDtypeStruct((b, s, qk_w), BF16),
                   jax.ShapeDtypeStruct((b, N_HEADS, s // TK, VT_ROWS, TK), BF16),
                   jax.ShapeDtypeStruct((b, s, d), BF16),
                   jax.ShapeDtypeStruct((b, s, d), BF16)],
        scratch_shapes=[pltpu.VMEM((8, cw), F32)],
        compiler_params=_cparams(2),
        name="inproj",
    )(x, pos3, mod3, norm_g, invf, w_in, conv_w, w_up_conv)


def _attn_kernel(lam_ref, qt_ref, k_ref, vt_ref, g_ref, o_ref, s_ref, p_ref, qq_ref, bias_ref, *,
                 out_scale):
    hps, n_q, tq = qt_ref.shape[1], qt_ref.shape[2], qt_ref.shape[4]
    feat = lax.broadcasted_iota(I32, (LANES, 1), 0)
    halves = [slice(h * tq, (h + 1) * tq) for h in range(2)]
    streams = [(h, half) for h in range(hps) for half in halves]

    k_sub = lax.broadcasted_iota(I32, (TK, 1), 0)
    q_lane = lax.broadcasted_iota(I32, (1, tq), 1)
    bias_ref[...] = jnp.where(k_sub <= q_lane, 0.0, MASK_VALUE)

    def load_queries(qi):
        for h in range(hps):
            qt = qt_ref[0, h, qi]
            zero = jnp.zeros_like(qt)
            qq_ref[h, :, halves[0]] = jnp.where(feat < HEAD_DIM, qt, zero)
            qq_ref[h, :, halves[1]] = jnp.where(feat >= HEAD_DIM, qt, zero)

    def scores(j):
        start = j * TK
        col_max = []
        for h, lanes in streams:
            kt = k_ref[0, pl.ds(start, TK), h * LANES:(h + 1) * LANES]
            s_ref[h, :, lanes] = jnp.dot(kt, qq_ref[h, :, lanes], preferred_element_type=F32)
        return None

    def values(j, state):
        return tuple((m, alpha * acc + jnp.dot(vt_ref[0, h, j], p_ref[h, :, lanes],
                                               preferred_element_type=F32), alpha)
                     for (h, lanes), (m, acc, alpha) in zip(streams, state))

    def softmax(col_max, state, diagonal):
        out = []
        for i, ((h, lanes), (m, acc, _)) in enumerate(zip(streams, state)):
            load = ((lambda: s_ref[h, :, lanes] + bias_ref[...]) if diagonal
                    else (lambda: s_ref[h, :, lanes]))
            m_new = jnp.maximum(m, jnp.max(load(), axis=0, keepdims=True))
            p_ref[h, :, lanes] = jnp.exp2(load() - m_new).astype(BF16)
            out.append((m_new, acc, jnp.exp2(m - m_new)))
        return tuple(out)

    def finish(qi, last_tile, state):
        state = values(last_tile, state)
        row0 = qi * tq
        for h in range(hps):
            (_, acc1, _), (_, acc2, _) = state[2 * h], state[2 * h + 1]
            o1 = acc1[0:LANES] / acc1[LANES:LANES + 1]
            o2 = acc2[0:LANES] / acc2[LANES:LANES + 1]
            a = o1 - lam_ref[0:1, 0:1] * o2
            y = a * lax.rsqrt(jnp.mean(a * a, axis=0, keepdims=True) + EPS) * g_ref[...]
            o_ref[0, pl.ds(row0, tq), h * LANES:(h + 1) * LANES] = (
                (y * out_scale).T.astype(BF16))

    def next_diagonal(qi):
        if qi + 1 < n_q:
            load_queries(qi + 1)
            scores(qi + 1)

    one = (jnp.full((1, tq), MASK_VALUE, F32), jnp.zeros((VT_ROWS, tq), F32),
           jnp.ones((1, tq), F32))
    init = tuple(one for _ in streams)

    load_queries(0)
    scores(0)
    pending = softmax(None, init, True)
    next_diagonal(0)

    for qi in range(1, n_q):
        finish(qi - 1, max(qi - 2, 0), pending)
        state = softmax(None, init, True)
        col_max = scores(0)
        for t in range(1, qi + 1):
            state = values(qi if t == 1 else t - 2, state)
            state = softmax(col_max, state, False)
            if t < qi:
                col_max = scores(t)
            else:
                next_diagonal(qi)
        pending = state
    finish(n_q - 1, max(n_q - 2, 0), pending)


def _attn(lam, qt, k, vt, subln_g_col, out_scale):
    b, s, w = k.shape
    hps = ATTN_HEADS_PER_STEP
    return pl.pallas_call(
        functools.partial(_attn_kernel, out_scale=out_scale),
        grid=(b, N_HEADS // hps),
        in_specs=[pl.BlockSpec((8, LANES), lambda bi, hi: (0, 0)),
                  pl.BlockSpec((1, hps, s // TQ, LANES, TQ), lambda bi, hi: (bi, hi, 0, 0, 0)),
                  pl.BlockSpec((1, s, hps * LANES), lambda bi, hi: (bi, 0, hi)),
                  pl.BlockSpec((1, hps, s // TK, VT_ROWS, TK), lambda bi, hi: (bi, hi, 0, 0, 0)),
                  pl.BlockSpec((LANES, 1), lambda bi, hi: (0, 0))],
        out_specs=pl.BlockSpec((1, s, hps * LANES), lambda bi, hi: (bi, 0, hi)),
        out_shape=jax.ShapeDtypeStruct((b, s, w), BF16),
        scratch_shapes=[pltpu.VMEM((hps, TK, 2 * TQ), F32), pltpu.VMEM((hps, TK, 2 * TQ), BF16),
                        pltpu.VMEM((hps, LANES, 2 * TQ), BF16), pltpu.VMEM((TK, TQ), F32)],
        compiler_params=_cparams(2),
        name="attn",
    )(lam, qt, k, vt, subln_g_col)


def _mixout_kernel(att_ref, sg_ref, mc_ref, x_ref, mod_ref, wua_ref, wo_ref, g_ref, wrt_ref,
                   br_ref, x1_ref, h2c_ref, bkt_ref, rank_ref, cnt_ref, tri_ref, carry_ref, *, d):
    tm = x_ref.shape[0]
    i = pl.program_id(0)

    @pl.when(i == 0)
    def _():
        carry_ref[...] = jnp.zeros_like(carry_ref)
        earlier = (lax.broadcasted_iota(I32, (tm, 1), 0)
                   <= lax.broadcasted_iota(I32, (1, tm), 1))
        tri_ref[...] = jnp.where(earlier, 1.0, 0.0).astype(BF16)

    mod = mod_ref[0]
    g_m = mod[:, 2 * d:3 * d]
    sh_f = mod[:, 3 * d:4 * d]
    sc_f = mod[:, 4 * d:5 * d]

    attp = jnp.dot(att_ref[...], wua_ref[...], preferred_element_type=F32)
    merged = sg_ref[...].astype(F32) * attp + mc_ref[...].astype(F32)
    mo = jnp.dot(merged.astype(BF16), wo_ref[...], preferred_element_type=F32)
    x1 = x_ref[...] + g_m * mo
    x1_ref[...] = x1
    ms = jnp.mean(x1 * x1, axis=-1, keepdims=True)
    h2 = (x1 * lax.rsqrt(ms + EPS) * g_ref[...]) * (1.0 + sc_f) + sh_f

    def split(a):
        hi = a.astype(BF16)
        return hi, (a - hi.astype(F32)).astype(BF16)

    def dot_t(a, b):
        return lax.dot_general(a, b, (((1,), (1,)), ((), ())), preferred_element_type=F32)

    w_hi, w_lo = split(wrt_ref[...])
    h_hi, h_lo = split(h2)
    by_hi = dot_t(jnp.concatenate([w_hi, w_lo], axis=0), h_hi)
    lt = (by_hi[0:ROUTER_ROWS] + (dot_t(w_hi, h_lo) + by_hi[ROUTER_ROWS:2 * ROUTER_ROWS])
          + br_ref[...])
    ridx = lax.broadcasted_iota(I32, (8, 1), 0)
    n_e = N_GROUPS * EXPERTS_PER_GROUP
    is_grp = ridx < N_GROUPS
    gl = jnp.where(is_grp, lt[n_e:n_e + 8, :], MASK_VALUE)
    gmax = jnp.max(gl, axis=0, keepdims=True)
    gsum = jnp.sum(jnp.where(is_grp, jnp.exp(gl - gmax), 0.0), axis=0, keepdims=True)
    g_w = 1.0 / gsum
    gidx = jnp.min(jnp.where(gl == gmax, ridx, 8), axis=0, keepdims=True)
    e_in = jnp.zeros((8, tm), F32)
    for g in range(N_GROUPS):
        e_in = jnp.where(gidx == g, lt[g * 8:(g + 1) * 8, :], e_in)
    t1 = jnp.max(e_in, axis=0, keepdims=True)
    i1 = jnp.min(jnp.where(e_in == t1, ridx, 8), axis=0, keepdims=True)
    e_in2 = jnp.where(ridx == i1, MASK_VALUE, e_in)
    t2 = jnp.max(e_in2, axis=0, keepdims=True)
    i2 = jnp.min(jnp.where(e_in2 == t2, ridx, 8), axis=0, keepdims=True)
    e2 = jnp.exp(t2 - t1)
    den = 1.0 + e2
    w1 = (1.0 / den) * g_w
    w2 = (e2 / den) * g_w
    first_lower = i1 < i2
    e_lo = jnp.minimum(i1, i2)
    e_hi = jnp.maximum(i1, i2)
    cw_t = jnp.where(ridx == 0, jnp.where(first_lower, w1, w2),
                     jnp.where(ridx == 1, jnp.where(first_lower, w2, w1), 0.0))
    cw_col = jnp.concatenate([cw_t, jnp.zeros((LANES - 8, tm), F32)], axis=0).T
    pair = (lax.shift_right_logical(e_lo * (2 * EXPERTS_PER_GROUP - 1 - e_lo), 1)
            + (e_hi - e_lo - 1))
    bucket = gidx * N_PAIRS + pair

    bidx = lax.broadcasted_iota(I32, (LANES, 1), 0)
    onehot = jnp.where(bidx == bucket, 1.0, 0.0)
    incl = jnp.dot(onehot.astype(BF16), tri_ref[...], preferred_element_type=F32)
    run = carry_ref[...][:, 0:1]
    rank = jnp.sum(onehot * (run + incl - 1.0), axis=0, keepdims=True).astype(I32)
    new_run = run + incl[:, tm - 1:tm]
    carry_ref[...] = jnp.broadcast_to(new_run, carry_ref.shape)
    cnt_ref[...] = jnp.broadcast_to(new_run, cnt_ref.shape).astype(I32)
    bkt_ref[0] = bucket
    rank_ref[0] = rank
    h2c_ref[:, 0:d] = h2
    h2c_ref[:, d:d + LANES] = cw_col


def _mixout(att, sg, mc, x, mod3, w_up_att, w_out, norm_g, wrt, br, tiles_per_batch):
    t, d = x.shape
    tm = TM_MIX
    dc = d + LANES
    assert N_BUCKETS <= LANES
    tok = lambda w: pl.BlockSpec((tm, w), lambda i: (i, 0))
    full = lambda a: pl.BlockSpec(a.shape, lambda i: (0,) * a.ndim)
    lane_row = pl.BlockSpec((1, 1, tm), lambda i: (i, 0, 0))
    return pl.pallas_call(
        functools.partial(_mixout_kernel, d=d),
        grid=(t // tm,),
        in_specs=[tok(att.shape[1]), tok(d), tok(d), tok(d),
                  pl.BlockSpec((1, 1, mod3.shape[2]), lambda i: (i // tiles_per_batch, 0, 0)),
                  full(w_up_att), full(w_out), full(norm_g), full(wrt), full(br)],
        out_specs=[tok(d), tok(dc), lane_row, lane_row,
                   pl.BlockSpec((LANES, LANES), lambda i: (0, 0))],
        out_shape=[jax.ShapeDtypeStruct((t, d), F32),
                   jax.ShapeDtypeStruct((t, dc), F32),
                   jax.ShapeDtypeStruct((t // tm, 1, tm), I32),
                   jax.ShapeDtypeStruct((t // tm, 1, tm), I32),
                   jax.ShapeDtypeStruct((LANES, LANES), I32)],
        scratch_shapes=[pltpu.VMEM((tm, tm), BF16), pltpu.VMEM((LANES, LANES), F32)],
        compiler_params=_cparams(1),
        name="mixout",
    )(att, sg, mc, x, mod3, w_up_att, w_out, norm_g, wrt, br)


def _routing_tables(cnt, tmf, n_steps):
    c = cnt[:, 0]
    n_tile = (c + (tmf - 1)) // tmf
    tile_end = jnp.cumsum(n_tile)
    total = tile_end[N_BUCKETS - 1]
    first_row = ((tile_end - n_tile) * tmf).astype(F32).reshape(LANES, 1)
    step = jnp.minimum(jnp.arange(n_steps, dtype=I32), total - 1)
    b_of = jnp.sum((tile_end[None, :N_BUCKETS] <= step[:, None]).astype(I32), axis=1)
    b_of = jnp.minimum(b_of, N_BUCKETS - 1)
    pair = b_of % N_PAIRS
    starts = [a * (2 * EXPERTS_PER_GROUP - 1 - a) // 2 for a in range(1, EXPERTS_PER_GROUP - 1)]
    e_lo = sum((pair >= st).astype(I32) for st in starts)
    e_hi = pair - e_lo * (2 * EXPERTS_PER_GROUP - 1 - e_lo) // 2 + e_lo + 1
    first = (b_of // N_PAIRS) * EXPERTS_PER_GROUP
    last_tile = jnp.where(n_tile > 0, tile_end - 1, -1)[:N_BUCKETS]
    return (first_row, (first + e_lo).astype(I32), (first + e_hi).astype(I32),
            total.reshape(1).astype(I32), last_tile.astype(I32))


def _scatter_kernel(zt_ref, bkt_ref, rank_ref, first_ref, h2c_ref, pos_ref, hs_ref,
                    zero_ref, posv_ref, poss_ref, sem, zsem, psem, *, tmf):
    tm = h2c_ref.shape[0]
    i = pl.program_id(0)

    bidx = lax.broadcasted_iota(I32, (LANES, 1), 0)
    pos = (jnp.sum(jnp.where(bidx == bkt_ref[0], first_ref[...], 0.0), axis=0, keepdims=True)
           .astype(I32) + rank_ref[0])
    pos_ref[0] = pos
    posv_ref[...] = pos
    to_smem = pltpu.make_async_copy(posv_ref.at[0], poss_ref, psem)
    to_smem.start()

    @pl.when(i == 0)
    def _():
        zero_ref[...] = jnp.zeros_like(zero_ref)

        def zero_tile(b):
            start = pl.multiple_of(zt_ref[b] * tmf, tmf)
            return pltpu.make_async_copy(zero_ref, hs_ref.at[pl.ds(start, tmf), :], zsem)

        for b in range(N_BUCKETS):
            @pl.when(zt_ref[b] >= 0)
            def _():
                zero_tile(b).start()
        for b in range(N_BUCKETS):
            @pl.when(zt_ref[b] >= 0)
            def _():
                zero_tile(b).wait()

    to_smem.wait()
    for r in range(tm):
        pltpu.make_async_copy(h2c_ref.at[pl.ds(r, 1), :],
                              hs_ref.at[pl.ds(poss_ref[r], 1), :], sem).start()
    pltpu.make_async_copy(h2c_ref, hs_ref.at[pl.ds(0, tm), :], sem).wait()


def _scatter(last_tile, bkt, rank, first_row, h2c, rows, tmf):
    t, dc = h2c.shape
    tm = TM_MIX
    lane_row = pl.BlockSpec((1, 1, tm), lambda i, z: (i, 0, 0))
    gs = pltpu.PrefetchScalarGridSpec(
        num_scalar_prefetch=1,
        grid=(t // tm,),
        in_specs=[lane_row, lane_row,
                  pl.BlockSpec((LANES, 1), lambda i, z: (0, 0)),
                  pl.BlockSpec((tm, dc), lambda i, z: (i, 0))],
        out_specs=[lane_row, pl.BlockSpec(memory_space=pl.ANY)],
        scratch_shapes=[pltpu.VMEM((tmf, dc), F32), pltpu.VMEM((1, tm), I32),
                        pltpu.SMEM((tm,), I32), pltpu.SemaphoreType.DMA,
                        pltpu.SemaphoreType.DMA, pltpu.SemaphoreType.DMA],
    )
    return pl.pallas_call(
        functools.partial(_scatter_kernel, tmf=tmf),
        grid_spec=gs,
        out_shape=[jax.ShapeDtypeStruct((t // tm, 1, tm), I32),
                   jax.ShapeDtypeStruct((rows, dc), F32)],
        compiler_params=_cparams(1),
        name="scatter",
    )(last_tile, bkt, rank, first_row, h2c)


def _ffn_kernel(ea_ref, eb_ref, tot_ref, hs_ref, wga_ref, wua_ref, wda_ref, wgb_ref, wub_ref,
                wdb_ref, y_ref, *, d):
    @pl.when(pl.program_id(0) < tot_ref[0])
    def _():
        h = hs_ref[:, 0:d].astype(BF16)
        cw = hs_ref[:, d:d + LANES]
        acc = None
        for lane, (wg, wu, wd) in enumerate(((wga_ref, wua_ref, wda_ref),
                                             (wgb_ref, wub_ref, wdb_ref))):
            hg = jnp.dot(h, wg[0], preferred_element_type=F32)
            hu = jnp.dot(h, wu[0], preferred_element_type=F32)
            a = hg * jax.nn.sigmoid(hg) * hu * cw[:, lane:lane + 1]
            y = jnp.dot(a.astype(BF16), wd[0], preferred_element_type=F32)
            acc = y if acc is None else acc + y
        y_ref[...] = acc


def _ffn(ea, eb, total, hs, w_gate, w_up, w_down, tmf):
    rows, dc = hs.shape
    d = dc - LANES
    ff = w_gate.shape[2]
    row_map = lambda s, a, b, n: (jnp.minimum(s, n[0] - 1), 0)
    a_map = lambda s, a, b, n: (a[s], 0, 0)
    b_map = lambda s, a, b, n: (b[s], 0, 0)
    gs = pltpu.PrefetchScalarGridSpec(
        num_scalar_prefetch=3,
        grid=(rows // tmf,),
        in_specs=[pl.BlockSpec((tmf, dc), row_map),
                  pl.BlockSpec((1, d, ff), a_map), pl.BlockSpec((1, d, ff), a_map),
                  pl.BlockSpec((1, ff, d), a_map),
                  pl.BlockSpec((1, d, ff), b_map), pl.BlockSpec((1, d, ff), b_map),
                  pl.BlockSpec((1, ff, d), b_map)],
        out_specs=pl.BlockSpec((tmf, d), row_map),
    )
    return pl.pallas_call(
        functools.partial(_ffn_kernel, d=d),
        grid_spec=gs,
        out_shape=jax.ShapeDtypeStruct((rows, d), F32),
        compiler_params=_cparams(1),
        name="ffn",
    )(ea, eb, total, hs, w_gate, w_up, w_down, w_gate, w_up, w_down)


def _final_kernel(pos_ref, ys_ref, x1_ref, mod_ref, g_ref, o_ref, ybuf_ref, sem, *, d, normalize):
    tm = x1_ref.shape[0]
    i = pl.program_id(0)

    def gather_tile(tile):
        slot = tile & 1
        base = tile * tm
        for r in range(tm):
            pltpu.make_async_copy(ys_ref.at[pl.ds(pos_ref[base + r], 1), :],
                                  ybuf_ref.at[slot, pl.ds(r, 1), :], sem.at[slot]).start()

    @pl.when(i == 0)
    def _():
        gather_tile(i)

    @pl.when(i + 1 < pl.num_programs(0))
    def _():
        gather_tile(i + 1)

    slot = i & 1
    pltpu.make_async_copy(ys_ref.at[pl.ds(0, tm), :], ybuf_ref.at[slot], sem.at[slot]).wait()

    g_f = mod_ref[0][:, 5 * d:6 * d]
    x2 = x1_ref[...] + g_f * ybuf_ref[slot]
    if normalize:
        x2 = x2 * lax.rsqrt(jnp.mean(x2 * x2, axis=-1, keepdims=True) + EPS) * g_ref[...]
    o_ref[...] = x2


def _final(pos, ys, x1, mod3, final_g, tiles_per_batch, normalize):
    t, d = x1.shape
    tm = TM_FIN
    gs = pltpu.PrefetchScalarGridSpec(
        num_scalar_prefetch=1,
        grid=(t // tm,),
        in_specs=[pl.BlockSpec(memory_space=pl.ANY),
                  pl.BlockSpec((tm, d), lambda i, p: (i, 0)),
                  pl.BlockSpec((1, 1, mod3.shape[2]), lambda i, p: (i // tiles_per_batch, 0, 0)),
                  pl.BlockSpec((1, d), lambda i, p: (0, 0))],
        out_specs=pl.BlockSpec((tm, d), lambda i, p: (i, 0)),
        scratch_shapes=[pltpu.VMEM((2, tm, d), F32), pltpu.SemaphoreType.DMA((2,))],
    )
    return pl.pallas_call(
        functools.partial(_final_kernel, d=d, normalize=normalize),
        grid_spec=gs,
        out_shape=jax.ShapeDtypeStruct((t, d), F32),
        compiler_params=_cparams(1),
        name="final",
    )(pos, ys, x1, mod3, final_g)


def kernel(x, c, positions, w_ada, b_ada, norm_mix_g, w_in, lambda_q1, lambda_k1, lambda_q2,
           lambda_k2, subln_g, conv_w, w_up_att, w_up_conv, w_out, norm_ffn_g, w_group_router,
           b_group_router, w_expert_router, b_expert_router, w_gate, w_up, w_down, final_norm_g):
    b, s, d = x.shape
    depth = w_ada.shape[0]
    t = b * s
    assert s % TQ == 0 and s % TM_IN == 0 and TQ == TK and TM_IN % TK == 0
    assert TQ % TM_IN == 0 or TM_IN % TQ == 0
    assert N_HEADS % ATTN_HEADS_PER_STEP == 0
    assert s % TM_MIX == 0
    assert TM_MIX == TM_FIN and t % TM_FFN == 0
    n_ffn_steps = t // TM_FFN + N_BUCKETS
    n_e = N_GROUPS * EXPERTS_PER_GROUP

    c8 = jnp.concatenate([c, jnp.zeros((8 - b, d), F32)], axis=0)
    pos3 = positions.reshape(b, s, 1)
    inv_freq = ROPE_THETA ** (-jnp.arange(0, HEAD_DIM, 2, dtype=F32) / HEAD_DIM)
    invf = jnp.tile(inv_freq, LANES // (HEAD_DIM // 2)).reshape(1, LANES)

    xf = x
    for l in range(depth):
        lambda_init = 0.8 - 0.6 * math.exp(-0.3 * l)
        row = lambda a: a[l].reshape(1, -1)
        mod8, lam = _ada(c8, w_ada[l], row(b_ada), row(lambda_q1), row(lambda_k1),
                         row(lambda_q2), row(lambda_k2), lambda_init)
        mod3 = mod8[0:b].reshape(b, 1, 6 * d)

        qt, k, vt, sg, mc = _inproj(xf.reshape(b, s, d), pos3, mod3, row(norm_mix_g), invf,
                                    w_in[l].astype(BF16), conv_w[l], w_up_conv[l].astype(BF16))
        att = _attn(lam, qt, k, vt, subln_g[l].reshape(-1, 1), 1.0 - lambda_init)

        pad = ROUTER_ROWS - n_e - N_GROUPS
        wrt = jnp.concatenate([w_expert_router[l].T, w_group_router[l].T,
                               jnp.zeros((pad, d), F32)], axis=0)
        br = jnp.concatenate([b_expert_router[l], b_group_router[l],
                              jnp.zeros((pad,), F32)]).reshape(ROUTER_ROWS, 1)
        x1, h2c, bkt, rank, cnt = _mixout(att.reshape(t, -1), sg.reshape(t, d), mc.reshape(t, d),
                                          xf.reshape(t, d), mod3, w_up_att[l].astype(BF16),
                                          w_out[l].astype(BF16), row(norm_ffn_g), wrt, br,
                                          s // TM_MIX)
        first_row, ea, eb, total, last_tile = _routing_tables(cnt, TM_FFN, n_ffn_steps)
        pos, hs = _scatter(last_tile, bkt, rank, first_row, h2c, n_ffn_steps * TM_FFN, TM_FFN)
        ys = _ffn(ea, eb, total, hs, w_gate[l].astype(BF16), w_up[l].astype(BF16),
                  w_down[l].astype(BF16), TM_FFN)
        xf = _final(pos.reshape(t), ys, x1, mod3, final_norm_g.reshape(1, d), s // TM_FIN,
                    l == depth - 1)
    return xf.reshape(b, s, d)
```

```python
import functools
import math

import jax
import jax.numpy as jnp
from jax import lax
from jax.experimental import pallas as pl
from jax.experimental.pallas import tpu as pltpu

F32 = jnp.float32
BF16 = jnp.bfloat16
I32 = jnp.int32

EPS = 1e-6
MASK_VALUE = -1e30
ROPE_THETA = 10000.0

N_HEADS = 4
HEAD_DIM = 64
LANES = 128
N_GROUPS = 4
EXPERTS_PER_GROUP = 8
CONV_K = 3
ROUTER_ROWS = 48

VMEM_LIMIT = 56 * 1024 * 1024

TM_IN = 1024
TQ = 512
TK = 512
ATTN_HEADS_PER_STEP = 1
VT_ROWS = LANES + 16
TM_MIX = 1024
TM_FFN = 512
TM_FIN = 512
TN_ADA = 1024


def _cparams(n_axes):
    return pltpu.CompilerParams(
        dimension_semantics=("arbitrary",) * n_axes, vmem_limit_bytes=VMEM_LIMIT)


def _ada_kernel(c_ref, w_ref, b_ref, lq1_ref, lk1_ref, lq2_ref, lk2_ref, mod_ref, lam_ref, *,
                lambda_init):
    c = c_ref[...]
    a = c * jax.nn.sigmoid(c)
    mod_ref[...] = jnp.dot(a.astype(BF16), w_ref[...].astype(BF16),
                           preferred_element_type=F32) + b_ref[...]
    s1 = jnp.sum(lq1_ref[...] * lk1_ref[...], axis=-1, keepdims=True)
    s2 = jnp.sum(lq2_ref[...] * lk2_ref[...], axis=-1, keepdims=True)
    lam = jnp.exp(s1) - jnp.exp(s2) + lambda_init
    lam_ref[...] = jnp.broadcast_to(lam, lam_ref.shape)


def _ada(c8, w_ada, b_ada, lq1, lk1, lq2, lk2, lambda_init):
    d, n = w_ada.shape
    small = pl.BlockSpec((1, HEAD_DIM), lambda j: (0, 0))
    return pl.pallas_call(
        functools.partial(_ada_kernel, lambda_init=lambda_init),
        grid=(n // TN_ADA,),
        in_specs=[pl.BlockSpec((8, d), lambda j: (0, 0)),
                  pl.BlockSpec((d, TN_ADA), lambda j: (0, j)),
                  pl.BlockSpec((1, TN_ADA), lambda j: (0, j)),
                  small, small, small, small],
        out_specs=[pl.BlockSpec((8, TN_ADA), lambda j: (0, j)),
                   pl.BlockSpec((8, LANES), lambda j: (0, 0))],
        out_shape=[jax.ShapeDtypeStruct((8, n), F32), jax.ShapeDtypeStruct((8, LANES), F32)],
        compiler_params=_cparams(1),
        name="ada",
    )(c8, w_ada, b_ada, lq1, lk1, lq2, lk2)


def _inproj_kernel(x_ref, pos_ref, mod_ref, g_ref, invf_ref, w_ref, cw_ref, wupc_ref,
                   qt_ref, k_ref, vt_ref, sg_ref, mc_ref, carry_ref, *, d, qk_w, conv_w):
    tm = x_ref.shape[1]
    x = x_ref[0]
    ms = jnp.mean(x * x, axis=-1, keepdims=True)
    mod = mod_ref[0]
    sh = mod[:, 0:d]
    sc = mod[:, d:2 * d]
    h = (x * lax.rsqrt(ms + EPS) * g_ref[...]) * (1.0 + sc) + sh
    hb = h.astype(BF16)

    def proj(a, b):
        return jnp.dot(hb, w_ref[:, a:b], preferred_element_type=F32)

    lane = lax.broadcasted_iota(I32, (1, LANES), 1)
    first_half = (lane % HEAD_DIM) < (HEAD_DIM // 2)
    ang = pos_ref[0].astype(F32) * invf_ref[...]
    cos = jnp.cos(ang)
    sin = jnp.sin(ang)
    sin_signed = jnp.where(first_half, -sin, sin)

    def rope(t):
        partner = jnp.where(first_half, pltpu.roll(t, LANES - HEAD_DIM // 2, 1),
                            pltpu.roll(t, HEAD_DIM // 2, 1))
        return t * cos + partner * sin_signed

    zq = proj(0, qk_w)
    zk = proj(qk_w, 2 * qk_w)
    scale = HEAD_DIM ** -0.5 * math.log2(math.e)
    zv = proj(2 * qk_w, 3 * qk_w)
    for hd in range(qk_w // LANES):
        sl = slice(hd * LANES, (hd + 1) * LANES)
        rq = rope(zq[:, sl]) * scale
        qw = qt_ref.shape[4]
        for ck in range(tm // qw):
            qt_ref[0, hd, ck] = rq[ck * qw:(ck + 1) * qw, :].T.astype(BF16)
        k_ref[0, :, sl] = rope(zk[:, sl]).astype(BF16)
        for ck in range(tm // TK):
            vt_ref[0, hd, ck, 0:LANES, :] = zv[ck * TK:(ck + 1) * TK, sl].T.astype(BF16)
            vt_ref[0, hd, ck, LANES:VT_ROWS, :] = jnp.ones((VT_ROWS - LANES, TK), BF16)

    o = 3 * qk_w
    cb = proj(o, o + conv_w)
    u = proj(o + conv_w, o + 2 * conv_w) * proj(o + 2 * conv_w, o + 3 * conv_w)

    @pl.when(pl.program_id(1) == 0)
    def _():
        carry_ref[...] = jnp.zeros_like(carry_ref)

    prev = carry_ref[...]
    row = lax.broadcasted_iota(I32, (tm, 1), 0)
    u1 = jnp.where(row == 0, prev[7:8, :], pltpu.roll(u, 1, 0))
    u2 = jnp.where(row == 0, prev[6:7, :],
                   jnp.where(row == 1, prev[7:8, :], pltpu.roll(u, 2, 0)))
    carry_ref[...] = u[tm - 8:tm, :]
    cw = cw_ref[...]
    conv = cb * (cw[0:1, :] * u + cw[1:2, :] * u1 + cw[2:3, :] * u2)
    convp = jnp.dot(conv.astype(BF16), wupc_ref[...], preferred_element_type=F32)

    o = o + 3 * conv_w
    sg_ref[0] = jax.nn.sigmoid(proj(o, o + d)).astype(BF16)
    mc_ref[0] = (jax.nn.sigmoid(proj(o + d, o + 2 * d)) * convp).astype(BF16)


def _inproj(x, pos3, mod3, norm_g, invf, w_in, conv_w, w_up_conv):
    b, s, d = x.shape
    qk_w = N_HEADS * 2 * HEAD_DIM
    cw = conv_w.shape[1]
    tm = TM_IN
    if tm >= TQ:
        qt_spec = pl.BlockSpec((1, N_HEADS, tm // TQ, LANES, TQ), lambda bi, si: (bi, 0, si, 0, 0))
    else:
        qt_spec = pl.BlockSpec((1, N_HEADS, 1, LANES, tm),
                               lambda bi, si: (bi, 0, si // (TQ // tm), 0, si % (TQ // tm)))
    tok = lambda w: pl.BlockSpec((1, tm, w), lambda bi, si: (bi, si, 0))
    full = lambda a: pl.BlockSpec(a.shape, lambda bi, si: (0,) * a.ndim)
    return pl.pallas_call(
        functools.partial(_inproj_kernel, d=d, qk_w=qk_w, conv_w=cw),
        grid=(b, s // tm),
        in_specs=[tok(d), tok(1),
                  pl.BlockSpec((1, 1, mod3.shape[2]), lambda bi, si: (bi, 0, 0)),
                  full(norm_g), full(invf), full(w_in), full(conv_w), full(w_up_conv)],
        out_specs=[qt_spec,
                   tok(qk_w),
                   pl.BlockSpec((1, N_HEADS, tm // TK, VT_ROWS, TK),
                                lambda bi, si: (bi, 0, si, 0, 0)),
                   tok(d), tok(d)],
        out_shape=[jax.ShapeDtypeStruct((b, N_HEADS, s // TQ, LANES, TQ), BF16),
                   jax.ShapeDtypeStruct((b, s, qk_w), BF16),
                   jax.ShapeDtypeStruct((b, N_HEADS, s // TK, VT_ROWS, TK), BF16),
                   jax.ShapeDtypeStruct((b, s, d), BF16),
                   jax.ShapeDtypeStruct((b, s, d), BF16)],
        scratch_shapes=[pltpu.VMEM((8, cw), F32)],
        compiler_params=_cparams(2),
        name="inproj",
    )(x, pos3, mod3, norm_g, invf, w_in, conv_w, w_up_conv)


def _attn_kernel(lam_ref, qt_ref, k_ref, vt_ref, g_ref, o_ref, s_ref, p_ref, qq_ref, bias_ref, *,
                 out_scale):
    hps, n_q, tq = qt_ref.shape[1], qt_ref.shape[2], qt_ref.shape[4]
    feat = lax.broadcasted_iota(I32, (LANES, 1), 0)
    halves = [slice(h * tq, (h + 1) * tq) for h in range(2)]
    streams = [(h, half) for h in range(hps) for half in halves]

    k_sub = lax.broadcasted_iota(I32, (TK, 1), 0)
    q_lane = lax.broadcasted_iota(I32, (1, tq), 1)
    bias_ref[...] = jnp.where(k_sub <= q_lane, 0.0, MASK_VALUE)

    def load_queries(qi):
        for h in range(hps):
            qt = qt_ref[0, h, qi]
            zero = jnp.zeros_like(qt)
            qq_ref[h, :, halves[0]] = jnp.where(feat < HEAD_DIM, qt, zero)
            qq_ref[h, :, halves[1]] = jnp.where(feat >= HEAD_DIM, qt, zero)

    def scores(j):
        for h, lanes in streams:
            kt = k_ref[0, pl.ds(j * TK, TK), h * LANES:(h + 1) * LANES]
            s_ref[h, :, lanes] = jnp.dot(kt, qq_ref[h, :, lanes], preferred_element_type=F32)

    def values(j, state):
        return tuple((m, alpha * acc + jnp.dot(vt_ref[0, h, j], p_ref[h, :, lanes],
                                               preferred_element_type=F32), alpha)
                     for (h, lanes), (m, acc, alpha) in zip(streams, state))

    def softmax(state, diagonal):
        out = []
        for (h, lanes), (m, acc, _) in zip(streams, state):
            load = ((lambda: s_ref[h, :, lanes] + bias_ref[...]) if diagonal
                    else (lambda: s_ref[h, :, lanes]))
            m_new = jnp.maximum(m, jnp.max(load(), axis=0, keepdims=True))
            p_ref[h, :, lanes] = jnp.exp2(load() - m_new).astype(BF16)
            out.append((m_new, acc, jnp.exp2(m - m_new)))
        return tuple(out)

    def finish(qi, last_tile, state):
        state = values(last_tile, state)
        row0 = qi * tq
        for h in range(hps):
            (_, acc1, _), (_, acc2, _) = state[2 * h], state[2 * h + 1]
            o1 = acc1[0:LANES] / acc1[LANES:LANES + 1]
            o2 = acc2[0:LANES] / acc2[LANES:LANES + 1]
            a = o1 - lam_ref[0:1, 0:1] * o2
            y = a * lax.rsqrt(jnp.mean(a * a, axis=0, keepdims=True) + EPS) * g_ref[...]
            o_ref[0, pl.ds(row0, tq), h * LANES:(h + 1) * LANES] = (
                (y * out_scale).T.astype(BF16))

    def next_diagonal(qi):
        if qi + 1 < n_q:
            load_queries(qi + 1)
            scores(qi + 1)

    one = (jnp.full((1, tq), MASK_VALUE, F32), jnp.zeros((VT_ROWS, tq), F32),
           jnp.ones((1, tq), F32))
    init = tuple(one for _ in streams)

    load_queries(0)
    scores(0)
    pending = softmax(init, True)
    next_diagonal(0)

    for qi in range(1, n_q):
        finish(qi - 1, max(qi - 2, 0), pending)
        state = softmax(init, True)
        scores(0)
        for t in range(1, qi + 1):
            state = values(qi if t == 1 else t - 2, state)
            state = softmax(state, False)
            if t < qi:
                scores(t)
            else:
                next_diagonal(qi)
        pending = state
    finish(n_q - 1, max(n_q - 2, 0), pending)


def _attn(lam, qt, k, vt, subln_g_col, out_scale):
    b, s, w = k.shape
    hps = ATTN_HEADS_PER_STEP
    return pl.pallas_call(
        functools.partial(_attn_kernel, out_scale=out_scale),
        grid=(b, N_HEADS // hps),
        in_specs=[pl.BlockSpec((8, LANES), lambda bi, hi: (0, 0)),
                  pl.BlockSpec((1, hps, s // TQ, LANES, TQ), lambda bi, hi: (bi, hi, 0, 0, 0)),
                  pl.BlockSpec((1, s, hps * LANES), lambda bi, hi: (bi, 0, hi)),
                  pl.BlockSpec((1, hps, s // TK, VT_ROWS, TK), lambda bi, hi: (bi, hi, 0, 0, 0)),
                  pl.BlockSpec((LANES, 1), lambda bi, hi: (0, 0))],
        out_specs=pl.BlockSpec((1, s, hps * LANES), lambda bi, hi: (bi, 0, hi)),
        out_shape=jax.ShapeDtypeStruct((b, s, w), BF16),
        scratch_shapes=[pltpu.VMEM((hps, TK, 2 * TQ), F32), pltpu.VMEM((hps, TK, 2 * TQ), BF16),
                        pltpu.VMEM((hps, LANES, 2 * TQ), BF16), pltpu.VMEM((TK, TQ), F32)],
        compiler_params=_cparams(2),
        name="attn",
    )(lam, qt, k, vt, subln_g_col)


def _mixout_kernel(att_ref, sg_ref, mc_ref, x_ref, mod_ref, wua_ref, wo_ref, g_ref, wrt_ref,
                   br_ref, x1_ref, pos_ref, cnt_ref, hs_ref,
                   h2c_ref, tri_ref, carry_ref, posv_ref, cntv_ref, poss_ref, cnts_ref,
                   sem_s, sem_r, *, d, cap, tail):
    tm = x_ref.shape[0]
    i = pl.program_id(0)
    last = pl.num_programs(0) - 1

    @pl.when(i == 0)
    def _():
        carry_ref[...] = jnp.zeros_like(carry_ref)
        earlier = (lax.broadcasted_iota(I32, (tm, 1), 0)
                   <= lax.broadcasted_iota(I32, (1, tm), 1))
        tri_ref[...] = jnp.where(earlier, 1.0, 0.0).astype(BF16)

    mod = mod_ref[0]
    g_m = mod[:, 2 * d:3 * d]
    sh_f = mod[:, 3 * d:4 * d]
    sc_f = mod[:, 4 * d:5 * d]

    attp = jnp.dot(att_ref[...], wua_ref[...], preferred_element_type=F32)
    merged = sg_ref[...].astype(F32) * attp + mc_ref[...].astype(F32)
    mo = jnp.dot(merged.astype(BF16), wo_ref[...], preferred_element_type=F32)
    x1 = x_ref[...] + g_m * mo
    x1_ref[...] = x1
    ms = jnp.mean(x1 * x1, axis=-1, keepdims=True)
    h2 = (x1 * lax.rsqrt(ms + EPS) * g_ref[...]) * (1.0 + sc_f) + sh_f

    def split(a):
        hi = a.astype(BF16)
        return hi, (a - hi.astype(F32)).astype(BF16)

    def dot_t(a, b):
        return lax.dot_general(a, b, (((1,), (1,)), ((), ())), preferred_element_type=F32)

    w_hi, w_lo = split(wrt_ref[...])
    h_hi, h_lo = split(h2)
    by_hi = dot_t(jnp.concatenate([w_hi, w_lo], axis=0), h_hi)
    lt = (by_hi[0:ROUTER_ROWS] + (dot_t(w_hi, h_lo) + by_hi[ROUTER_ROWS:2 * ROUTER_ROWS])
          + br_ref[...])
    ridx = lax.broadcasted_iota(I32, (8, 1), 0)
    n_e = N_GROUPS * EXPERTS_PER_GROUP
    is_grp = ridx < N_GROUPS
    gl = jnp.where(is_grp, lt[n_e:n_e + 8, :], MASK_VALUE)
    gmax = jnp.max(gl, axis=0, keepdims=True)
    gsum = jnp.sum(jnp.where(is_grp, jnp.exp(gl - gmax), 0.0), axis=0, keepdims=True)
    g_w = 1.0 / gsum
    gidx = jnp.min(jnp.where(gl == gmax, ridx, 8), axis=0, keepdims=True)
    e_in = jnp.zeros((8, tm), F32)
    for g in range(N_GROUPS):
        e_in = jnp.where(gidx == g, lt[g * 8:(g + 1) * 8, :], e_in)
    t1 = jnp.max(e_in, axis=0, keepdims=True)
    i1 = jnp.min(jnp.where(e_in == t1, ridx, 8), axis=0, keepdims=True)
    e_in2 = jnp.where(ridx == i1, MASK_VALUE, e_in)
    t2 = jnp.max(e_in2, axis=0, keepdims=True)
    i2 = jnp.min(jnp.where(e_in2 == t2, ridx, 8), axis=0, keepdims=True)
    e2 = jnp.exp(t2 - t1)
    den = 1.0 + e2
    w1 = (1.0 / den) * g_w
    w2 = (e2 / den) * g_w
    cw_t = jnp.where(ridx == i1, w1, jnp.where(ridx == i2, w2, 0.0))
    cw_col = jnp.concatenate([cw_t, jnp.zeros((LANES - 8, tm), F32)], axis=0).T

    onehot = jnp.where(ridx == gidx, 1.0, 0.0)
    oh16 = jnp.concatenate([onehot, jnp.zeros_like(onehot)], axis=0).astype(BF16)
    incl = jnp.dot(oh16, tri_ref[...], preferred_element_type=F32)[0:8, :]
    run = carry_ref[...][:, 0:1]
    base = ridx.astype(F32) * float(cap)
    pos = jnp.sum(onehot * (base + run + incl - onehot), axis=0, keepdims=True).astype(I32)
    new_run = run + incl[:, tm - 1:tm]
    carry_ref[...] = jnp.broadcast_to(new_run, carry_ref.shape)
    cnt_i = jnp.broadcast_to(new_run, cnt_ref.shape).astype(I32)
    cnt_ref[...] = cnt_i
    pos_ref[0] = pos
    posv_ref[...] = pos
    cp = pltpu.make_async_copy(posv_ref.at[0], poss_ref, sem_s)
    cp.start()

    def row_copy(r, p):
        return pltpu.make_async_copy(h2c_ref.at[pl.ds(r, 1), :], hs_ref.at[pl.ds(p, 1), :], sem_r)

    def wait_tile():
        pltpu.make_async_copy(h2c_ref, hs_ref.at[pl.ds(0, tm), :], sem_r).wait()

    @pl.when(i > 0)
    def _():
        wait_tile()

    h2c_ref[:, 0:d] = h2
    h2c_ref[:, d:d + LANES] = cw_col
    cp.wait()

    for r in range(tm):
        row_copy(r, poss_ref[r]).start()

    @pl.when(i == last)
    def _():
        wait_tile()
        cntv_ref[...] = cnt_i
        cc = pltpu.make_async_copy(cntv_ref, cnts_ref, sem_s)
        cc.start()
        cc.wait()
        h2c_ref[0:8, :] = jnp.zeros((8, h2c_ref.shape[1]), F32)

        def zissue(r, c):
            for g in range(N_GROUPS):
                row_copy(0, g * cap + cnts_ref[g, 0] + r).start()
            return c

        lax.fori_loop(0, tail, zissue, 0, unroll=2)
        for g in range(N_GROUPS):
            pltpu.make_async_copy(h2c_ref.at[pl.ds(0, tail), :], hs_ref.at[pl.ds(0, tail), :],
                                  sem_r).wait()


def _mixout(att, sg, mc, x, mod3, w_up_att, w_out, norm_g, wrt, br, tiles_per_batch, cap):
    t, d = x.shape
    tm = TM_MIX
    dc = d + LANES
    tok = lambda w: pl.BlockSpec((tm, w), lambda i: (i, 0))
    full = lambda a: pl.BlockSpec(a.shape, lambda i: (0,) * a.ndim)
    return pl.pallas_call(
        functools.partial(_mixout_kernel, d=d, cap=cap, tail=cap - t),
        grid=(t // tm,),
        in_specs=[tok(att.shape[1]), tok(d), tok(d), tok(d),
                  pl.BlockSpec((1, 1, mod3.shape[2]), lambda i: (i // tiles_per_batch, 0, 0)),
                  full(w_up_att), full(w_out), full(norm_g), full(wrt), full(br)],
        out_specs=[tok(d),
                   pl.BlockSpec((1, 1, tm), lambda i: (i, 0, 0)),
                   pl.BlockSpec((8, LANES), lambda i: (0, 0)),
                   pl.BlockSpec(memory_space=pl.ANY)],
        out_shape=[jax.ShapeDtypeStruct((t, d), F32),
                   jax.ShapeDtypeStruct((t // tm, 1, tm), I32),
                   jax.ShapeDtypeStruct((8, LANES), I32),
                   jax.ShapeDtypeStruct((N_GROUPS * cap, dc), F32)],
        scratch_shapes=[pltpu.VMEM((tm, dc), F32), pltpu.VMEM((tm, tm), BF16),
                        pltpu.VMEM((8, LANES), F32),
                        pltpu.VMEM((1, tm), I32), pltpu.VMEM((8, LANES), I32),
                        pltpu.SMEM((tm,), I32), pltpu.SMEM((8, LANES), I32),
                        pltpu.SemaphoreType.DMA, pltpu.SemaphoreType.DMA],
        compiler_params=_cparams(1),
        name="mixout",
    )(att, sg, mc, x, mod3, w_up_att, w_out, norm_g, wrt, br)


def _ffn_tile(step, cnt_ref, tm):
    log_tm = tm.bit_length() - 1
    ends = []
    tot = jnp.int32(0)
    for g in range(N_GROUPS):
        tot = tot + lax.shift_right_logical(cnt_ref[g] + (tm - 1), log_tm)
        ends.append(tot)
    sc = jnp.maximum(jnp.minimum(step, tot - 1), 0)
    g = sum((sc >= e).astype(I32) for e in ends[:-1])
    start = jnp.where(g == 0, 0, jnp.where(g == 1, ends[0], jnp.where(g == 2, ends[1], ends[2])))
    return g, sc - start, step < tot


def _ffn_kernel(cnt_ref, hs_ref, wg_ref, wu_ref, wd_ref, y_ref, *, d):
    tm = hs_ref.shape[0]
    _, _, real = _ffn_tile(pl.program_id(0), cnt_ref, tm)

    @pl.when(real)
    def _():
        h = hs_ref[:, 0:d].astype(BF16)
        cw = hs_ref[:, d:d + LANES]
        acc = jnp.zeros((tm, d), F32)
        for e in range(EXPERTS_PER_GROUP):
            hg = jnp.dot(h, wg_ref[e], preferred_element_type=F32)
            hu = jnp.dot(h, wu_ref[e], preferred_element_type=F32)
            a = hg * jax.nn.sigmoid(hg) * hu * cw[:, e:e + 1]
            acc = acc + jnp.dot(a.astype(BF16), wd_ref[e], preferred_element_type=F32)
        y_ref[...] = acc


def _ffn(cnt, hs, w_gate, w_up, w_down, cap):
    rows, dc = hs.shape
    d = dc - LANES
    tm = TM_FFN
    blocks_per_group = cap // tm
    n_steps = (blocks_per_group - 1) + N_GROUPS

    def row_map(s, c):
        g, blk, _ = _ffn_tile(s, c, tm)
        return (g * blocks_per_group + blk, 0)

    def w_map(s, c):
        g, _, _ = _ffn_tile(s, c, tm)
        return (g, 0, 0)

    ff = w_gate.shape[2]
    gs = pltpu.PrefetchScalarGridSpec(
        num_scalar_prefetch=1,
        grid=(n_steps,),
        in_specs=[pl.BlockSpec((tm, dc), row_map),
                  pl.BlockSpec((EXPERTS_PER_GROUP, d, ff), w_map),
                  pl.BlockSpec((EXPERTS_PER_GROUP, d, ff), w_map),
                  pl.BlockSpec((EXPERTS_PER_GROUP, ff, d), w_map)],
        out_specs=pl.BlockSpec((tm, d), row_map),
    )
    return pl.pallas_call(
        functools.partial(_ffn_kernel, d=d),
        grid_spec=gs,
        out_shape=jax.ShapeDtypeStruct((rows, d), F32),
        compiler_params=_cparams(1),
        name="ffn",
    )(cnt, hs, w_gate, w_up, w_down)


def _final_kernel(pos_ref, ys_ref, x1_ref, mod_ref, g_ref, o_ref, ybuf_ref, sem, *, d, normalize):
    tm = x1_ref.shape[0]
    i = pl.program_id(0)

    def gather_tile(tile):
        slot = tile & 1
        base = tile * tm
        for r in range(tm):
            pltpu.make_async_copy(ys_ref.at[pl.ds(pos_ref[base + r], 1), :],
                                  ybuf_ref.at[slot, pl.ds(r, 1), :], sem.at[slot]).start()

    @pl.when(i == 0)
    def _():
        gather_tile(i)

    @pl.when(i + 1 < pl.num_programs(0))
    def _():
        gather_tile(i + 1)

    slot = i & 1
    pltpu.make_async_copy(ys_ref.at[pl.ds(0, tm), :], ybuf_ref.at[slot], sem.at[slot]).wait()

    g_f = mod_ref[0][:, 5 * d:6 * d]
    x2 = x1_ref[...] + g_f * ybuf_ref[slot]
    if normalize:
        x2 = x2 * lax.rsqrt(jnp.mean(x2 * x2, axis=-1, keepdims=True) + EPS) * g_ref[...]
    o_ref[...] = x2


def _final(pos, ys, x1, mod3, final_g, tiles_per_batch, normalize):
    t, d = x1.shape
    tm = TM_FIN
    gs = pltpu.PrefetchScalarGridSpec(
        num_scalar_prefetch=1,
        grid=(t // tm,),
        in_specs=[pl.BlockSpec(memory_space=pl.ANY),
                  pl.BlockSpec((tm, d), lambda i, p: (i, 0)),
                  pl.BlockSpec((1, 1, mod3.shape[2]), lambda i, p: (i // tiles_per_batch, 0, 0)),
                  pl.BlockSpec((1, d), lambda i, p: (0, 0))],
        out_specs=pl.BlockSpec((tm, d), lambda i, p: (i, 0)),
        scratch_shapes=[pltpu.VMEM((2, tm, d), F32), pltpu.SemaphoreType.DMA((2,))],
    )
    return pl.pallas_call(
        functools.partial(_final_kernel, d=d, normalize=normalize),
        grid_spec=gs,
        out_shape=jax.ShapeDtypeStruct((t, d), F32),
        compiler_params=_cparams(1),
        name="final",
    )(pos, ys, x1, mod3, final_g)


def kernel(x, c, positions, w_ada, b_ada, norm_mix_g, w_in, lambda_q1, lambda_k1, lambda_q2,
           lambda_k2, subln_g, conv_w, w_up_att, w_up_conv, w_out, norm_ffn_g, w_group_router,
           b_group_router, w_expert_router, b_expert_router, w_gate, w_up, w_down, final_norm_g):
    b, s, d = x.shape
    depth = w_ada.shape[0]
    t = b * s
    assert s % TQ == 0 and s % TM_IN == 0 and TQ == TK and TM_IN % TK == 0
    assert TQ % TM_IN == 0 or TM_IN % TQ == 0
    assert N_HEADS % ATTN_HEADS_PER_STEP == 0
    assert s % TM_MIX == 0 and s % TM_FIN == 0 and t % TM_FFN == 0
    cap = t + TM_FFN
    n_e = N_GROUPS * EXPERTS_PER_GROUP

    c8 = jnp.concatenate([c, jnp.zeros((8 - b, d), F32)], axis=0)
    pos3 = positions.reshape(b, s, 1)
    inv_freq = ROPE_THETA ** (-jnp.arange(0, HEAD_DIM, 2, dtype=F32) / HEAD_DIM)
    invf = jnp.tile(inv_freq, LANES // (HEAD_DIM // 2)).reshape(1, LANES)

    xf = x
    for l in range(depth):
        lambda_init = 0.8 - 0.6 * math.exp(-0.3 * l)
        row = lambda a: a[l].reshape(1, -1)
        mod8, lam = _ada(c8, w_ada[l], row(b_ada), row(lambda_q1), row(lambda_k1),
                         row(lambda_q2), row(lambda_k2), lambda_init)
        mod3 = mod8[0:b].reshape(b, 1, 6 * d)

        qt, k, vt, sg, mc = _inproj(xf.reshape(b, s, d), pos3, mod3, row(norm_mix_g), invf,
                                    w_in[l].astype(BF16), conv_w[l], w_up_conv[l].astype(BF16))
        att = _attn(lam, qt, k, vt, subln_g[l].reshape(-1, 1), 1.0 - lambda_init)

        pad = ROUTER_ROWS - n_e - N_GROUPS
        wrt = jnp.concatenate([w_expert_router[l].T, w_group_router[l].T,
                               jnp.zeros((pad, d), F32)], axis=0)
        br = jnp.concatenate([b_expert_router[l], b_group_router[l],
                              jnp.zeros((pad,), F32)]).reshape(ROUTER_ROWS, 1)
        x1, pos, cnt, hs = _mixout(att.reshape(t, -1), sg.reshape(t, d), mc.reshape(t, d),
                                   xf.reshape(t, d), mod3, w_up_att[l].astype(BF16),
                                   w_out[l].astype(BF16), row(norm_ffn_g), wrt, br,
                                   s // TM_MIX, cap)
        ys = _ffn(cnt[0:N_GROUPS, 0], hs, w_gate[l].astype(BF16), w_up[l].astype(BF16),
                  w_down[l].astype(BF16), cap)
        xf = _final(pos.reshape(t), ys, x1, mod3, final_norm_g.reshape(1, d), s // TM_FIN,
                    l == depth - 1)
    return xf.reshape(b, s, d)
```

```python
import functools
import math

import jax
import jax.numpy as jnp
from jax import lax
from jax.experimental import pallas as pl
from jax.experimental.pallas import tpu as pltpu

F32 = jnp.float32
BF16 = jnp.bfloat16
I32 = jnp.int32

EPS = 1e-6
MASK_VALUE = -1e30
ROPE_THETA = 10000.0

N_HEADS = 4
HEAD_DIM = 64
LANES = 128
N_GROUPS = 4
EXPERTS_PER_GROUP = 8
CONV_K = 3
ROUTER_ROWS = 48

VMEM_LIMIT = 56 * 1024 * 1024

TM_IN = 1024
TQ = 512
TK = 512
ATTN_HEADS_PER_STEP = 1
VT_ROWS = LANES + 16
TM_MIX = 1024
TM_FFN = 512
TM_FIN = 512
TN_ADA = 1024


def _cparams(n_axes):
    return pltpu.CompilerParams(
        dimension_semantics=("arbitrary",) * n_axes, vmem_limit_bytes=VMEM_LIMIT)


def _ada_kernel(c_ref, w_ref, b_ref, lq1_ref, lk1_ref, lq2_ref, lk2_ref, mod_ref, lam_ref, *,
                lambda_init):
    c = c_ref[...]
    a = c * jax.nn.sigmoid(c)
    mod_ref[...] = jnp.dot(a.astype(BF16), w_ref[...].astype(BF16),
                           preferred_element_type=F32) + b_ref[...]
    s1 = jnp.sum(lq1_ref[...] * lk1_ref[...], axis=-1, keepdims=True)
    s2 = jnp.sum(lq2_ref[...] * lk2_ref[...], axis=-1, keepdims=True)
    lam = jnp.exp(s1) - jnp.exp(s2) + lambda_init
    lam_ref[...] = jnp.broadcast_to(lam, lam_ref.shape)


def _ada(c8, w_ada, b_ada, lq1, lk1, lq2, lk2, lambda_init):
    d, n = w_ada.shape
    small = pl.BlockSpec((1, HEAD_DIM), lambda j: (0, 0))
    return pl.pallas_call(
        functools.partial(_ada_kernel, lambda_init=lambda_init),
        grid=(n // TN_ADA,),
        in_specs=[pl.BlockSpec((8, d), lambda j: (0, 0)),
                  pl.BlockSpec((d, TN_ADA), lambda j: (0, j)),
                  pl.BlockSpec((1, TN_ADA), lambda j: (0, j)),
                  small, small, small, small],
        out_specs=[pl.BlockSpec((8, TN_ADA), lambda j: (0, j)),
                   pl.BlockSpec((8, LANES), lambda j: (0, 0))],
        out_shape=[jax.ShapeDtypeStruct((8, n), F32), jax.ShapeDtypeStruct((8, LANES), F32)],
        compiler_params=_cparams(1),
        name="ada",
    )(c8, w_ada, b_ada, lq1, lk1, lq2, lk2)


def _inproj_kernel(x_ref, pos_ref, mod_ref, g_ref, invf_ref, w_ref, cw_ref, wupc_ref,
                   qt_ref, k_ref, vt_ref, sg_ref, mc_ref, carry_ref, *, d, qk_w, conv_w):
    tm = x_ref.shape[1]
    x = x_ref[0]
    ms = jnp.mean(x * x, axis=-1, keepdims=True)
    mod = mod_ref[0]
    sh = mod[:, 0:d]
    sc = mod[:, d:2 * d]
    h = (x * lax.rsqrt(ms + EPS) * g_ref[...]) * (1.0 + sc) + sh
    hb = h.astype(BF16)

    def proj(a, b):
        return jnp.dot(hb, w_ref[:, a:b], preferred_element_type=F32)

    lane = lax.broadcasted_iota(I32, (1, LANES), 1)
    first_half = (lane % HEAD_DIM) < (HEAD_DIM // 2)
    ang = pos_ref[0].astype(F32) * invf_ref[...]
    cos = jnp.cos(ang)
    sin = jnp.sin(ang)
    sin_signed = jnp.where(first_half, -sin, sin)

    def rope(t):
        partner = jnp.where(first_half, pltpu.roll(t, LANES - HEAD_DIM // 2, 1),
                            pltpu.roll(t, HEAD_DIM // 2, 1))
        return t * cos + partner * sin_signed

    zq = proj(0, qk_w)
    zk = proj(qk_w, 2 * qk_w)
    scale = HEAD_DIM ** -0.5 * math.log2(math.e)
    zv = proj(2 * qk_w, 3 * qk_w)
    for hd in range(qk_w // LANES):
        sl = slice(hd * LANES, (hd + 1) * LANES)
        rq = rope(zq[:, sl]) * scale
        qw = qt_ref.shape[4]
        for ck in range(tm // qw):
            qt_ref[0, hd, ck] = rq[ck * qw:(ck + 1) * qw, :].T.astype(BF16)
        k_ref[0, :, sl] = rope(zk[:, sl]).astype(BF16)
        for ck in range(tm // TK):
            vt_ref[0, hd, ck, 0:LANES, :] = zv[ck * TK:(ck + 1) * TK, sl].T.astype(BF16)
            vt_ref[0, hd, ck, LANES:VT_ROWS, :] = jnp.ones((VT_ROWS - LANES, TK), BF16)

    o = 3 * qk_w
    cb = proj(o, o + conv_w)
    u = proj(o + conv_w, o + 2 * conv_w) * proj(o + 2 * conv_w, o + 3 * conv_w)

    @pl.when(pl.program_id(1) == 0)
    def _():
        carry_ref[...] = jnp.zeros_like(carry_ref)

    prev = carry_ref[...]
    row = lax.broadcasted_iota(I32, (tm, 1), 0)
    u1 = jnp.where(row == 0, prev[7:8, :], pltpu.roll(u, 1, 0))
    u2 = jnp.where(row == 0, prev[6:7, :],
                   jnp.where(row == 1, prev[7:8, :], pltpu.roll(u, 2, 0)))
    carry_ref[...] = u[tm - 8:tm, :]
    cw = cw_ref[...]
    conv = cb * (cw[0:1, :] * u + cw[1:2, :] * u1 + cw[2:3, :] * u2)
    convp = jnp.dot(conv.astype(BF16), wupc_ref[...], preferred_element_type=F32)

    o = o + 3 * conv_w
    sg_ref[0] = jax.nn.sigmoid(proj(o, o + d)).astype(BF16)
    mc_ref[0] = (jax.nn.sigmoid(proj(o + d, o + 2 * d)) * convp).astype(BF16)


def _inproj(x, pos3, mod3, norm_g, invf, w_in, conv_w, w_up_conv):
    b, s, d = x.shape
    qk_w = N_HEADS * 2 * HEAD_DIM
    cw = conv_w.shape[1]
    tm = TM_IN
    if tm >= TQ:
        qt_spec = pl.BlockSpec((1, N_HEADS, tm // TQ, LANES, TQ), lambda bi, si: (bi, 0, si, 0, 0))
    else:
        qt_spec = pl.BlockSpec((1, N_HEADS, 1, LANES, tm),
                               lambda bi, si: (bi, 0, si // (TQ // tm), 0, si % (TQ // tm)))
    tok = lambda w: pl.BlockSpec((1, tm, w), lambda bi, si: (bi, si, 0))
    full = lambda a: pl.BlockSpec(a.shape, lambda bi, si: (0,) * a.ndim)
    return pl.pallas_call(
        functools.partial(_inproj_kernel, d=d, qk_w=qk_w, conv_w=cw),
        grid=(b, s // tm),
        in_specs=[tok(d), tok(1),
                  pl.BlockSpec((1, 1, mod3.shape[2]), lambda bi, si: (bi, 0, 0)),
                  full(norm_g), full(invf), full(w_in), full(conv_w), full(w_up_conv)],
        out_specs=[qt_spec,
                   tok(qk_w),
                   pl.BlockSpec((1, N_HEADS, tm // TK, VT_ROWS, TK),
                                lambda bi, si: (bi, 0, si, 0, 0)),
                   tok(d), tok(d)],
        out_shape=[jax.ShapeDtypeStruct((b, N_HEADS, s // TQ, LANES, TQ), BF16),
                   jax.ShapeDtypeStruct((b, s, qk_w), BF16),
                   jax.ShapeDtypeStruct((b, N_HEADS, s // TK, VT_ROWS, TK), BF16),
                   jax.ShapeDtypeStruct((b, s, d), BF16),
                   jax.ShapeDtypeStruct((b, s, d), BF16)],
        scratch_shapes=[pltpu.VMEM((8, cw), F32)],
        compiler_params=_cparams(2),
        name="inproj",
    )(x, pos3, mod3, norm_g, invf, w_in, conv_w, w_up_conv)


def _attn_kernel(lam_ref, qt_ref, k_ref, vt_ref, g_ref, o_ref, s_ref, p_ref, qq_ref, bias_ref, *,
                 out_scale):
    hps, n_q, tq = qt_ref.shape[1], qt_ref.shape[2], qt_ref.shape[4]
    feat = lax.broadcasted_iota(I32, (LANES, 1), 0)
    halves = [slice(h * tq, (h + 1) * tq) for h in range(2)]
    streams = [(h, half) for h in range(hps) for half in halves]

    k_sub = lax.broadcasted_iota(I32, (TK, 1), 0)
    q_lane = lax.broadcasted_iota(I32, (1, tq), 1)
    bias_ref[...] = jnp.where(k_sub <= q_lane, 0.0, MASK_VALUE)

    def load_queries(qi):
        for h in range(hps):
            qt = qt_ref[0, h, qi]
            zero = jnp.zeros_like(qt)
            qq_ref[h, :, halves[0]] = jnp.where(feat < HEAD_DIM, qt, zero)
            qq_ref[h, :, halves[1]] = jnp.where(feat >= HEAD_DIM, qt, zero)

    def scores(j):
        for h, lanes in streams:
            kt = k_ref[0, pl.ds(j * TK, TK), h * LANES:(h + 1) * LANES]
            s_ref[h, :, lanes] = jnp.dot(kt, qq_ref[h, :, lanes], preferred_element_type=F32)

    def values(j, state):
        return tuple((m, alpha * acc + jnp.dot(vt_ref[0, h, j], p_ref[h, :, lanes],
                                               preferred_element_type=F32), alpha)
                     for (h, lanes), (m, acc, alpha) in zip(streams, state))

    def softmax(state, diagonal):
        out = []
        for (h, lanes), (m, acc, _) in zip(streams, state):
            load = ((lambda: s_ref[h, :, lanes] + bias_ref[...]) if diagonal
                    else (lambda: s_ref[h, :, lanes]))
            m_new = jnp.maximum(m, jnp.max(load(), axis=0, keepdims=True))
            p_ref[h, :, lanes] = jnp.exp2(load() - m_new).astype(BF16)
            out.append((m_new, acc, jnp.exp2(m - m_new)))
        return tuple(out)

    def finish(qi, last_tile, state):
        state = values(last_tile, state)
        row0 = qi * tq
        for h in range(hps):
            (_, acc1, _), (_, acc2, _) = state[2 * h], state[2 * h + 1]
            o1 = acc1[0:LANES] / acc1[LANES:LANES + 1]
            o2 = acc2[0:LANES] / acc2[LANES:LANES + 1]
            a = o1 - lam_ref[0:1, 0:1] * o2
            y = a * lax.rsqrt(jnp.mean(a * a, axis=0, keepdims=True) + EPS) * g_ref[...]
            o_ref[0, pl.ds(row0, tq), h * LANES:(h + 1) * LANES] = (
                (y * out_scale).T.astype(BF16))

    def next_diagonal(qi):
        if qi + 1 < n_q:
            load_queries(qi + 1)
            scores(qi + 1)

    one = (jnp.full((1, tq), MASK_VALUE, F32), jnp.zeros((VT_ROWS, tq), F32),
           jnp.ones((1, tq), F32))
    init = tuple(one for _ in streams)

    load_queries(0)
    scores(0)
    pending = softmax(init, True)
    next_diagonal(0)

    for qi in range(1, n_q):
        finish(qi - 1, max(qi - 2, 0), pending)
        state = softmax(init, True)
        scores(0)
        for t in range(1, qi + 1):
            state = values(qi if t == 1 else t - 2, state)
            state = softmax(state, False)
            if t < qi:
                scores(t)
            else:
                next_diagonal(qi)
        pending = state
    finish(n_q - 1, max(n_q - 2, 0), pending)


def _attn(lam, qt, k, vt, subln_g_col, out_scale):
    b, s, w = k.shape
    hps = ATTN_HEADS_PER_STEP
    return pl.pallas_call(
        functools.partial(_attn_kernel, out_scale=out_scale),
        grid=(b, N_HEADS // hps),
        in_specs=[pl.BlockSpec((8, LANES), lambda bi, hi: (0, 0)),
                  pl.BlockSpec((1, hps, s // TQ, LANES, TQ), lambda bi, hi: (bi, hi, 0, 0, 0)),
                  pl.BlockSpec((1, s, hps * LANES), lambda bi, hi: (bi, 0, hi)),
                  pl.BlockSpec((1, hps, s // TK, VT_ROWS, TK), lambda bi, hi: (bi, hi, 0, 0, 0)),
                  pl.BlockSpec((LANES, 1), lambda bi, hi: (0, 0))],
        out_specs=pl.BlockSpec((1, s, hps * LANES), lambda bi, hi: (bi, 0, hi)),
        out_shape=jax.ShapeDtypeStruct((b, s, w), BF16),
        scratch_shapes=[pltpu.VMEM((hps, TK, 2 * TQ), F32), pltpu.VMEM((hps, TK, 2 * TQ), BF16),
                        pltpu.VMEM((hps, LANES, 2 * TQ), BF16), pltpu.VMEM((TK, TQ), F32)],
        compiler_params=_cparams(2),
        name="attn",
    )(lam, qt, k, vt, subln_g_col)


def _mixout_kernel(att_ref, sg_ref, mc_ref, x_ref, mod_ref, wua_ref, wo_ref, g_ref, wrt_ref,
                   br_ref, x1_ref, pos_ref, cnt_ref, hs_ref,
                   h2c_ref, tri_ref, carry_ref, posv_ref, cntv_ref, poss_ref, cnts_ref,
                   sem_s, sem_r, *, d, cap, tail):
    tm = x_ref.shape[0]
    i = pl.program_id(0)
    last = pl.num_programs(0) - 1

    @pl.when(i == 0)
    def _():
        carry_ref[...] = jnp.zeros_like(carry_ref)
        earlier = (lax.broadcasted_iota(I32, (tm, 1), 0)
                   <= lax.broadcasted_iota(I32, (1, tm), 1))
        tri_ref[...] = jnp.where(earlier, 1.0, 0.0).astype(BF16)

    mod = mod_ref[0]
    g_m = mod[:, 2 * d:3 * d]
    sh_f = mod[:, 3 * d:4 * d]
    sc_f = mod[:, 4 * d:5 * d]

    attp = jnp.dot(att_ref[...], wua_ref[...], preferred_element_type=F32)
    merged = sg_ref[...].astype(F32) * attp + mc_ref[...].astype(F32)
    mo = jnp.dot(merged.astype(BF16), wo_ref[...], preferred_element_type=F32)
    x1 = x_ref[...] + g_m * mo
    x1_ref[...] = x1
    ms = jnp.mean(x1 * x1, axis=-1, keepdims=True)
    h2 = (x1 * lax.rsqrt(ms + EPS) * g_ref[...]) * (1.0 + sc_f) + sh_f

    def split(a):
        hi = a.astype(BF16)
        return hi, (a - hi.astype(F32)).astype(BF16)

    def dot_t(a, b):
        return lax.dot_general(a, b, (((1,), (1,)), ((), ())), preferred_element_type=F32)

    w_hi, w_lo = split(wrt_ref[...])
    h_hi, h_lo = split(h2)
    by_hi = dot_t(jnp.concatenate([w_hi, w_lo], axis=0), h_hi)
    lt = (by_hi[0:ROUTER_ROWS] + (dot_t(w_hi, h_lo) + by_hi[ROUTER_ROWS:2 * ROUTER_ROWS])
          + br_ref[...])
    ridx = lax.broadcasted_iota(I32, (8, 1), 0)
    n_e = N_GROUPS * EXPERTS_PER_GROUP
    is_grp = ridx < N_GROUPS
    gl = jnp.where(is_grp, lt[n_e:n_e + 8, :], MASK_VALUE)
    gmax = jnp.max(gl, axis=0, keepdims=True)
    gsum = jnp.sum(jnp.where(is_grp, jnp.exp(gl - gmax), 0.0), axis=0, keepdims=True)
    g_w = 1.0 / gsum
    gidx = jnp.min(jnp.where(gl == gmax, ridx, 8), axis=0, keepdims=True)
    e_in = jnp.zeros((8, tm), F32)
    for g in range(N_GROUPS):
        e_in = jnp.where(gidx == g, lt[g * 8:(g + 1) * 8, :], e_in)
    t1 = jnp.max(e_in, axis=0, keepdims=True)
    i1 = jnp.min(jnp.where(e_in == t1, ridx, 8), axis=0, keepdims=True)
    e_in2 = jnp.where(ridx == i1, MASK_VALUE, e_in)
    t2 = jnp.max(e_in2, axis=0, keepdims=True)
    i2 = jnp.min(jnp.where(e_in2 == t2, ridx, 8), axis=0, keepdims=True)
    e2 = jnp.exp(t2 - t1)
    den = 1.0 + e2
    w1 = (1.0 / den) * g_w
    w2 = (e2 / den) * g_w
    cw_t = jnp.where(ridx == i1, w1, jnp.where(ridx == i2, w2, 0.0))
    cw_col = jnp.concatenate([cw_t, jnp.zeros((LANES - 8, tm), F32)], axis=0).T

    onehot = jnp.where(ridx == gidx, 1.0, 0.0)
    oh16 = jnp.concatenate([onehot, jnp.zeros_like(onehot)], axis=0).astype(BF16)
    incl = jnp.dot(oh16, tri_ref[...], preferred_element_type=F32)[0:8, :]
    run = carry_ref[...][:, 0:1]
    base = ridx.astype(F32) * float(cap)
    pos = jnp.sum(onehot * (base + run + incl - onehot), axis=0, keepdims=True).astype(I32)
    new_run = run + incl[:, tm - 1:tm]
    carry_ref[...] = jnp.broadcast_to(new_run, carry_ref.shape)
    cnt_i = jnp.broadcast_to(new_run, cnt_ref.shape).astype(I32)
    cnt_ref[...] = cnt_i
    pos_ref[0] = pos
    posv_ref[...] = pos
    cp = pltpu.make_async_copy(posv_ref.at[0], poss_ref, sem_s)
    cp.start()

    def row_copy(r, p):
        return pltpu.make_async_copy(h2c_ref.at[pl.ds(r, 1), :], hs_ref.at[pl.ds(p, 1), :], sem_r)

    def wait_tile():
        pltpu.make_async_copy(h2c_ref, hs_ref.at[pl.ds(0, tm), :], sem_r).wait()

    @pl.when(i > 0)
    def _():
        wait_tile()

    h2c_ref[:, 0:d] = h2
    h2c_ref[:, d:d + LANES] = cw_col
    cp.wait()

    for r in range(tm):
        row_copy(r, poss_ref[r]).start(priority=r % 2)

    @pl.when(i == last)
    def _():
        wait_tile()
        cntv_ref[...] = cnt_i
        cc = pltpu.make_async_copy(cntv_ref, cnts_ref, sem_s)
        cc.start()
        cc.wait()
        h2c_ref[0:8, :] = jnp.zeros((8, h2c_ref.shape[1]), F32)

        def zissue(r, c):
            for g in range(N_GROUPS):
                row_copy(0, g * cap + cnts_ref[g, 0] + r).start()
            return c

        lax.fori_loop(0, tail, zissue, 0, unroll=2)
        for g in range(N_GROUPS):
            pltpu.make_async_copy(h2c_ref.at[pl.ds(0, tail), :], hs_ref.at[pl.ds(0, tail), :],
                                  sem_r).wait()


def _mixout(att, sg, mc, x, mod3, w_up_att, w_out, norm_g, wrt, br, tiles_per_batch, cap):
    t, d = x.shape
    tm = TM_MIX
    dc = d + LANES
    tok = lambda w: pl.BlockSpec((tm, w), lambda i: (i, 0))
    full = lambda a: pl.BlockSpec(a.shape, lambda i: (0,) * a.ndim)
    return pl.pallas_call(
        functools.partial(_mixout_kernel, d=d, cap=cap, tail=cap - t),
        grid=(t // tm,),
        in_specs=[tok(att.shape[1]), tok(d), tok(d), tok(d),
                  pl.BlockSpec((1, 1, mod3.shape[2]), lambda i: (i // tiles_per_batch, 0, 0)),
                  full(w_up_att), full(w_out), full(norm_g), full(wrt), full(br)],
        out_specs=[tok(d),
                   pl.BlockSpec((1, 1, tm), lambda i: (i, 0, 0)),
                   pl.BlockSpec((8, LANES), lambda i: (0, 0)),
                   pl.BlockSpec(memory_space=pl.ANY)],
        out_shape=[jax.ShapeDtypeStruct((t, d), F32),
                   jax.ShapeDtypeStruct((t // tm, 1, tm), I32),
                   jax.ShapeDtypeStruct((8, LANES), I32),
                   jax.ShapeDtypeStruct((N_GROUPS * cap, dc), F32)],
        scratch_shapes=[pltpu.VMEM((tm, dc), F32), pltpu.VMEM((tm, tm), BF16),
                        pltpu.VMEM((8, LANES), F32),
                        pltpu.VMEM((1, tm), I32), pltpu.VMEM((8, LANES), I32),
                        pltpu.SMEM((tm,), I32), pltpu.SMEM((8, LANES), I32),
                        pltpu.SemaphoreType.DMA, pltpu.SemaphoreType.DMA],
        compiler_params=_cparams(1),
        name="mixout",
    )(att, sg, mc, x, mod3, w_up_att, w_out, norm_g, wrt, br)


def _ffn_tile(step, cnt_ref, tm):
    log_tm = tm.bit_length() - 1
    ends = []
    tot = jnp.int32(0)
    for g in range(N_GROUPS):
        tot = tot + lax.shift_right_logical(cnt_ref[g] + (tm - 1), log_tm)
        ends.append(tot)
    sc = jnp.maximum(jnp.minimum(step, tot - 1), 0)
    g = sum((sc >= e).astype(I32) for e in ends[:-1])
    start = jnp.where(g == 0, 0, jnp.where(g == 1, ends[0], jnp.where(g == 2, ends[1], ends[2])))
    return g, sc - start, step < tot


def _ffn_kernel(cnt_ref, hs_ref, wg_ref, wu_ref, wd_ref, y_ref, *, d):
    tm = hs_ref.shape[0]
    _, _, real = _ffn_tile(pl.program_id(0), cnt_ref, tm)

    @pl.when(real)
    def _():
        h = hs_ref[:, 0:d].astype(BF16)
        cw = hs_ref[:, d:d + LANES]
        acc = jnp.zeros((tm, d), F32)
        for e in range(EXPERTS_PER_GROUP):
            hg = jnp.dot(h, wg_ref[e], preferred_element_type=F32)
            hu = jnp.dot(h, wu_ref[e], preferred_element_type=F32)
            a = hg * jax.nn.sigmoid(hg) * hu * cw[:, e:e + 1]
            acc = acc + jnp.dot(a.astype(BF16), wd_ref[e], preferred_element_type=F32)
        y_ref[...] = acc


def _ffn(cnt, hs, w_gate, w_up, w_down, cap):
    rows, dc = hs.shape
    d = dc - LANES
    tm = TM_FFN
    blocks_per_group = cap // tm
    n_steps = (blocks_per_group - 1) + N_GROUPS

    def row_map(s, c):
        g, blk, _ = _ffn_tile(s, c, tm)
        return (g * blocks_per_group + blk, 0)

    def w_map(s, c):
        g, _, _ = _ffn_tile(s, c, tm)
        return (g, 0, 0)

    ff = w_gate.shape[2]
    gs = pltpu.PrefetchScalarGridSpec(
        num_scalar_prefetch=1,
        grid=(n_steps,),
        in_specs=[pl.BlockSpec((tm, dc), row_map),
                  pl.BlockSpec((EXPERTS_PER_GROUP, d, ff), w_map),
                  pl.BlockSpec((EXPERTS_PER_GROUP, d, ff), w_map),
                  pl.BlockSpec((EXPERTS_PER_GROUP, ff, d), w_map)],
        out_specs=pl.BlockSpec((tm, d), row_map),
    )
    return pl.pallas_call(
        functools.partial(_ffn_kernel, d=d),
        grid_spec=gs,
        out_shape=jax.ShapeDtypeStruct((rows, d), F32),
        compiler_params=_cparams(1),
        name="ffn",
    )(cnt, hs, w_gate, w_up, w_down)


def _final_kernel(pos_ref, ys_ref, x1_ref, mod_ref, g_ref, o_ref, ybuf_ref, sem, *, d, normalize):
    tm = x1_ref.shape[0]
    i = pl.program_id(0)

    def gather_tile(tile):
        slot = tile & 1
        base = tile * tm
        for r in range(tm):
            pltpu.make_async_copy(ys_ref.at[pl.ds(pos_ref[base + r], 1), :],
                                  ybuf_ref.at[slot, pl.ds(r, 1), :],
                                  sem.at[slot]).start(priority=r % 2)

    @pl.when(i == 0)
    def _():
        gather_tile(i)

    @pl.when(i + 1 < pl.num_programs(0))
    def _():
        gather_tile(i + 1)

    slot = i & 1
    pltpu.make_async_copy(ys_ref.at[pl.ds(0, tm), :], ybuf_ref.at[slot], sem.at[slot]).wait()

    g_f = mod_ref[0][:, 5 * d:6 * d]
    x2 = x1_ref[...] + g_f * ybuf_ref[slot]
    if normalize:
        x2 = x2 * lax.rsqrt(jnp.mean(x2 * x2, axis=-1, keepdims=True) + EPS) * g_ref[...]
    o_ref[...] = x2


def _final(pos, ys, x1, mod3, final_g, tiles_per_batch, normalize):
    t, d = x1.shape
    tm = TM_FIN
    gs = pltpu.PrefetchScalarGridSpec(
        num_scalar_prefetch=1,
        grid=(t // tm,),
        in_specs=[pl.BlockSpec(memory_space=pl.ANY),
                  pl.BlockSpec((tm, d), lambda i, p: (i, 0)),
                  pl.BlockSpec((1, 1, mod3.shape[2]), lambda i, p: (i // tiles_per_batch, 0, 0)),
                  pl.BlockSpec((1, d), lambda i, p: (0, 0))],
        out_specs=pl.BlockSpec((tm, d), lambda i, p: (i, 0)),
        scratch_shapes=[pltpu.VMEM((2, tm, d), F32), pltpu.SemaphoreType.DMA((2,))],
    )
    return pl.pallas_call(
        functools.partial(_final_kernel, d=d, normalize=normalize),
        grid_spec=gs,
        out_shape=jax.ShapeDtypeStruct((t, d), F32),
        compiler_params=_cparams(1),
        name="final",
    )(pos, ys, x1, mod3, final_g)


def kernel(x, c, positions, w_ada, b_ada, norm_mix_g, w_in, lambda_q1, lambda_k1, lambda_q2,
           lambda_k2, subln_g, conv_w, w_up_att, w_up_conv, w_out, norm_ffn_g, w_group_router,
           b_group_router, w_expert_router, b_expert_router, w_gate, w_up, w_down, final_norm_g):
    b, s, d = x.shape
    depth = w_ada.shape[0]
    t = b * s
    assert s % TQ == 0 and s % TM_IN == 0 and TQ == TK and TM_IN % TK == 0
    assert TQ % TM_IN == 0 or TM_IN % TQ == 0
    assert N_HEADS % ATTN_HEADS_PER_STEP == 0
    assert s % TM_MIX == 0 and s % TM_FIN == 0 and t % TM_FFN == 0
    cap = t + TM_FFN
    n_e = N_GROUPS * EXPERTS_PER_GROUP

    c8 = jnp.concatenate([c, jnp.zeros((8 - b, d), F32)], axis=0)
    pos3 = positions.reshape(b, s, 1)
    inv_freq = ROPE_THETA ** (-jnp.arange(0, HEAD_DIM, 2, dtype=F32) / HEAD_DIM)
    invf = jnp.tile(inv_freq, LANES // (HEAD_DIM // 2)).reshape(1, LANES)

    xf = x
    for l in range(depth):
        lambda_init = 0.8 - 0.6 * math.exp(-0.3 * l)
        row = lambda a: a[l].reshape(1, -1)
        mod8, lam = _ada(c8, w_ada[l], row(b_ada), row(lambda_q1), row(lambda_k1),
                         row(lambda_q2), row(lambda_k2), lambda_init)
        mod3 = mod8[0:b].reshape(b, 1, 6 * d)

        qt, k, vt, sg, mc = _inproj(xf.reshape(b, s, d), pos3, mod3, row(norm_mix_g), invf,
                                    w_in[l].astype(BF16), conv_w[l], w_up_conv[l].astype(BF16))
        att = _attn(lam, qt, k, vt, subln_g[l].reshape(-1, 1), 1.0 - lambda_init)

        pad = ROUTER_ROWS - n_e - N_GROUPS
        wrt = jnp.concatenate([w_expert_router[l].T, w_group_router[l].T,
                               jnp.zeros((pad, d), F32)], axis=0)
        br = jnp.concatenate([b_expert_router[l], b_group_router[l],
                              jnp.zeros((pad,), F32)]).reshape(ROUTER_ROWS, 1)
        x1, pos, cnt, hs = _mixout(att.reshape(t, -1), sg.reshape(t, d), mc.reshape(t, d),
                                   xf.reshape(t, d), mod3, w_up_att[l].astype(BF16),
                                   w_out[l].astype(BF16), row(norm_ffn_g), wrt, br,
                                   s // TM_MIX, cap)
        ys = _ffn(cnt[0:N_GROUPS, 0], hs, w_gate[l].astype(BF16), w_up[l].astype(BF16),
                  w_down[l].astype(BF16), cap)
        xf = _final(pos.reshape(t), ys, x1, mod3, final_norm_g.reshape(1, d), s // TM_FIN,
                    l == depth - 1)
    return xf.reshape(b, s, d)
```

```python
import functools
import math

import jax
import jax.numpy as jnp
from jax import lax
from jax.experimental import pallas as pl
from jax.experimental.pallas import tpu as pltpu

F32 = jnp.float32
BF16 = jnp.bfloat16
I32 = jnp.int32

EPS = 1e-6
MASK_VALUE = -1e30
ROPE_THETA = 10000.0

N_HEADS = 4
HEAD_DIM = 64
LANES = 128
N_GROUPS = 4
EXPERTS_PER_GROUP = 8
CONV_K = 3
ROUTER_ROWS = 48

VMEM_LIMIT = 56 * 1024 * 1024

TM_IN = 1024
TQ = 512
TK = 512
ATTN_HEADS_PER_STEP = 1
VT_ROWS = LANES + 16
TM_MIX = 1024
TM_FFN = 512
TM_FIN = 512
TN_ADA = 1024


def _cparams(n_axes):
    return pltpu.CompilerParams(
        dimension_semantics=("arbitrary",) * n_axes, vmem_limit_bytes=VMEM_LIMIT)


def _ada_kernel(c_ref, w_ref, b_ref, lq1_ref, lk1_ref, lq2_ref, lk2_ref, mod_ref, lam_ref, *,
                lambda_init):
    c = c_ref[...]
    a = c * jax.nn.sigmoid(c)
    mod_ref[...] = jnp.dot(a.astype(BF16), w_ref[...].astype(BF16),
                           preferred_element_type=F32) + b_ref[...]
    s1 = jnp.sum(lq1_ref[...] * lk1_ref[...], axis=-1, keepdims=True)
    s2 = jnp.sum(lq2_ref[...] * lk2_ref[...], axis=-1, keepdims=True)
    lam = jnp.exp(s1) - jnp.exp(s2) + lambda_init
    lam_ref[...] = jnp.broadcast_to(lam, lam_ref.shape)


def _ada(c8, w_ada, b_ada, lq1, lk1, lq2, lk2, lambda_init):
    d, n = w_ada.shape
    small = pl.BlockSpec((1, HEAD_DIM), lambda j: (0, 0))
    return pl.pallas_call(
        functools.partial(_ada_kernel, lambda_init=lambda_init),
        grid=(n // TN_ADA,),
        in_specs=[pl.BlockSpec((8, d), lambda j: (0, 0)),
                  pl.BlockSpec((d, TN_ADA), lambda j: (0, j)),
                  pl.BlockSpec((1, TN_ADA), lambda j: (0, j)),
                  small, small, small, small],
        out_specs=[pl.BlockSpec((8, TN_ADA), lambda j: (0, j)),
                   pl.BlockSpec((8, LANES), lambda j: (0, 0))],
        out_shape=[jax.ShapeDtypeStruct((8, n), F32), jax.ShapeDtypeStruct((8, LANES), F32)],
        compiler_params=_cparams(1),
        name="ada",
    )(c8, w_ada, b_ada, lq1, lk1, lq2, lk2)


def _inproj_kernel(x_ref, pos_ref, mod_ref, g_ref, invf_ref, w_ref, cw_ref, wupc_ref,
                   qt_ref, k_ref, vt_ref, sg_ref, mc_ref, carry_ref, *, d, qk_w, conv_w):
    tm = x_ref.shape[1]
    x = x_ref[0]
    ms = jnp.mean(x * x, axis=-1, keepdims=True)
    mod = mod_ref[0]
    sh = mod[:, 0:d]
    sc = mod[:, d:2 * d]
    h = (x * lax.rsqrt(ms + EPS) * g_ref[...]) * (1.0 + sc) + sh
    hb = h.astype(BF16)

    def proj(a, b):
        return jnp.dot(hb, w_ref[:, a:b], preferred_element_type=F32)

    lane = lax.broadcasted_iota(I32, (1, LANES), 1)
    first_half = (lane % HEAD_DIM) < (HEAD_DIM // 2)
    ang = pos_ref[0].astype(F32) * invf_ref[...]
    cos = jnp.cos(ang)
    sin = jnp.sin(ang)
    sin_signed = jnp.where(first_half, -sin, sin)

    def rope(t):
        partner = jnp.where(first_half, pltpu.roll(t, LANES - HEAD_DIM // 2, 1),
                            pltpu.roll(t, HEAD_DIM // 2, 1))
        return t * cos + partner * sin_signed

    zq = proj(0, qk_w)
    zk = proj(qk_w, 2 * qk_w)
    scale = HEAD_DIM ** -0.5 * math.log2(math.e)
    zv = proj(2 * qk_w, 3 * qk_w)
    for hd in range(qk_w // LANES):
        sl = slice(hd * LANES, (hd + 1) * LANES)
        rq = rope(zq[:, sl]) * scale
        qw = qt_ref.shape[4]
        for ck in range(tm // qw):
            qt_ref[0, hd, ck] = rq[ck * qw:(ck + 1) * qw, :].T.astype(BF16)
        k_ref[0, :, sl] = rope(zk[:, sl]).astype(BF16)
        for ck in range(tm // TK):
            vt_ref[0, hd, ck, 0:LANES, :] = zv[ck * TK:(ck + 1) * TK, sl].T.astype(BF16)
            vt_ref[0, hd, ck, LANES:VT_ROWS, :] = jnp.ones((VT_ROWS - LANES, TK), BF16)

    o = 3 * qk_w
    cb = proj(o, o + conv_w)
    u = proj(o + conv_w, o + 2 * conv_w) * proj(o + 2 * conv_w, o + 3 * conv_w)

    @pl.when(pl.program_id(1) == 0)
    def _():
        carry_ref[...] = jnp.zeros_like(carry_ref)

    prev = carry_ref[...]
    row = lax.broadcasted_iota(I32, (tm, 1), 0)
    u1 = jnp.where(row == 0, prev[7:8, :], pltpu.roll(u, 1, 0))
    u2 = jnp.where(row == 0, prev[6:7, :],
                   jnp.where(row == 1, prev[7:8, :], pltpu.roll(u, 2, 0)))
    carry_ref[...] = u[tm - 8:tm, :]
    cw = cw_ref[...]
    conv = cb * (cw[0:1, :] * u + cw[1:2, :] * u1 + cw[2:3, :] * u2)
    convp = jnp.dot(conv.astype(BF16), wupc_ref[...], preferred_element_type=F32)

    o = o + 3 * conv_w
    sg_ref[0] = jax.nn.sigmoid(proj(o, o + d)).astype(BF16)
    mc_ref[0] = (jax.nn.sigmoid(proj(o + d, o + 2 * d)) * convp).astype(BF16)


def _inproj(x, pos3, mod3, norm_g, invf, w_in, conv_w, w_up_conv):
    b, s, d = x.shape
    qk_w = N_HEADS * 2 * HEAD_DIM
    cw = conv_w.shape[1]
    tm = TM_IN
    if tm >= TQ:
        qt_spec = pl.BlockSpec((1, N_HEADS, tm // TQ, LANES, TQ), lambda bi, si: (bi, 0, si, 0, 0))
    else:
        qt_spec = pl.BlockSpec((1, N_HEADS, 1, LANES, tm),
                               lambda bi, si: (bi, 0, si // (TQ // tm), 0, si % (TQ // tm)))
    tok = lambda w: pl.BlockSpec((1, tm, w), lambda bi, si: (bi, si, 0))
    full = lambda a: pl.BlockSpec(a.shape, lambda bi, si: (0,) * a.ndim)
    return pl.pallas_call(
        functools.partial(_inproj_kernel, d=d, qk_w=qk_w, conv_w=cw),
        grid=(b, s // tm),
        in_specs=[tok(d), tok(1),
                  pl.BlockSpec((1, 1, mod3.shape[2]), lambda bi, si: (bi, 0, 0)),
                  full(norm_g), full(invf), full(w_in), full(conv_w), full(w_up_conv)],
        out_specs=[qt_spec,
                   tok(qk_w),
                   pl.BlockSpec((1, N_HEADS, tm // TK, VT_ROWS, TK),
                                lambda bi, si: (bi, 0, si, 0, 0)),
                   tok(d), tok(d)],
        out_shape=[jax.ShapeDtypeStruct((b, N_HEADS, s // TQ, LANES, TQ), BF16),
                   jax.ShapeDtypeStruct((b, s, qk_w), BF16),
                   jax.ShapeDtypeStruct((b, N_HEADS, s // TK, VT_ROWS, TK), BF16),
                   jax.ShapeDtypeStruct((b, s, d), BF16),
                   jax.ShapeDtypeStruct((b, s, d), BF16)],
        scratch_shapes=[pltpu.VMEM((8, cw), F32)],
        compiler_params=_cparams(2),
        name="inproj",
    )(x, pos3, mod3, norm_g, invf, w_in, conv_w, w_up_conv)


def _attn_kernel(lam_ref, qt_ref, k_ref, vt_ref, g_ref, o_ref, s_ref, p_ref, qq_ref, bias_ref, *,
                 out_scale):
    hps, n_q, tq = qt_ref.shape[1], qt_ref.shape[2], qt_ref.shape[4]
    feat = lax.broadcasted_iota(I32, (LANES, 1), 0)
    halves = [slice(h * tq, (h + 1) * tq) for h in range(2)]
    streams = [(h, half) for h in range(hps) for half in halves]

    k_sub = lax.broadcasted_iota(I32, (TK, 1), 0)
    q_lane = lax.broadcasted_iota(I32, (1, tq), 1)
    bias_ref[...] = jnp.where(k_sub <= q_lane, 0.0, MASK_VALUE)

    def load_queries(qi):
        for h in range(hps):
            qt = qt_ref[0, h, qi]
            zero = jnp.zeros_like(qt)
            qq_ref[h, :, halves[0]] = jnp.where(feat < HEAD_DIM, qt, zero)
            qq_ref[h, :, halves[1]] = jnp.where(feat >= HEAD_DIM, qt, zero)

    def scores(j):
        for h, lanes in streams:
            kt = k_ref[0, pl.ds(j * TK, TK), h * LANES:(h + 1) * LANES]
            s_ref[h, :, lanes] = jnp.dot(kt, qq_ref[h, :, lanes], preferred_element_type=F32)

    def values(j, state):
        return tuple((m, alpha * acc + jnp.dot(vt_ref[0, h, j], p_ref[h, :, lanes],
                                               preferred_element_type=F32), alpha)
                     for (h, lanes), (m, acc, alpha) in zip(streams, state))

    def softmax(state, diagonal):
        out = []
        for (h, lanes), (m, acc, _) in zip(streams, state):
            load = ((lambda: s_ref[h, :, lanes] + bias_ref[...]) if diagonal
                    else (lambda: s_ref[h, :, lanes]))
            m_new = jnp.maximum(m, jnp.max(load(), axis=0, keepdims=True))
            p_ref[h, :, lanes] = jnp.exp2(load() - m_new).astype(BF16)
            out.append((m_new, acc, jnp.exp2(m - m_new)))
        return tuple(out)

    def finish(qi, last_tile, state):
        state = values(last_tile, state)
        row0 = qi * tq
        for h in range(hps):
            (_, acc1, _), (_, acc2, _) = state[2 * h], state[2 * h + 1]
            o1 = acc1[0:LANES] / acc1[LANES:LANES + 1]
            o2 = acc2[0:LANES] / acc2[LANES:LANES + 1]
            a = o1 - lam_ref[0:1, 0:1] * o2
            y = a * lax.rsqrt(jnp.mean(a * a, axis=0, keepdims=True) + EPS) * g_ref[...]
            o_ref[0, pl.ds(row0, tq), h * LANES:(h + 1) * LANES] = (
                (y * out_scale).T.astype(BF16))

    def next_diagonal(qi):
        if qi + 1 < n_q:
            load_queries(qi + 1)
            scores(qi + 1)

    one = (jnp.full((1, tq), MASK_VALUE, F32), jnp.zeros((VT_ROWS, tq), F32),
           jnp.ones((1, tq), F32))
    init = tuple(one for _ in streams)

    load_queries(0)
    scores(0)
    pending = softmax(init, True)
    next_diagonal(0)

    for qi in range(1, n_q):
        finish(qi - 1, max(qi - 2, 0), pending)
        state = softmax(init, True)
        scores(0)
        for t in range(1, qi + 1):
            state = values(qi if t == 1 else t - 2, state)
            state = softmax(state, False)
            if t < qi:
                scores(t)
            else:
                next_diagonal(qi)
        pending = state
    finish(n_q - 1, max(n_q - 2, 0), pending)


def _attn(lam, qt, k, vt, subln_g_col, out_scale):
    b, s, w = k.shape
    hps = ATTN_HEADS_PER_STEP
    return pl.pallas_call(
        functools.partial(_attn_kernel, out_scale=out_scale),
        grid=(b, N_HEADS // hps),
        in_specs=[pl.BlockSpec((8, LANES), lambda bi, hi: (0, 0)),
                  pl.BlockSpec((1, hps, s // TQ, LANES, TQ), lambda bi, hi: (bi, hi, 0, 0, 0)),
                  pl.BlockSpec((1, s, hps * LANES), lambda bi, hi: (bi, 0, hi)),
                  pl.BlockSpec((1, hps, s // TK, VT_ROWS, TK), lambda bi, hi: (bi, hi, 0, 0, 0)),
                  pl.BlockSpec((LANES, 1), lambda bi, hi: (0, 0))],
        out_specs=pl.BlockSpec((1, s, hps * LANES), lambda bi, hi: (bi, 0, hi)),
        out_shape=jax.ShapeDtypeStruct((b, s, w), BF16),
        scratch_shapes=[pltpu.VMEM((hps, TK, 2 * TQ), F32), pltpu.VMEM((hps, TK, 2 * TQ), BF16),
                        pltpu.VMEM((hps, LANES, 2 * TQ), BF16), pltpu.VMEM((TK, TQ), F32)],
        compiler_params=_cparams(2),
        name="attn",
    )(lam, qt, k, vt, subln_g_col)


def _mixout_kernel(att_ref, sg_ref, mc_ref, x_ref, mod_ref, wua_ref, wo_ref, g_ref, wrt_ref,
                   br_ref, x1_ref, pos_ref, cnt_ref, hs_ref,
                   h2c_ref, tri_ref, carry_ref, posv_ref, cntv_ref, poss_ref, cnts_ref,
                   sem_s, sem_r, *, d, cap, tail):
    tm = x_ref.shape[0]
    i = pl.program_id(0)
    last = pl.num_programs(0) - 1

    @pl.when(i == 0)
    def _():
        carry_ref[...] = jnp.zeros_like(carry_ref)
        earlier = (lax.broadcasted_iota(I32, (tm, 1), 0)
                   <= lax.broadcasted_iota(I32, (1, tm), 1))
        tri_ref[...] = jnp.where(earlier, 1.0, 0.0).astype(BF16)

    mod = mod_ref[0]
    g_m = mod[:, 2 * d:3 * d]
    sh_f = mod[:, 3 * d:4 * d]
    sc_f = mod[:, 4 * d:5 * d]

    attp = jnp.dot(att_ref[...], wua_ref[...], preferred_element_type=F32)
    merged = sg_ref[...].astype(F32) * attp + mc_ref[...].astype(F32)
    mo = jnp.dot(merged.astype(BF16), wo_ref[...], preferred_element_type=F32)
    x1 = x_ref[...] + g_m * mo
    x1_ref[...] = x1
    ms = jnp.mean(x1 * x1, axis=-1, keepdims=True)
    h2 = (x1 * lax.rsqrt(ms + EPS) * g_ref[...]) * (1.0 + sc_f) + sh_f

    def split(a):
        hi = a.astype(BF16)
        return hi, (a - hi.astype(F32)).astype(BF16)

    def dot_t(a, b):
        return lax.dot_general(a, b, (((1,), (1,)), ((), ())), preferred_element_type=F32)

    w_hi, w_lo = split(wrt_ref[...])
    h_hi, h_lo = split(h2)
    by_hi = dot_t(jnp.concatenate([w_hi, w_lo], axis=0), h_hi)
    lt = (by_hi[0:ROUTER_ROWS] + (dot_t(w_hi, h_lo) + by_hi[ROUTER_ROWS:2 * ROUTER_ROWS])
          + br_ref[...])
    ridx = lax.broadcasted_iota(I32, (8, 1), 0)
    n_e = N_GROUPS * EXPERTS_PER_GROUP
    is_grp = ridx < N_GROUPS
    gl = jnp.where(is_grp, lt[n_e:n_e + 8, :], MASK_VALUE)
    gmax = jnp.max(gl, axis=0, keepdims=True)
    gsum = jnp.sum(jnp.where(is_grp, jnp.exp(gl - gmax), 0.0), axis=0, keepdims=True)
    g_w = 1.0 / gsum
    gidx = jnp.min(jnp.where(gl == gmax, ridx, 8), axis=0, keepdims=True)
    e_in = jnp.zeros((8, tm), F32)
    for g in range(N_GROUPS):
        e_in = jnp.where(gidx == g, lt[g * 8:(g + 1) * 8, :], e_in)
    t1 = jnp.max(e_in, axis=0, keepdims=True)
    i1 = jnp.min(jnp.where(e_in == t1, ridx, 8), axis=0, keepdims=True)
    e_in2 = jnp.where(ridx == i1, MASK_VALUE, e_in)
    t2 = jnp.max(e_in2, axis=0, keepdims=True)
    i2 = jnp.min(jnp.where(e_in2 == t2, ridx, 8), axis=0, keepdims=True)
    e2 = jnp.exp(t2 - t1)
    den = 1.0 + e2
    w1 = (1.0 / den) * g_w
    w2 = (e2 / den) * g_w
    cw_t = jnp.where(ridx == i1, w1, jnp.where(ridx == i2, w2, 0.0))
    cw_col = jnp.concatenate([cw_t, jnp.zeros((LANES - 8, tm), F32)], axis=0).T

    onehot = jnp.where(ridx == gidx, 1.0, 0.0)
    oh16 = jnp.concatenate([onehot, jnp.zeros_like(onehot)], axis=0).astype(BF16)
    incl = jnp.dot(oh16, tri_ref[...], preferred_element_type=F32)[0:8, :]
    run = carry_ref[...][:, 0:1]
    base = ridx.astype(F32) * float(cap)
    pos = jnp.sum(onehot * (base + run + incl - onehot), axis=0, keepdims=True).astype(I32)
    new_run = run + incl[:, tm - 1:tm]
    carry_ref[...] = jnp.broadcast_to(new_run, carry_ref.shape)
    cnt_i = jnp.broadcast_to(new_run, cnt_ref.shape).astype(I32)
    cnt_ref[...] = cnt_i
    pos_ref[0] = pos
    posv_ref[...] = pos
    cp = pltpu.make_async_copy(posv_ref.at[0], poss_ref, sem_s)
    cp.start()

    def row_copy(r, p):
        return pltpu.make_async_copy(h2c_ref.at[pl.ds(r, 1), :], hs_ref.at[pl.ds(p, 1), :], sem_r)

    def wait_tile():
        pltpu.make_async_copy(h2c_ref, hs_ref.at[pl.ds(0, tm), :], sem_r).wait()

    @pl.when(i > 0)
    def _():
        wait_tile()

    h2c_ref[:, 0:d] = h2
    h2c_ref[:, d:d + LANES] = cw_col
    cp.wait()

    for r in range(tm):
        row_copy(r, poss_ref[r]).start(priority=r % 2)

    @pl.when(i == last)
    def _():
        wait_tile()
        cntv_ref[...] = cnt_i
        cc = pltpu.make_async_copy(cntv_ref, cnts_ref, sem_s)
        cc.start()
        cc.wait()
        h2c_ref[0:8, :] = jnp.zeros((8, h2c_ref.shape[1]), F32)

        def zissue(r, c):
            for g in range(N_GROUPS):
                row_copy(0, g * cap + cnts_ref[g, 0] + r).start()
            return c

        lax.fori_loop(0, tail, zissue, 0, unroll=2)
        for g in range(N_GROUPS):
            pltpu.make_async_copy(h2c_ref.at[pl.ds(0, tail), :], hs_ref.at[pl.ds(0, tail), :],
                                  sem_r).wait()


def _mixout(att, sg, mc, x, mod3, w_up_att, w_out, norm_g, wrt, br, tiles_per_batch, cap):
    t, d = x.shape
    tm = TM_MIX
    dc = d + LANES
    tok = lambda w: pl.BlockSpec((tm, w), lambda i: (i, 0))
    full = lambda a: pl.BlockSpec(a.shape, lambda i: (0,) * a.ndim)
    return pl.pallas_call(
        functools.partial(_mixout_kernel, d=d, cap=cap, tail=cap - t),
        grid=(t // tm,),
        in_specs=[tok(att.shape[1]), tok(d), tok(d), tok(d),
                  pl.BlockSpec((1, 1, mod3.shape[2]), lambda i: (i // tiles_per_batch, 0, 0)),
                  full(w_up_att), full(w_out), full(norm_g), full(wrt), full(br)],
        out_specs=[tok(d),
                   pl.BlockSpec((1, 1, tm), lambda i: (i, 0, 0)),
                   pl.BlockSpec((8, LANES), lambda i: (0, 0)),
                   pl.BlockSpec(memory_space=pl.ANY)],
        out_shape=[jax.ShapeDtypeStruct((t, d), F32),
                   jax.ShapeDtypeStruct((t // tm, 1, tm), I32),
                   jax.ShapeDtypeStruct((8, LANES), I32),
                   jax.ShapeDtypeStruct((N_GROUPS * cap, dc), F32)],
        scratch_shapes=[pltpu.VMEM((tm, dc), F32), pltpu.VMEM((tm, tm), BF16),
                        pltpu.VMEM((8, LANES), F32),
                        pltpu.VMEM((1, tm), I32), pltpu.VMEM((8, LANES), I32),
                        pltpu.SMEM((tm,), I32), pltpu.SMEM((8, LANES), I32),
                        pltpu.SemaphoreType.DMA, pltpu.SemaphoreType.DMA],
        compiler_params=_cparams(1),
        name="mixout",
    )(att, sg, mc, x, mod3, w_up_att, w_out, norm_g, wrt, br)


def _ffn_tile(step, cnt_ref, tm):
    log_tm = tm.bit_length() - 1
    ends = []
    tot = jnp.int32(0)
    for g in range(N_GROUPS):
        tot = tot + lax.shift_right_logical(cnt_ref[g] + (tm - 1), log_tm)
        ends.append(tot)
    sc = jnp.maximum(jnp.minimum(step, tot - 1), 0)
    g = sum((sc >= e).astype(I32) for e in ends[:-1])
    start = jnp.where(g == 0, 0, jnp.where(g == 1, ends[0], jnp.where(g == 2, ends[1], ends[2])))
    return g, sc - start, step < tot


def _ffn_kernel(cnt_ref, hs_ref, wg_ref, wu_ref, wd_ref, y_ref, *, d):
    tm = hs_ref.shape[0]
    _, _, real = _ffn_tile(pl.program_id(0), cnt_ref, tm)

    @pl.when(real)
    def _():
        h = hs_ref[:, 0:d].astype(BF16)
        cw = hs_ref[:, d:d + LANES]
        acc = jnp.zeros((tm, d), F32)
        for e in range(EXPERTS_PER_GROUP):
            hg = jnp.dot(h, wg_ref[e].astype(BF16), preferred_element_type=F32)
            hu = jnp.dot(h, wu_ref[e].astype(BF16), preferred_element_type=F32)
            a = hg * jax.nn.sigmoid(hg) * hu * cw[:, e:e + 1]
            acc = acc + jnp.dot(a.astype(BF16), wd_ref[e].astype(BF16),
                                preferred_element_type=F32)
        y_ref[...] = acc


def _ffn(cnt, hs, w_gate, w_up, w_down, cap):
    rows, dc = hs.shape
    d = dc - LANES
    tm = TM_FFN
    blocks_per_group = cap // tm
    n_steps = (blocks_per_group - 1) + N_GROUPS

    def row_map(s, c):
        g, blk, _ = _ffn_tile(s, c, tm)
        return (g * blocks_per_group + blk, 0)

    def w_map(s, c):
        g, _, _ = _ffn_tile(s, c, tm)
        return (g, 0, 0)

    ff = w_gate.shape[2]
    once = pl.Buffered(1)
    gs = pltpu.PrefetchScalarGridSpec(
        num_scalar_prefetch=1,
        grid=(n_steps,),
        in_specs=[pl.BlockSpec((tm, dc), row_map),
                  pl.BlockSpec((EXPERTS_PER_GROUP, d, ff), w_map, pipeline_mode=once),
                  pl.BlockSpec((EXPERTS_PER_GROUP, d, ff), w_map, pipeline_mode=once),
                  pl.BlockSpec((EXPERTS_PER_GROUP, ff, d), w_map, pipeline_mode=once)],
        out_specs=pl.BlockSpec((tm, d), row_map),
    )
    return pl.pallas_call(
        functools.partial(_ffn_kernel, d=d),
        grid_spec=gs,
        out_shape=jax.ShapeDtypeStruct((rows, d), F32),
        compiler_params=_cparams(1),
        name="ffn",
    )(cnt, hs, w_gate, w_up, w_down)


def _final_kernel(pos_ref, ys_ref, x1_ref, mod_ref, g_ref, o_ref, ybuf_ref, sem, *, d, normalize):
    tm = x1_ref.shape[0]
    i = pl.program_id(0)

    def gather_tile(tile):
        slot = tile & 1
        base = tile * tm
        for r in range(tm):
            pltpu.make_async_copy(ys_ref.at[pl.ds(pos_ref[base + r], 1), :],
                                  ybuf_ref.at[slot, pl.ds(r, 1), :],
                                  sem.at[slot]).start(priority=r % 2)

    @pl.when(i == 0)
    def _():
        gather_tile(i)

    @pl.when(i + 1 < pl.num_programs(0))
    def _():
        gather_tile(i + 1)

    slot = i & 1
    pltpu.make_async_copy(ys_ref.at[pl.ds(0, tm), :], ybuf_ref.at[slot], sem.at[slot]).wait()

    g_f = mod_ref[0][:, 5 * d:6 * d]
    x2 = x1_ref[...] + g_f * ybuf_ref[slot]
    if normalize:
        x2 = x2 * lax.rsqrt(jnp.mean(x2 * x2, axis=-1, keepdims=True) + EPS) * g_ref[...]
    o_ref[...] = x2


def _final(pos, ys, x1, mod3, final_g, tiles_per_batch, normalize):
    t, d = x1.shape
    tm = TM_FIN
    gs = pltpu.PrefetchScalarGridSpec(
        num_scalar_prefetch=1,
        grid=(t // tm,),
        in_specs=[pl.BlockSpec(memory_space=pl.ANY),
                  pl.BlockSpec((tm, d), lambda i, p: (i, 0)),
                  pl.BlockSpec((1, 1, mod3.shape[2]), lambda i, p: (i // tiles_per_batch, 0, 0)),
                  pl.BlockSpec((1, d), lambda i, p: (0, 0))],
        out_specs=pl.BlockSpec((tm, d), lambda i, p: (i, 0)),
        scratch_shapes=[pltpu.VMEM((2, tm, d), F32), pltpu.SemaphoreType.DMA((2,))],
    )
    return pl.pallas_call(
        functools.partial(_final_kernel, d=d, normalize=normalize),
        grid_spec=gs,
        out_shape=jax.ShapeDtypeStruct((t, d), F32),
        compiler_params=_cparams(1),
        name="final",
    )(pos, ys, x1, mod3, final_g)


def kernel(x, c, positions, w_ada, b_ada, norm_mix_g, w_in, lambda_q1, lambda_k1, lambda_q2,
           lambda_k2, subln_g, conv_w, w_up_att, w_up_conv, w_out, norm_ffn_g, w_group_router,
           b_group_router, w_expert_router, b_expert_router, w_gate, w_up, w_down, final_norm_g):
    b, s, d = x.shape
    depth = w_ada.shape[0]
    t = b * s
    assert s % TQ == 0 and s % TM_IN == 0 and TQ == TK and TM_IN % TK == 0
    assert TQ % TM_IN == 0 or TM_IN % TQ == 0
    assert N_HEADS % ATTN_HEADS_PER_STEP == 0
    assert s % TM_MIX == 0 and s % TM_FIN == 0 and t % TM_FFN == 0
    cap = t + TM_FFN
    n_e = N_GROUPS * EXPERTS_PER_GROUP

    c8 = jnp.concatenate([c, jnp.zeros((8 - b, d), F32)], axis=0)
    pos3 = positions.reshape(b, s, 1)
    inv_freq = ROPE_THETA ** (-jnp.arange(0, HEAD_DIM, 2, dtype=F32) / HEAD_DIM)
    invf = jnp.tile(inv_freq, LANES // (HEAD_DIM // 2)).reshape(1, LANES)

    xf = x
    for l in range(depth):
        lambda_init = 0.8 - 0.6 * math.exp(-0.3 * l)
        row = lambda a: a[l].reshape(1, -1)
        mod8, lam = _ada(c8, w_ada[l], row(b_ada), row(lambda_q1), row(lambda_k1),
                         row(lambda_q2), row(lambda_k2), lambda_init)
        mod3 = mod8[0:b].reshape(b, 1, 6 * d)

        qt, k, vt, sg, mc = _inproj(xf.reshape(b, s, d), pos3, mod3, row(norm_mix_g), invf,
                                    w_in[l].astype(BF16), conv_w[l], w_up_conv[l].astype(BF16))
        att = _attn(lam, qt, k, vt, subln_g[l].reshape(-1, 1), 1.0 - lambda_init)

        pad = ROUTER_ROWS - n_e - N_GROUPS
        wrt = jnp.concatenate([w_expert_router[l].T, w_group_router[l].T,
                               jnp.zeros((pad, d), F32)], axis=0)
        br = jnp.concatenate([b_expert_router[l], b_group_router[l],
                              jnp.zeros((pad,), F32)]).reshape(ROUTER_ROWS, 1)
        x1, pos, cnt, hs = _mixout(att.reshape(t, -1), sg.reshape(t, d), mc.reshape(t, d),
                                   xf.reshape(t, d), mod3, w_up_att[l].astype(BF16),
                                   w_out[l].astype(BF16), row(norm_ffn_g), wrt, br,
                                   s // TM_MIX, cap)
        ys = _ffn(cnt[0:N_GROUPS, 0], hs, w_gate[l], w_up[l], w_down[l], cap)
        xf = _final(pos.reshape(t), ys, x1, mod3, final_norm_g.reshape(1, d), s // TM_FIN,
                    l == depth - 1)
    return xf.reshape(b, s, d)
```

```python
import functools
import math

import jax
import jax.numpy as jnp
from jax import lax
from jax.experimental import pallas as pl
from jax.experimental.pallas import tpu as pltpu

F32 = jnp.float32
BF16 = jnp.bfloat16
I32 = jnp.int32

EPS = 1e-6
MASK_VALUE = -1e30
ROPE_THETA = 10000.0

N_HEADS = 4
HEAD_DIM = 64
LANES = 128
N_GROUPS = 4
EXPERTS_PER_GROUP = 8
CONV_K = 3
ROUTER_ROWS = 48

VMEM_LIMIT = 56 * 1024 * 1024

TM_IN = 1024
TQ = 512
TK = 512
ATTN_HEADS_PER_STEP = 1
VT_ROWS = LANES + 16
TM_MIX = 1024
TM_FFN = 512
TM_FIN = 512
TN_ADA = 1024


def _cparams(n_axes):
    return pltpu.CompilerParams(
        dimension_semantics=("arbitrary",) * n_axes, vmem_limit_bytes=VMEM_LIMIT)


def _ada_kernel(c_ref, w_ref, b_ref, lq1_ref, lk1_ref, lq2_ref, lk2_ref, mod_ref, lam_ref, *,
                lambda_init):
    c = c_ref[...]
    a = c * jax.nn.sigmoid(c)
    mod_ref[...] = jnp.dot(a.astype(BF16), w_ref[...].astype(BF16),
                           preferred_element_type=F32) + b_ref[...]
    s1 = jnp.sum(lq1_ref[...] * lk1_ref[...], axis=-1, keepdims=True)
    s2 = jnp.sum(lq2_ref[...] * lk2_ref[...], axis=-1, keepdims=True)
    lam = jnp.exp(s1) - jnp.exp(s2) + lambda_init
    lam_ref[...] = jnp.broadcast_to(lam, lam_ref.shape)


def _ada(c8, w_ada, b_ada, lq1, lk1, lq2, lk2, lambda_init):
    d, n = w_ada.shape
    small = pl.BlockSpec((1, HEAD_DIM), lambda j: (0, 0))
    return pl.pallas_call(
        functools.partial(_ada_kernel, lambda_init=lambda_init),
        grid=(n // TN_ADA,),
        in_specs=[pl.BlockSpec((8, d), lambda j: (0, 0)),
                  pl.BlockSpec((d, TN_ADA), lambda j: (0, j)),
                  pl.BlockSpec((1, TN_ADA), lambda j: (0, j)),
                  small, small, small, small],
        out_specs=[pl.BlockSpec((8, TN_ADA), lambda j: (0, j)),
                   pl.BlockSpec((8, LANES), lambda j: (0, 0))],
        out_shape=[jax.ShapeDtypeStruct((8, n), F32), jax.ShapeDtypeStruct((8, LANES), F32)],
        compiler_params=_cparams(1),
        name="ada",
    )(c8, w_ada, b_ada, lq1, lk1, lq2, lk2)


def _inproj_kernel(x_ref, pos_ref, mod_ref, g_ref, invf_ref, w_ref, cw_ref, wupc_ref,
                   qt_ref, k_ref, vt_ref, sg_ref, mc_ref, carry_ref, *, d, qk_w, conv_w):
    tm = x_ref.shape[1]
    x = x_ref[0]
    ms = jnp.mean(x * x, axis=-1, keepdims=True)
    mod = mod_ref[0]
    sh = mod[:, 0:d]
    sc = mod[:, d:2 * d]
    h = (x * lax.rsqrt(ms + EPS) * g_ref[...]) * (1.0 + sc) + sh
    hb = h.astype(BF16)

    def proj(a, b):
        return jnp.dot(hb, w_ref[:, a:b].astype(BF16), preferred_element_type=F32)

    lane = lax.broadcasted_iota(I32, (1, LANES), 1)
    first_half = (lane % HEAD_DIM) < (HEAD_DIM // 2)
    ang = pos_ref[0].astype(F32) * invf_ref[...]
    cos = jnp.cos(ang)
    sin = jnp.sin(ang)
    sin_signed = jnp.where(first_half, -sin, sin)

    def rope(t):
        partner = jnp.where(first_half, pltpu.roll(t, LANES - HEAD_DIM // 2, 1),
                            pltpu.roll(t, HEAD_DIM // 2, 1))
        return t * cos + partner * sin_signed

    zq = proj(0, qk_w)
    zk = proj(qk_w, 2 * qk_w)
    scale = HEAD_DIM ** -0.5 * math.log2(math.e)
    zv = proj(2 * qk_w, 3 * qk_w)
    for hd in range(qk_w // LANES):
        sl = slice(hd * LANES, (hd + 1) * LANES)
        rq = rope(zq[:, sl]) * scale
        qw = qt_ref.shape[4]
        for ck in range(tm // qw):
            qt_ref[0, hd, ck] = rq[ck * qw:(ck + 1) * qw, :].T.astype(BF16)
        k_ref[0, :, sl] = rope(zk[:, sl]).astype(BF16)
        for ck in range(tm // TK):
            vt_ref[0, hd, ck, 0:LANES, :] = zv[ck * TK:(ck + 1) * TK, sl].T.astype(BF16)
            vt_ref[0, hd, ck, LANES:VT_ROWS, :] = jnp.ones((VT_ROWS - LANES, TK), BF16)

    o = 3 * qk_w
    cb = proj(o, o + conv_w)
    u = proj(o + conv_w, o + 2 * conv_w) * proj(o + 2 * conv_w, o + 3 * conv_w)

    @pl.when(pl.program_id(1) == 0)
    def _():
        carry_ref[...] = jnp.zeros_like(carry_ref)

    prev = carry_ref[...]
    row = lax.broadcasted_iota(I32, (tm, 1), 0)
    u1 = jnp.where(row == 0, prev[7:8, :], pltpu.roll(u, 1, 0))
    u2 = jnp.where(row == 0, prev[6:7, :],
                   jnp.where(row == 1, prev[7:8, :], pltpu.roll(u, 2, 0)))
    carry_ref[...] = u[tm - 8:tm, :]
    cw = cw_ref[...]
    conv = cb * (cw[0:1, :] * u + cw[1:2, :] * u1 + cw[2:3, :] * u2)
    convp = jnp.dot(conv.astype(BF16), wupc_ref[...].astype(BF16), preferred_element_type=F32)

    o = o + 3 * conv_w
    sg_ref[0] = jax.nn.sigmoid(proj(o, o + d)).astype(BF16)
    mc_ref[0] = (jax.nn.sigmoid(proj(o + d, o + 2 * d)) * convp).astype(BF16)


def _inproj(x, pos3, mod3, norm_g, invf, w_in, conv_w, w_up_conv):
    b, s, d = x.shape
    qk_w = N_HEADS * 2 * HEAD_DIM
    cw = conv_w.shape[1]
    tm = TM_IN
    if tm >= TQ:
        qt_spec = pl.BlockSpec((1, N_HEADS, tm // TQ, LANES, TQ), lambda bi, si: (bi, 0, si, 0, 0))
    else:
        qt_spec = pl.BlockSpec((1, N_HEADS, 1, LANES, tm),
                               lambda bi, si: (bi, 0, si // (TQ // tm), 0, si % (TQ // tm)))
    tok = lambda w: pl.BlockSpec((1, tm, w), lambda bi, si: (bi, si, 0))
    full = lambda a: pl.BlockSpec(a.shape, lambda bi, si: (0,) * a.ndim)
    return pl.pallas_call(
        functools.partial(_inproj_kernel, d=d, qk_w=qk_w, conv_w=cw),
        grid=(b, s // tm),
        in_specs=[tok(d), tok(1),
                  pl.BlockSpec((1, 1, mod3.shape[2]), lambda bi, si: (bi, 0, 0)),
                  full(norm_g), full(invf), full(w_in), full(conv_w), full(w_up_conv)],
        out_specs=[qt_spec,
                   tok(qk_w),
                   pl.BlockSpec((1, N_HEADS, tm // TK, VT_ROWS, TK),
                                lambda bi, si: (bi, 0, si, 0, 0)),
                   tok(d), tok(d)],
        out_shape=[jax.ShapeDtypeStruct((b, N_HEADS, s // TQ, LANES, TQ), BF16),
                   jax.ShapeDtypeStruct((b, s, qk_w), BF16),
                   jax.ShapeDtypeStruct((b, N_HEADS, s // TK, VT_ROWS, TK), BF16),
                   jax.ShapeDtypeStruct((b, s, d), BF16),
                   jax.ShapeDtypeStruct((b, s, d), BF16)],
        scratch_shapes=[pltpu.VMEM((8, cw), F32)],
        compiler_params=_cparams(2),
        name="inproj",
    )(x, pos3, mod3, norm_g, invf, w_in, conv_w, w_up_conv)


def _attn_kernel(lam_ref, qt_ref, k_ref, vt_ref, g_ref, o_ref, s_ref, p_ref, qq_ref, bias_ref, *,
                 out_scale):
    hps, n_q, tq = qt_ref.shape[1], qt_ref.shape[2], qt_ref.shape[4]
    feat = lax.broadcasted_iota(I32, (LANES, 1), 0)
    halves = [slice(h * tq, (h + 1) * tq) for h in range(2)]
    streams = [(h, half) for h in range(hps) for half in halves]

    k_sub = lax.broadcasted_iota(I32, (TK, 1), 0)
    q_lane = lax.broadcasted_iota(I32, (1, tq), 1)
    bias_ref[...] = jnp.where(k_sub <= q_lane, 0.0, MASK_VALUE)

    def load_queries(qi):
        for h in range(hps):
            qt = qt_ref[0, h, qi]
            zero = jnp.zeros_like(qt)
            qq_ref[h, :, halves[0]] = jnp.where(feat < HEAD_DIM, qt, zero)
            qq_ref[h, :, halves[1]] = jnp.where(feat >= HEAD_DIM, qt, zero)

    def scores(j):
        for h, lanes in streams:
            kt = k_ref[0, pl.ds(j * TK, TK), h * LANES:(h + 1) * LANES]
            s_ref[h, :, lanes] = jnp.dot(kt, qq_ref[h, :, lanes], preferred_element_type=F32)

    def values(j, state):
        return tuple((m, alpha * acc + jnp.dot(vt_ref[0, h, j], p_ref[h, :, lanes],
                                               preferred_element_type=F32), alpha)
                     for (h, lanes), (m, acc, alpha) in zip(streams, state))

    def softmax(state, diagonal):
        out = []
        for (h, lanes), (m, acc, _) in zip(streams, state):
            load = ((lambda: s_ref[h, :, lanes] + bias_ref[...]) if diagonal
                    else (lambda: s_ref[h, :, lanes]))
            m_new = jnp.maximum(m, jnp.max(load(), axis=0, keepdims=True))
            p_ref[h, :, lanes] = jnp.exp2(load() - m_new).astype(BF16)
            out.append((m_new, acc, jnp.exp2(m - m_new)))
        return tuple(out)

    def finish(qi, last_tile, state):
        state = values(last_tile, state)
        row0 = qi * tq
        for h in range(hps):
            (_, acc1, _), (_, acc2, _) = state[2 * h], state[2 * h + 1]
            o1 = acc1[0:LANES] / acc1[LANES:LANES + 1]
            o2 = acc2[0:LANES] / acc2[LANES:LANES + 1]
            a = o1 - lam_ref[0:1, 0:1] * o2
            y = a * lax.rsqrt(jnp.mean(a * a, axis=0, keepdims=True) + EPS) * g_ref[...]
            o_ref[0, pl.ds(row0, tq), h * LANES:(h + 1) * LANES] = (
                (y * out_scale).T.astype(BF16))

    def next_diagonal(qi):
        if qi + 1 < n_q:
            load_queries(qi + 1)
            scores(qi + 1)

    one = (jnp.full((1, tq), MASK_VALUE, F32), jnp.zeros((VT_ROWS, tq), F32),
           jnp.ones((1, tq), F32))
    init = tuple(one for _ in streams)

    load_queries(0)
    scores(0)
    pending = softmax(init, True)
    next_diagonal(0)

    for qi in range(1, n_q):
        finish(qi - 1, max(qi - 2, 0), pending)
        state = softmax(init, True)
        scores(0)
        for t in range(1, qi + 1):
            state = values(qi if t == 1 else t - 2, state)
            state = softmax(state, False)
            if t < qi:
                scores(t)
            else:
                next_diagonal(qi)
        pending = state
    finish(n_q - 1, max(n_q - 2, 0), pending)


def _attn(lam, qt, k, vt, subln_g_col, out_scale):
    b, s, w = k.shape
    hps = ATTN_HEADS_PER_STEP
    return pl.pallas_call(
        functools.partial(_attn_kernel, out_scale=out_scale),
        grid=(b, N_HEADS // hps),
        in_specs=[pl.BlockSpec((8, LANES), lambda bi, hi: (0, 0)),
                  pl.BlockSpec((1, hps, s // TQ, LANES, TQ), lambda bi, hi: (bi, hi, 0, 0, 0)),
                  pl.BlockSpec((1, s, hps * LANES), lambda bi, hi: (bi, 0, hi)),
                  pl.BlockSpec((1, hps, s // TK, VT_ROWS, TK), lambda bi, hi: (bi, hi, 0, 0, 0)),
                  pl.BlockSpec((LANES, 1), lambda bi, hi: (0, 0))],
        out_specs=pl.BlockSpec((1, s, hps * LANES), lambda bi, hi: (bi, 0, hi)),
        out_shape=jax.ShapeDtypeStruct((b, s, w), BF16),
        scratch_shapes=[pltpu.VMEM((hps, TK, 2 * TQ), F32), pltpu.VMEM((hps, TK, 2 * TQ), BF16),
                        pltpu.VMEM((hps, LANES, 2 * TQ), BF16), pltpu.VMEM((TK, TQ), F32)],
        compiler_params=_cparams(2),
        name="attn",
    )(lam, qt, k, vt, subln_g_col)


def _mixout_kernel(att_ref, sg_ref, mc_ref, x_ref, mod_ref, wua_ref, wo_ref, g_ref, wrt_ref,
                   br_ref, x1_ref, pos_ref, cnt_ref, hs_ref,
                   h2c_ref, tri_ref, carry_ref, posv_ref, cntv_ref, poss_ref, cnts_ref,
                   sem_s, sem_r, *, d, cap, tail):
    tm = x_ref.shape[0]
    i = pl.program_id(0)
    last = pl.num_programs(0) - 1

    @pl.when(i == 0)
    def _():
        carry_ref[...] = jnp.zeros_like(carry_ref)
        earlier = (lax.broadcasted_iota(I32, (tm, 1), 0)
                   <= lax.broadcasted_iota(I32, (1, tm), 1))
        tri_ref[...] = jnp.where(earlier, 1.0, 0.0).astype(BF16)

    mod = mod_ref[0]
    g_m = mod[:, 2 * d:3 * d]
    sh_f = mod[:, 3 * d:4 * d]
    sc_f = mod[:, 4 * d:5 * d]

    attp = jnp.dot(att_ref[...], wua_ref[...].astype(BF16), preferred_element_type=F32)
    merged = sg_ref[...].astype(F32) * attp + mc_ref[...].astype(F32)
    mo = jnp.dot(merged.astype(BF16), wo_ref[...].astype(BF16), preferred_element_type=F32)
    x1 = x_ref[...] + g_m * mo
    x1_ref[...] = x1
    ms = jnp.mean(x1 * x1, axis=-1, keepdims=True)
    h2 = (x1 * lax.rsqrt(ms + EPS) * g_ref[...]) * (1.0 + sc_f) + sh_f

    def split(a):
        hi = a.astype(BF16)
        return hi, (a - hi.astype(F32)).astype(BF16)

    def dot_t(a, b):
        return lax.dot_general(a, b, (((1,), (1,)), ((), ())), preferred_element_type=F32)

    w_hi, w_lo = split(wrt_ref[...])
    h_hi, h_lo = split(h2)
    by_hi = dot_t(jnp.concatenate([w_hi, w_lo], axis=0), h_hi)
    lt = (by_hi[0:ROUTER_ROWS] + (dot_t(w_hi, h_lo) + by_hi[ROUTER_ROWS:2 * ROUTER_ROWS])
          + br_ref[...])
    ridx = lax.broadcasted_iota(I32, (8, 1), 0)
    n_e = N_GROUPS * EXPERTS_PER_GROUP
    is_grp = ridx < N_GROUPS
    gl = jnp.where(is_grp, lt[n_e:n_e + 8, :], MASK_VALUE)
    gmax = jnp.max(gl, axis=0, keepdims=True)
    gsum = jnp.sum(jnp.where(is_grp, jnp.exp(gl - gmax), 0.0), axis=0, keepdims=True)
    g_w = 1.0 / gsum
    gidx = jnp.min(jnp.where(gl == gmax, ridx, 8), axis=0, keepdims=True)
    e_in = jnp.zeros((8, tm), F32)
    for g in range(N_GROUPS):
        e_in = jnp.where(gidx == g, lt[g * 8:(g + 1) * 8, :], e_in)
    t1 = jnp.max(e_in, axis=0, keepdims=True)
    i1 = jnp.min(jnp.where(e_in == t1, ridx, 8), axis=0, keepdims=True)
    e_in2 = jnp.where(ridx == i1, MASK_VALUE, e_in)
    t2 = jnp.max(e_in2, axis=0, keepdims=True)
    i2 = jnp.min(jnp.where(e_in2 == t2, ridx, 8), axis=0, keepdims=True)
    e2 = jnp.exp(t2 - t1)
    den = 1.0 + e2
    w1 = (1.0 / den) * g_w
    w2 = (e2 / den) * g_w
    cw_t = jnp.where(ridx == i1, w1, jnp.where(ridx == i2, w2, 0.0))
    cw_col = jnp.concatenate([cw_t, jnp.zeros((LANES - 8, tm), F32)], axis=0).T

    onehot = jnp.where(ridx == gidx, 1.0, 0.0)
    oh16 = jnp.concatenate([onehot, jnp.zeros_like(onehot)], axis=0).astype(BF16)
    incl = jnp.dot(oh16, tri_ref[...], preferred_element_type=F32)[0:8, :]
    run = carry_ref[...][:, 0:1]
    base = ridx.astype(F32) * float(cap)
    pos = jnp.sum(onehot * (base + run + incl - onehot), axis=0, keepdims=True).astype(I32)
    new_run = run + incl[:, tm - 1:tm]
    carry_ref[...] = jnp.broadcast_to(new_run, carry_ref.shape)
    cnt_i = jnp.broadcast_to(new_run, cnt_ref.shape).astype(I32)
    cnt_ref[...] = cnt_i
    pos_ref[0] = pos
    posv_ref[...] = pos
    cp = pltpu.make_async_copy(posv_ref.at[0], poss_ref, sem_s)
    cp.start()

    def row_copy(r, p):
        return pltpu.make_async_copy(h2c_ref.at[pl.ds(r, 1), :], hs_ref.at[pl.ds(p, 1), :], sem_r)

    def wait_tile():
        pltpu.make_async_copy(h2c_ref, hs_ref.at[pl.ds(0, tm), :], sem_r).wait()

    @pl.when(i > 0)
    def _():
        wait_tile()

    h2c_ref[:, 0:d] = h2
    h2c_ref[:, d:d + LANES] = cw_col
    cp.wait()

    for r in range(tm):
        row_copy(r, poss_ref[r]).start(priority=r % 2)

    @pl.when(i == last)
    def _():
        wait_tile()
        cntv_ref[...] = cnt_i
        cc = pltpu.make_async_copy(cntv_ref, cnts_ref, sem_s)
        cc.start()
        cc.wait()
        h2c_ref[0:8, :] = jnp.zeros((8, h2c_ref.shape[1]), F32)

        def zissue(r, c):
            for g in range(N_GROUPS):
                row_copy(0, g * cap + cnts_ref[g, 0] + r).start()
            return c

        lax.fori_loop(0, tail, zissue, 0, unroll=2)
        for g in range(N_GROUPS):
            pltpu.make_async_copy(h2c_ref.at[pl.ds(0, tail), :], hs_ref.at[pl.ds(0, tail), :],
                                  sem_r).wait()


def _mixout(att, sg, mc, x, mod3, w_up_att, w_out, norm_g, wrt, br, tiles_per_batch, cap):
    t, d = x.shape
    tm = TM_MIX
    dc = d + LANES
    tok = lambda w: pl.BlockSpec((tm, w), lambda i: (i, 0))
    full = lambda a: pl.BlockSpec(a.shape, lambda i: (0,) * a.ndim)
    return pl.pallas_call(
        functools.partial(_mixout_kernel, d=d, cap=cap, tail=cap - t),
        grid=(t // tm,),
        in_specs=[tok(att.shape[1]), tok(d), tok(d), tok(d),
                  pl.BlockSpec((1, 1, mod3.shape[2]), lambda i: (i // tiles_per_batch, 0, 0)),
                  full(w_up_att), full(w_out), full(norm_g), full(wrt), full(br)],
        out_specs=[tok(d),
                   pl.BlockSpec((1, 1, tm), lambda i: (i, 0, 0)),
                   pl.BlockSpec((8, LANES), lambda i: (0, 0)),
                   pl.BlockSpec(memory_space=pl.ANY)],
        out_shape=[jax.ShapeDtypeStruct((t, d), F32),
                   jax.ShapeDtypeStruct((t // tm, 1, tm), I32),
                   jax.ShapeDtypeStruct((8, LANES), I32),
                   jax.ShapeDtypeStruct((N_GROUPS * cap, dc), F32)],
        scratch_shapes=[pltpu.VMEM((tm, dc), F32), pltpu.VMEM((tm, tm), BF16),
                        pltpu.VMEM((8, LANES), F32),
                        pltpu.VMEM((1, tm), I32), pltpu.VMEM((8, LANES), I32),
                        pltpu.SMEM((tm,), I32), pltpu.SMEM((8, LANES), I32),
                        pltpu.SemaphoreType.DMA, pltpu.SemaphoreType.DMA],
        compiler_params=_cparams(1),
        name="mixout",
    )(att, sg, mc, x, mod3, w_up_att, w_out, norm_g, wrt, br)


def _ffn_tile(step, cnt_ref, tm):
    log_tm = tm.bit_length() - 1
    ends = []
    tot = jnp.int32(0)
    for g in range(N_GROUPS):
        tot = tot + lax.shift_right_logical(cnt_ref[g] + (tm - 1), log_tm)
        ends.append(tot)
    sc = jnp.maximum(jnp.minimum(step, tot - 1), 0)
    g = sum((sc >= e).astype(I32) for e in ends[:-1])
    start = jnp.where(g == 0, 0, jnp.where(g == 1, ends[0], jnp.where(g == 2, ends[1], ends[2])))
    return g, sc - start, step < tot


def _ffn_kernel(cnt_ref, hs_ref, wg_ref, wu_ref, wd_ref, y_ref, *, d):
    tm = hs_ref.shape[0]
    _, _, real = _ffn_tile(pl.program_id(0), cnt_ref, tm)

    @pl.when(real)
    def _():
        h = hs_ref[:, 0:d].astype(BF16)
        cw = hs_ref[:, d:d + LANES]
        acc = jnp.zeros((tm, d), F32)
        for e in range(EXPERTS_PER_GROUP):
            hg = jnp.dot(h, wg_ref[e].astype(BF16), preferred_element_type=F32)
            hu = jnp.dot(h, wu_ref[e].astype(BF16), preferred_element_type=F32)
            a = hg * jax.nn.sigmoid(hg) * hu * cw[:, e:e + 1]
            acc = acc + jnp.dot(a.astype(BF16), wd_ref[e].astype(BF16),
                                preferred_element_type=F32)
        y_ref[...] = acc


def _ffn(cnt, hs, w_gate, w_up, w_down, cap):
    rows, dc = hs.shape
    d = dc - LANES
    tm = TM_FFN
    blocks_per_group = cap // tm
    n_steps = (blocks_per_group - 1) + N_GROUPS

    def row_map(s, c):
        g, blk, _ = _ffn_tile(s, c, tm)
        return (g * blocks_per_group + blk, 0)

    def w_map(s, c):
        g, _, _ = _ffn_tile(s, c, tm)
        return (g, 0, 0)

    ff = w_gate.shape[2]
    once = pl.Buffered(1)
    gs = pltpu.PrefetchScalarGridSpec(
        num_scalar_prefetch=1,
        grid=(n_steps,),
        in_specs=[pl.BlockSpec((tm, dc), row_map),
                  pl.BlockSpec((EXPERTS_PER_GROUP, d, ff), w_map),
                  pl.BlockSpec((EXPERTS_PER_GROUP, d, ff), w_map, pipeline_mode=once),
                  pl.BlockSpec((EXPERTS_PER_GROUP, ff, d), w_map, pipeline_mode=once)],
        out_specs=pl.BlockSpec((tm, d), row_map),
    )
    return pl.pallas_call(
        functools.partial(_ffn_kernel, d=d),
        grid_spec=gs,
        out_shape=jax.ShapeDtypeStruct((rows, d), F32),
        compiler_params=_cparams(1),
        name="ffn",
    )(cnt, hs, w_gate, w_up, w_down)


def _final_kernel(pos_ref, ys_ref, x1_ref, mod_ref, g_ref, o_ref, ybuf_ref, sem, *, d, normalize):
    tm = x1_ref.shape[0]
    i = pl.program_id(0)

    def gather_tile(tile):
        slot = tile & 1
        base = tile * tm
        for r in range(tm):
            pltpu.make_async_copy(ys_ref.at[pl.ds(pos_ref[base + r], 1), :],
                                  ybuf_ref.at[slot, pl.ds(r, 1), :],
                                  sem.at[slot]).start(priority=r % 2)

    @pl.when(i == 0)
    def _():
        gather_tile(i)

    @pl.when(i + 1 < pl.num_programs(0))
    def _():
        gather_tile(i + 1)

    slot = i & 1
    pltpu.make_async_copy(ys_ref.at[pl.ds(0, tm), :], ybuf_ref.at[slot], sem.at[slot]).wait()

    g_f = mod_ref[0][:, 5 * d:6 * d]
    x2 = x1_ref[...] + g_f * ybuf_ref[slot]
    if normalize:
        x2 = x2 * lax.rsqrt(jnp.mean(x2 * x2, axis=-1, keepdims=True) + EPS) * g_ref[...]
    o_ref[...] = x2


def _final(pos, ys, x1, mod3, final_g, tiles_per_batch, normalize):
    t, d = x1.shape
    tm = TM_FIN
    gs = pltpu.PrefetchScalarGridSpec(
        num_scalar_prefetch=1,
        grid=(t // tm,),
        in_specs=[pl.BlockSpec(memory_space=pl.ANY),
                  pl.BlockSpec((tm, d), lambda i, p: (i, 0)),
                  pl.BlockSpec((1, 1, mod3.shape[2]), lambda i, p: (i // tiles_per_batch, 0, 0)),
                  pl.BlockSpec((1, d), lambda i, p: (0, 0))],
        out_specs=pl.BlockSpec((tm, d), lambda i, p: (i, 0)),
        scratch_shapes=[pltpu.VMEM((2, tm, d), F32), pltpu.SemaphoreType.DMA((2,))],
    )
    return pl.pallas_call(
        functools.partial(_final_kernel, d=d, normalize=normalize),
        grid_spec=gs,
        out_shape=jax.ShapeDtypeStruct((t, d), F32),
        compiler_params=_cparams(1),
        name="final",
    )(pos, ys, x1, mod3, final_g)


def kernel(x, c, positions, w_ada, b_ada, norm_mix_g, w_in, lambda_q1, lambda_k1, lambda_q2,
           lambda_k2, subln_g, conv_w, w_up_att, w_up_conv, w_out, norm_ffn_g, w_group_router,
           b_group_router, w_expert_router, b_expert_router, w_gate, w_up, w_down, final_norm_g):
    b, s, d = x.shape
    depth = w_ada.shape[0]
    t = b * s
    assert s % TQ == 0 and s % TM_IN == 0 and TQ == TK and TM_IN % TK == 0
    assert TQ % TM_IN == 0 or TM_IN % TQ == 0
    assert N_HEADS % ATTN_HEADS_PER_STEP == 0
    assert s % TM_MIX == 0 and s % TM_FIN == 0 and t % TM_FFN == 0
    cap = t + TM_FFN
    n_e = N_GROUPS * EXPERTS_PER_GROUP

    c8 = jnp.concatenate([c, jnp.zeros((8 - b, d), F32)], axis=0)
    pos3 = positions.reshape(b, s, 1)
    inv_freq = ROPE_THETA ** (-jnp.arange(0, HEAD_DIM, 2, dtype=F32) / HEAD_DIM)
    invf = jnp.tile(inv_freq, LANES // (HEAD_DIM // 2)).reshape(1, LANES)

    xf = x
    for l in range(depth):
        lambda_init = 0.8 - 0.6 * math.exp(-0.3 * l)
        row = lambda a: a[l].reshape(1, -1)
        mod8, lam = _ada(c8, w_ada[l], row(b_ada), row(lambda_q1), row(lambda_k1),
                         row(lambda_q2), row(lambda_k2), lambda_init)
        mod3 = mod8[0:b].reshape(b, 1, 6 * d)

        qt, k, vt, sg, mc = _inproj(xf.reshape(b, s, d), pos3, mod3, row(norm_mix_g), invf,
                                    w_in[l], conv_w[l], w_up_conv[l])
        att = _attn(lam, qt, k, vt, subln_g[l].reshape(-1, 1), 1.0 - lambda_init)

        pad = ROUTER_ROWS - n_e - N_GROUPS
        wrt = jnp.concatenate([w_expert_router[l].T, w_group_router[l].T,
                               jnp.zeros((pad, d), F32)], axis=0)
        br = jnp.concatenate([b_expert_router[l], b_group_router[l],
                              jnp.zeros((pad,), F32)]).reshape(ROUTER_ROWS, 1)
        x1, pos, cnt, hs = _mixout(att.reshape(t, -1), sg.reshape(t, d), mc.reshape(t, d),
                                   xf.reshape(t, d), mod3, w_up_att[l], w_out[l],
                                   row(norm_ffn_g), wrt, br,
                                   s // TM_MIX, cap)
        ys = _ffn(cnt[0:N_GROUPS, 0], hs, w_gate[l], w_up[l], w_down[l], cap)
        xf = _final(pos.reshape(t), ys, x1, mod3, final_norm_g.reshape(1, d), s // TM_FIN,
                    l == depth - 1)
    return xf.reshape(b, s, d)
```

```python
import functools
import math

import jax
import jax.numpy as jnp
from jax import lax
from jax.experimental import pallas as pl
from jax.experimental.pallas import tpu as pltpu

F32 = jnp.float32
BF16 = jnp.bfloat16
I32 = jnp.int32

EPS = 1e-6
MASK_VALUE = -1e30
ROPE_THETA = 10000.0

N_HEADS = 4
HEAD_DIM = 64
LANES = 128
N_GROUPS = 4
EXPERTS_PER_GROUP = 8
CONV_K = 3
ROUTER_ROWS = 48

VMEM_LIMIT = 56 * 1024 * 1024

TM_IN = 1024
TQ = 512
TK = 512
ATTN_HEADS_PER_STEP = 1
VT_ROWS = LANES + 16
TM_MIX = 1024
TM_FFN = 512
TM_FIN = 512
TN_ADA = 1024


def _cparams(n_axes):
    return pltpu.CompilerParams(
        dimension_semantics=("arbitrary",) * n_axes, vmem_limit_bytes=VMEM_LIMIT)


def _ada_kernel(c_ref, w_ref, b_ref, lq1_ref, lk1_ref, lq2_ref, lk2_ref, mod_ref, lam_ref, *,
                lambda_init):
    c = c_ref[...]
    a = c * jax.nn.sigmoid(c)
    mod_ref[...] = jnp.dot(a.astype(BF16), w_ref[...].astype(BF16),
                           preferred_element_type=F32) + b_ref[...]
    s1 = jnp.sum(lq1_ref[...] * lk1_ref[...], axis=-1, keepdims=True)
    s2 = jnp.sum(lq2_ref[...] * lk2_ref[...], axis=-1, keepdims=True)
    lam = jnp.exp(s1) - jnp.exp(s2) + lambda_init
    lam_ref[...] = jnp.broadcast_to(lam, lam_ref.shape)


def _ada(c8, w_ada, b_ada, lq1, lk1, lq2, lk2, lambda_init):
    d, n = w_ada.shape
    small = pl.BlockSpec((1, HEAD_DIM), lambda j: (0, 0))
    return pl.pallas_call(
        functools.partial(_ada_kernel, lambda_init=lambda_init),
        grid=(n // TN_ADA,),
        in_specs=[pl.BlockSpec((8, d), lambda j: (0, 0)),
                  pl.BlockSpec((d, TN_ADA), lambda j: (0, j)),
                  pl.BlockSpec((1, TN_ADA), lambda j: (0, j)),
                  small, small, small, small],
        out_specs=[pl.BlockSpec((8, TN_ADA), lambda j: (0, j)),
                   pl.BlockSpec((8, LANES), lambda j: (0, 0))],
        out_shape=[jax.ShapeDtypeStruct((8, n), F32), jax.ShapeDtypeStruct((8, LANES), F32)],
        compiler_params=_cparams(1),
        name="ada",
    )(c8, w_ada, b_ada, lq1, lk1, lq2, lk2)


def _inproj_kernel(x_ref, pos_ref, mod_ref, g_ref, invf_ref, w_ref, cw_ref, wupc_ref,
                   qt_ref, k_ref, vt_ref, sg_ref, mc_ref, carry_ref, *, d, qk_w, conv_w):
    tm = x_ref.shape[1]
    x = x_ref[0]
    ms = jnp.mean(x * x, axis=-1, keepdims=True)
    mod = mod_ref[0]
    sh = mod[:, 0:d]
    sc = mod[:, d:2 * d]
    h = (x * lax.rsqrt(ms + EPS) * g_ref[...]) * (1.0 + sc) + sh
    hb = h.astype(BF16)

    def proj(a, b):
        return jnp.dot(hb, w_ref[:, a:b].astype(BF16), preferred_element_type=F32)

    lane = lax.broadcasted_iota(I32, (1, LANES), 1)
    first_half = (lane % HEAD_DIM) < (HEAD_DIM // 2)
    ang = pos_ref[0].astype(F32) * invf_ref[...]
    cos = jnp.cos(ang)
    sin = jnp.sin(ang)
    sin_signed = jnp.where(first_half, -sin, sin)

    def rope(t):
        partner = jnp.where(first_half, pltpu.roll(t, LANES - HEAD_DIM // 2, 1),
                            pltpu.roll(t, HEAD_DIM // 2, 1))
        return t * cos + partner * sin_signed

    zq = proj(0, qk_w)
    zk = proj(qk_w, 2 * qk_w)
    scale = HEAD_DIM ** -0.5 * math.log2(math.e)
    zv = proj(2 * qk_w, 3 * qk_w)
    for hd in range(qk_w // LANES):
        sl = slice(hd * LANES, (hd + 1) * LANES)
        rq = rope(zq[:, sl]) * scale
        qw = qt_ref.shape[4]
        for ck in range(tm // qw):
            qt_ref[0, hd, ck] = rq[ck * qw:(ck + 1) * qw, :].T.astype(BF16)
        k_ref[0, :, sl] = rope(zk[:, sl]).astype(BF16)
        for ck in range(tm // TK):
            vt_ref[0, hd, ck, 0:LANES, :] = zv[ck * TK:(ck + 1) * TK, sl].T.astype(BF16)
            vt_ref[0, hd, ck, LANES:VT_ROWS, :] = jnp.ones((VT_ROWS - LANES, TK), BF16)

    o = 3 * qk_w
    cb = proj(o, o + conv_w)
    u = proj(o + conv_w, o + 2 * conv_w) * proj(o + 2 * conv_w, o + 3 * conv_w)

    @pl.when(pl.program_id(1) == 0)
    def _():
        carry_ref[...] = jnp.zeros_like(carry_ref)

    prev = carry_ref[...]
    row = lax.broadcasted_iota(I32, (tm, 1), 0)
    u1 = jnp.where(row == 0, prev[7:8, :], pltpu.roll(u, 1, 0))
    u2 = jnp.where(row == 0, prev[6:7, :],
                   jnp.where(row == 1, prev[7:8, :], pltpu.roll(u, 2, 0)))
    carry_ref[...] = u[tm - 8:tm, :]
    cw = cw_ref[...]
    conv = cb * (cw[0:1, :] * u + cw[1:2, :] * u1 + cw[2:3, :] * u2)
    convp = jnp.dot(conv.astype(BF16), wupc_ref[...].astype(BF16), preferred_element_type=F32)

    o = o + 3 * conv_w
    sg_ref[0] = jax.nn.sigmoid(proj(o, o + d)).astype(BF16)
    mc_ref[0] = (jax.nn.sigmoid(proj(o + d, o + 2 * d)) * convp).astype(BF16)


def _inproj(x, pos3, mod3, norm_g, invf, w_in, conv_w, w_up_conv):
    b, s, d = x.shape
    qk_w = N_HEADS * 2 * HEAD_DIM
    cw = conv_w.shape[1]
    tm = TM_IN
    if tm >= TQ:
        qt_spec = pl.BlockSpec((1, N_HEADS, tm // TQ, LANES, TQ), lambda bi, si: (bi, 0, si, 0, 0))
    else:
        qt_spec = pl.BlockSpec((1, N_HEADS, 1, LANES, tm),
                               lambda bi, si: (bi, 0, si // (TQ // tm), 0, si % (TQ // tm)))
    tok = lambda w: pl.BlockSpec((1, tm, w), lambda bi, si: (bi, si, 0))
    full = lambda a: pl.BlockSpec(a.shape, lambda bi, si: (0,) * a.ndim)
    return pl.pallas_call(
        functools.partial(_inproj_kernel, d=d, qk_w=qk_w, conv_w=cw),
        grid=(b, s // tm),
        in_specs=[tok(d), tok(1),
                  pl.BlockSpec((1, 1, mod3.shape[2]), lambda bi, si: (bi, 0, 0)),
                  full(norm_g), full(invf), full(w_in), full(conv_w), full(w_up_conv)],
        out_specs=[qt_spec,
                   tok(qk_w),
                   pl.BlockSpec((1, N_HEADS, tm // TK, VT_ROWS, TK),
                                lambda bi, si: (bi, 0, si, 0, 0)),
                   tok(d), tok(d)],
        out_shape=[jax.ShapeDtypeStruct((b, N_HEADS, s // TQ, LANES, TQ), BF16),
                   jax.ShapeDtypeStruct((b, s, qk_w), BF16),
                   jax.ShapeDtypeStruct((b, N_HEADS, s // TK, VT_ROWS, TK), BF16),
                   jax.ShapeDtypeStruct((b, s, d), BF16),
                   jax.ShapeDtypeStruct((b, s, d), BF16)],
        scratch_shapes=[pltpu.VMEM((8, cw), F32)],
        compiler_params=_cparams(2),
        name="inproj",
    )(x, pos3, mod3, norm_g, invf, w_in, conv_w, w_up_conv)


def _attn_kernel(lam_ref, qt_ref, k_ref, vt_ref, g_ref, o_ref, s_ref, p_ref, qq_ref, bias_ref, *,
                 out_scale):
    hps, n_q, tq = qt_ref.shape[1], qt_ref.shape[2], qt_ref.shape[4]
    feat = lax.broadcasted_iota(I32, (LANES, 1), 0)
    halves = [slice(h * tq, (h + 1) * tq) for h in range(2)]
    streams = [(h, half) for h in range(hps) for half in halves]

    k_sub = lax.broadcasted_iota(I32, (TK, 1), 0)
    q_lane = lax.broadcasted_iota(I32, (1, tq), 1)
    bias_ref[...] = jnp.where(k_sub <= q_lane, 0.0, MASK_VALUE)

    def load_queries(qi):
        for h in range(hps):
            qt = qt_ref[0, h, qi]
            zero = jnp.zeros_like(qt)
            qq_ref[h, :, halves[0]] = jnp.where(feat < HEAD_DIM, qt, zero)
            qq_ref[h, :, halves[1]] = jnp.where(feat >= HEAD_DIM, qt, zero)

    def scores(j):
        for h, lanes in streams:
            kt = k_ref[0, pl.ds(j * TK, TK), h * LANES:(h + 1) * LANES]
            s_ref[h, :, lanes] = jnp.dot(kt, qq_ref[h, :, lanes], preferred_element_type=F32)

    def values(j, state):
        return tuple((m, alpha * acc + jnp.dot(vt_ref[0, h, j], p_ref[h, :, lanes],
                                               preferred_element_type=F32), alpha)
                     for (h, lanes), (m, acc, alpha) in zip(streams, state))

    def softmax(state, diagonal):
        out = []
        for (h, lanes), (m, acc, _) in zip(streams, state):
            load = ((lambda: s_ref[h, :, lanes] + bias_ref[...]) if diagonal
                    else (lambda: s_ref[h, :, lanes]))
            m_new = jnp.maximum(m, jnp.max(load(), axis=0, keepdims=True))
            p_ref[h, :, lanes] = jnp.exp2(load() - m_new).astype(BF16)
            out.append((m_new, acc, jnp.exp2(m - m_new)))
        return tuple(out)

    def finish(qi, last_tile, state):
        state = values(last_tile, state)
        row0 = qi * tq
        for h in range(hps):
            (_, acc1, _), (_, acc2, _) = state[2 * h], state[2 * h + 1]
            o1 = acc1[0:LANES] / acc1[LANES:LANES + 1]
            o2 = acc2[0:LANES] / acc2[LANES:LANES + 1]
            a = o1 - lam_ref[0:1, 0:1] * o2
            y = a * lax.rsqrt(jnp.mean(a * a, axis=0, keepdims=True) + EPS) * g_ref[...]
            o_ref[0, pl.ds(row0, tq), h * LANES:(h + 1) * LANES] = (
                (y * out_scale).T.astype(BF16))

    def next_diagonal(qi):
        if qi + 1 < n_q:
            load_queries(qi + 1)
            scores(qi + 1)

    one = (jnp.full((1, tq), MASK_VALUE, F32), jnp.zeros((VT_ROWS, tq), F32),
           jnp.ones((1, tq), F32))
    init = tuple(one for _ in streams)

    load_queries(0)
    scores(0)
    pending = softmax(init, True)
    next_diagonal(0)

    for qi in range(1, n_q):
        finish(qi - 1, max(qi - 2, 0), pending)
        state = softmax(init, True)
        scores(0)
        for t in range(1, qi + 1):
            state = values(qi if t == 1 else t - 2, state)
            state = softmax(state, False)
            if t < qi:
                scores(t)
            else:
                next_diagonal(qi)
        pending = state
    finish(n_q - 1, max(n_q - 2, 0), pending)


def _attn(lam, qt, k, vt, subln_g_col, out_scale):
    b, s, w = k.shape
    hps = ATTN_HEADS_PER_STEP
    return pl.pallas_call(
        functools.partial(_attn_kernel, out_scale=out_scale),
        grid=(b, N_HEADS // hps),
        in_specs=[pl.BlockSpec((8, LANES), lambda bi, hi: (0, 0)),
                  pl.BlockSpec((1, hps, s // TQ, LANES, TQ), lambda bi, hi: (bi, hi, 0, 0, 0)),
                  pl.BlockSpec((1, s, hps * LANES), lambda bi, hi: (bi, 0, hi)),
                  pl.BlockSpec((1, hps, s // TK, VT_ROWS, TK), lambda bi, hi: (bi, hi, 0, 0, 0)),
                  pl.BlockSpec((LANES, 1), lambda bi, hi: (0, 0))],
        out_specs=pl.BlockSpec((1, s, hps * LANES), lambda bi, hi: (bi, 0, hi)),
        out_shape=jax.ShapeDtypeStruct((b, s, w), BF16),
        scratch_shapes=[pltpu.VMEM((hps, TK, 2 * TQ), F32), pltpu.VMEM((hps, TK, 2 * TQ), BF16),
                        pltpu.VMEM((hps, LANES, 2 * TQ), BF16), pltpu.VMEM((TK, TQ), F32)],
        compiler_params=_cparams(2),
        name="attn",
    )(lam, qt, k, vt, subln_g_col)


def _mixout_kernel(att_ref, sg_ref, mc_ref, x_ref, mod_ref, wua_ref, wo_ref, g_ref, wrt_ref,
                   br_ref, x1_ref, pos_ref, cnt_ref, hs_ref,
                   h2c_ref, tri_ref, carry_ref, posv_ref, cntv_ref, poss_ref, cnts_ref,
                   sem_s, sem_r, *, d, cap, tail):
    tm = x_ref.shape[0]
    i = pl.program_id(0)
    last = pl.num_programs(0) - 1

    @pl.when(i == 0)
    def _():
        carry_ref[...] = jnp.zeros_like(carry_ref)
        earlier = (lax.broadcasted_iota(I32, (tm, 1), 0)
                   <= lax.broadcasted_iota(I32, (1, tm), 1))
        tri_ref[...] = jnp.where(earlier, 1.0, 0.0).astype(BF16)

    mod = mod_ref[0]
    g_m = mod[:, 2 * d:3 * d]
    sh_f = mod[:, 3 * d:4 * d]
    sc_f = mod[:, 4 * d:5 * d]

    attp = jnp.dot(att_ref[...], wua_ref[...].astype(BF16), preferred_element_type=F32)
    merged = sg_ref[...].astype(F32) * attp + mc_ref[...].astype(F32)
    mo = jnp.dot(merged.astype(BF16), wo_ref[...].astype(BF16), preferred_element_type=F32)
    x1 = x_ref[...] + g_m * mo
    x1_ref[...] = x1
    ms = jnp.mean(x1 * x1, axis=-1, keepdims=True)
    h2 = (x1 * lax.rsqrt(ms + EPS) * g_ref[...]) * (1.0 + sc_f) + sh_f

    def split(a):
        hi = a.astype(BF16)
        return hi, (a - hi.astype(F32)).astype(BF16)

    def dot_t(a, b):
        return lax.dot_general(a, b, (((1,), (1,)), ((), ())), preferred_element_type=F32)

    w_hi, w_lo = split(wrt_ref[...])
    h_hi, h_lo = split(h2)
    by_hi = dot_t(jnp.concatenate([w_hi, w_lo], axis=0), h_hi)
    lt = (by_hi[0:ROUTER_ROWS] + (dot_t(w_hi, h_lo) + by_hi[ROUTER_ROWS:2 * ROUTER_ROWS])
          + br_ref[...])
    ridx = lax.broadcasted_iota(I32, (8, 1), 0)
    n_e = N_GROUPS * EXPERTS_PER_GROUP
    is_grp = ridx < N_GROUPS
    gl = jnp.where(is_grp, lt[n_e:n_e + 8, :], MASK_VALUE)
    gmax = jnp.max(gl, axis=0, keepdims=True)
    gsum = jnp.sum(jnp.where(is_grp, jnp.exp(gl - gmax), 0.0), axis=0, keepdims=True)
    g_w = 1.0 / gsum
    gidx = jnp.min(jnp.where(gl == gmax, ridx, 8), axis=0, keepdims=True)
    e_in = jnp.zeros((8, tm), F32)
    for g in range(N_GROUPS):
        e_in = jnp.where(gidx == g, lt[g * 8:(g + 1) * 8, :], e_in)
    t1 = jnp.max(e_in, axis=0, keepdims=True)
    i1 = jnp.min(jnp.where(e_in == t1, ridx, 8), axis=0, keepdims=True)
    e_in2 = jnp.where(ridx == i1, MASK_VALUE, e_in)
    t2 = jnp.max(e_in2, axis=0, keepdims=True)
    i2 = jnp.min(jnp.where(e_in2 == t2, ridx, 8), axis=0, keepdims=True)
    e2 = jnp.exp(t2 - t1)
    den = 1.0 + e2
    w1 = (1.0 / den) * g_w
    w2 = (e2 / den) * g_w
    cw_t = jnp.where(ridx == i1, w1, jnp.where(ridx == i2, w2, 0.0))
    cw_col = jnp.concatenate([cw_t, jnp.zeros((LANES - 8, tm), F32)], axis=0).T

    onehot = jnp.where(ridx == gidx, 1.0, 0.0)
    oh16 = jnp.concatenate([onehot, jnp.zeros_like(onehot)], axis=0).astype(BF16)
    incl = jnp.dot(oh16, tri_ref[...], preferred_element_type=F32)[0:8, :]
    run = carry_ref[...][:, 0:1]
    base = ridx.astype(F32) * float(cap)
    pos = jnp.sum(onehot * (base + run + incl - onehot), axis=0, keepdims=True).astype(I32)
    new_run = run + incl[:, tm - 1:tm]
    carry_ref[...] = jnp.broadcast_to(new_run, carry_ref.shape)
    cnt_i = jnp.broadcast_to(new_run, cnt_ref.shape).astype(I32)
    cnt_ref[...] = cnt_i
    pos_ref[0] = pos
    posv_ref[...] = pos
    cp = pltpu.make_async_copy(posv_ref.at[0], poss_ref, sem_s)
    cp.start()

    def row_copy(r, p):
        return pltpu.make_async_copy(h2c_ref.at[pl.ds(r, 1), :], hs_ref.at[pl.ds(p, 1), :], sem_r)

    def wait_tile():
        pltpu.make_async_copy(h2c_ref, hs_ref.at[pl.ds(0, tm), :], sem_r).wait()

    @pl.when(i > 0)
    def _():
        wait_tile()

    h2c_ref[:, 0:d] = h2
    h2c_ref[:, d:d + LANES] = cw_col
    cp.wait()

    for r in range(tm):
        row_copy(r, poss_ref[r]).start(priority=r % 2)

    @pl.when(i == last)
    def _():
        wait_tile()
        cntv_ref[...] = cnt_i
        cc = pltpu.make_async_copy(cntv_ref, cnts_ref, sem_s)
        cc.start()
        cc.wait()
        h2c_ref[0:8, :] = jnp.zeros((8, h2c_ref.shape[1]), F32)

        def zissue(r, c):
            for g in range(N_GROUPS):
                row_copy(0, g * cap + cnts_ref[g, 0] + r).start()
            return c

        lax.fori_loop(0, tail, zissue, 0, unroll=2)
        for g in range(N_GROUPS):
            pltpu.make_async_copy(h2c_ref.at[pl.ds(0, tail), :], hs_ref.at[pl.ds(0, tail), :],
                                  sem_r).wait()


def _mixout(att, sg, mc, x, mod3, w_up_att, w_out, norm_g, wrt, br, tiles_per_batch, cap):
    t, d = x.shape
    tm = TM_MIX
    dc = d + LANES
    tok = lambda w: pl.BlockSpec((tm, w), lambda i: (i, 0))
    full = lambda a: pl.BlockSpec(a.shape, lambda i: (0,) * a.ndim)
    return pl.pallas_call(
        functools.partial(_mixout_kernel, d=d, cap=cap, tail=cap - t),
        grid=(t // tm,),
        in_specs=[tok(att.shape[1]), tok(d), tok(d), tok(d),
                  pl.BlockSpec((1, 1, mod3.shape[2]), lambda i: (i // tiles_per_batch, 0, 0)),
                  full(w_up_att), full(w_out), full(norm_g), full(wrt), full(br)],
        out_specs=[tok(d),
                   pl.BlockSpec((1, 1, tm), lambda i: (i, 0, 0)),
                   pl.BlockSpec((8, LANES), lambda i: (0, 0)),
                   pl.BlockSpec(memory_space=pl.ANY)],
        out_shape=[jax.ShapeDtypeStruct((t, d), F32),
                   jax.ShapeDtypeStruct((t // tm, 1, tm), I32),
                   jax.ShapeDtypeStruct((8, LANES), I32),
                   jax.ShapeDtypeStruct((N_GROUPS * cap, dc), F32)],
        scratch_shapes=[pltpu.VMEM((tm, dc), F32), pltpu.VMEM((tm, tm), BF16),
                        pltpu.VMEM((8, LANES), F32),
                        pltpu.VMEM((1, tm), I32), pltpu.VMEM((8, LANES), I32),
                        pltpu.SMEM((tm,), I32), pltpu.SMEM((8, LANES), I32),
                        pltpu.SemaphoreType.DMA, pltpu.SemaphoreType.DMA],
        compiler_params=_cparams(1),
        name="mixout",
    )(att, sg, mc, x, mod3, w_up_att, w_out, norm_g, wrt, br)


def _ffn_tile(step, cnt_ref, tm):
    log_tm = tm.bit_length() - 1
    ends = []
    tot = jnp.int32(0)
    for g in range(N_GROUPS):
        tot = tot + lax.shift_right_logical(cnt_ref[g] + (tm - 1), log_tm)
        ends.append(tot)
    sc = jnp.maximum(jnp.minimum(step, tot - 1), 0)
    g = sum((sc >= e).astype(I32) for e in ends[:-1])
    start = jnp.where(g == 0, 0, jnp.where(g == 1, ends[0], jnp.where(g == 2, ends[1], ends[2])))
    return g, sc - start, step < tot


def _ffn_kernel(cnt_ref, hs_ref, wg_ref, wu_ref, wd_ref, y_ref, *, d):
    tm = hs_ref.shape[0]
    _, _, real = _ffn_tile(pl.program_id(0), cnt_ref, tm)

    @pl.when(real)
    def _():
        h = hs_ref[:, 0:d].astype(BF16)
        cw = hs_ref[:, d:d + LANES]
        acc = jnp.zeros((tm, d), F32)
        for e in range(EXPERTS_PER_GROUP):
            hg = jnp.dot(h, wg_ref[e].astype(BF16), preferred_element_type=F32)
            hu = jnp.dot(h, wu_ref[e].astype(BF16), preferred_element_type=F32)
            a = hg * jax.nn.sigmoid(hg) * hu * cw[:, e:e + 1]
            acc = acc + jnp.dot(a.astype(BF16), wd_ref[e].astype(BF16),
                                preferred_element_type=F32)
        y_ref[...] = acc


def _ffn(cnt, hs, w_gate, w_up, w_down, cap):
    rows, dc = hs.shape
    d = dc - LANES
    tm = TM_FFN
    blocks_per_group = cap // tm
    n_steps = (blocks_per_group - 1) + N_GROUPS

    def row_map(s, c):
        g, blk, _ = _ffn_tile(s, c, tm)
        return (g * blocks_per_group + blk, 0)

    def w_map(s, c):
        g, _, _ = _ffn_tile(s, c, tm)
        return (g, 0, 0)

    ff = w_gate.shape[2]
    once = pl.Buffered(1)
    gs = pltpu.PrefetchScalarGridSpec(
        num_scalar_prefetch=1,
        grid=(n_steps,),
        in_specs=[pl.BlockSpec((tm, dc), row_map),
                  pl.BlockSpec((EXPERTS_PER_GROUP, d, ff), w_map),
                  pl.BlockSpec((EXPERTS_PER_GROUP, d, ff), w_map),
                  pl.BlockSpec((EXPERTS_PER_GROUP, ff, d), w_map, pipeline_mode=once)],
        out_specs=pl.BlockSpec((tm, d), row_map),
    )
    return pl.pallas_call(
        functools.partial(_ffn_kernel, d=d),
        grid_spec=gs,
        out_shape=jax.ShapeDtypeStruct((rows, d), F32),
        compiler_params=_cparams(1),
        name="ffn",
    )(cnt, hs, w_gate, w_up, w_down)


def _final_kernel(pos_ref, ys_ref, x1_ref, mod_ref, g_ref, o_ref, ybuf_ref, sem, *, d, normalize):
    tm = x1_ref.shape[0]
    i = pl.program_id(0)

    def gather_tile(tile):
        slot = tile & 1
        base = tile * tm
        for r in range(tm):
            pltpu.make_async_copy(ys_ref.at[pl.ds(pos_ref[base + r], 1), :],
                                  ybuf_ref.at[slot, pl.ds(r, 1), :],
                                  sem.at[slot]).start(priority=r % 2)

    @pl.when(i == 0)
    def _():
        gather_tile(i)

    @pl.when(i + 1 < pl.num_programs(0))
    def _():
        gather_tile(i + 1)

    slot = i & 1
    pltpu.make_async_copy(ys_ref.at[pl.ds(0, tm), :], ybuf_ref.at[slot], sem.at[slot]).wait()

    g_f = mod_ref[0][:, 5 * d:6 * d]
    x2 = x1_ref[...] + g_f * ybuf_ref[slot]
    if normalize:
        x2 = x2 * lax.rsqrt(jnp.mean(x2 * x2, axis=-1, keepdims=True) + EPS) * g_ref[...]
    o_ref[...] = x2


def _final(pos, ys, x1, mod3, final_g, tiles_per_batch, normalize):
    t, d = x1.shape
    tm = TM_FIN
    gs = pltpu.PrefetchScalarGridSpec(
        num_scalar_prefetch=1,
        grid=(t // tm,),
        in_specs=[pl.BlockSpec(memory_space=pl.ANY),
                  pl.BlockSpec((tm, d), lambda i, p: (i, 0)),
                  pl.BlockSpec((1, 1, mod3.shape[2]), lambda i, p: (i // tiles_per_batch, 0, 0)),
                  pl.BlockSpec((1, d), lambda i, p: (0, 0))],
        out_specs=pl.BlockSpec((tm, d), lambda i, p: (i, 0)),
        scratch_shapes=[pltpu.VMEM((2, tm, d), F32), pltpu.SemaphoreType.DMA((2,))],
    )
    return pl.pallas_call(
        functools.partial(_final_kernel, d=d, normalize=normalize),
        grid_spec=gs,
        out_shape=jax.ShapeDtypeStruct((t, d), F32),
        compiler_params=_cparams(1),
        name="final",
    )(pos, ys, x1, mod3, final_g)


def kernel(x, c, positions, w_ada, b_ada, norm_mix_g, w_in, lambda_q1, lambda_k1, lambda_q2,
           lambda_k2, subln_g, conv_w, w_up_att, w_up_conv, w_out, norm_ffn_g, w_group_router,
           b_group_router, w_expert_router, b_expert_router, w_gate, w_up, w_down, final_norm_g):
    b, s, d = x.shape
    depth = w_ada.shape[0]
    t = b * s
    assert s % TQ == 0 and s % TM_IN == 0 and TQ == TK and TM_IN % TK == 0
    assert TQ % TM_IN == 0 or TM_IN % TQ == 0
    assert N_HEADS % ATTN_HEADS_PER_STEP == 0
    assert s % TM_MIX == 0 and s % TM_FIN == 0 and t % TM_FFN == 0
    cap = t + TM_FFN
    n_e = N_GROUPS * EXPERTS_PER_GROUP

    c8 = jnp.concatenate([c, jnp.zeros((8 - b, d), F32)], axis=0)
    pos3 = positions.reshape(b, s, 1)
    inv_freq = ROPE_THETA ** (-jnp.arange(0, HEAD_DIM, 2, dtype=F32) / HEAD_DIM)
    invf = jnp.tile(inv_freq, LANES // (HEAD_DIM // 2)).reshape(1, LANES)

    xf = x
    for l in range(depth):
        lambda_init = 0.8 - 0.6 * math.exp(-0.3 * l)
        row = lambda a: a[l].reshape(1, -1)
        mod8, lam = _ada(c8, w_ada[l], row(b_ada), row(lambda_q1), row(lambda_k1),
                         row(lambda_q2), row(lambda_k2), lambda_init)
        mod3 = mod8[0:b].reshape(b, 1, 6 * d)

        qt, k, vt, sg, mc = _inproj(xf.reshape(b, s, d), pos3, mod3, row(norm_mix_g), invf,
                                    w_in[l], conv_w[l], w_up_conv[l])
        att = _attn(lam, qt, k, vt, subln_g[l].reshape(-1, 1), 1.0 - lambda_init)

        pad = ROUTER_ROWS - n_e - N_GROUPS
        wrt = jnp.concatenate([w_expert_router[l].T, w_group_router[l].T,
                               jnp.zeros((pad, d), F32)], axis=0)
        br = jnp.concatenate([b_expert_router[l], b_group_router[l],
                              jnp.zeros((pad,), F32)]).reshape(ROUTER_ROWS, 1)
        x1, pos, cnt, hs = _mixout(att.reshape(t, -1), sg.reshape(t, d), mc.reshape(t, d),
                                   xf.reshape(t, d), mod3, w_up_att[l], w_out[l],
                                   row(norm_ffn_g), wrt, br,
                                   s // TM_MIX, cap)
        ys = _ffn(cnt[0:N_GROUPS, 0], hs, w_gate[l], w_up[l], w_down[l], cap)
        xf = _final(pos.reshape(t), ys, x1, mod3, final_norm_g.reshape(1, d), s // TM_FIN,
                    l == depth - 1)
    return xf.reshape(b, s, d)
```

```python
import functools
import math

import jax
import jax.numpy as jnp
from jax import lax
from jax.experimental import pallas as pl
from jax.experimental.pallas import tpu as pltpu

F32 = jnp.float32
BF16 = jnp.bfloat16
I32 = jnp.int32

EPS = 1e-6
MASK_VALUE = -1e30
ROPE_THETA = 10000.0

N_HEADS = 4
HEAD_DIM = 64
LANES = 128
SUBLANES = 8
N_GROUPS = 4
EXPERTS_PER_GROUP = 8
ROUTER_ROWS = 48

VMEM_LIMIT = 56 * 1024 * 1024

TM_IN = 1024
TQ = 512
TK = 512
ATTN_HEADS_PER_STEP = 1
VT_ROWS = LANES + 16
TM_MIX = 1024
TM_FFN = 512
TM_FIN = 512
TN_ADA = 1024


def _cparams(n_axes):
    return pltpu.CompilerParams(
        dimension_semantics=("arbitrary",) * n_axes, vmem_limit_bytes=VMEM_LIMIT)


def _ada_kernel(c_ref, w_ref, b_ref, lq1_ref, lk1_ref, lq2_ref, lk2_ref, mod_ref, lam_ref, *,
                lambda_init):
    c = c_ref[...]
    a = c * jax.nn.sigmoid(c)
    mod_ref[...] = jnp.dot(a.astype(BF16), w_ref[...].astype(BF16),
                           preferred_element_type=F32) + b_ref[...]
    s1 = jnp.sum(lq1_ref[...] * lk1_ref[...], axis=-1, keepdims=True)
    s2 = jnp.sum(lq2_ref[...] * lk2_ref[...], axis=-1, keepdims=True)
    lam = jnp.exp(s1) - jnp.exp(s2) + lambda_init
    lam_ref[...] = jnp.broadcast_to(lam, lam_ref.shape)


def _ada(c8, w_ada, b_ada, lq1, lk1, lq2, lk2, lambda_init):
    d, n = w_ada.shape
    small = pl.BlockSpec((1, HEAD_DIM), lambda j: (0, 0))
    return pl.pallas_call(
        functools.partial(_ada_kernel, lambda_init=lambda_init),
        grid=(n // TN_ADA,),
        in_specs=[pl.BlockSpec((SUBLANES, d), lambda j: (0, 0)),
                  pl.BlockSpec((d, TN_ADA), lambda j: (0, j)),
                  pl.BlockSpec((1, TN_ADA), lambda j: (0, j)),
                  small, small, small, small],
        out_specs=[pl.BlockSpec((SUBLANES, TN_ADA), lambda j: (0, j)),
                   pl.BlockSpec((SUBLANES, LANES), lambda j: (0, 0))],
        out_shape=[jax.ShapeDtypeStruct((SUBLANES, n), F32),
                   jax.ShapeDtypeStruct((SUBLANES, LANES), F32)],
        compiler_params=_cparams(1),
        name="ada",
    )(c8, w_ada, b_ada, lq1, lk1, lq2, lk2)


def _inproj_kernel(x_ref, pos_ref, mod_ref, g_ref, invf_ref, w_ref, cw_ref, wupc_ref,
                   qt_ref, k_ref, vt_ref, sg_ref, mc_ref, carry_ref, *, d, qk_w, conv_w):
    tm = x_ref.shape[1]
    x = x_ref[0]
    ms = jnp.mean(x * x, axis=-1, keepdims=True)
    mod = mod_ref[0]
    sh = mod[:, 0:d]
    sc = mod[:, d:2 * d]
    h = (x * lax.rsqrt(ms + EPS) * g_ref[...]) * (1.0 + sc) + sh
    hb = h.astype(BF16)

    def proj(a, b):
        return jnp.dot(hb, w_ref[:, a:b].astype(BF16), preferred_element_type=F32)

    lane = lax.broadcasted_iota(I32, (1, LANES), 1)
    first_half = (lane % HEAD_DIM) < (HEAD_DIM // 2)
    ang = pos_ref[0].astype(F32) * invf_ref[...]
    cos = jnp.cos(ang)
    sin = jnp.sin(ang)
    sin_signed = jnp.where(first_half, -sin, sin)

    def rope(t):
        partner = jnp.where(first_half, pltpu.roll(t, LANES - HEAD_DIM // 2, 1),
                            pltpu.roll(t, HEAD_DIM // 2, 1))
        return t * cos + partner * sin_signed

    zq = proj(0, qk_w)
    zk = proj(qk_w, 2 * qk_w)
    scale = HEAD_DIM ** -0.5 * math.log2(math.e)
    zv = proj(2 * qk_w, 3 * qk_w)
    for hd in range(qk_w // LANES):
        sl = slice(hd * LANES, (hd + 1) * LANES)
        rq = rope(zq[:, sl]) * scale
        qw = qt_ref.shape[4]
        for ck in range(tm // qw):
            qt_ref[0, hd, ck] = rq[ck * qw:(ck + 1) * qw, :].T.astype(BF16)
        k_ref[0, :, sl] = rope(zk[:, sl]).astype(BF16)
        for ck in range(tm // TK):
            vt_ref[0, hd, ck, 0:LANES, :] = zv[ck * TK:(ck + 1) * TK, sl].T.astype(BF16)
            vt_ref[0, hd, ck, LANES:VT_ROWS, :] = jnp.ones((VT_ROWS - LANES, TK), BF16)

    o = 3 * qk_w
    cb = proj(o, o + conv_w)
    u = proj(o + conv_w, o + 2 * conv_w) * proj(o + 2 * conv_w, o + 3 * conv_w)

    @pl.when(pl.program_id(1) == 0)
    def _():
        carry_ref[...] = jnp.zeros_like(carry_ref)

    prev = carry_ref[...]
    row = lax.broadcasted_iota(I32, (tm, 1), 0)
    last = SUBLANES - 1
    u1 = jnp.where(row == 0, prev[last:last + 1, :], pltpu.roll(u, 1, 0))
    u2 = jnp.where(row == 0, prev[last - 1:last, :],
                   jnp.where(row == 1, prev[last:last + 1, :], pltpu.roll(u, 2, 0)))
    carry_ref[...] = u[tm - SUBLANES:tm, :]
    cw = cw_ref[...]
    conv = cb * (cw[0:1, :] * u + cw[1:2, :] * u1 + cw[2:3, :] * u2)
    convp = jnp.dot(conv.astype(BF16), wupc_ref[...].astype(BF16), preferred_element_type=F32)

    o = o + 3 * conv_w
    sg_ref[0] = jax.nn.sigmoid(proj(o, o + d)).astype(BF16)
    mc_ref[0] = (jax.nn.sigmoid(proj(o + d, o + 2 * d)) * convp).astype(BF16)


def _inproj(x, pos3, mod3, norm_g, invf, w_in, conv_w, w_up_conv):
    b, s, d = x.shape
    qk_w = N_HEADS * 2 * HEAD_DIM
    cw = conv_w.shape[1]
    tm = TM_IN
    if tm >= TQ:
        qt_spec = pl.BlockSpec((1, N_HEADS, tm // TQ, LANES, TQ), lambda bi, si: (bi, 0, si, 0, 0))
    else:
        qt_spec = pl.BlockSpec((1, N_HEADS, 1, LANES, tm),
                               lambda bi, si: (bi, 0, si // (TQ // tm), 0, si % (TQ // tm)))
    tok = lambda w: pl.BlockSpec((1, tm, w), lambda bi, si: (bi, si, 0))
    full = lambda a: pl.BlockSpec(a.shape, lambda bi, si: (0,) * a.ndim)
    return pl.pallas_call(
        functools.partial(_inproj_kernel, d=d, qk_w=qk_w, conv_w=cw),
        grid=(b, s // tm),
        in_specs=[tok(d), tok(1),
                  pl.BlockSpec((1, 1, mod3.shape[2]), lambda bi, si: (bi, 0, 0)),
                  full(norm_g), full(invf), full(w_in), full(conv_w), full(w_up_conv)],
        out_specs=[qt_spec,
                   tok(qk_w),
                   pl.BlockSpec((1, N_HEADS, tm // TK, VT_ROWS, TK),
                                lambda bi, si: (bi, 0, si, 0, 0)),
                   tok(d), tok(d)],
        out_shape=[jax.ShapeDtypeStruct((b, N_HEADS, s // TQ, LANES, TQ), BF16),
                   jax.ShapeDtypeStruct((b, s, qk_w), BF16),
                   jax.ShapeDtypeStruct((b, N_HEADS, s // TK, VT_ROWS, TK), BF16),
                   jax.ShapeDtypeStruct((b, s, d), BF16),
                   jax.ShapeDtypeStruct((b, s, d), BF16)],
        scratch_shapes=[pltpu.VMEM((SUBLANES, cw), F32)],
        compiler_params=_cparams(2),
        name="inproj",
    )(x, pos3, mod3, norm_g, invf, w_in, conv_w, w_up_conv)


def _attn_kernel(lam_ref, qt_ref, k_ref, vt_ref, g_ref, o_ref, s_ref, p_ref, qq_ref, bias_ref, *,
                 out_scale):
    hps, n_q, tq = qt_ref.shape[1], qt_ref.shape[2], qt_ref.shape[4]
    feat = lax.broadcasted_iota(I32, (LANES, 1), 0)
    halves = [slice(h * tq, (h + 1) * tq) for h in range(2)]
    streams = [(h, half) for h in range(hps) for half in halves]

    k_sub = lax.broadcasted_iota(I32, (TK, 1), 0)
    q_lane = lax.broadcasted_iota(I32, (1, tq), 1)
    bias_ref[...] = jnp.where(k_sub <= q_lane, 0.0, MASK_VALUE)

    def load_queries(qi):
        for h in range(hps):
            qt = qt_ref[0, h, qi]
            zero = jnp.zeros_like(qt)
            qq_ref[h, :, halves[0]] = jnp.where(feat < HEAD_DIM, qt, zero)
            qq_ref[h, :, halves[1]] = jnp.where(feat >= HEAD_DIM, qt, zero)

    def scores(j):
        for h, lanes in streams:
            kt = k_ref[0, pl.ds(j * TK, TK), h * LANES:(h + 1) * LANES]
            s_ref[h, :, lanes] = jnp.dot(kt, qq_ref[h, :, lanes], preferred_element_type=F32)

    def values(j, state):
        return tuple((m, alpha * acc + jnp.dot(vt_ref[0, h, j], p_ref[h, :, lanes],
                                               preferred_element_type=F32), alpha)
                     for (h, lanes), (m, acc, alpha) in zip(streams, state))

    def softmax(state, diagonal):
        out = []
        for (h, lanes), (m, acc, _) in zip(streams, state):
            load = ((lambda: s_ref[h, :, lanes] + bias_ref[...]) if diagonal
                    else (lambda: s_ref[h, :, lanes]))
            m_new = jnp.maximum(m, jnp.max(load(), axis=0, keepdims=True))
            p_ref[h, :, lanes] = jnp.exp2(load() - m_new).astype(BF16)
            out.append((m_new, acc, jnp.exp2(m - m_new)))
        return tuple(out)

    def finish(qi, last_tile, state):
        state = values(last_tile, state)
        row0 = qi * tq
        for h in range(hps):
            (_, acc1, _), (_, acc2, _) = state[2 * h], state[2 * h + 1]
            o1 = acc1[0:LANES] / acc1[LANES:LANES + 1]
            o2 = acc2[0:LANES] / acc2[LANES:LANES + 1]
            a = o1 - lam_ref[0:1, 0:1] * o2
            y = a * lax.rsqrt(jnp.mean(a * a, axis=0, keepdims=True) + EPS) * g_ref[...]
            o_ref[0, pl.ds(row0, tq), h * LANES:(h + 1) * LANES] = (
                (y * out_scale).T.astype(BF16))

    def next_diagonal(qi):
        if qi + 1 < n_q:
            load_queries(qi + 1)
            scores(qi + 1)

    one = (jnp.full((1, tq), MASK_VALUE, F32), jnp.zeros((VT_ROWS, tq), F32),
           jnp.ones((1, tq), F32))
    init = tuple(one for _ in streams)

    load_queries(0)
    scores(0)
    pending = softmax(init, True)
    next_diagonal(0)

    for qi in range(1, n_q):
        finish(qi - 1, max(qi - 2, 0), pending)
        state = softmax(init, True)
        scores(0)
        for t in range(1, qi + 1):
            state = values(qi if t == 1 else t - 2, state)
            state = softmax(state, False)
            if t < qi:
                scores(t)
            else:
                next_diagonal(qi)
        pending = state
    finish(n_q - 1, max(n_q - 2, 0), pending)


def _attn(lam, qt, k, vt, subln_g_col, out_scale):
    b, s, w = k.shape
    hps = ATTN_HEADS_PER_STEP
    return pl.pallas_call(
        functools.partial(_attn_kernel, out_scale=out_scale),
        grid=(b, N_HEADS // hps),
        in_specs=[pl.BlockSpec((SUBLANES, LANES), lambda bi, hi: (0, 0)),
                  pl.BlockSpec((1, hps, s // TQ, LANES, TQ), lambda bi, hi: (bi, hi, 0, 0, 0)),
                  pl.BlockSpec((1, s, hps * LANES), lambda bi, hi: (bi, 0, hi)),
                  pl.BlockSpec((1, hps, s // TK, VT_ROWS, TK), lambda bi, hi: (bi, hi, 0, 0, 0)),
                  pl.BlockSpec((LANES, 1), lambda bi, hi: (0, 0))],
        out_specs=pl.BlockSpec((1, s, hps * LANES), lambda bi, hi: (bi, 0, hi)),
        out_shape=jax.ShapeDtypeStruct((b, s, w), BF16),
        scratch_shapes=[pltpu.VMEM((hps, TK, 2 * TQ), F32), pltpu.VMEM((hps, TK, 2 * TQ), BF16),
                        pltpu.VMEM((hps, LANES, 2 * TQ), BF16), pltpu.VMEM((TK, TQ), F32)],
        compiler_params=_cparams(2),
        name="attn",
    )(lam, qt, k, vt, subln_g_col)


def _mixout_kernel(att_ref, sg_ref, mc_ref, x_ref, mod_ref, wua_ref, wo_ref, g_ref, wrt_ref,
                   br_ref, x1_ref, pos_ref, cnt_ref, hs_ref,
                   h2c_ref, tri_ref, carry_ref, posv_ref, cntv_ref, poss_ref, cnts_ref,
                   sem_s, sem_r, *, d, cap, tail):
    tm = x_ref.shape[0]
    i = pl.program_id(0)
    last = pl.num_programs(0) - 1

    @pl.when(i == 0)
    def _():
        carry_ref[...] = jnp.zeros_like(carry_ref)
        earlier = (lax.broadcasted_iota(I32, (tm, 1), 0)
                   <= lax.broadcasted_iota(I32, (1, tm), 1))
        tri_ref[...] = jnp.where(earlier, 1.0, 0.0).astype(BF16)

    mod = mod_ref[0]
    g_m = mod[:, 2 * d:3 * d]
    sh_f = mod[:, 3 * d:4 * d]
    sc_f = mod[:, 4 * d:5 * d]

    attp = jnp.dot(att_ref[...], wua_ref[...].astype(BF16), preferred_element_type=F32)
    merged = sg_ref[...].astype(F32) * attp + mc_ref[...].astype(F32)
    mo = jnp.dot(merged.astype(BF16), wo_ref[...].astype(BF16), preferred_element_type=F32)
    x1 = x_ref[...] + g_m * mo
    x1_ref[...] = x1
    ms = jnp.mean(x1 * x1, axis=-1, keepdims=True)
    h2 = (x1 * lax.rsqrt(ms + EPS) * g_ref[...]) * (1.0 + sc_f) + sh_f

    def split(a):
        hi = a.astype(BF16)
        return hi, (a - hi.astype(F32)).astype(BF16)

    def dot_t(a, b):
        return lax.dot_general(a, b, (((1,), (1,)), ((), ())), preferred_element_type=F32)

    w_hi, w_lo = split(wrt_ref[...])
    h_hi, h_lo = split(h2)
    by_hi = dot_t(jnp.concatenate([w_hi, w_lo], axis=0), h_hi)
    lt = (by_hi[0:ROUTER_ROWS] + (dot_t(w_hi, h_lo) + by_hi[ROUTER_ROWS:2 * ROUTER_ROWS])
          + br_ref[...])
    n_sel = EXPERTS_PER_GROUP
    ridx = lax.broadcasted_iota(I32, (n_sel, 1), 0)
    n_e = N_GROUPS * EXPERTS_PER_GROUP
    is_grp = ridx < N_GROUPS
    gl = jnp.where(is_grp, lt[n_e:n_e + n_sel, :], MASK_VALUE)
    gmax = jnp.max(gl, axis=0, keepdims=True)
    gsum = jnp.sum(jnp.where(is_grp, jnp.exp(gl - gmax), 0.0), axis=0, keepdims=True)
    g_w = 1.0 / gsum
    gidx = jnp.min(jnp.where(gl == gmax, ridx, n_sel), axis=0, keepdims=True)
    e_in = jnp.zeros((n_sel, tm), F32)
    for g in range(N_GROUPS):
        e_in = jnp.where(gidx == g, lt[g * n_sel:(g + 1) * n_sel, :], e_in)
    t1 = jnp.max(e_in, axis=0, keepdims=True)
    i1 = jnp.min(jnp.where(e_in == t1, ridx, n_sel), axis=0, keepdims=True)
    e_in2 = jnp.where(ridx == i1, MASK_VALUE, e_in)
    t2 = jnp.max(e_in2, axis=0, keepdims=True)
    i2 = jnp.min(jnp.where(e_in2 == t2, ridx, n_sel), axis=0, keepdims=True)
    e2 = jnp.exp(t2 - t1)
    den = 1.0 + e2
    w1 = (1.0 / den) * g_w
    w2 = (e2 / den) * g_w
    cw_t = jnp.where(ridx == i1, w1, jnp.where(ridx == i2, w2, 0.0))
    cw_col = jnp.concatenate([cw_t, jnp.zeros((LANES - n_sel, tm), F32)], axis=0).T

    onehot = jnp.where(ridx == gidx, 1.0, 0.0)
    oh16 = jnp.concatenate([onehot, jnp.zeros_like(onehot)], axis=0).astype(BF16)
    incl = jnp.dot(oh16, tri_ref[...], preferred_element_type=F32)[0:n_sel, :]
    run = carry_ref[...][:, 0:1]
    base = ridx.astype(F32) * float(cap)
    pos = jnp.sum(onehot * (base + run + incl - onehot), axis=0, keepdims=True).astype(I32)
    new_run = run + incl[:, tm - 1:tm]
    carry_ref[...] = jnp.broadcast_to(new_run, carry_ref.shape)
    cnt_i = jnp.broadcast_to(new_run, cnt_ref.shape).astype(I32)
    cnt_ref[...] = cnt_i
    pos_ref[0] = pos
    posv_ref[...] = pos
    cp = pltpu.make_async_copy(posv_ref.at[0], poss_ref, sem_s)
    cp.start()

    def row_copy(r, p):
        return pltpu.make_async_copy(h2c_ref.at[pl.ds(r, 1), :], hs_ref.at[pl.ds(p, 1), :], sem_r)

    def wait_tile():
        pltpu.make_async_copy(h2c_ref, hs_ref.at[pl.ds(0, tm), :], sem_r).wait()

    @pl.when(i > 0)
    def _():
        wait_tile()

    h2c_ref[:, 0:d] = h2
    h2c_ref[:, d:d + LANES] = cw_col
    cp.wait()

    for r in range(tm):
        row_copy(r, poss_ref[r]).start(priority=r % 2)

    @pl.when(i == last)
    def _():
        wait_tile()
        cntv_ref[...] = cnt_i
        cc = pltpu.make_async_copy(cntv_ref, cnts_ref, sem_s)
        cc.start()
        cc.wait()
        h2c_ref[0:SUBLANES, :] = jnp.zeros((SUBLANES, h2c_ref.shape[1]), F32)

        def zissue(r, c):
            for g in range(N_GROUPS):
                row_copy(0, g * cap + cnts_ref[g, 0] + r).start()
            return c

        lax.fori_loop(0, tail, zissue, 0, unroll=2)
        for g in range(N_GROUPS):
            pltpu.make_async_copy(h2c_ref.at[pl.ds(0, tail), :], hs_ref.at[pl.ds(0, tail), :],
                                  sem_r).wait()


def _mixout(att, sg, mc, x, mod3, w_up_att, w_out, norm_g, wrt, br, tiles_per_batch, cap):
    t, d = x.shape
    tm = TM_MIX
    dc = d + LANES
    tok = lambda w: pl.BlockSpec((tm, w), lambda i: (i, 0))
    full = lambda a: pl.BlockSpec(a.shape, lambda i: (0,) * a.ndim)
    return pl.pallas_call(
        functools.partial(_mixout_kernel, d=d, cap=cap, tail=cap - t),
        grid=(t // tm,),
        in_specs=[tok(att.shape[1]), tok(d), tok(d), tok(d),
                  pl.BlockSpec((1, 1, mod3.shape[2]), lambda i: (i // tiles_per_batch, 0, 0)),
                  full(w_up_att), full(w_out), full(norm_g), full(wrt), full(br)],
        out_specs=[tok(d),
                   pl.BlockSpec((1, 1, tm), lambda i: (i, 0, 0)),
                   pl.BlockSpec((SUBLANES, LANES), lambda i: (0, 0)),
                   pl.BlockSpec(memory_space=pl.ANY)],
        out_shape=[jax.ShapeDtypeStruct((t, d), F32),
                   jax.ShapeDtypeStruct((t // tm, 1, tm), I32),
                   jax.ShapeDtypeStruct((SUBLANES, LANES), I32),
                   jax.ShapeDtypeStruct((N_GROUPS * cap, dc), F32)],
        scratch_shapes=[pltpu.VMEM((tm, dc), F32), pltpu.VMEM((tm, tm), BF16),
                        pltpu.VMEM((SUBLANES, LANES), F32),
                        pltpu.VMEM((1, tm), I32), pltpu.VMEM((SUBLANES, LANES), I32),
                        pltpu.SMEM((tm,), I32), pltpu.SMEM((SUBLANES, LANES), I32),
                        pltpu.SemaphoreType.DMA, pltpu.SemaphoreType.DMA],
        compiler_params=_cparams(1),
        name="mixout",
    )(att, sg, mc, x, mod3, w_up_att, w_out, norm_g, wrt, br)


def _ffn_tile(step, cnt_ref, tm):
    log_tm = tm.bit_length() - 1
    ends = []
    tot = jnp.int32(0)
    for g in range(N_GROUPS):
        tot = tot + lax.shift_right_logical(cnt_ref[g] + (tm - 1), log_tm)
        ends.append(tot)
    sc = jnp.maximum(jnp.minimum(step, tot - 1), 0)
    g = sum((sc >= e).astype(I32) for e in ends[:-1])
    start = jnp.where(g == 0, 0, jnp.where(g == 1, ends[0], jnp.where(g == 2, ends[1], ends[2])))
    return g, sc - start, step < tot


def _ffn_kernel(cnt_ref, hs_ref, wg_ref, wu_ref, wd_ref, y_ref, *, d):
    tm = hs_ref.shape[0]
    _, _, real = _ffn_tile(pl.program_id(0), cnt_ref, tm)

    @pl.when(real)
    def _():
        h = hs_ref[:, 0:d].astype(BF16)
        cw = hs_ref[:, d:d + LANES]
        acc = jnp.zeros((tm, d), F32)
        for e in range(EXPERTS_PER_GROUP):
            hg = jnp.dot(h, wg_ref[e].astype(BF16), preferred_element_type=F32)
            hu = jnp.dot(h, wu_ref[e].astype(BF16), preferred_element_type=F32)
            a = hg * jax.nn.sigmoid(hg) * hu * cw[:, e:e + 1]
            acc = acc + jnp.dot(a.astype(BF16), wd_ref[e].astype(BF16),
                                preferred_element_type=F32)
        y_ref[...] = acc


def _ffn(cnt, hs, w_gate, w_up, w_down, cap):
    rows, dc = hs.shape
    d = dc - LANES
    tm = TM_FFN
    blocks_per_group = cap // tm
    n_steps = (blocks_per_group - 1) + N_GROUPS

    def row_map(s, c):
        g, blk, _ = _ffn_tile(s, c, tm)
        return (g * blocks_per_group + blk, 0)

    def w_map(s, c):
        g, _, _ = _ffn_tile(s, c, tm)
        return (g, 0, 0)

    ff = w_gate.shape[2]
    once = pl.Buffered(1)
    gs = pltpu.PrefetchScalarGridSpec(
        num_scalar_prefetch=1,
        grid=(n_steps,),
        in_specs=[pl.BlockSpec((tm, dc), row_map),
                  pl.BlockSpec((EXPERTS_PER_GROUP, d, ff), w_map),
                  pl.BlockSpec((EXPERTS_PER_GROUP, d, ff), w_map),
                  pl.BlockSpec((EXPERTS_PER_GROUP, ff, d), w_map, pipeline_mode=once)],
        out_specs=pl.BlockSpec((tm, d), row_map),
    )
    return pl.pallas_call(
        functools.partial(_ffn_kernel, d=d),
        grid_spec=gs,
        out_shape=jax.ShapeDtypeStruct((rows, d), F32),
        compiler_params=_cparams(1),
        name="ffn",
    )(cnt, hs, w_gate, w_up, w_down)


def _final_kernel(pos_ref, ys_ref, x1_ref, mod_ref, g_ref, o_ref, ybuf_ref, sem, *, d, normalize):
    tm = x1_ref.shape[0]
    i = pl.program_id(0)

    def gather_tile(tile):
        slot = tile & 1
        base = tile * tm
        for r in range(tm):
            pltpu.make_async_copy(ys_ref.at[pl.ds(pos_ref[base + r], 1), :],
                                  ybuf_ref.at[slot, pl.ds(r, 1), :],
                                  sem.at[slot]).start(priority=r % 2)

    @pl.when(i == 0)
    def _():
        gather_tile(i)

    @pl.when(i + 1 < pl.num_programs(0))
    def _():
        gather_tile(i + 1)

    slot = i & 1
    pltpu.make_async_copy(ys_ref.at[pl.ds(0, tm), :], ybuf_ref.at[slot], sem.at[slot]).wait()

    g_f = mod_ref[0][:, 5 * d:6 * d]
    x2 = x1_ref[...] + g_f * ybuf_ref[slot]
    if normalize:
        x2 = x2 * lax.rsqrt(jnp.mean(x2 * x2, axis=-1, keepdims=True) + EPS) * g_ref[...]
    o_ref[...] = x2


def _final(pos, ys, x1, mod3, final_g, tiles_per_batch, normalize):
    t, d = x1.shape
    tm = TM_FIN
    gs = pltpu.PrefetchScalarGridSpec(
        num_scalar_prefetch=1,
        grid=(t // tm,),
        in_specs=[pl.BlockSpec(memory_space=pl.ANY),
                  pl.BlockSpec((tm, d), lambda i, p: (i, 0)),
                  pl.BlockSpec((1, 1, mod3.shape[2]), lambda i, p: (i // tiles_per_batch, 0, 0)),
                  pl.BlockSpec((1, d), lambda i, p: (0, 0))],
        out_specs=pl.BlockSpec((tm, d), lambda i, p: (i, 0)),
        scratch_shapes=[pltpu.VMEM((2, tm, d), F32), pltpu.SemaphoreType.DMA((2,))],
    )
    return pl.pallas_call(
        functools.partial(_final_kernel, d=d, normalize=normalize),
        grid_spec=gs,
        out_shape=jax.ShapeDtypeStruct((t, d), F32),
        compiler_params=_cparams(1),
        name="final",
    )(pos, ys, x1, mod3, final_g)


def kernel(x, c, positions, w_ada, b_ada, norm_mix_g, w_in, lambda_q1, lambda_k1, lambda_q2,
           lambda_k2, subln_g, conv_w, w_up_att, w_up_conv, w_out, norm_ffn_g, w_group_router,
           b_group_router, w_expert_router, b_expert_router, w_gate, w_up, w_down, final_norm_g):
    b, s, d = x.shape
    depth = w_ada.shape[0]
    t = b * s
    assert s % TQ == 0 and s % TM_IN == 0 and TQ == TK and TM_IN % TK == 0
    assert TQ % TM_IN == 0 or TM_IN % TQ == 0
    assert N_HEADS % ATTN_HEADS_PER_STEP == 0
    assert s % TM_MIX == 0 and s % TM_FIN == 0 and t % TM_FFN == 0
    cap = t + TM_FFN
    n_e = N_GROUPS * EXPERTS_PER_GROUP

    assert b <= SUBLANES and EXPERTS_PER_GROUP == SUBLANES and N_GROUPS <= SUBLANES
    c8 = jnp.concatenate([c, jnp.zeros((SUBLANES - b, d), F32)], axis=0)
    pos3 = positions.reshape(b, s, 1)
    inv_freq = ROPE_THETA ** (-jnp.arange(0, HEAD_DIM, 2, dtype=F32) / HEAD_DIM)
    invf = jnp.tile(inv_freq, LANES // (HEAD_DIM // 2)).reshape(1, LANES)

    xf = x
    for l in range(depth):
        lambda_init = 0.8 - 0.6 * math.exp(-0.3 * l)
        row = lambda a: a[l].reshape(1, -1)
        mod8, lam = _ada(c8, w_ada[l], row(b_ada), row(lambda_q1), row(lambda_k1),
                         row(lambda_q2), row(lambda_k2), lambda_init)
        mod3 = mod8[0:b].reshape(b, 1, 6 * d)

        qt, k, vt, sg, mc = _inproj(xf.reshape(b, s, d), pos3, mod3, row(norm_mix_g), invf,
                                    w_in[l], conv_w[l], w_up_conv[l])
        att = _attn(lam, qt, k, vt, subln_g[l].reshape(-1, 1), 1.0 - lambda_init)

        pad = ROUTER_ROWS - n_e - N_GROUPS
        wrt = jnp.concatenate([w_expert_router[l].T, w_group_router[l].T,
                               jnp.zeros((pad, d), F32)], axis=0)
        br = jnp.concatenate([b_expert_router[l], b_group_router[l],
                              jnp.zeros((pad,), F32)]).reshape(ROUTER_ROWS, 1)
        x1, pos, cnt, hs = _mixout(att.reshape(t, -1), sg.reshape(t, d), mc.reshape(t, d),
                                   xf.reshape(t, d), mod3, w_up_att[l], w_out[l],
                                   row(norm_ffn_g), wrt, br,
                                   s // TM_MIX, cap)
        ys = _ffn(cnt[0:N_GROUPS, 0], hs, w_gate[l], w_up[l], w_down[l], cap)
        xf = _final(pos.reshape(t), ys, x1, mod3, final_norm_g.reshape(1, d), s // TM_FIN,
                    l == depth - 1)
    return xf.reshape(b, s, d)
```

```python
import functools
import math

import jax
import jax.numpy as jnp
from jax import lax
from jax.experimental import pallas as pl
from jax.experimental.pallas import tpu as pltpu

F32 = jnp.float32
BF16 = jnp.bfloat16
I32 = jnp.int32

EPS = 1e-6
MASK_VALUE = -1e30
ROPE_THETA = 10000.0

N_HEADS = 4
HEAD_DIM = 64
LANES = 128
SUBLANES = 8
N_GROUPS = 4
EXPERTS_PER_GROUP = 8
ROUTER_ROWS = 48

VMEM_LIMIT = 56 * 1024 * 1024

TM_IN = 1024
TQ = 512
TK = 512
ATTN_HEADS_PER_STEP = 1
VT_ROWS = LANES + 16
TM_MIX = 1024
TM_FFN = 512
TM_FIN = 1024
TN_ADA = 1024


def _cparams(n_axes):
    return pltpu.CompilerParams(
        dimension_semantics=("arbitrary",) * n_axes, vmem_limit_bytes=VMEM_LIMIT)


def _ada_kernel(c_ref, w_ref, b_ref, lq1_ref, lk1_ref, lq2_ref, lk2_ref, mod_ref, lam_ref, *,
                lambda_init):
    c = c_ref[...]
    a = c * jax.nn.sigmoid(c)
    mod_ref[...] = jnp.dot(a.astype(BF16), w_ref[...].astype(BF16),
                           preferred_element_type=F32) + b_ref[...]
    s1 = jnp.sum(lq1_ref[...] * lk1_ref[...], axis=-1, keepdims=True)
    s2 = jnp.sum(lq2_ref[...] * lk2_ref[...], axis=-1, keepdims=True)
    lam = jnp.exp(s1) - jnp.exp(s2) + lambda_init
    lam_ref[...] = jnp.broadcast_to(lam, lam_ref.shape)


def _ada(c8, w_ada, b_ada, lq1, lk1, lq2, lk2, lambda_init):
    d, n = w_ada.shape
    small = pl.BlockSpec((1, HEAD_DIM), lambda j: (0, 0))
    return pl.pallas_call(
        functools.partial(_ada_kernel, lambda_init=lambda_init),
        grid=(n // TN_ADA,),
        in_specs=[pl.BlockSpec((SUBLANES, d), lambda j: (0, 0)),
                  pl.BlockSpec((d, TN_ADA), lambda j: (0, j)),
                  pl.BlockSpec((1, TN_ADA), lambda j: (0, j)),
                  small, small, small, small],
        out_specs=[pl.BlockSpec((SUBLANES, TN_ADA), lambda j: (0, j)),
                   pl.BlockSpec((SUBLANES, LANES), lambda j: (0, 0))],
        out_shape=[jax.ShapeDtypeStruct((SUBLANES, n), F32),
                   jax.ShapeDtypeStruct((SUBLANES, LANES), F32)],
        compiler_params=_cparams(1),
        name="ada",
    )(c8, w_ada, b_ada, lq1, lk1, lq2, lk2)


def _inproj_kernel(x_ref, pos_ref, mod_ref, g_ref, invf_ref, w_ref, cw_ref, wupc_ref,
                   qt_ref, k_ref, vt_ref, sg_ref, mc_ref, carry_ref, *, d, qk_w, conv_w):
    tm = x_ref.shape[1]
    x = x_ref[0]
    ms = jnp.mean(x * x, axis=-1, keepdims=True)
    mod = mod_ref[0]
    sh = mod[:, 0:d]
    sc = mod[:, d:2 * d]
    h = (x * lax.rsqrt(ms + EPS) * g_ref[...]) * (1.0 + sc) + sh
    hb = h.astype(BF16)

    def proj(a, b):
        return jnp.dot(hb, w_ref[:, a:b].astype(BF16), preferred_element_type=F32)

    lane = lax.broadcasted_iota(I32, (1, LANES), 1)
    first_half = (lane % HEAD_DIM) < (HEAD_DIM // 2)
    ang = pos_ref[0].astype(F32) * invf_ref[...]
    cos = jnp.cos(ang)
    sin = jnp.sin(ang)
    sin_signed = jnp.where(first_half, -sin, sin)

    def rope(t):
        partner = jnp.where(first_half, pltpu.roll(t, LANES - HEAD_DIM // 2, 1),
                            pltpu.roll(t, HEAD_DIM // 2, 1))
        return t * cos + partner * sin_signed

    zq = proj(0, qk_w)
    zk = proj(qk_w, 2 * qk_w)
    scale = HEAD_DIM ** -0.5 * math.log2(math.e)
    zv = proj(2 * qk_w, 3 * qk_w)
    for hd in range(qk_w // LANES):
        sl = slice(hd * LANES, (hd + 1) * LANES)
        rq = rope(zq[:, sl]) * scale
        qw = qt_ref.shape[4]
        for ck in range(tm // qw):
            qt_ref[0, hd, ck] = rq[ck * qw:(ck + 1) * qw, :].T.astype(BF16)
        k_ref[0, :, sl] = rope(zk[:, sl]).astype(BF16)
        for ck in range(tm // TK):
            vt_ref[0, hd, ck, 0:LANES, :] = zv[ck * TK:(ck + 1) * TK, sl].T.astype(BF16)
            vt_ref[0, hd, ck, LANES:VT_ROWS, :] = jnp.ones((VT_ROWS - LANES, TK), BF16)

    o = 3 * qk_w
    cb = proj(o, o + conv_w)
    u = proj(o + conv_w, o + 2 * conv_w) * proj(o + 2 * conv_w, o + 3 * conv_w)

    @pl.when(pl.program_id(1) == 0)
    def _():
        carry_ref[...] = jnp.zeros_like(carry_ref)

    prev = carry_ref[...]
    row = lax.broadcasted_iota(I32, (tm, 1), 0)
    last = SUBLANES - 1
    u1 = jnp.where(row == 0, prev[last:last + 1, :], pltpu.roll(u, 1, 0))
    u2 = jnp.where(row == 0, prev[last - 1:last, :],
                   jnp.where(row == 1, prev[last:last + 1, :], pltpu.roll(u, 2, 0)))
    carry_ref[...] = u[tm - SUBLANES:tm, :]
    cw = cw_ref[...]
    conv = cb * (cw[0:1, :] * u + cw[1:2, :] * u1 + cw[2:3, :] * u2)
    convp = jnp.dot(conv.astype(BF16), wupc_ref[...].astype(BF16), preferred_element_type=F32)

    o = o + 3 * conv_w
    sg_ref[0] = jax.nn.sigmoid(proj(o, o + d)).astype(BF16)
    mc_ref[0] = (jax.nn.sigmoid(proj(o + d, o + 2 * d)) * convp).astype(BF16)


def _inproj(x, pos3, mod3, norm_g, invf, w_in, conv_w, w_up_conv):
    b, s, d = x.shape
    qk_w = N_HEADS * 2 * HEAD_DIM
    cw = conv_w.shape[1]
    tm = TM_IN
    if tm >= TQ:
        qt_spec = pl.BlockSpec((1, N_HEADS, tm // TQ, LANES, TQ), lambda bi, si: (bi, 0, si, 0, 0))
    else:
        qt_spec = pl.BlockSpec((1, N_HEADS, 1, LANES, tm),
                               lambda bi, si: (bi, 0, si // (TQ // tm), 0, si % (TQ // tm)))
    tok = lambda w: pl.BlockSpec((1, tm, w), lambda bi, si: (bi, si, 0))
    full = lambda a: pl.BlockSpec(a.shape, lambda bi, si: (0,) * a.ndim)
    return pl.pallas_call(
        functools.partial(_inproj_kernel, d=d, qk_w=qk_w, conv_w=cw),
        grid=(b, s // tm),
        in_specs=[tok(d), tok(1),
                  pl.BlockSpec((1, 1, mod3.shape[2]), lambda bi, si: (bi, 0, 0)),
                  full(norm_g), full(invf), full(w_in), full(conv_w), full(w_up_conv)],
        out_specs=[qt_spec,
                   tok(qk_w),
                   pl.BlockSpec((1, N_HEADS, tm // TK, VT_ROWS, TK),
                                lambda bi, si: (bi, 0, si, 0, 0)),
                   tok(d), tok(d)],
        out_shape=[jax.ShapeDtypeStruct((b, N_HEADS, s // TQ, LANES, TQ), BF16),
                   jax.ShapeDtypeStruct((b, s, qk_w), BF16),
                   jax.ShapeDtypeStruct((b, N_HEADS, s // TK, VT_ROWS, TK), BF16),
                   jax.ShapeDtypeStruct((b, s, d), BF16),
                   jax.ShapeDtypeStruct((b, s, d), BF16)],
        scratch_shapes=[pltpu.VMEM((SUBLANES, cw), F32)],
        compiler_params=_cparams(2),
        name="inproj",
    )(x, pos3, mod3, norm_g, invf, w_in, conv_w, w_up_conv)


def _attn_kernel(lam_ref, qt_ref, k_ref, vt_ref, g_ref, o_ref, s_ref, p_ref, qq_ref, bias_ref, *,
                 out_scale):
    hps, n_q, tq = qt_ref.shape[1], qt_ref.shape[2], qt_ref.shape[4]
    feat = lax.broadcasted_iota(I32, (LANES, 1), 0)
    halves = [slice(h * tq, (h + 1) * tq) for h in range(2)]
    streams = [(h, half) for h in range(hps) for half in halves]

    k_sub = lax.broadcasted_iota(I32, (TK, 1), 0)
    q_lane = lax.broadcasted_iota(I32, (1, tq), 1)
    bias_ref[...] = jnp.where(k_sub <= q_lane, 0.0, MASK_VALUE)

    def load_queries(qi):
        for h in range(hps):
            qt = qt_ref[0, h, qi]
            zero = jnp.zeros_like(qt)
            qq_ref[h, :, halves[0]] = jnp.where(feat < HEAD_DIM, qt, zero)
            qq_ref[h, :, halves[1]] = jnp.where(feat >= HEAD_DIM, qt, zero)

    def scores(j):
        for h, lanes in streams:
            kt = k_ref[0, pl.ds(j * TK, TK), h * LANES:(h + 1) * LANES]
            s_ref[h, :, lanes] = jnp.dot(kt, qq_ref[h, :, lanes], preferred_element_type=F32)

    def values(j, state):
        return tuple((m, alpha * acc + jnp.dot(vt_ref[0, h, j], p_ref[h, :, lanes],
                                               preferred_element_type=F32), alpha)
                     for (h, lanes), (m, acc, alpha) in zip(streams, state))

    def softmax(state, diagonal):
        out = []
        for (h, lanes), (m, acc, _) in zip(streams, state):
            load = ((lambda: s_ref[h, :, lanes] + bias_ref[...]) if diagonal
                    else (lambda: s_ref[h, :, lanes]))
            m_new = jnp.maximum(m, jnp.max(load(), axis=0, keepdims=True))
            p_ref[h, :, lanes] = jnp.exp2(load() - m_new).astype(BF16)
            out.append((m_new, acc, jnp.exp2(m - m_new)))
        return tuple(out)

    def finish(qi, last_tile, state):
        state = values(last_tile, state)
        row0 = qi * tq
        for h in range(hps):
            (_, acc1, _), (_, acc2, _) = state[2 * h], state[2 * h + 1]
            o1 = acc1[0:LANES] / acc1[LANES:LANES + 1]
            o2 = acc2[0:LANES] / acc2[LANES:LANES + 1]
            a = o1 - lam_ref[0:1, 0:1] * o2
            y = a * lax.rsqrt(jnp.mean(a * a, axis=0, keepdims=True) + EPS) * g_ref[...]
            o_ref[0, pl.ds(row0, tq), h * LANES:(h + 1) * LANES] = (
                (y * out_scale).T.astype(BF16))

    def next_diagonal(qi):
        if qi + 1 < n_q:
            load_queries(qi + 1)
            scores(qi + 1)

    one = (jnp.full((1, tq), MASK_VALUE, F32), jnp.zeros((VT_ROWS, tq), F32),
           jnp.ones((1, tq), F32))
    init = tuple(one for _ in streams)

    load_queries(0)
    scores(0)
    pending = softmax(init, True)
    next_diagonal(0)

    for qi in range(1, n_q):
        finish(qi - 1, max(qi - 2, 0), pending)
        state = softmax(init, True)
        scores(0)
        for t in range(1, qi + 1):
            state = values(qi if t == 1 else t - 2, state)
            state = softmax(state, False)
            if t < qi:
                scores(t)
            else:
                next_diagonal(qi)
        pending = state
    finish(n_q - 1, max(n_q - 2, 0), pending)


def _attn(lam, qt, k, vt, subln_g_col, out_scale):
    b, s, w = k.shape
    hps = ATTN_HEADS_PER_STEP
    return pl.pallas_call(
        functools.partial(_attn_kernel, out_scale=out_scale),
        grid=(b, N_HEADS // hps),
        in_specs=[pl.BlockSpec((SUBLANES, LANES), lambda bi, hi: (0, 0)),
                  pl.BlockSpec((1, hps, s // TQ, LANES, TQ), lambda bi, hi: (bi, hi, 0, 0, 0)),
                  pl.BlockSpec((1, s, hps * LANES), lambda bi, hi: (bi, 0, hi)),
                  pl.BlockSpec((1, hps, s // TK, VT_ROWS, TK), lambda bi, hi: (bi, hi, 0, 0, 0)),
                  pl.BlockSpec((LANES, 1), lambda bi, hi: (0, 0))],
        out_specs=pl.BlockSpec((1, s, hps * LANES), lambda bi, hi: (bi, 0, hi)),
        out_shape=jax.ShapeDtypeStruct((b, s, w), BF16),
        scratch_shapes=[pltpu.VMEM((hps, TK, 2 * TQ), F32), pltpu.VMEM((hps, TK, 2 * TQ), BF16),
                        pltpu.VMEM((hps, LANES, 2 * TQ), BF16), pltpu.VMEM((TK, TQ), F32)],
        compiler_params=_cparams(2),
        name="attn",
    )(lam, qt, k, vt, subln_g_col)


def _mixout_kernel(att_ref, sg_ref, mc_ref, x_ref, mod_ref, wua_ref, wo_ref, g_ref, wrt_ref,
                   br_ref, x1_ref, pos_ref, cnt_ref, hs_ref,
                   h2c_ref, tri_ref, carry_ref, posv_ref, cntv_ref, poss_ref, cnts_ref,
                   sem_s, sem_r, *, d, cap, tail):
    tm = x_ref.shape[0]
    i = pl.program_id(0)
    last = pl.num_programs(0) - 1

    @pl.when(i == 0)
    def _():
        carry_ref[...] = jnp.zeros_like(carry_ref)
        earlier = (lax.broadcasted_iota(I32, (tm, 1), 0)
                   <= lax.broadcasted_iota(I32, (1, tm), 1))
        tri_ref[...] = jnp.where(earlier, 1.0, 0.0).astype(BF16)

    mod = mod_ref[0]
    g_m = mod[:, 2 * d:3 * d]
    sh_f = mod[:, 3 * d:4 * d]
    sc_f = mod[:, 4 * d:5 * d]

    attp = jnp.dot(att_ref[...], wua_ref[...].astype(BF16), preferred_element_type=F32)
    merged = sg_ref[...].astype(F32) * attp + mc_ref[...].astype(F32)
    mo = jnp.dot(merged.astype(BF16), wo_ref[...].astype(BF16), preferred_element_type=F32)
    x1 = x_ref[...] + g_m * mo
    x1_ref[...] = x1
    ms = jnp.mean(x1 * x1, axis=-1, keepdims=True)
    h2 = (x1 * lax.rsqrt(ms + EPS) * g_ref[...]) * (1.0 + sc_f) + sh_f

    def split(a):
        hi = a.astype(BF16)
        return hi, (a - hi.astype(F32)).astype(BF16)

    def dot_t(a, b):
        return lax.dot_general(a, b, (((1,), (1,)), ((), ())), preferred_element_type=F32)

    w_hi, w_lo = split(wrt_ref[...])
    h_hi, h_lo = split(h2)
    by_hi = dot_t(jnp.concatenate([w_hi, w_lo], axis=0), h_hi)
    lt = (by_hi[0:ROUTER_ROWS] + (dot_t(w_hi, h_lo) + by_hi[ROUTER_ROWS:2 * ROUTER_ROWS])
          + br_ref[...])
    n_sel = EXPERTS_PER_GROUP
    ridx = lax.broadcasted_iota(I32, (n_sel, 1), 0)
    n_e = N_GROUPS * EXPERTS_PER_GROUP
    is_grp = ridx < N_GROUPS
    gl = jnp.where(is_grp, lt[n_e:n_e + n_sel, :], MASK_VALUE)
    gmax = jnp.max(gl, axis=0, keepdims=True)
    gsum = jnp.sum(jnp.where(is_grp, jnp.exp(gl - gmax), 0.0), axis=0, keepdims=True)
    g_w = 1.0 / gsum
    gidx = jnp.min(jnp.where(gl == gmax, ridx, n_sel), axis=0, keepdims=True)
    e_in = jnp.zeros((n_sel, tm), F32)
    for g in range(N_GROUPS):
        e_in = jnp.where(gidx == g, lt[g * n_sel:(g + 1) * n_sel, :], e_in)
    t1 = jnp.max(e_in, axis=0, keepdims=True)
    i1 = jnp.min(jnp.where(e_in == t1, ridx, n_sel), axis=0, keepdims=True)
    e_in2 = jnp.where(ridx == i1, MASK_VALUE, e_in)
    t2 = jnp.max(e_in2, axis=0, keepdims=True)
    i2 = jnp.min(jnp.where(e_in2 == t2, ridx, n_sel), axis=0, keepdims=True)
    e2 = jnp.exp(t2 - t1)
    den = 1.0 + e2
    w1 = (1.0 / den) * g_w
    w2 = (e2 / den) * g_w
    cw_t = jnp.where(ridx == i1, w1, jnp.where(ridx == i2, w2, 0.0))
    cw_col = jnp.concatenate([cw_t, jnp.zeros((LANES - n_sel, tm), F32)], axis=0).T

    onehot = jnp.where(ridx == gidx, 1.0, 0.0)
    oh16 = jnp.concatenate([onehot, jnp.zeros_like(onehot)], axis=0).astype(BF16)
    incl = jnp.dot(oh16, tri_ref[...], preferred_element_type=F32)[0:n_sel, :]
    run = carry_ref[...][:, 0:1]
    base = ridx.astype(F32) * float(cap)
    pos = jnp.sum(onehot * (base + run + incl - onehot), axis=0, keepdims=True).astype(I32)
    new_run = run + incl[:, tm - 1:tm]
    carry_ref[...] = jnp.broadcast_to(new_run, carry_ref.shape)
    cnt_i = jnp.broadcast_to(new_run, cnt_ref.shape).astype(I32)
    cnt_ref[...] = cnt_i
    pos_ref[0] = pos
    posv_ref[...] = pos
    cp = pltpu.make_async_copy(posv_ref.at[0], poss_ref, sem_s)
    cp.start()

    def row_copy(r, p):
        return pltpu.make_async_copy(h2c_ref.at[pl.ds(r, 1), :], hs_ref.at[pl.ds(p, 1), :], sem_r)

    def wait_tile():
        pltpu.make_async_copy(h2c_ref, hs_ref.at[pl.ds(0, tm), :], sem_r).wait()

    @pl.when(i > 0)
    def _():
        wait_tile()

    h2c_ref[:, 0:d] = h2
    h2c_ref[:, d:d + LANES] = cw_col
    cp.wait()

    for r in range(tm):
        row_copy(r, poss_ref[r]).start(priority=r % 2)

    @pl.when(i == last)
    def _():
        wait_tile()
        cntv_ref[...] = cnt_i
        cc = pltpu.make_async_copy(cntv_ref, cnts_ref, sem_s)
        cc.start()
        cc.wait()
        h2c_ref[0:SUBLANES, :] = jnp.zeros((SUBLANES, h2c_ref.shape[1]), F32)

        def zissue(r, c):
            for g in range(N_GROUPS):
                row_copy(0, g * cap + cnts_ref[g, 0] + r).start()
            return c

        lax.fori_loop(0, tail, zissue, 0, unroll=2)
        for g in range(N_GROUPS):
            pltpu.make_async_copy(h2c_ref.at[pl.ds(0, tail), :], hs_ref.at[pl.ds(0, tail), :],
                                  sem_r).wait()


def _mixout(att, sg, mc, x, mod3, w_up_att, w_out, norm_g, wrt, br, tiles_per_batch, cap):
    t, d = x.shape
    tm = TM_MIX
    dc = d + LANES
    tok = lambda w: pl.BlockSpec((tm, w), lambda i: (i, 0))
    full = lambda a: pl.BlockSpec(a.shape, lambda i: (0,) * a.ndim)
    return pl.pallas_call(
        functools.partial(_mixout_kernel, d=d, cap=cap, tail=cap - t),
        grid=(t // tm,),
        in_specs=[tok(att.shape[1]), tok(d), tok(d), tok(d),
                  pl.BlockSpec((1, 1, mod3.shape[2]), lambda i: (i // tiles_per_batch, 0, 0)),
                  full(w_up_att), full(w_out), full(norm_g), full(wrt), full(br)],
        out_specs=[tok(d),
                   pl.BlockSpec((1, 1, tm), lambda i: (i, 0, 0)),
                   pl.BlockSpec((SUBLANES, LANES), lambda i: (0, 0)),
                   pl.BlockSpec(memory_space=pl.ANY)],
        out_shape=[jax.ShapeDtypeStruct((t, d), F32),
                   jax.ShapeDtypeStruct((t // tm, 1, tm), I32),
                   jax.ShapeDtypeStruct((SUBLANES, LANES), I32),
                   jax.ShapeDtypeStruct((N_GROUPS * cap, dc), F32)],
        scratch_shapes=[pltpu.VMEM((tm, dc), F32), pltpu.VMEM((tm, tm), BF16),
                        pltpu.VMEM((SUBLANES, LANES), F32),
                        pltpu.VMEM((1, tm), I32), pltpu.VMEM((SUBLANES, LANES), I32),
                        pltpu.SMEM((tm,), I32), pltpu.SMEM((SUBLANES, LANES), I32),
                        pltpu.SemaphoreType.DMA, pltpu.SemaphoreType.DMA],
        compiler_params=_cparams(1),
        name="mixout",
    )(att, sg, mc, x, mod3, w_up_att, w_out, norm_g, wrt, br)


def _ffn_tile(step, cnt_ref, tm):
    log_tm = tm.bit_length() - 1
    ends = []
    tot = jnp.int32(0)
    for g in range(N_GROUPS):
        tot = tot + lax.shift_right_logical(cnt_ref[g] + (tm - 1), log_tm)
        ends.append(tot)
    sc = jnp.maximum(jnp.minimum(step, tot - 1), 0)
    g = sum((sc >= e).astype(I32) for e in ends[:-1])
    start = jnp.where(g == 0, 0, jnp.where(g == 1, ends[0], jnp.where(g == 2, ends[1], ends[2])))
    return g, sc - start, step < tot


def _ffn_kernel(cnt_ref, hs_ref, wg_ref, wu_ref, wd_ref, y_ref, *, d):
    tm = hs_ref.shape[0]
    _, _, real = _ffn_tile(pl.program_id(0), cnt_ref, tm)

    @pl.when(real)
    def _():
        h = hs_ref[:, 0:d].astype(BF16)
        cw = hs_ref[:, d:d + LANES]
        acc = jnp.zeros((tm, d), F32)
        for e in range(EXPERTS_PER_GROUP):
            hg = jnp.dot(h, wg_ref[e].astype(BF16), preferred_element_type=F32)
            hu = jnp.dot(h, wu_ref[e].astype(BF16), preferred_element_type=F32)
            a = hg * jax.nn.sigmoid(hg) * hu * cw[:, e:e + 1]
            acc = acc + jnp.dot(a.astype(BF16), wd_ref[e].astype(BF16),
                                preferred_element_type=F32)
        y_ref[...] = acc


def _ffn(cnt, hs, w_gate, w_up, w_down, cap):
    rows, dc = hs.shape
    d = dc - LANES
    tm = TM_FFN
    blocks_per_group = cap // tm
    n_steps = (blocks_per_group - 1) + N_GROUPS

    def row_map(s, c):
        g, blk, _ = _ffn_tile(s, c, tm)
        return (g * blocks_per_group + blk, 0)

    def w_map(s, c):
        g, _, _ = _ffn_tile(s, c, tm)
        return (g, 0, 0)

    ff = w_gate.shape[2]
    once = pl.Buffered(1)
    gs = pltpu.PrefetchScalarGridSpec(
        num_scalar_prefetch=1,
        grid=(n_steps,),
        in_specs=[pl.BlockSpec((tm, dc), row_map),
                  pl.BlockSpec((EXPERTS_PER_GROUP, d, ff), w_map),
                  pl.BlockSpec((EXPERTS_PER_GROUP, d, ff), w_map),
                  pl.BlockSpec((EXPERTS_PER_GROUP, ff, d), w_map, pipeline_mode=once)],
        out_specs=pl.BlockSpec((tm, d), row_map),
    )
    return pl.pallas_call(
        functools.partial(_ffn_kernel, d=d),
        grid_spec=gs,
        out_shape=jax.ShapeDtypeStruct((rows, d), F32),
        compiler_params=_cparams(1),
        name="ffn",
    )(cnt, hs, w_gate, w_up, w_down)


def _final_kernel(pos_ref, ys_ref, x1_ref, mod_ref, g_ref, o_ref, ybuf_ref, sem, *, d, normalize):
    tm = x1_ref.shape[0]
    i = pl.program_id(0)

    def gather_tile(tile):
        slot = tile & 1
        base = tile * tm
        for r in range(tm):
            pltpu.make_async_copy(ys_ref.at[pl.ds(pos_ref[base + r], 1), :],
                                  ybuf_ref.at[slot, pl.ds(r, 1), :],
                                  sem.at[slot]).start(priority=r % 2)

    @pl.when(i == 0)
    def _():
        gather_tile(i)

    @pl.when(i + 1 < pl.num_programs(0))
    def _():
        gather_tile(i + 1)

    slot = i & 1
    pltpu.make_async_copy(ys_ref.at[pl.ds(0, tm), :], ybuf_ref.at[slot], sem.at[slot]).wait()

    g_f = mod_ref[0][:, 5 * d:6 * d]
    x2 = x1_ref[...] + g_f * ybuf_ref[slot]
    if normalize:
        x2 = x2 * lax.rsqrt(jnp.mean(x2 * x2, axis=-1, keepdims=True) + EPS) * g_ref[...]
    o_ref[...] = x2


def _final(pos, ys, x1, mod3, final_g, tiles_per_batch, normalize):
    t, d = x1.shape
    tm = TM_FIN
    gs = pltpu.PrefetchScalarGridSpec(
        num_scalar_prefetch=1,
        grid=(t // tm,),
        in_specs=[pl.BlockSpec(memory_space=pl.ANY),
                  pl.BlockSpec((tm, d), lambda i, p: (i, 0)),
                  pl.BlockSpec((1, 1, mod3.shape[2]), lambda i, p: (i // tiles_per_batch, 0, 0)),
                  pl.BlockSpec((1, d), lambda i, p: (0, 0))],
        out_specs=pl.BlockSpec((tm, d), lambda i, p: (i, 0)),
        scratch_shapes=[pltpu.VMEM((2, tm, d), F32), pltpu.SemaphoreType.DMA((2,))],
    )
    return pl.pallas_call(
        functools.partial(_final_kernel, d=d, normalize=normalize),
        grid_spec=gs,
        out_shape=jax.ShapeDtypeStruct((t, d), F32),
        compiler_params=_cparams(1),
        name="final",
    )(pos, ys, x1, mod3, final_g)


def kernel(x, c, positions, w_ada, b_ada, norm_mix_g, w_in, lambda_q1, lambda_k1, lambda_q2,
           lambda_k2, subln_g, conv_w, w_up_att, w_up_conv, w_out, norm_ffn_g, w_group_router,
           b_group_router, w_expert_router, b_expert_router, w_gate, w_up, w_down, final_norm_g):
    b, s, d = x.shape
    depth = w_ada.shape[0]
    t = b * s
    assert s % TQ == 0 and s % TM_IN == 0 and TQ == TK and TM_IN % TK == 0
    assert TQ % TM_IN == 0 or TM_IN % TQ == 0
    assert N_HEADS % ATTN_HEADS_PER_STEP == 0
    assert s % TM_MIX == 0 and s % TM_FIN == 0 and t % TM_FFN == 0
    cap = t + TM_FFN
    n_e = N_GROUPS * EXPERTS_PER_GROUP

    assert b <= SUBLANES and EXPERTS_PER_GROUP == SUBLANES and N_GROUPS <= SUBLANES
    c8 = jnp.concatenate([c, jnp.zeros((SUBLANES - b, d), F32)], axis=0)
    pos3 = positions.reshape(b, s, 1)
    inv_freq = ROPE_THETA ** (-jnp.arange(0, HEAD_DIM, 2, dtype=F32) / HEAD_DIM)
    invf = jnp.tile(inv_freq, LANES // (HEAD_DIM // 2)).reshape(1, LANES)

    xf = x
    for l in range(depth):
        lambda_init = 0.8 - 0.6 * math.exp(-0.3 * l)
        row = lambda a: a[l].reshape(1, -1)
        mod8, lam = _ada(c8, w_ada[l], row(b_ada), row(lambda_q1), row(lambda_k1),
                         row(lambda_q2), row(lambda_k2), lambda_init)
        mod3 = mod8[0:b].reshape(b, 1, 6 * d)

        qt, k, vt, sg, mc = _inproj(xf.reshape(b, s, d), pos3, mod3, row(norm_mix_g), invf,
                                    w_in[l], conv_w[l], w_up_conv[l])
        att = _attn(lam, qt, k, vt, subln_g[l].reshape(-1, 1), 1.0 - lambda_init)

        pad = ROUTER_ROWS - n_e - N_GROUPS
        wrt = jnp.concatenate([w_expert_router[l].T, w_group_router[l].T,
                               jnp.zeros((pad, d), F32)], axis=0)
        br = jnp.concatenate([b_expert_router[l], b_group_router[l],
                              jnp.zeros((pad,), F32)]).reshape(ROUTER_ROWS, 1)
        x1, pos, cnt, hs = _mixout(att.reshape(t, -1), sg.reshape(t, d), mc.reshape(t, d),
                                   xf.reshape(t, d), mod3, w_up_att[l], w_out[l],
                                   row(norm_ffn_g), wrt, br,
                                   s // TM_MIX, cap)
        ys = _ffn(cnt[0:N_GROUPS, 0], hs, w_gate[l], w_up[l], w_down[l], cap)
        xf = _final(pos.reshape(t), ys, x1, mod3, final_norm_g.reshape(1, d), s // TM_FIN,
                    l == depth - 1)
    return xf.reshape(b, s, d)
```

```python
import functools
import math

import jax
import jax.numpy as jnp
from jax import lax
from jax.experimental import pallas as pl
from jax.experimental.pallas import tpu as pltpu

F32 = jnp.float32
BF16 = jnp.bfloat16
I32 = jnp.int32

EPS = 1e-6
MASK_VALUE = -1e30
ROPE_THETA = 10000.0

N_HEADS = 4
HEAD_DIM = 64
LANES = 128
SUBLANES = 8
N_GROUPS = 4
EXPERTS_PER_GROUP = 8
ROUTER_ROWS = 48

VMEM_LIMIT = 56 * 1024 * 1024

TM_IN = 1024
TQ = 512
TK = 512
ATTN_HEADS_PER_STEP = 1
VT_ROWS = LANES + 16
TM_MIX = 1024
TM_FFN = 512
TM_FIN = 512
TN_ADA = 2048


def _cparams(n_axes):
    return pltpu.CompilerParams(
        dimension_semantics=("arbitrary",) * n_axes, vmem_limit_bytes=VMEM_LIMIT)


def _ada_kernel(c_ref, w_ref, b_ref, lq1_ref, lk1_ref, lq2_ref, lk2_ref, mod_ref, lam_ref, *,
                lambda_init):
    c = c_ref[...]
    a = c * jax.nn.sigmoid(c)
    mod_ref[...] = jnp.dot(a.astype(BF16), w_ref[...].astype(BF16),
                           preferred_element_type=F32) + b_ref[...]
    s1 = jnp.sum(lq1_ref[...] * lk1_ref[...], axis=-1, keepdims=True)
    s2 = jnp.sum(lq2_ref[...] * lk2_ref[...], axis=-1, keepdims=True)
    lam = jnp.exp(s1) - jnp.exp(s2) + lambda_init
    lam_ref[...] = jnp.broadcast_to(lam, lam_ref.shape)


def _ada(c8, w_ada, b_ada, lq1, lk1, lq2, lk2, lambda_init):
    d, n = w_ada.shape
    small = pl.BlockSpec((1, HEAD_DIM), lambda j: (0, 0))
    return pl.pallas_call(
        functools.partial(_ada_kernel, lambda_init=lambda_init),
        grid=(n // TN_ADA,),
        in_specs=[pl.BlockSpec((SUBLANES, d), lambda j: (0, 0)),
                  pl.BlockSpec((d, TN_ADA), lambda j: (0, j)),
                  pl.BlockSpec((1, TN_ADA), lambda j: (0, j)),
                  small, small, small, small],
        out_specs=[pl.BlockSpec((SUBLANES, TN_ADA), lambda j: (0, j)),
                   pl.BlockSpec((SUBLANES, LANES), lambda j: (0, 0))],
        out_shape=[jax.ShapeDtypeStruct((SUBLANES, n), F32),
                   jax.ShapeDtypeStruct((SUBLANES, LANES), F32)],
        compiler_params=_cparams(1),
        name="ada",
    )(c8, w_ada, b_ada, lq1, lk1, lq2, lk2)


def _inproj_kernel(x_ref, pos_ref, mod_ref, g_ref, invf_ref, w_ref, cw_ref, wupc_ref,
                   qt_ref, k_ref, vt_ref, sg_ref, mc_ref, carry_ref, *, d, qk_w, conv_w):
    tm = x_ref.shape[1]
    x = x_ref[0]
    ms = jnp.mean(x * x, axis=-1, keepdims=True)
    mod = mod_ref[0]
    sh = mod[:, 0:d]
    sc = mod[:, d:2 * d]
    h = (x * lax.rsqrt(ms + EPS) * g_ref[...]) * (1.0 + sc) + sh
    hb = h.astype(BF16)

    def proj(a, b):
        return jnp.dot(hb, w_ref[:, a:b].astype(BF16), preferred_element_type=F32)

    lane = lax.broadcasted_iota(I32, (1, LANES), 1)
    first_half = (lane % HEAD_DIM) < (HEAD_DIM // 2)
    ang = pos_ref[0].astype(F32) * invf_ref[...]
    cos = jnp.cos(ang)
    sin = jnp.sin(ang)
    sin_signed = jnp.where(first_half, -sin, sin)

    def rope(t):
        partner = jnp.where(first_half, pltpu.roll(t, LANES - HEAD_DIM // 2, 1),
                            pltpu.roll(t, HEAD_DIM // 2, 1))
        return t * cos + partner * sin_signed

    zq = proj(0, qk_w)
    zk = proj(qk_w, 2 * qk_w)
    scale = HEAD_DIM ** -0.5 * math.log2(math.e)
    zv = proj(2 * qk_w, 3 * qk_w)
    for hd in range(qk_w // LANES):
        sl = slice(hd * LANES, (hd + 1) * LANES)
        rq = rope(zq[:, sl]) * scale
        qw = qt_ref.shape[4]
        for ck in range(tm // qw):
            qt_ref[0, hd, ck] = rq[ck * qw:(ck + 1) * qw, :].T.astype(BF16)
        k_ref[0, :, sl] = rope(zk[:, sl]).astype(BF16)
        for ck in range(tm // TK):
            vt_ref[0, hd, ck, 0:LANES, :] = zv[ck * TK:(ck + 1) * TK, sl].T.astype(BF16)
            vt_ref[0, hd, ck, LANES:VT_ROWS, :] = jnp.ones((VT_ROWS - LANES, TK), BF16)

    o = 3 * qk_w
    cb = proj(o, o + conv_w)
    u = proj(o + conv_w, o + 2 * conv_w) * proj(o + 2 * conv_w, o + 3 * conv_w)

    @pl.when(pl.program_id(1) == 0)
    def _():
        carry_ref[...] = jnp.zeros_like(carry_ref)

    prev = carry_ref[...]
    row = lax.broadcasted_iota(I32, (tm, 1), 0)
    last = SUBLANES - 1
    u1 = jnp.where(row == 0, prev[last:last + 1, :], pltpu.roll(u, 1, 0))
    u2 = jnp.where(row == 0, prev[last - 1:last, :],
                   jnp.where(row == 1, prev[last:last + 1, :], pltpu.roll(u, 2, 0)))
    carry_ref[...] = u[tm - SUBLANES:tm, :]
    cw = cw_ref[...]
    conv = cb * (cw[0:1, :] * u + cw[1:2, :] * u1 + cw[2:3, :] * u2)
    convp = jnp.dot(conv.astype(BF16), wupc_ref[...].astype(BF16), preferred_element_type=F32)

    o = o + 3 * conv_w
    sg_ref[0] = jax.nn.sigmoid(proj(o, o + d)).astype(BF16)
    mc_ref[0] = (jax.nn.sigmoid(proj(o + d, o + 2 * d)) * convp).astype(BF16)


def _inproj(x, pos3, mod3, norm_g, invf, w_in, conv_w, w_up_conv):
    b, s, d = x.shape
    qk_w = N_HEADS * 2 * HEAD_DIM
    cw = conv_w.shape[1]
    tm = TM_IN
    if tm >= TQ:
        qt_spec = pl.BlockSpec((1, N_HEADS, tm // TQ, LANES, TQ), lambda bi, si: (bi, 0, si, 0, 0))
    else:
        qt_spec = pl.BlockSpec((1, N_HEADS, 1, LANES, tm),
                               lambda bi, si: (bi, 0, si // (TQ // tm), 0, si % (TQ // tm)))
    tok = lambda w: pl.BlockSpec((1, tm, w), lambda bi, si: (bi, si, 0))
    full = lambda a: pl.BlockSpec(a.shape, lambda bi, si: (0,) * a.ndim)
    return pl.pallas_call(
        functools.partial(_inproj_kernel, d=d, qk_w=qk_w, conv_w=cw),
        grid=(b, s // tm),
        in_specs=[tok(d), tok(1),
                  pl.BlockSpec((1, 1, mod3.shape[2]), lambda bi, si: (bi, 0, 0)),
                  full(norm_g), full(invf), full(w_in), full(conv_w), full(w_up_conv)],
        out_specs=[qt_spec,
                   tok(qk_w),
                   pl.BlockSpec((1, N_HEADS, tm // TK, VT_ROWS, TK),
                                lambda bi, si: (bi, 0, si, 0, 0)),
                   tok(d), tok(d)],
        out_shape=[jax.ShapeDtypeStruct((b, N_HEADS, s // TQ, LANES, TQ), BF16),
                   jax.ShapeDtypeStruct((b, s, qk_w), BF16),
                   jax.ShapeDtypeStruct((b, N_HEADS, s // TK, VT_ROWS, TK), BF16),
                   jax.ShapeDtypeStruct((b, s, d), BF16),
                   jax.ShapeDtypeStruct((b, s, d), BF16)],
        scratch_shapes=[pltpu.VMEM((SUBLANES, cw), F32)],
        compiler_params=_cparams(2),
        name="inproj",
    )(x, pos3, mod3, norm_g, invf, w_in, conv_w, w_up_conv)


def _attn_kernel(lam_ref, qt_ref, k_ref, vt_ref, g_ref, o_ref, s_ref, p_ref, qq_ref, bias_ref, *,
                 out_scale):
    hps, n_q, tq = qt_ref.shape[1], qt_ref.shape[2], qt_ref.shape[4]
    feat = lax.broadcasted_iota(I32, (LANES, 1), 0)
    halves = [slice(h * tq, (h + 1) * tq) for h in range(2)]
    streams = [(h, half) for h in range(hps) for half in halves]

    k_sub = lax.broadcasted_iota(I32, (TK, 1), 0)
    q_lane = lax.broadcasted_iota(I32, (1, tq), 1)
    bias_ref[...] = jnp.where(k_sub <= q_lane, 0.0, MASK_VALUE)

    def load_queries(qi):
        for h in range(hps):
            qt = qt_ref[0, h, qi]
            zero = jnp.zeros_like(qt)
            qq_ref[h, :, halves[0]] = jnp.where(feat < HEAD_DIM, qt, zero)
            qq_ref[h, :, halves[1]] = jnp.where(feat >= HEAD_DIM, qt, zero)

    def scores(j):
        for h, lanes in streams:
            kt = k_ref[0, pl.ds(j * TK, TK), h * LANES:(h + 1) * LANES]
            s_ref[h, :, lanes] = jnp.dot(kt, qq_ref[h, :, lanes], preferred_element_type=F32)

    def values(j, state):
        return tuple((m, alpha * acc + jnp.dot(vt_ref[0, h, j], p_ref[h, :, lanes],
                                               preferred_element_type=F32), alpha)
                     for (h, lanes), (m, acc, alpha) in zip(streams, state))

    def softmax(state, diagonal):
        out = []
        for (h, lanes), (m, acc, _) in zip(streams, state):
            load = ((lambda: s_ref[h, :, lanes] + bias_ref[...]) if diagonal
                    else (lambda: s_ref[h, :, lanes]))
            m_new = jnp.maximum(m, jnp.max(load(), axis=0, keepdims=True))
            p_ref[h, :, lanes] = jnp.exp2(load() - m_new).astype(BF16)
            out.append((m_new, acc, jnp.exp2(m - m_new)))
        return tuple(out)

    def finish(qi, last_tile, state):
        state = values(last_tile, state)
        row0 = qi * tq
        for h in range(hps):
            (_, acc1, _), (_, acc2, _) = state[2 * h], state[2 * h + 1]
            o1 = acc1[0:LANES] / acc1[LANES:LANES + 1]
            o2 = acc2[0:LANES] / acc2[LANES:LANES + 1]
            a = o1 - lam_ref[0:1, 0:1] * o2
            y = a * lax.rsqrt(jnp.mean(a * a, axis=0, keepdims=True) + EPS) * g_ref[...]
            o_ref[0, pl.ds(row0, tq), h * LANES:(h + 1) * LANES] = (
                (y * out_scale).T.astype(BF16))

    def next_diagonal(qi):
        if qi + 1 < n_q:
            load_queries(qi + 1)
            scores(qi + 1)

    one = (jnp.full((1, tq), MASK_VALUE, F32), jnp.zeros((VT_ROWS, tq), F32),
           jnp.ones((1, tq), F32))
    init = tuple(one for _ in streams)

    load_queries(0)
    scores(0)
    pending = softmax(init, True)
    next_diagonal(0)

    for qi in range(1, n_q):
        finish(qi - 1, max(qi - 2, 0), pending)
        state = softmax(init, True)
        scores(0)
        for t in range(1, qi + 1):
            state = values(qi if t == 1 else t - 2, state)
            state = softmax(state, False)
            if t < qi:
                scores(t)
            else:
                next_diagonal(qi)
        pending = state
    finish(n_q - 1, max(n_q - 2, 0), pending)


def _attn(lam, qt, k, vt, subln_g_col, out_scale):
    b, s, w = k.shape
    hps = ATTN_HEADS_PER_STEP
    return pl.pallas_call(
        functools.partial(_attn_kernel, out_scale=out_scale),
        grid=(b, N_HEADS // hps),
        in_specs=[pl.BlockSpec((SUBLANES, LANES), lambda bi, hi: (0, 0)),
                  pl.BlockSpec((1, hps, s // TQ, LANES, TQ), lambda bi, hi: (bi, hi, 0, 0, 0)),
                  pl.BlockSpec((1, s, hps * LANES), lambda bi, hi: (bi, 0, hi)),
                  pl.BlockSpec((1, hps, s // TK, VT_ROWS, TK), lambda bi, hi: (bi, hi, 0, 0, 0)),
                  pl.BlockSpec((LANES, 1), lambda bi, hi: (0, 0))],
        out_specs=pl.BlockSpec((1, s, hps * LANES), lambda bi, hi: (bi, 0, hi)),
        out_shape=jax.ShapeDtypeStruct((b, s, w), BF16),
        scratch_shapes=[pltpu.VMEM((hps, TK, 2 * TQ), F32), pltpu.VMEM((hps, TK, 2 * TQ), BF16),
                        pltpu.VMEM((hps, LANES, 2 * TQ), BF16), pltpu.VMEM((TK, TQ), F32)],
        compiler_params=_cparams(2),
        name="attn",
    )(lam, qt, k, vt, subln_g_col)


def _mixout_kernel(att_ref, sg_ref, mc_ref, x_ref, mod_ref, wua_ref, wo_ref, g_ref, wrt_ref,
                   br_ref, x1_ref, pos_ref, cnt_ref, hs_ref,
                   h2c_ref, tri_ref, carry_ref, posv_ref, cntv_ref, poss_ref, cnts_ref,
                   sem_s, sem_r, *, d, cap, tail):
    tm = x_ref.shape[0]
    i = pl.program_id(0)
    last = pl.num_programs(0) - 1

    @pl.when(i == 0)
    def _():
        carry_ref[...] = jnp.zeros_like(carry_ref)
        earlier = (lax.broadcasted_iota(I32, (tm, 1), 0)
                   <= lax.broadcasted_iota(I32, (1, tm), 1))
        tri_ref[...] = jnp.where(earlier, 1.0, 0.0).astype(BF16)

    mod = mod_ref[0]
    g_m = mod[:, 2 * d:3 * d]
    sh_f = mod[:, 3 * d:4 * d]
    sc_f = mod[:, 4 * d:5 * d]

    attp = jnp.dot(att_ref[...], wua_ref[...].astype(BF16), preferred_element_type=F32)
    merged = sg_ref[...].astype(F32) * attp + mc_ref[...].astype(F32)
    mo = jnp.dot(merged.astype(BF16), wo_ref[...].astype(BF16), preferred_element_type=F32)
    x1 = x_ref[...] + g_m * mo
    x1_ref[...] = x1
    ms = jnp.mean(x1 * x1, axis=-1, keepdims=True)
    h2 = (x1 * lax.rsqrt(ms + EPS) * g_ref[...]) * (1.0 + sc_f) + sh_f

    def split(a):
        hi = a.astype(BF16)
        return hi, (a - hi.astype(F32)).astype(BF16)

    def dot_t(a, b):
        return lax.dot_general(a, b, (((1,), (1,)), ((), ())), preferred_element_type=F32)

    w_hi, w_lo = split(wrt_ref[...])
    h_hi, h_lo = split(h2)
    by_hi = dot_t(jnp.concatenate([w_hi, w_lo], axis=0), h_hi)
    lt = (by_hi[0:ROUTER_ROWS] + (dot_t(w_hi, h_lo) + by_hi[ROUTER_ROWS:2 * ROUTER_ROWS])
          + br_ref[...])
    n_sel = EXPERTS_PER_GROUP
    ridx = lax.broadcasted_iota(I32, (n_sel, 1), 0)
    n_e = N_GROUPS * EXPERTS_PER_GROUP
    is_grp = ridx < N_GROUPS
    gl = jnp.where(is_grp, lt[n_e:n_e + n_sel, :], MASK_VALUE)
    gmax = jnp.max(gl, axis=0, keepdims=True)
    gsum = jnp.sum(jnp.where(is_grp, jnp.exp(gl - gmax), 0.0), axis=0, keepdims=True)
    g_w = 1.0 / gsum
    gidx = jnp.min(jnp.where(gl == gmax, ridx, n_sel), axis=0, keepdims=True)
    e_in = jnp.zeros((n_sel, tm), F32)
    for g in range(N_GROUPS):
        e_in = jnp.where(gidx == g, lt[g * n_sel:(g + 1) * n_sel, :], e_in)
    t1 = jnp.max(e_in, axis=0, keepdims=True)
    i1 = jnp.min(jnp.where(e_in == t1, ridx, n_sel), axis=0, keepdims=True)
    e_in2 = jnp.where(ridx == i1, MASK_VALUE, e_in)
    t2 = jnp.max(e_in2, axis=0, keepdims=True)
    i2 = jnp.min(jnp.where(e_in2 == t2, ridx, n_sel), axis=0, keepdims=True)
    e2 = jnp.exp(t2 - t1)
    den = 1.0 + e2
    w1 = (1.0 / den) * g_w
    w2 = (e2 / den) * g_w
    cw_t = jnp.where(ridx == i1, w1, jnp.where(ridx == i2, w2, 0.0))
    cw_col = jnp.concatenate([cw_t, jnp.zeros((LANES - n_sel, tm), F32)], axis=0).T

    onehot = jnp.where(ridx == gidx, 1.0, 0.0)
    oh16 = jnp.concatenate([onehot, jnp.zeros_like(onehot)], axis=0).astype(BF16)
    incl = jnp.dot(oh16, tri_ref[...], preferred_element_type=F32)[0:n_sel, :]
    run = carry_ref[...][:, 0:1]
    base = ridx.astype(F32) * float(cap)
    pos = jnp.sum(onehot * (base + run + incl - onehot), axis=0, keepdims=True).astype(I32)
    new_run = run + incl[:, tm - 1:tm]
    carry_ref[...] = jnp.broadcast_to(new_run, carry_ref.shape)
    cnt_i = jnp.broadcast_to(new_run, cnt_ref.shape).astype(I32)
    cnt_ref[...] = cnt_i
    pos_ref[0] = pos
    posv_ref[...] = pos
    cp = pltpu.make_async_copy(posv_ref.at[0], poss_ref, sem_s)
    cp.start()

    def row_copy(r, p):
        return pltpu.make_async_copy(h2c_ref.at[pl.ds(r, 1), :], hs_ref.at[pl.ds(p, 1), :], sem_r)

    def wait_tile():
        pltpu.make_async_copy(h2c_ref, hs_ref.at[pl.ds(0, tm), :], sem_r).wait()

    @pl.when(i > 0)
    def _():
        wait_tile()

    h2c_ref[:, 0:d] = h2
    h2c_ref[:, d:d + LANES] = cw_col
    cp.wait()

    for r in range(tm):
        row_copy(r, poss_ref[r]).start(priority=r % 2)

    @pl.when(i == last)
    def _():
        wait_tile()
        cntv_ref[...] = cnt_i
        cc = pltpu.make_async_copy(cntv_ref, cnts_ref, sem_s)
        cc.start()
        cc.wait()
        h2c_ref[0:SUBLANES, :] = jnp.zeros((SUBLANES, h2c_ref.shape[1]), F32)

        def zissue(r, c):
            for g in range(N_GROUPS):
                row_copy(0, g * cap + cnts_ref[g, 0] + r).start()
            return c

        lax.fori_loop(0, tail, zissue, 0, unroll=2)
        for g in range(N_GROUPS):
            pltpu.make_async_copy(h2c_ref.at[pl.ds(0, tail), :], hs_ref.at[pl.ds(0, tail), :],
                                  sem_r).wait()


def _mixout(att, sg, mc, x, mod3, w_up_att, w_out, norm_g, wrt, br, tiles_per_batch, cap):
    t, d = x.shape
    tm = TM_MIX
    dc = d + LANES
    tok = lambda w: pl.BlockSpec((tm, w), lambda i: (i, 0))
    full = lambda a: pl.BlockSpec(a.shape, lambda i: (0,) * a.ndim)
    return pl.pallas_call(
        functools.partial(_mixout_kernel, d=d, cap=cap, tail=cap - t),
        grid=(t // tm,),
        in_specs=[tok(att.shape[1]), tok(d), tok(d), tok(d),
                  pl.BlockSpec((1, 1, mod3.shape[2]), lambda i: (i // tiles_per_batch, 0, 0)),
                  full(w_up_att), full(w_out), full(norm_g), full(wrt), full(br)],
        out_specs=[tok(d),
                   pl.BlockSpec((1, 1, tm), lambda i: (i, 0, 0)),
                   pl.BlockSpec((SUBLANES, LANES), lambda i: (0, 0)),
                   pl.BlockSpec(memory_space=pl.ANY)],
        out_shape=[jax.ShapeDtypeStruct((t, d), F32),
                   jax.ShapeDtypeStruct((t // tm, 1, tm), I32),
                   jax.ShapeDtypeStruct((SUBLANES, LANES), I32),
                   jax.ShapeDtypeStruct((N_GROUPS * cap, dc), F32)],
        scratch_shapes=[pltpu.VMEM((tm, dc), F32), pltpu.VMEM((tm, tm), BF16),
                        pltpu.VMEM((SUBLANES, LANES), F32),
                        pltpu.VMEM((1, tm), I32), pltpu.VMEM((SUBLANES, LANES), I32),
                        pltpu.SMEM((tm,), I32), pltpu.SMEM((SUBLANES, LANES), I32),
                        pltpu.SemaphoreType.DMA, pltpu.SemaphoreType.DMA],
        compiler_params=_cparams(1),
        name="mixout",
    )(att, sg, mc, x, mod3, w_up_att, w_out, norm_g, wrt, br)


def _ffn_tile(step, cnt_ref, tm):
    log_tm = tm.bit_length() - 1
    ends = []
    tot = jnp.int32(0)
    for g in range(N_GROUPS):
        tot = tot + lax.shift_right_logical(cnt_ref[g] + (tm - 1), log_tm)
        ends.append(tot)
    sc = jnp.maximum(jnp.minimum(step, tot - 1), 0)
    g = sum((sc >= e).astype(I32) for e in ends[:-1])
    start = jnp.where(g == 0, 0, jnp.where(g == 1, ends[0], jnp.where(g == 2, ends[1], ends[2])))
    return g, sc - start, step < tot


def _ffn_kernel(cnt_ref, hs_ref, wg_ref, wu_ref, wd_ref, y_ref, *, d):
    tm = hs_ref.shape[0]
    _, _, real = _ffn_tile(pl.program_id(0), cnt_ref, tm)

    @pl.when(real)
    def _():
        h = hs_ref[:, 0:d].astype(BF16)
        cw = hs_ref[:, d:d + LANES]
        acc = jnp.zeros((tm, d), F32)
        for e in range(EXPERTS_PER_GROUP):
            hg = jnp.dot(h, wg_ref[e].astype(BF16), preferred_element_type=F32)
            hu = jnp.dot(h, wu_ref[e].astype(BF16), preferred_element_type=F32)
            a = hg * jax.nn.sigmoid(hg) * hu * cw[:, e:e + 1]
            acc = acc + jnp.dot(a.astype(BF16), wd_ref[e].astype(BF16),
                                preferred_element_type=F32)
        y_ref[...] = acc


def _ffn(cnt, hs, w_gate, w_up, w_down, cap):
    rows, dc = hs.shape
    d = dc - LANES
    tm = TM_FFN
    blocks_per_group = cap // tm
    n_steps = (blocks_per_group - 1) + N_GROUPS

    def row_map(s, c):
        g, blk, _ = _ffn_tile(s, c, tm)
        return (g * blocks_per_group + blk, 0)

    def w_map(s, c):
        g, _, _ = _ffn_tile(s, c, tm)
        return (g, 0, 0)

    ff = w_gate.shape[2]
    once = pl.Buffered(1)
    gs = pltpu.PrefetchScalarGridSpec(
        num_scalar_prefetch=1,
        grid=(n_steps,),
        in_specs=[pl.BlockSpec((tm, dc), row_map),
                  pl.BlockSpec((EXPERTS_PER_GROUP, d, ff), w_map),
                  pl.BlockSpec((EXPERTS_PER_GROUP, d, ff), w_map),
                  pl.BlockSpec((EXPERTS_PER_GROUP, ff, d), w_map, pipeline_mode=once)],
        out_specs=pl.BlockSpec((tm, d), row_map),
    )
    return pl.pallas_call(
        functools.partial(_ffn_kernel, d=d),
        grid_spec=gs,
        out_shape=jax.ShapeDtypeStruct((rows, d), F32),
        compiler_params=_cparams(1),
        name="ffn",
    )(cnt, hs, w_gate, w_up, w_down)


def _final_kernel(pos_ref, ys_ref, x1_ref, mod_ref, g_ref, o_ref, ybuf_ref, sem, *, d, normalize):
    tm = x1_ref.shape[0]
    i = pl.program_id(0)

    def gather_tile(tile):
        slot = tile & 1
        base = tile * tm
        for r in range(tm):
            pltpu.make_async_copy(ys_ref.at[pl.ds(pos_ref[base + r], 1), :],
                                  ybuf_ref.at[slot, pl.ds(r, 1), :],
                                  sem.at[slot]).start(priority=r % 2)

    @pl.when(i == 0)
    def _():
        gather_tile(i)

    @pl.when(i + 1 < pl.num_programs(0))
    def _():
        gather_tile(i + 1)

    slot = i & 1
    pltpu.make_async_copy(ys_ref.at[pl.ds(0, tm), :], ybuf_ref.at[slot], sem.at[slot]).wait()

    g_f = mod_ref[0][:, 5 * d:6 * d]
    x2 = x1_ref[...] + g_f * ybuf_ref[slot]
    if normalize:
        x2 = x2 * lax.rsqrt(jnp.mean(x2 * x2, axis=-1, keepdims=True) + EPS) * g_ref[...]
    o_ref[...] = x2


def _final(pos, ys, x1, mod3, final_g, tiles_per_batch, normalize):
    t, d = x1.shape
    tm = TM_FIN
    gs = pltpu.PrefetchScalarGridSpec(
        num_scalar_prefetch=1,
        grid=(t // tm,),
        in_specs=[pl.BlockSpec(memory_space=pl.ANY),
                  pl.BlockSpec((tm, d), lambda i, p: (i, 0)),
                  pl.BlockSpec((1, 1, mod3.shape[2]), lambda i, p: (i // tiles_per_batch, 0, 0)),
                  pl.BlockSpec((1, d), lambda i, p: (0, 0))],
        out_specs=pl.BlockSpec((tm, d), lambda i, p: (i, 0)),
        scratch_shapes=[pltpu.VMEM((2, tm, d), F32), pltpu.SemaphoreType.DMA((2,))],
    )
    return pl.pallas_call(
        functools.partial(_final_kernel, d=d, normalize=normalize),
        grid_spec=gs,
        out_shape=jax.ShapeDtypeStruct((t, d), F32),
        compiler_params=_cparams(1),
        name="final",
    )(pos, ys, x1, mod3, final_g)


def kernel(x, c, positions, w_ada, b_ada, norm_mix_g, w_in, lambda_q1, lambda_k1, lambda_q2,
           lambda_k2, subln_g, conv_w, w_up_att, w_up_conv, w_out, norm_ffn_g, w_group_router,
           b_group_router, w_expert_router, b_expert_router, w_gate, w_up, w_down, final_norm_g):
    b, s, d = x.shape
    depth = w_ada.shape[0]
    t = b * s
    assert s % TQ == 0 and s % TM_IN == 0 and TQ == TK and TM_IN % TK == 0
    assert TQ % TM_IN == 0 or TM_IN % TQ == 0
    assert N_HEADS % ATTN_HEADS_PER_STEP == 0
    assert s % TM_MIX == 0 and s % TM_FIN == 0 and t % TM_FFN == 0
    cap = t + TM_FFN
    n_e = N_GROUPS * EXPERTS_PER_GROUP

    assert b <= SUBLANES and EXPERTS_PER_GROUP == SUBLANES and N_GROUPS <= SUBLANES
    c8 = jnp.concatenate([c, jnp.zeros((SUBLANES - b, d), F32)], axis=0)
    pos3 = positions.reshape(b, s, 1)
    inv_freq = ROPE_THETA ** (-jnp.arange(0, HEAD_DIM, 2, dtype=F32) / HEAD_DIM)
    invf = jnp.tile(inv_freq, LANES // (HEAD_DIM // 2)).reshape(1, LANES)

    xf = x
    for l in range(depth):
        lambda_init = 0.8 - 0.6 * math.exp(-0.3 * l)
        row = lambda a: a[l].reshape(1, -1)
        mod8, lam = _ada(c8, w_ada[l], row(b_ada), row(lambda_q1), row(lambda_k1),
                         row(lambda_q2), row(lambda_k2), lambda_init)
        mod3 = mod8[0:b].reshape(b, 1, 6 * d)

        qt, k, vt, sg, mc = _inproj(xf.reshape(b, s, d), pos3, mod3, row(norm_mix_g), invf,
                                    w_in[l], conv_w[l], w_up_conv[l])
        att = _attn(lam, qt, k, vt, subln_g[l].reshape(-1, 1), 1.0 - lambda_init)

        pad = ROUTER_ROWS - n_e - N_GROUPS
        wrt = jnp.concatenate([w_expert_router[l].T, w_group_router[l].T,
                               jnp.zeros((pad, d), F32)], axis=0)
        br = jnp.concatenate([b_expert_router[l], b_group_router[l],
                              jnp.zeros((pad,), F32)]).reshape(ROUTER_ROWS, 1)
        x1, pos, cnt, hs = _mixout(att.reshape(t, -1), sg.reshape(t, d), mc.reshape(t, d),
                                   xf.reshape(t, d), mod3, w_up_att[l], w_out[l],
                                   row(norm_ffn_g), wrt, br,
                                   s // TM_MIX, cap)
        ys = _ffn(cnt[0:N_GROUPS, 0], hs, w_gate[l], w_up[l], w_down[l], cap)
        xf = _final(pos.reshape(t), ys, x1, mod3, final_norm_g.reshape(1, d), s // TM_FIN,
                    l == depth - 1)
    return xf.reshape(b, s, d)
```

```python
import functools
import math

import jax
import jax.numpy as jnp
from jax import lax
from jax.experimental import pallas as pl
from jax.experimental.pallas import tpu as pltpu

F32 = jnp.float32
BF16 = jnp.bfloat16
I32 = jnp.int32

EPS = 1e-6
MASK_VALUE = -1e30
ROPE_THETA = 10000.0

N_HEADS = 4
HEAD_DIM = 64
LANES = 128
SUBLANES = 8
N_GROUPS = 4
EXPERTS_PER_GROUP = 8
ROUTER_ROWS = 48

VMEM_LIMIT = 56 * 1024 * 1024

TM_IN = 1024
TQ = 512
TK = 512
ATTN_HEADS_PER_STEP = 1
VT_ROWS = LANES + 16
TM_MIX = 1024
TM_FFN = 512
TM_FIN = 512
FIN_ROWS = 64
TN_ADA = 2048


def _cparams(n_axes):
    return pltpu.CompilerParams(
        dimension_semantics=("arbitrary",) * n_axes, vmem_limit_bytes=VMEM_LIMIT)


def _ada_kernel(c_ref, w_ref, b_ref, lq1_ref, lk1_ref, lq2_ref, lk2_ref, mod_ref, lam_ref, *,
                lambda_init):
    c = c_ref[...]
    a = c * jax.nn.sigmoid(c)
    mod_ref[...] = jnp.dot(a.astype(BF16), w_ref[...].astype(BF16),
                           preferred_element_type=F32) + b_ref[...]
    s1 = jnp.sum(lq1_ref[...] * lk1_ref[...], axis=-1, keepdims=True)
    s2 = jnp.sum(lq2_ref[...] * lk2_ref[...], axis=-1, keepdims=True)
    lam = jnp.exp(s1) - jnp.exp(s2) + lambda_init
    lam_ref[...] = jnp.broadcast_to(lam, lam_ref.shape)


def _ada(c8, w_ada, b_ada, lq1, lk1, lq2, lk2, lambda_init):
    d, n = w_ada.shape
    small = pl.BlockSpec((1, HEAD_DIM), lambda j: (0, 0))
    return pl.pallas_call(
        functools.partial(_ada_kernel, lambda_init=lambda_init),
        grid=(n // TN_ADA,),
        in_specs=[pl.BlockSpec((SUBLANES, d), lambda j: (0, 0)),
                  pl.BlockSpec((d, TN_ADA), lambda j: (0, j)),
                  pl.BlockSpec((1, TN_ADA), lambda j: (0, j)),
                  small, small, small, small],
        out_specs=[pl.BlockSpec((SUBLANES, TN_ADA), lambda j: (0, j)),
                   pl.BlockSpec((SUBLANES, LANES), lambda j: (0, 0))],
        out_shape=[jax.ShapeDtypeStruct((SUBLANES, n), F32),
                   jax.ShapeDtypeStruct((SUBLANES, LANES), F32)],
        compiler_params=_cparams(1),
        name="ada",
    )(c8, w_ada, b_ada, lq1, lk1, lq2, lk2)


def _inproj_kernel(x_ref, pos_ref, mod_ref, g_ref, invf_ref, w_ref, cw_ref, wupc_ref,
                   qt_ref, k_ref, vt_ref, sg_ref, mc_ref, carry_ref, *, d, qk_w, conv_w):
    tm = x_ref.shape[1]
    x = x_ref[0]
    ms = jnp.mean(x * x, axis=-1, keepdims=True)
    mod = mod_ref[0]
    sh = mod[:, 0:d]
    sc = mod[:, d:2 * d]
    h = (x * lax.rsqrt(ms + EPS) * g_ref[...]) * (1.0 + sc) + sh
    hb = h.astype(BF16)

    def proj(a, b):
        return jnp.dot(hb, w_ref[:, a:b].astype(BF16), preferred_element_type=F32)

    lane = lax.broadcasted_iota(I32, (1, LANES), 1)
    first_half = (lane % HEAD_DIM) < (HEAD_DIM // 2)
    ang = pos_ref[0].astype(F32) * invf_ref[...]
    cos = jnp.cos(ang)
    sin = jnp.sin(ang)
    sin_signed = jnp.where(first_half, -sin, sin)

    def rope(t):
        partner = jnp.where(first_half, pltpu.roll(t, LANES - HEAD_DIM // 2, 1),
                            pltpu.roll(t, HEAD_DIM // 2, 1))
        return t * cos + partner * sin_signed

    zq = proj(0, qk_w)
    zk = proj(qk_w, 2 * qk_w)
    scale = HEAD_DIM ** -0.5 * math.log2(math.e)
    zv = proj(2 * qk_w, 3 * qk_w)
    for hd in range(qk_w // LANES):
        sl = slice(hd * LANES, (hd + 1) * LANES)
        rq = rope(zq[:, sl]) * scale
        qw = qt_ref.shape[4]
        for ck in range(tm // qw):
            qt_ref[0, hd, ck] = rq[ck * qw:(ck + 1) * qw, :].T.astype(BF16)
        k_ref[0, :, sl] = rope(zk[:, sl]).astype(BF16)
        for ck in range(tm // TK):
            vt_ref[0, hd, ck, 0:LANES, :] = zv[ck * TK:(ck + 1) * TK, sl].T.astype(BF16)
            vt_ref[0, hd, ck, LANES:VT_ROWS, :] = jnp.ones((VT_ROWS - LANES, TK), BF16)

    o = 3 * qk_w
    cb = proj(o, o + conv_w)
    u = proj(o + conv_w, o + 2 * conv_w) * proj(o + 2 * conv_w, o + 3 * conv_w)

    @pl.when(pl.program_id(1) == 0)
    def _():
        carry_ref[...] = jnp.zeros_like(carry_ref)

    prev = carry_ref[...]
    row = lax.broadcasted_iota(I32, (tm, 1), 0)
    last = SUBLANES - 1
    u1 = jnp.where(row == 0, prev[last:last + 1, :], pltpu.roll(u, 1, 0))
    u2 = jnp.where(row == 0, prev[last - 1:last, :],
                   jnp.where(row == 1, prev[last:last + 1, :], pltpu.roll(u, 2, 0)))
    carry_ref[...] = u[tm - SUBLANES:tm, :]
    cw = cw_ref[...]
    conv = cb * (cw[0:1, :] * u + cw[1:2, :] * u1 + cw[2:3, :] * u2)
    convp = jnp.dot(conv.astype(BF16), wupc_ref[...].astype(BF16), preferred_element_type=F32)

    o = o + 3 * conv_w
    sg_ref[0] = jax.nn.sigmoid(proj(o, o + d)).astype(BF16)
    mc_ref[0] = (jax.nn.sigmoid(proj(o + d, o + 2 * d)) * convp).astype(BF16)


def _inproj(x, pos3, mod3, norm_g, invf, w_in, conv_w, w_up_conv):
    b, s, d = x.shape
    qk_w = N_HEADS * 2 * HEAD_DIM
    cw = conv_w.shape[1]
    tm = TM_IN
    if tm >= TQ:
        qt_spec = pl.BlockSpec((1, N_HEADS, tm // TQ, LANES, TQ), lambda bi, si: (bi, 0, si, 0, 0))
    else:
        qt_spec = pl.BlockSpec((1, N_HEADS, 1, LANES, tm),
                               lambda bi, si: (bi, 0, si // (TQ // tm), 0, si % (TQ // tm)))
    tok = lambda w: pl.BlockSpec((1, tm, w), lambda bi, si: (bi, si, 0))
    full = lambda a: pl.BlockSpec(a.shape, lambda bi, si: (0,) * a.ndim)
    return pl.pallas_call(
        functools.partial(_inproj_kernel, d=d, qk_w=qk_w, conv_w=cw),
        grid=(b, s // tm),
        in_specs=[tok(d), tok(1),
                  pl.BlockSpec((1, 1, mod3.shape[2]), lambda bi, si: (bi, 0, 0)),
                  full(norm_g), full(invf), full(w_in), full(conv_w), full(w_up_conv)],
        out_specs=[qt_spec,
                   tok(qk_w),
                   pl.BlockSpec((1, N_HEADS, tm // TK, VT_ROWS, TK),
                                lambda bi, si: (bi, 0, si, 0, 0)),
                   tok(d), tok(d)],
        out_shape=[jax.ShapeDtypeStruct((b, N_HEADS, s // TQ, LANES, TQ), BF16),
                   jax.ShapeDtypeStruct((b, s, qk_w), BF16),
                   jax.ShapeDtypeStruct((b, N_HEADS, s // TK, VT_ROWS, TK), BF16),
                   jax.ShapeDtypeStruct((b, s, d), BF16),
                   jax.ShapeDtypeStruct((b, s, d), BF16)],
        scratch_shapes=[pltpu.VMEM((SUBLANES, cw), F32)],
        compiler_params=_cparams(2),
        name="inproj",
    )(x, pos3, mod3, norm_g, invf, w_in, conv_w, w_up_conv)


def _attn_kernel(lam_ref, qt_ref, k_ref, vt_ref, g_ref, o_ref, s_ref, p_ref, qq_ref, bias_ref, *,
                 out_scale):
    hps, n_q, tq = qt_ref.shape[1], qt_ref.shape[2], qt_ref.shape[4]
    feat = lax.broadcasted_iota(I32, (LANES, 1), 0)
    halves = [slice(h * tq, (h + 1) * tq) for h in range(2)]
    streams = [(h, half) for h in range(hps) for half in halves]

    k_sub = lax.broadcasted_iota(I32, (TK, 1), 0)
    q_lane = lax.broadcasted_iota(I32, (1, tq), 1)
    bias_ref[...] = jnp.where(k_sub <= q_lane, 0.0, MASK_VALUE)

    def load_queries(qi):
        for h in range(hps):
            qt = qt_ref[0, h, qi]
            zero = jnp.zeros_like(qt)
            qq_ref[h, :, halves[0]] = jnp.where(feat < HEAD_DIM, qt, zero)
            qq_ref[h, :, halves[1]] = jnp.where(feat >= HEAD_DIM, qt, zero)

    def scores(j):
        for h, lanes in streams:
            kt = k_ref[0, pl.ds(j * TK, TK), h * LANES:(h + 1) * LANES]
            s_ref[h, :, lanes] = jnp.dot(kt, qq_ref[h, :, lanes], preferred_element_type=F32)

    def values(j, state):
        return tuple((m, alpha * acc + jnp.dot(vt_ref[0, h, j], p_ref[h, :, lanes],
                                               preferred_element_type=F32), alpha)
                     for (h, lanes), (m, acc, alpha) in zip(streams, state))

    def softmax(state, diagonal):
        out = []
        for (h, lanes), (m, acc, _) in zip(streams, state):
            load = ((lambda: s_ref[h, :, lanes] + bias_ref[...]) if diagonal
                    else (lambda: s_ref[h, :, lanes]))
            m_new = jnp.maximum(m, jnp.max(load(), axis=0, keepdims=True))
            p_ref[h, :, lanes] = jnp.exp2(load() - m_new).astype(BF16)
            out.append((m_new, acc, jnp.exp2(m - m_new)))
        return tuple(out)

    def finish(qi, last_tile, state):
        state = values(last_tile, state)
        row0 = qi * tq
        for h in range(hps):
            (_, acc1, _), (_, acc2, _) = state[2 * h], state[2 * h + 1]
            o1 = acc1[0:LANES] / acc1[LANES:LANES + 1]
            o2 = acc2[0:LANES] / acc2[LANES:LANES + 1]
            a = o1 - lam_ref[0:1, 0:1] * o2
            y = a * lax.rsqrt(jnp.mean(a * a, axis=0, keepdims=True) + EPS) * g_ref[...]
            o_ref[0, pl.ds(row0, tq), h * LANES:(h + 1) * LANES] = (
                (y * out_scale).T.astype(BF16))

    def next_diagonal(qi):
        if qi + 1 < n_q:
            load_queries(qi + 1)
            scores(qi + 1)

    one = (jnp.full((1, tq), MASK_VALUE, F32), jnp.zeros((VT_ROWS, tq), F32),
           jnp.ones((1, tq), F32))
    init = tuple(one for _ in streams)

    load_queries(0)
    scores(0)
    pending = softmax(init, True)
    next_diagonal(0)

    for qi in range(1, n_q):
        finish(qi - 1, max(qi - 2, 0), pending)
        state = softmax(init, True)
        scores(0)
        for t in range(1, qi + 1):
            state = values(qi if t == 1 else t - 2, state)
            state = softmax(state, False)
            if t < qi:
                scores(t)
            else:
                next_diagonal(qi)
        pending = state
    finish(n_q - 1, max(n_q - 2, 0), pending)


def _attn(lam, qt, k, vt, subln_g_col, out_scale):
    b, s, w = k.shape
    hps = ATTN_HEADS_PER_STEP
    return pl.pallas_call(
        functools.partial(_attn_kernel, out_scale=out_scale),
        grid=(b, N_HEADS // hps),
        in_specs=[pl.BlockSpec((SUBLANES, LANES), lambda bi, hi: (0, 0)),
                  pl.BlockSpec((1, hps, s // TQ, LANES, TQ), lambda bi, hi: (bi, hi, 0, 0, 0)),
                  pl.BlockSpec((1, s, hps * LANES), lambda bi, hi: (bi, 0, hi)),
                  pl.BlockSpec((1, hps, s // TK, VT_ROWS, TK), lambda bi, hi: (bi, hi, 0, 0, 0)),
                  pl.BlockSpec((LANES, 1), lambda bi, hi: (0, 0))],
        out_specs=pl.BlockSpec((1, s, hps * LANES), lambda bi, hi: (bi, 0, hi)),
        out_shape=jax.ShapeDtypeStruct((b, s, w), BF16),
        scratch_shapes=[pltpu.VMEM((hps, TK, 2 * TQ), F32), pltpu.VMEM((hps, TK, 2 * TQ), BF16),
                        pltpu.VMEM((hps, LANES, 2 * TQ), BF16), pltpu.VMEM((TK, TQ), F32)],
        compiler_params=_cparams(2),
        name="attn",
    )(lam, qt, k, vt, subln_g_col)


def _mixout_kernel(att_ref, sg_ref, mc_ref, x_ref, mod_ref, wua_ref, wo_ref, g_ref, wrt_ref,
                   br_ref, x1_ref, pos_ref, cnt_ref, hs_ref,
                   h2c_ref, tri_ref, carry_ref, posv_ref, cntv_ref, poss_ref, cnts_ref,
                   sem_s, sem_r, *, d, cap, tail):
    tm = x_ref.shape[0]
    i = pl.program_id(0)
    last = pl.num_programs(0) - 1

    @pl.when(i == 0)
    def _():
        carry_ref[...] = jnp.zeros_like(carry_ref)
        earlier = (lax.broadcasted_iota(I32, (tm, 1), 0)
                   <= lax.broadcasted_iota(I32, (1, tm), 1))
        tri_ref[...] = jnp.where(earlier, 1.0, 0.0).astype(BF16)

    mod = mod_ref[0]
    g_m = mod[:, 2 * d:3 * d]
    sh_f = mod[:, 3 * d:4 * d]
    sc_f = mod[:, 4 * d:5 * d]

    attp = jnp.dot(att_ref[...], wua_ref[...].astype(BF16), preferred_element_type=F32)
    merged = sg_ref[...].astype(F32) * attp + mc_ref[...].astype(F32)
    mo = jnp.dot(merged.astype(BF16), wo_ref[...].astype(BF16), preferred_element_type=F32)
    x1 = x_ref[...] + g_m * mo
    x1_ref[...] = x1
    ms = jnp.mean(x1 * x1, axis=-1, keepdims=True)
    h2 = (x1 * lax.rsqrt(ms + EPS) * g_ref[...]) * (1.0 + sc_f) + sh_f

    def split(a):
        hi = a.astype(BF16)
        return hi, (a - hi.astype(F32)).astype(BF16)

    def dot_t(a, b):
        return lax.dot_general(a, b, (((1,), (1,)), ((), ())), preferred_element_type=F32)

    w_hi, w_lo = split(wrt_ref[...])
    h_hi, h_lo = split(h2)
    by_hi = dot_t(jnp.concatenate([w_hi, w_lo], axis=0), h_hi)
    lt = (by_hi[0:ROUTER_ROWS] + (dot_t(w_hi, h_lo) + by_hi[ROUTER_ROWS:2 * ROUTER_ROWS])
          + br_ref[...])
    n_sel = EXPERTS_PER_GROUP
    ridx = lax.broadcasted_iota(I32, (n_sel, 1), 0)
    n_e = N_GROUPS * EXPERTS_PER_GROUP
    is_grp = ridx < N_GROUPS
    gl = jnp.where(is_grp, lt[n_e:n_e + n_sel, :], MASK_VALUE)
    gmax = jnp.max(gl, axis=0, keepdims=True)
    gsum = jnp.sum(jnp.where(is_grp, jnp.exp(gl - gmax), 0.0), axis=0, keepdims=True)
    g_w = 1.0 / gsum
    gidx = jnp.min(jnp.where(gl == gmax, ridx, n_sel), axis=0, keepdims=True)
    e_in = jnp.zeros((n_sel, tm), F32)
    for g in range(N_GROUPS):
        e_in = jnp.where(gidx == g, lt[g * n_sel:(g + 1) * n_sel, :], e_in)
    t1 = jnp.max(e_in, axis=0, keepdims=True)
    i1 = jnp.min(jnp.where(e_in == t1, ridx, n_sel), axis=0, keepdims=True)
    e_in2 = jnp.where(ridx == i1, MASK_VALUE, e_in)
    t2 = jnp.max(e_in2, axis=0, keepdims=True)
    i2 = jnp.min(jnp.where(e_in2 == t2, ridx, n_sel), axis=0, keepdims=True)
    e2 = jnp.exp(t2 - t1)
    den = 1.0 + e2
    w1 = (1.0 / den) * g_w
    w2 = (e2 / den) * g_w
    cw_t = jnp.where(ridx == i1, w1, jnp.where(ridx == i2, w2, 0.0))
    cw_col = jnp.concatenate([cw_t, jnp.zeros((LANES - n_sel, tm), F32)], axis=0).T

    onehot = jnp.where(ridx == gidx, 1.0, 0.0)
    oh16 = jnp.concatenate([onehot, jnp.zeros_like(onehot)], axis=0).astype(BF16)
    incl = jnp.dot(oh16, tri_ref[...], preferred_element_type=F32)[0:n_sel, :]
    run = carry_ref[...][:, 0:1]
    base = ridx.astype(F32) * float(cap)
    pos = jnp.sum(onehot * (base + run + incl - onehot), axis=0, keepdims=True).astype(I32)
    new_run = run + incl[:, tm - 1:tm]
    carry_ref[...] = jnp.broadcast_to(new_run, carry_ref.shape)
    cnt_i = jnp.broadcast_to(new_run, cnt_ref.shape).astype(I32)
    cnt_ref[...] = cnt_i
    pos_ref[0] = pos
    posv_ref[...] = pos
    cp = pltpu.make_async_copy(posv_ref.at[0], poss_ref, sem_s)
    cp.start()

    def row_copy(r, p):
        return pltpu.make_async_copy(h2c_ref.at[pl.ds(r, 1), :], hs_ref.at[pl.ds(p, 1), :], sem_r)

    def wait_tile():
        pltpu.make_async_copy(h2c_ref, hs_ref.at[pl.ds(0, tm), :], sem_r).wait()

    @pl.when(i > 0)
    def _():
        wait_tile()

    h2c_ref[:, 0:d] = h2
    h2c_ref[:, d:d + LANES] = cw_col
    cp.wait()

    for r in range(tm):
        row_copy(r, poss_ref[r]).start(priority=r % 2)

    @pl.when(i == last)
    def _():
        wait_tile()
        cntv_ref[...] = cnt_i
        cc = pltpu.make_async_copy(cntv_ref, cnts_ref, sem_s)
        cc.start()
        cc.wait()
        h2c_ref[0:SUBLANES, :] = jnp.zeros((SUBLANES, h2c_ref.shape[1]), F32)

        def zissue(r, c):
            for g in range(N_GROUPS):
                row_copy(0, g * cap + cnts_ref[g, 0] + r).start()
            return c

        lax.fori_loop(0, tail, zissue, 0, unroll=2)
        for g in range(N_GROUPS):
            pltpu.make_async_copy(h2c_ref.at[pl.ds(0, tail), :], hs_ref.at[pl.ds(0, tail), :],
                                  sem_r).wait()


def _mixout(att, sg, mc, x, mod3, w_up_att, w_out, norm_g, wrt, br, tiles_per_batch, cap):
    t, d = x.shape
    tm = TM_MIX
    dc = d + LANES
    tok = lambda w: pl.BlockSpec((tm, w), lambda i: (i, 0))
    full = lambda a: pl.BlockSpec(a.shape, lambda i: (0,) * a.ndim)
    return pl.pallas_call(
        functools.partial(_mixout_kernel, d=d, cap=cap, tail=cap - t),
        grid=(t // tm,),
        in_specs=[tok(att.shape[1]), tok(d), tok(d), tok(d),
                  pl.BlockSpec((1, 1, mod3.shape[2]), lambda i: (i // tiles_per_batch, 0, 0)),
                  full(w_up_att), full(w_out), full(norm_g), full(wrt), full(br)],
        out_specs=[tok(d),
                   pl.BlockSpec((1, 1, tm), lambda i: (i, 0, 0)),
                   pl.BlockSpec((SUBLANES, LANES), lambda i: (0, 0)),
                   pl.BlockSpec(memory_space=pl.ANY)],
        out_shape=[jax.ShapeDtypeStruct((t, d), F32),
                   jax.ShapeDtypeStruct((t // tm, 1, tm), I32),
                   jax.ShapeDtypeStruct((SUBLANES, LANES), I32),
                   jax.ShapeDtypeStruct((N_GROUPS * cap, dc), F32)],
        scratch_shapes=[pltpu.VMEM((tm, dc), F32), pltpu.VMEM((tm, tm), BF16),
                        pltpu.VMEM((SUBLANES, LANES), F32),
                        pltpu.VMEM((1, tm), I32), pltpu.VMEM((SUBLANES, LANES), I32),
                        pltpu.SMEM((tm,), I32), pltpu.SMEM((SUBLANES, LANES), I32),
                        pltpu.SemaphoreType.DMA, pltpu.SemaphoreType.DMA],
        compiler_params=_cparams(1),
        name="mixout",
    )(att, sg, mc, x, mod3, w_up_att, w_out, norm_g, wrt, br)


def _ffn_tile(step, cnt_ref, tm):
    log_tm = tm.bit_length() - 1
    ends = []
    tot = jnp.int32(0)
    for g in range(N_GROUPS):
        tot = tot + lax.shift_right_logical(cnt_ref[g] + (tm - 1), log_tm)
        ends.append(tot)
    sc = jnp.maximum(jnp.minimum(step, tot - 1), 0)
    g = sum((sc >= e).astype(I32) for e in ends[:-1])
    start = jnp.where(g == 0, 0, jnp.where(g == 1, ends[0], jnp.where(g == 2, ends[1], ends[2])))
    return g, sc - start, step < tot


def _ffn_kernel(cnt_ref, hs_ref, wg_ref, wu_ref, wd_ref, y_ref, *, d):
    tm = hs_ref.shape[0]
    _, _, real = _ffn_tile(pl.program_id(0), cnt_ref, tm)

    @pl.when(real)
    def _():
        h = hs_ref[:, 0:d].astype(BF16)
        cw = hs_ref[:, d:d + LANES]
        acc = jnp.zeros((tm, d), F32)
        for e in range(EXPERTS_PER_GROUP):
            hg = jnp.dot(h, wg_ref[e].astype(BF16), preferred_element_type=F32)
            hu = jnp.dot(h, wu_ref[e].astype(BF16), preferred_element_type=F32)
            a = hg * jax.nn.sigmoid(hg) * hu * cw[:, e:e + 1]
            acc = acc + jnp.dot(a.astype(BF16), wd_ref[e].astype(BF16),
                                preferred_element_type=F32)
        y_ref[...] = acc


def _ffn(cnt, hs, w_gate, w_up, w_down, cap):
    rows, dc = hs.shape
    d = dc - LANES
    tm = TM_FFN
    blocks_per_group = cap // tm
    n_steps = (blocks_per_group - 1) + N_GROUPS

    def row_map(s, c):
        g, blk, _ = _ffn_tile(s, c, tm)
        return (g * blocks_per_group + blk, 0)

    def w_map(s, c):
        g, _, _ = _ffn_tile(s, c, tm)
        return (g, 0, 0)

    ff = w_gate.shape[2]
    once = pl.Buffered(1)
    gs = pltpu.PrefetchScalarGridSpec(
        num_scalar_prefetch=1,
        grid=(n_steps,),
        in_specs=[pl.BlockSpec((tm, dc), row_map),
                  pl.BlockSpec((EXPERTS_PER_GROUP, d, ff), w_map),
                  pl.BlockSpec((EXPERTS_PER_GROUP, d, ff), w_map),
                  pl.BlockSpec((EXPERTS_PER_GROUP, ff, d), w_map, pipeline_mode=once)],
        out_specs=pl.BlockSpec((tm, d), row_map),
    )
    return pl.pallas_call(
        functools.partial(_ffn_kernel, d=d),
        grid_spec=gs,
        out_shape=jax.ShapeDtypeStruct((rows, d), F32),
        compiler_params=_cparams(1),
        name="ffn",
    )(cnt, hs, w_gate, w_up, w_down)


def _final_kernel(pos_ref, ys_ref, x1_ref, mod_ref, g_ref, o_ref, ybuf_ref, sem, *, d, normalize):
    tm = x1_ref.shape[0]
    i = pl.program_id(0)
    n = pl.num_programs(0)
    n_buf = ybuf_ref.shape[0]

    def gather_rows(tile, rows):
        slot = lax.rem(tile, n_buf)
        base = jnp.minimum(tile, n - 1) * tm
        for r in rows:
            pltpu.make_async_copy(ys_ref.at[pl.ds(pos_ref[base + r], 1), :],
                                  ybuf_ref.at[slot, pl.ds(r, 1), :],
                                  sem.at[slot]).start(priority=r % 2)

    def wait_tile(tile):
        slot = lax.rem(tile, n_buf)
        pltpu.make_async_copy(ys_ref.at[pl.ds(0, tm), :], ybuf_ref.at[slot], sem.at[slot]).wait()

    @pl.when(i == 0)
    def _():
        gather_rows(i, range(tm))
        gather_rows(i + 1, range(tm))

    wait_tile(i)
    slot = lax.rem(i, n_buf)
    g_f = mod_ref[0][:, 5 * d:6 * d]
    for c in range(tm // FIN_ROWS):
        rows = slice(c * FIN_ROWS, (c + 1) * FIN_ROWS)
        x2 = x1_ref[rows, :] + g_f * ybuf_ref[slot, rows, :]
        if normalize:
            x2 = x2 * lax.rsqrt(jnp.mean(x2 * x2, axis=-1, keepdims=True) + EPS) * g_ref[...]
        o_ref[rows, :] = x2
        gather_rows(i + 2, range(c * FIN_ROWS, (c + 1) * FIN_ROWS))

    @pl.when(i == n - 1)
    def _():
        wait_tile(i + 1)
        wait_tile(i + 2)


def _final(pos, ys, x1, mod3, final_g, tiles_per_batch, normalize):
    t, d = x1.shape
    tm = TM_FIN
    gs = pltpu.PrefetchScalarGridSpec(
        num_scalar_prefetch=1,
        grid=(t // tm,),
        in_specs=[pl.BlockSpec(memory_space=pl.ANY),
                  pl.BlockSpec((tm, d), lambda i, p: (i, 0)),
                  pl.BlockSpec((1, 1, mod3.shape[2]), lambda i, p: (i // tiles_per_batch, 0, 0)),
                  pl.BlockSpec((1, d), lambda i, p: (0, 0))],
        out_specs=pl.BlockSpec((tm, d), lambda i, p: (i, 0)),
        scratch_shapes=[pltpu.VMEM((3, tm, d), F32), pltpu.SemaphoreType.DMA((3,))],
    )
    return pl.pallas_call(
        functools.partial(_final_kernel, d=d, normalize=normalize),
        grid_spec=gs,
        out_shape=jax.ShapeDtypeStruct((t, d), F32),
        compiler_params=_cparams(1),
        name="final",
    )(pos, ys, x1, mod3, final_g)


def kernel(x, c, positions, w_ada, b_ada, norm_mix_g, w_in, lambda_q1, lambda_k1, lambda_q2,
           lambda_k2, subln_g, conv_w, w_up_att, w_up_conv, w_out, norm_ffn_g, w_group_router,
           b_group_router, w_expert_router, b_expert_router, w_gate, w_up, w_down, final_norm_g):
    b, s, d = x.shape
    depth = w_ada.shape[0]
    t = b * s
    assert s % TQ == 0 and s % TM_IN == 0 and TQ == TK and TM_IN % TK == 0
    assert TQ % TM_IN == 0 or TM_IN % TQ == 0
    assert N_HEADS % ATTN_HEADS_PER_STEP == 0
    assert s % TM_MIX == 0 and s % TM_FIN == 0 and t % TM_FFN == 0
    cap = t + TM_FFN
    n_e = N_GROUPS * EXPERTS_PER_GROUP

    assert b <= SUBLANES and EXPERTS_PER_GROUP == SUBLANES and N_GROUPS <= SUBLANES
    c8 = jnp.concatenate([c, jnp.zeros((SUBLANES - b, d), F32)], axis=0)
    pos3 = positions.reshape(b, s, 1)
    inv_freq = ROPE_THETA ** (-jnp.arange(0, HEAD_DIM, 2, dtype=F32) / HEAD_DIM)
    invf = jnp.tile(inv_freq, LANES // (HEAD_DIM // 2)).reshape(1, LANES)

    xf = x
    for l in range(depth):
        lambda_init = 0.8 - 0.6 * math.exp(-0.3 * l)
        row = lambda a: a[l].reshape(1, -1)
        mod8, lam = _ada(c8, w_ada[l], row(b_ada), row(lambda_q1), row(lambda_k1),
                         row(lambda_q2), row(lambda_k2), lambda_init)
        mod3 = mod8[0:b].reshape(b, 1, 6 * d)

        qt, k, vt, sg, mc = _inproj(xf.reshape(b, s, d), pos3, mod3, row(norm_mix_g), invf,
                                    w_in[l], conv_w[l], w_up_conv[l])
        att = _attn(lam, qt, k, vt, subln_g[l].reshape(-1, 1), 1.0 - lambda_init)

        pad = ROUTER_ROWS - n_e - N_GROUPS
        wrt = jnp.concatenate([w_expert_router[l].T, w_group_router[l].T,
                               jnp.zeros((pad, d), F32)], axis=0)
        br = jnp.concatenate([b_expert_router[l], b_group_router[l],
                              jnp.zeros((pad,), F32)]).reshape(ROUTER_ROWS, 1)
        x1, pos, cnt, hs = _mixout(att.reshape(t, -1), sg.reshape(t, d), mc.reshape(t, d),
                                   xf.reshape(t, d), mod3, w_up_att[l], w_out[l],
                                   row(norm_ffn_g), wrt, br,
                                   s // TM_MIX, cap)
        ys = _ffn(cnt[0:N_GROUPS, 0], hs, w_gate[l], w_up[l], w_down[l], cap)
        xf = _final(pos.reshape(t), ys, x1, mod3, final_norm_g.reshape(1, d), s // TM_FIN,
                    l == depth - 1)
    return xf.reshape(b, s, d)
```

```python
import functools
import math

import jax
import jax.numpy as jnp
from jax import lax
from jax.experimental import pallas as pl
from jax.experimental.pallas import tpu as pltpu

F32 = jnp.float32
BF16 = jnp.bfloat16
I32 = jnp.int32

EPS = 1e-6
MASK_VALUE = -1e30
ROPE_THETA = 10000.0

N_HEADS = 4
HEAD_DIM = 64
LANES = 128
SUBLANES = 8
N_GROUPS = 4
EXPERTS_PER_GROUP = 8
ROUTER_ROWS = 48

VMEM_LIMIT = 56 * 1024 * 1024

TM_IN = 1024
TQ = 512
TK = 512
ATTN_HEADS_PER_STEP = 1
VT_ROWS = LANES + 16
TM_MIX = 1024
TM_FFN = 512
TM_FIN = 512
FIN_ROWS = 64
TN_ADA = 2048


def _cparams(n_axes):
    return pltpu.CompilerParams(
        dimension_semantics=("arbitrary",) * n_axes, vmem_limit_bytes=VMEM_LIMIT)


def _ada_kernel(c_ref, w_ref, b_ref, lq1_ref, lk1_ref, lq2_ref, lk2_ref, mod_ref, lam_ref, *,
                lambda_init):
    c = c_ref[...]
    a = c * jax.nn.sigmoid(c)
    mod_ref[...] = jnp.dot(a.astype(BF16), w_ref[...].astype(BF16),
                           preferred_element_type=F32) + b_ref[...]
    s1 = jnp.sum(lq1_ref[...] * lk1_ref[...], axis=-1, keepdims=True)
    s2 = jnp.sum(lq2_ref[...] * lk2_ref[...], axis=-1, keepdims=True)
    lam = jnp.exp(s1) - jnp.exp(s2) + lambda_init
    lam_ref[...] = jnp.broadcast_to(lam, lam_ref.shape)


def _ada(c8, w_ada, b_ada, lq1, lk1, lq2, lk2, lambda_init):
    d, n = w_ada.shape
    small = pl.BlockSpec((1, HEAD_DIM), lambda j: (0, 0))
    return pl.pallas_call(
        functools.partial(_ada_kernel, lambda_init=lambda_init),
        grid=(n // TN_ADA,),
        in_specs=[pl.BlockSpec((SUBLANES, d), lambda j: (0, 0)),
                  pl.BlockSpec((d, TN_ADA), lambda j: (0, j)),
                  pl.BlockSpec((1, TN_ADA), lambda j: (0, j)),
                  small, small, small, small],
        out_specs=[pl.BlockSpec((SUBLANES, TN_ADA), lambda j: (0, j)),
                   pl.BlockSpec((SUBLANES, LANES), lambda j: (0, 0))],
        out_shape=[jax.ShapeDtypeStruct((SUBLANES, n), F32),
                   jax.ShapeDtypeStruct((SUBLANES, LANES), F32)],
        compiler_params=_cparams(1),
        name="ada",
    )(c8, w_ada, b_ada, lq1, lk1, lq2, lk2)


def _inproj_kernel(x_ref, pos_ref, mod_ref, g_ref, invf_ref, w_ref, cw_ref, wupc_ref,
                   qt_ref, k_ref, vt_ref, sg_ref, mc_ref, carry_ref, *, d, qk_w, conv_w):
    tm = x_ref.shape[1]
    x = x_ref[0]
    ms = jnp.mean(x * x, axis=-1, keepdims=True)
    mod = mod_ref[0]
    sh = mod[:, 0:d]
    sc = mod[:, d:2 * d]
    h = (x * lax.rsqrt(ms + EPS) * g_ref[...]) * (1.0 + sc) + sh
    hb = h.astype(BF16)

    def proj(a, b):
        return jnp.dot(hb, w_ref[:, a:b].astype(BF16), preferred_element_type=F32)

    lane = lax.broadcasted_iota(I32, (1, LANES), 1)
    first_half = (lane % HEAD_DIM) < (HEAD_DIM // 2)
    ang = pos_ref[0].astype(F32) * invf_ref[...]
    cos = jnp.cos(ang)
    sin = jnp.sin(ang)
    sin_signed = jnp.where(first_half, -sin, sin)

    def rope(t):
        partner = jnp.where(first_half, pltpu.roll(t, LANES - HEAD_DIM // 2, 1),
                            pltpu.roll(t, HEAD_DIM // 2, 1))
        return t * cos + partner * sin_signed

    zq = proj(0, qk_w)
    zk = proj(qk_w, 2 * qk_w)
    scale = HEAD_DIM ** -0.5 * math.log2(math.e)
    zv = proj(2 * qk_w, 3 * qk_w)
    for hd in range(qk_w // LANES):
        sl = slice(hd * LANES, (hd + 1) * LANES)
        rq = rope(zq[:, sl]) * scale
        qw = qt_ref.shape[4]
        for ck in range(tm // qw):
            qt_ref[0, hd, ck] = rq[ck * qw:(ck + 1) * qw, :].T.astype(BF16)
        k_ref[0, :, sl] = rope(zk[:, sl]).astype(BF16)
        for ck in range(tm // TK):
            vt_ref[0, hd, ck, 0:LANES, :] = zv[ck * TK:(ck + 1) * TK, sl].T.astype(BF16)
            vt_ref[0, hd, ck, LANES:VT_ROWS, :] = jnp.ones((VT_ROWS - LANES, TK), BF16)

    o = 3 * qk_w
    cb = proj(o, o + conv_w)
    u = proj(o + conv_w, o + 2 * conv_w) * proj(o + 2 * conv_w, o + 3 * conv_w)

    @pl.when(pl.program_id(1) == 0)
    def _():
        carry_ref[...] = jnp.zeros_like(carry_ref)

    prev = carry_ref[...]
    row = lax.broadcasted_iota(I32, (tm, 1), 0)
    last = SUBLANES - 1
    u1 = jnp.where(row == 0, prev[last:last + 1, :], pltpu.roll(u, 1, 0))
    u2 = jnp.where(row == 0, prev[last - 1:last, :],
                   jnp.where(row == 1, prev[last:last + 1, :], pltpu.roll(u, 2, 0)))
    carry_ref[...] = u[tm - SUBLANES:tm, :]
    cw = cw_ref[...]
    conv = cb * (cw[0:1, :] * u + cw[1:2, :] * u1 + cw[2:3, :] * u2)
    convp = jnp.dot(conv.astype(BF16), wupc_ref[...].astype(BF16), preferred_element_type=F32)

    o = o + 3 * conv_w
    sg_ref[0] = jax.nn.sigmoid(proj(o, o + d)).astype(BF16)
    mc_ref[0] = (jax.nn.sigmoid(proj(o + d, o + 2 * d)) * convp).astype(BF16)


def _inproj(x, pos3, mod3, norm_g, invf, w_in, conv_w, w_up_conv):
    b, s, d = x.shape
    qk_w = N_HEADS * 2 * HEAD_DIM
    cw = conv_w.shape[1]
    tm = TM_IN
    if tm >= TQ:
        qt_spec = pl.BlockSpec((1, N_HEADS, tm // TQ, LANES, TQ), lambda bi, si: (bi, 0, si, 0, 0))
    else:
        qt_spec = pl.BlockSpec((1, N_HEADS, 1, LANES, tm),
                               lambda bi, si: (bi, 0, si // (TQ // tm), 0, si % (TQ // tm)))
    tok = lambda w: pl.BlockSpec((1, tm, w), lambda bi, si: (bi, si, 0))
    full = lambda a: pl.BlockSpec(a.shape, lambda bi, si: (0,) * a.ndim)
    return pl.pallas_call(
        functools.partial(_inproj_kernel, d=d, qk_w=qk_w, conv_w=cw),
        grid=(b, s // tm),
        in_specs=[tok(d), tok(1),
                  pl.BlockSpec((1, 1, mod3.shape[2]), lambda bi, si: (bi, 0, 0)),
                  full(norm_g), full(invf), full(w_in), full(conv_w), full(w_up_conv)],
        out_specs=[qt_spec,
                   tok(qk_w),
                   pl.BlockSpec((1, N_HEADS, tm // TK, VT_ROWS, TK),
                                lambda bi, si: (bi, 0, si, 0, 0)),
                   tok(d), tok(d)],
        out_shape=[jax.ShapeDtypeStruct((b, N_HEADS, s // TQ, LANES, TQ), BF16),
                   jax.ShapeDtypeStruct((b, s, qk_w), BF16),
                   jax.ShapeDtypeStruct((b, N_HEADS, s // TK, VT_ROWS, TK), BF16),
                   jax.ShapeDtypeStruct((b, s, d), BF16),
                   jax.ShapeDtypeStruct((b, s, d), BF16)],
        scratch_shapes=[pltpu.VMEM((SUBLANES, cw), F32)],
        compiler_params=_cparams(2),
        name="inproj",
    )(x, pos3, mod3, norm_g, invf, w_in, conv_w, w_up_conv)


def _attn_kernel(lam_ref, qt_ref, k_ref, vt_ref, g_ref, o_ref, s_ref, p_ref, qq_ref, bias_ref, *,
                 out_scale):
    hps, n_q, tq = qt_ref.shape[1], qt_ref.shape[2], qt_ref.shape[4]
    feat = lax.broadcasted_iota(I32, (LANES, 1), 0)
    halves = [slice(h * tq, (h + 1) * tq) for h in range(2)]
    streams = [(h, half) for h in range(hps) for half in halves]

    k_sub = lax.broadcasted_iota(I32, (TK, 1), 0)
    q_lane = lax.broadcasted_iota(I32, (1, tq), 1)
    bias_ref[...] = jnp.where(k_sub <= q_lane, 0.0, MASK_VALUE)

    def load_queries(qi):
        for h in range(hps):
            qt = qt_ref[0, h, qi]
            zero = jnp.zeros_like(qt)
            qq_ref[h, :, halves[0]] = jnp.where(feat < HEAD_DIM, qt, zero)
            qq_ref[h, :, halves[1]] = jnp.where(feat >= HEAD_DIM, qt, zero)

    half = TK // 2

    def late(lanes):
        return slice(lanes.start + half, lanes.stop)

    def scores(j, diagonal=False):
        for h, lanes in streams:
            kt = k_ref[0, pl.ds(j * TK, TK), h * LANES:(h + 1) * LANES]
            if diagonal:
                s_ref[h, 0:half, lanes] = jnp.dot(kt[0:half], qq_ref[h, :, lanes],
                                                  preferred_element_type=F32)
                s_ref[h, half:TK, late(lanes)] = jnp.dot(kt[half:TK], qq_ref[h, :, late(lanes)],
                                                         preferred_element_type=F32)
            else:
                s_ref[h, :, lanes] = jnp.dot(kt, qq_ref[h, :, lanes],
                                             preferred_element_type=F32)

    def values(j, state):
        return tuple((m, alpha * acc + jnp.dot(vt_ref[0, h, j], p_ref[h, :, lanes],
                                               preferred_element_type=F32), alpha)
                     for (h, lanes), (m, acc, alpha) in zip(streams, state))

    def softmax(state, diagonal):
        out = []
        for (h, lanes), (m, acc, _) in zip(streams, state):
            if diagonal:
                top = lambda: s_ref[h, 0:half, lanes] + bias_ref[0:half, :]
                bot = lambda: s_ref[h, half:TK, late(lanes)] + bias_ref[half:TK, half:tq]
                cm = jnp.max(top(), axis=0, keepdims=True)
                cm = jnp.concatenate(
                    [cm[:, 0:half],
                     jnp.maximum(cm[:, half:tq], jnp.max(bot(), axis=0, keepdims=True))], axis=1)
                m_new = jnp.maximum(m, cm)
                p_ref[h, 0:half, lanes] = jnp.exp2(top() - m_new).astype(BF16)
                p_ref[h, half:TK, late(lanes)] = jnp.exp2(bot() - m_new[:, half:tq]).astype(BF16)
                p_ref[h, half:TK, lanes.start:lanes.start + half] = jnp.zeros((TK - half, half),
                                                                              BF16)
            else:
                load = lambda: s_ref[h, :, lanes]
                m_new = jnp.maximum(m, jnp.max(load(), axis=0, keepdims=True))
                p_ref[h, :, lanes] = jnp.exp2(load() - m_new).astype(BF16)
            out.append((m_new, acc, jnp.exp2(m - m_new)))
        return tuple(out)

    def finish(qi, last_tile, state):
        state = values(last_tile, state)
        row0 = qi * tq
        for h in range(hps):
            (_, acc1, _), (_, acc2, _) = state[2 * h], state[2 * h + 1]
            o1 = acc1[0:LANES] / acc1[LANES:LANES + 1]
            o2 = acc2[0:LANES] / acc2[LANES:LANES + 1]
            a = o1 - lam_ref[0:1, 0:1] * o2
            y = a * lax.rsqrt(jnp.mean(a * a, axis=0, keepdims=True) + EPS) * g_ref[...]
            o_ref[0, pl.ds(row0, tq), h * LANES:(h + 1) * LANES] = (
                (y * out_scale).T.astype(BF16))

    def next_diagonal(qi):
        if qi + 1 < n_q:
            load_queries(qi + 1)
            scores(qi + 1, diagonal=True)

    one = (jnp.full((1, tq), MASK_VALUE, F32), jnp.zeros((VT_ROWS, tq), F32),
           jnp.ones((1, tq), F32))
    init = tuple(one for _ in streams)

    load_queries(0)
    scores(0, diagonal=True)
    pending = softmax(init, True)
    next_diagonal(0)

    for qi in range(1, n_q):
        finish(qi - 1, max(qi - 2, 0), pending)
        state = softmax(init, True)
        scores(0)
        for t in range(1, qi + 1):
            state = values(qi if t == 1 else t - 2, state)
            state = softmax(state, False)
            if t < qi:
                scores(t)
            else:
                next_diagonal(qi)
        pending = state
    finish(n_q - 1, max(n_q - 2, 0), pending)


def _attn(lam, qt, k, vt, subln_g_col, out_scale):
    b, s, w = k.shape
    hps = ATTN_HEADS_PER_STEP
    return pl.pallas_call(
        functools.partial(_attn_kernel, out_scale=out_scale),
        grid=(b, N_HEADS // hps),
        in_specs=[pl.BlockSpec((SUBLANES, LANES), lambda bi, hi: (0, 0)),
                  pl.BlockSpec((1, hps, s // TQ, LANES, TQ), lambda bi, hi: (bi, hi, 0, 0, 0)),
                  pl.BlockSpec((1, s, hps * LANES), lambda bi, hi: (bi, 0, hi)),
                  pl.BlockSpec((1, hps, s // TK, VT_ROWS, TK), lambda bi, hi: (bi, hi, 0, 0, 0)),
                  pl.BlockSpec((LANES, 1), lambda bi, hi: (0, 0))],
        out_specs=pl.BlockSpec((1, s, hps * LANES), lambda bi, hi: (bi, 0, hi)),
        out_shape=jax.ShapeDtypeStruct((b, s, w), BF16),
        scratch_shapes=[pltpu.VMEM((hps, TK, 2 * TQ), F32), pltpu.VMEM((hps, TK, 2 * TQ), BF16),
                        pltpu.VMEM((hps, LANES, 2 * TQ), BF16), pltpu.VMEM((TK, TQ), F32)],
        compiler_params=_cparams(2),
        name="attn",
    )(lam, qt, k, vt, subln_g_col)


def _mixout_kernel(att_ref, sg_ref, mc_ref, x_ref, mod_ref, wua_ref, wo_ref, g_ref, wrt_ref,
                   br_ref, x1_ref, pos_ref, cnt_ref, hs_ref,
                   h2c_ref, tri_ref, carry_ref, posv_ref, cntv_ref, poss_ref, cnts_ref,
                   sem_s, sem_r, *, d, cap, tail):
    tm = x_ref.shape[0]
    i = pl.program_id(0)
    last = pl.num_programs(0) - 1

    @pl.when(i == 0)
    def _():
        carry_ref[...] = jnp.zeros_like(carry_ref)
        earlier = (lax.broadcasted_iota(I32, (tm, 1), 0)
                   <= lax.broadcasted_iota(I32, (1, tm), 1))
        tri_ref[...] = jnp.where(earlier, 1.0, 0.0).astype(BF16)

    mod = mod_ref[0]
    g_m = mod[:, 2 * d:3 * d]
    sh_f = mod[:, 3 * d:4 * d]
    sc_f = mod[:, 4 * d:5 * d]

    attp = jnp.dot(att_ref[...], wua_ref[...].astype(BF16), preferred_element_type=F32)
    merged = sg_ref[...].astype(F32) * attp + mc_ref[...].astype(F32)
    mo = jnp.dot(merged.astype(BF16), wo_ref[...].astype(BF16), preferred_element_type=F32)
    x1 = x_ref[...] + g_m * mo
    x1_ref[...] = x1
    ms = jnp.mean(x1 * x1, axis=-1, keepdims=True)
    h2 = (x1 * lax.rsqrt(ms + EPS) * g_ref[...]) * (1.0 + sc_f) + sh_f

    def split(a):
        hi = a.astype(BF16)
        return hi, (a - hi.astype(F32)).astype(BF16)

    def dot_t(a, b):
        return lax.dot_general(a, b, (((1,), (1,)), ((), ())), preferred_element_type=F32)

    w_hi, w_lo = split(wrt_ref[...])
    h_hi, h_lo = split(h2)
    by_hi = dot_t(jnp.concatenate([w_hi, w_lo], axis=0), h_hi)
    lt = (by_hi[0:ROUTER_ROWS] + (dot_t(w_hi, h_lo) + by_hi[ROUTER_ROWS:2 * ROUTER_ROWS])
          + br_ref[...])
    n_sel = EXPERTS_PER_GROUP
    ridx = lax.broadcasted_iota(I32, (n_sel, 1), 0)
    n_e = N_GROUPS * EXPERTS_PER_GROUP
    is_grp = ridx < N_GROUPS
    gl = jnp.where(is_grp, lt[n_e:n_e + n_sel, :], MASK_VALUE)
    gmax = jnp.max(gl, axis=0, keepdims=True)
    gsum = jnp.sum(jnp.where(is_grp, jnp.exp(gl - gmax), 0.0), axis=0, keepdims=True)
    g_w = 1.0 / gsum
    gidx = jnp.min(jnp.where(gl == gmax, ridx, n_sel), axis=0, keepdims=True)
    e_in = jnp.zeros((n_sel, tm), F32)
    for g in range(N_GROUPS):
        e_in = jnp.where(gidx == g, lt[g * n_sel:(g + 1) * n_sel, :], e_in)
    t1 = jnp.max(e_in, axis=0, keepdims=True)
    i1 = jnp.min(jnp.where(e_in == t1, ridx, n_sel), axis=0, keepdims=True)
    e_in2 = jnp.where(ridx == i1, MASK_VALUE, e_in)
    t2 = jnp.max(e_in2, axis=0, keepdims=True)
    i2 = jnp.min(jnp.where(e_in2 == t2, ridx, n_sel), axis=0, keepdims=True)
    e2 = jnp.exp(t2 - t1)
    den = 1.0 + e2
    w1 = (1.0 / den) * g_w
    w2 = (e2 / den) * g_w
    cw_t = jnp.where(ridx == i1, w1, jnp.where(ridx == i2, w2, 0.0))
    cw_col = jnp.concatenate([cw_t, jnp.zeros((LANES - n_sel, tm), F32)], axis=0).T

    onehot = jnp.where(ridx == gidx, 1.0, 0.0)
    oh16 = jnp.concatenate([onehot, jnp.zeros_like(onehot)], axis=0).astype(BF16)
    incl = jnp.dot(oh16, tri_ref[...], preferred_element_type=F32)[0:n_sel, :]
    run = carry_ref[...][:, 0:1]
    base = ridx.astype(F32) * float(cap)
    pos = jnp.sum(onehot * (base + run + incl - onehot), axis=0, keepdims=True).astype(I32)
    new_run = run + incl[:, tm - 1:tm]
    carry_ref[...] = jnp.broadcast_to(new_run, carry_ref.shape)
    cnt_i = jnp.broadcast_to(new_run, cnt_ref.shape).astype(I32)
    cnt_ref[...] = cnt_i
    pos_ref[0] = pos
    posv_ref[...] = pos
    cp = pltpu.make_async_copy(posv_ref.at[0], poss_ref, sem_s)
    cp.start()

    def row_copy(r, p):
        return pltpu.make_async_copy(h2c_ref.at[pl.ds(r, 1), :], hs_ref.at[pl.ds(p, 1), :], sem_r)

    def wait_tile():
        pltpu.make_async_copy(h2c_ref, hs_ref.at[pl.ds(0, tm), :], sem_r).wait()

    @pl.when(i > 0)
    def _():
        wait_tile()

    h2c_ref[:, 0:d] = h2
    h2c_ref[:, d:d + LANES] = cw_col
    cp.wait()

    for r in range(tm):
        row_copy(r, poss_ref[r]).start(priority=r % 2)

    @pl.when(i == last)
    def _():
        wait_tile()
        cntv_ref[...] = cnt_i
        cc = pltpu.make_async_copy(cntv_ref, cnts_ref, sem_s)
        cc.start()
        cc.wait()
        h2c_ref[0:SUBLANES, :] = jnp.zeros((SUBLANES, h2c_ref.shape[1]), F32)

        def zissue(r, c):
            for g in range(N_GROUPS):
                row_copy(0, g * cap + cnts_ref[g, 0] + r).start()
            return c

        lax.fori_loop(0, tail, zissue, 0, unroll=2)
        for g in range(N_GROUPS):
            pltpu.make_async_copy(h2c_ref.at[pl.ds(0, tail), :], hs_ref.at[pl.ds(0, tail), :],
                                  sem_r).wait()


def _mixout(att, sg, mc, x, mod3, w_up_att, w_out, norm_g, wrt, br, tiles_per_batch, cap):
    t, d = x.shape
    tm = TM_MIX
    dc = d + LANES
    tok = lambda w: pl.BlockSpec((tm, w), lambda i: (i, 0))
    full = lambda a: pl.BlockSpec(a.shape, lambda i: (0,) * a.ndim)
    return pl.pallas_call(
        functools.partial(_mixout_kernel, d=d, cap=cap, tail=cap - t),
        grid=(t // tm,),
        in_specs=[tok(att.shape[1]), tok(d), tok(d), tok(d),
                  pl.BlockSpec((1, 1, mod3.shape[2]), lambda i: (i // tiles_per_batch, 0, 0)),
                  full(w_up_att), full(w_out), full(norm_g), full(wrt), full(br)],
        out_specs=[tok(d),
                   pl.BlockSpec((1, 1, tm), lambda i: (i, 0, 0)),
                   pl.BlockSpec((SUBLANES, LANES), lambda i: (0, 0)),
                   pl.BlockSpec(memory_space=pl.ANY)],
        out_shape=[jax.ShapeDtypeStruct((t, d), F32),
                   jax.ShapeDtypeStruct((t // tm, 1, tm), I32),
                   jax.ShapeDtypeStruct((SUBLANES, LANES), I32),
                   jax.ShapeDtypeStruct((N_GROUPS * cap, dc), F32)],
        scratch_shapes=[pltpu.VMEM((tm, dc), F32), pltpu.VMEM((tm, tm), BF16),
                        pltpu.VMEM((SUBLANES, LANES), F32),
                        pltpu.VMEM((1, tm), I32), pltpu.VMEM((SUBLANES, LANES), I32),
                        pltpu.SMEM((tm,), I32), pltpu.SMEM((SUBLANES, LANES), I32),
                        pltpu.SemaphoreType.DMA, pltpu.SemaphoreType.DMA],
        compiler_params=_cparams(1),
        name="mixout",
    )(att, sg, mc, x, mod3, w_up_att, w_out, norm_g, wrt, br)


def _ffn_tile(step, cnt_ref, tm):
    log_tm = tm.bit_length() - 1
    ends = []
    tot = jnp.int32(0)
    for g in range(N_GROUPS):
        tot = tot + lax.shift_right_logical(cnt_ref[g] + (tm - 1), log_tm)
        ends.append(tot)
    sc = jnp.maximum(jnp.minimum(step, tot - 1), 0)
    g = sum((sc >= e).astype(I32) for e in ends[:-1])
    start = jnp.where(g == 0, 0, jnp.where(g == 1, ends[0], jnp.where(g == 2, ends[1], ends[2])))
    return g, sc - start, step < tot


def _ffn_kernel(cnt_ref, hs_ref, wg_ref, wu_ref, wd_ref, y_ref, *, d):
    tm = hs_ref.shape[0]
    _, _, real = _ffn_tile(pl.program_id(0), cnt_ref, tm)

    @pl.when(real)
    def _():
        h = hs_ref[:, 0:d].astype(BF16)
        cw = hs_ref[:, d:d + LANES]
        acc = jnp.zeros((tm, d), F32)
        for e in range(EXPERTS_PER_GROUP):
            hg = jnp.dot(h, wg_ref[e].astype(BF16), preferred_element_type=F32)
            hu = jnp.dot(h, wu_ref[e].astype(BF16), preferred_element_type=F32)
            a = hg * jax.nn.sigmoid(hg) * hu * cw[:, e:e + 1]
            acc = acc + jnp.dot(a.astype(BF16), wd_ref[e].astype(BF16),
                                preferred_element_type=F32)
        y_ref[...] = acc


def _ffn(cnt, hs, w_gate, w_up, w_down, cap):
    rows, dc = hs.shape
    d = dc - LANES
    tm = TM_FFN
    blocks_per_group = cap // tm
    n_steps = (blocks_per_group - 1) + N_GROUPS

    def row_map(s, c):
        g, blk, _ = _ffn_tile(s, c, tm)
        return (g * blocks_per_group + blk, 0)

    def w_map(s, c):
        g, _, _ = _ffn_tile(s, c, tm)
        return (g, 0, 0)

    ff = w_gate.shape[2]
    once = pl.Buffered(1)
    gs = pltpu.PrefetchScalarGridSpec(
        num_scalar_prefetch=1,
        grid=(n_steps,),
        in_specs=[pl.BlockSpec((tm, dc), row_map),
                  pl.BlockSpec((EXPERTS_PER_GROUP, d, ff), w_map),
                  pl.BlockSpec((EXPERTS_PER_GROUP, d, ff), w_map),
                  pl.BlockSpec((EXPERTS_PER_GROUP, ff, d), w_map, pipeline_mode=once)],
        out_specs=pl.BlockSpec((tm, d), row_map),
    )
    return pl.pallas_call(
        functools.partial(_ffn_kernel, d=d),
        grid_spec=gs,
        out_shape=jax.ShapeDtypeStruct((rows, d), F32),
        compiler_params=_cparams(1),
        name="ffn",
    )(cnt, hs, w_gate, w_up, w_down)


def _final_kernel(pos_ref, ys_ref, x1_ref, mod_ref, g_ref, o_ref, ybuf_ref, sem, *, d, normalize):
    tm = x1_ref.shape[0]
    i = pl.program_id(0)
    n = pl.num_programs(0)
    n_buf = ybuf_ref.shape[0]

    def gather_rows(tile, rows):
        slot = lax.rem(tile, n_buf)
        base = jnp.minimum(tile, n - 1) * tm
        for r in rows:
            pltpu.make_async_copy(ys_ref.at[pl.ds(pos_ref[base + r], 1), :],
                                  ybuf_ref.at[slot, pl.ds(r, 1), :],
                                  sem.at[slot]).start(priority=r % 2)

    def wait_tile(tile):
        slot = lax.rem(tile, n_buf)
        pltpu.make_async_copy(ys_ref.at[pl.ds(0, tm), :], ybuf_ref.at[slot], sem.at[slot]).wait()

    @pl.when(i == 0)
    def _():
        gather_rows(i, range(tm))
        gather_rows(i + 1, range(tm))

    wait_tile(i)
    slot = lax.rem(i, n_buf)
    g_f = mod_ref[0][:, 5 * d:6 * d]
    for c in range(tm // FIN_ROWS):
        rows = slice(c * FIN_ROWS, (c + 1) * FIN_ROWS)
        x2 = x1_ref[rows, :] + g_f * ybuf_ref[slot, rows, :]
        if normalize:
            x2 = x2 * lax.rsqrt(jnp.mean(x2 * x2, axis=-1, keepdims=True) + EPS) * g_ref[...]
        o_ref[rows, :] = x2
        gather_rows(i + 2, range(c * FIN_ROWS, (c + 1) * FIN_ROWS))

    @pl.when(i == n - 1)
    def _():
        wait_tile(i + 1)
        wait_tile(i + 2)


def _final(pos, ys, x1, mod3, final_g, tiles_per_batch, normalize):
    t, d = x1.shape
    tm = TM_FIN
    gs = pltpu.PrefetchScalarGridSpec(
        num_scalar_prefetch=1,
        grid=(t // tm,),
        in_specs=[pl.BlockSpec(memory_space=pl.ANY),
                  pl.BlockSpec((tm, d), lambda i, p: (i, 0)),
                  pl.BlockSpec((1, 1, mod3.shape[2]), lambda i, p: (i // tiles_per_batch, 0, 0)),
                  pl.BlockSpec((1, d), lambda i, p: (0, 0))],
        out_specs=pl.BlockSpec((tm, d), lambda i, p: (i, 0)),
        scratch_shapes=[pltpu.VMEM((3, tm, d), F32), pltpu.SemaphoreType.DMA((3,))],
    )
    return pl.pallas_call(
        functools.partial(_final_kernel, d=d, normalize=normalize),
        grid_spec=gs,
        out_shape=jax.ShapeDtypeStruct((t, d), F32),
        compiler_params=_cparams(1),
        name="final",
    )(pos, ys, x1, mod3, final_g)


def kernel(x, c, positions, w_ada, b_ada, norm_mix_g, w_in, lambda_q1, lambda_k1, lambda_q2,
           lambda_k2, subln_g, conv_w, w_up_att, w_up_conv, w_out, norm_ffn_g, w_group_router,
           b_group_router, w_expert_router, b_expert_router, w_gate, w_up, w_down, final_norm_g):
    b, s, d = x.shape
    depth = w_ada.shape[0]
    t = b * s
    assert s % TQ == 0 and s % TM_IN == 0 and TQ == TK and TM_IN % TK == 0
    assert TQ % TM_IN == 0 or TM_IN % TQ == 0
    assert N_HEADS % ATTN_HEADS_PER_STEP == 0
    assert s % TM_MIX == 0 and s % TM_FIN == 0 and t % TM_FFN == 0
    cap = t + TM_FFN
    n_e = N_GROUPS * EXPERTS_PER_GROUP

    assert b <= SUBLANES and EXPERTS_PER_GROUP == SUBLANES and N_GROUPS <= SUBLANES
    c8 = jnp.concatenate([c, jnp.zeros((SUBLANES - b, d), F32)], axis=0)
    pos3 = positions.reshape(b, s, 1)
    inv_freq = ROPE_THETA ** (-jnp.arange(0, HEAD_DIM, 2, dtype=F32) / HEAD_DIM)
    invf = jnp.tile(inv_freq, LANES // (HEAD_DIM // 2)).reshape(1, LANES)

    xf = x
    for l in range(depth):
        lambda_init = 0.8 - 0.6 * math.exp(-0.3 * l)
        row = lambda a: a[l].reshape(1, -1)
        mod8, lam = _ada(c8, w_ada[l], row(b_ada), row(lambda_q1), row(lambda_k1),
                         row(lambda_q2), row(lambda_k2), lambda_init)
        mod3 = mod8[0:b].reshape(b, 1, 6 * d)

        qt, k, vt, sg, mc = _inproj(xf.reshape(b, s, d), pos3, mod3, row(norm_mix_g), invf,
                                    w_in[l], conv_w[l], w_up_conv[l])
        att = _attn(lam, qt, k, vt, subln_g[l].reshape(-1, 1), 1.0 - lambda_init)

        pad = ROUTER_ROWS - n_e - N_GROUPS
        wrt = jnp.concatenate([w_expert_router[l].T, w_group_router[l].T,
                               jnp.zeros((pad, d), F32)], axis=0)
        br = jnp.concatenate([b_expert_router[l], b_group_router[l],
                              jnp.zeros((pad,), F32)]).reshape(ROUTER_ROWS, 1)
        x1, pos, cnt, hs = _mixout(att.reshape(t, -1), sg.reshape(t, d), mc.reshape(t, d),
                                   xf.reshape(t, d), mod3, w_up_att[l], w_out[l],
                                   row(norm_ffn_g), wrt, br,
                                   s // TM_MIX, cap)
        ys = _ffn(cnt[0:N_GROUPS, 0], hs, w_gate[l], w_up[l], w_down[l], cap)
        xf = _final(pos.reshape(t), ys, x1, mod3, final_norm_g.reshape(1, d), s // TM_FIN,
                    l == depth - 1)
    return xf.reshape(b, s, d)
```

```python
import functools
import math

import jax
import jax.numpy as jnp
from jax import lax
from jax.experimental import pallas as pl
from jax.experimental.pallas import tpu as pltpu

F32 = jnp.float32
BF16 = jnp.bfloat16
I32 = jnp.int32

EPS = 1e-6
MASK_VALUE = -1e30
ROPE_THETA = 10000.0

N_HEADS = 4
HEAD_DIM = 64
LANES = 128
SUBLANES = 8
N_GROUPS = 4
EXPERTS_PER_GROUP = 8
ROUTER_ROWS = 48

VMEM_LIMIT = 56 * 1024 * 1024

TM_IN = 1024
TQ = 512
TK = 512
ATTN_HEADS_PER_STEP = 1
VT_ROWS = LANES + 16
TM_MIX = 1024
TM_FFN = 512
TM_FIN = 512
FIN_ROWS = 64
TN_ADA = 2048


def _cparams(n_axes):
    return pltpu.CompilerParams(
        dimension_semantics=("arbitrary",) * n_axes, vmem_limit_bytes=VMEM_LIMIT)


def _ada_kernel(c_ref, w_ref, b_ref, lq1_ref, lk1_ref, lq2_ref, lk2_ref, mod_ref, lam_ref, *,
                lambda_init):
    c = c_ref[...]
    a = c * jax.nn.sigmoid(c)
    mod_ref[...] = jnp.dot(a.astype(BF16), w_ref[...].astype(BF16),
                           preferred_element_type=F32) + b_ref[...]
    s1 = jnp.sum(lq1_ref[...] * lk1_ref[...], axis=-1, keepdims=True)
    s2 = jnp.sum(lq2_ref[...] * lk2_ref[...], axis=-1, keepdims=True)
    lam = jnp.exp(s1) - jnp.exp(s2) + lambda_init
    lam_ref[...] = jnp.broadcast_to(lam, lam_ref.shape)


def _ada(c8, w_ada, b_ada, lq1, lk1, lq2, lk2, lambda_init):
    d, n = w_ada.shape
    small = pl.BlockSpec((1, HEAD_DIM), lambda j: (0, 0))
    return pl.pallas_call(
        functools.partial(_ada_kernel, lambda_init=lambda_init),
        grid=(n // TN_ADA,),
        in_specs=[pl.BlockSpec((SUBLANES, d), lambda j: (0, 0)),
                  pl.BlockSpec((d, TN_ADA), lambda j: (0, j)),
                  pl.BlockSpec((1, TN_ADA), lambda j: (0, j)),
                  small, small, small, small],
        out_specs=[pl.BlockSpec((SUBLANES, TN_ADA), lambda j: (0, j)),
                   pl.BlockSpec((SUBLANES, LANES), lambda j: (0, 0))],
        out_shape=[jax.ShapeDtypeStruct((SUBLANES, n), F32),
                   jax.ShapeDtypeStruct((SUBLANES, LANES), F32)],
        compiler_params=_cparams(1),
        name="ada",
    )(c8, w_ada, b_ada, lq1, lk1, lq2, lk2)


def _inproj_kernel(x_ref, pos_ref, mod_ref, g_ref, invf_ref, w_ref, cw_ref, wupc_ref,
                   qt_ref, k_ref, vt_ref, sg_ref, mc_ref, carry_ref, *, d, qk_w, conv_w):
    tm = x_ref.shape[1]
    x = x_ref[0]
    ms = jnp.mean(x * x, axis=-1, keepdims=True)
    mod = mod_ref[0]
    sh = mod[:, 0:d]
    sc = mod[:, d:2 * d]
    h = (x * lax.rsqrt(ms + EPS) * g_ref[...]) * (1.0 + sc) + sh
    hb = h.astype(BF16)

    def proj(a, b):
        return jnp.dot(hb, w_ref[:, a:b].astype(BF16), preferred_element_type=F32)

    lane = lax.broadcasted_iota(I32, (1, LANES), 1)
    first_half = (lane % HEAD_DIM) < (HEAD_DIM // 2)
    ang = pos_ref[0].astype(F32) * invf_ref[...]
    cos = jnp.cos(ang)
    sin = jnp.sin(ang)
    sin_signed = jnp.where(first_half, -sin, sin)

    def rope(t):
        partner = jnp.where(first_half, pltpu.roll(t, LANES - HEAD_DIM // 2, 1),
                            pltpu.roll(t, HEAD_DIM // 2, 1))
        return t * cos + partner * sin_signed

    zq = proj(0, qk_w)
    zk = proj(qk_w, 2 * qk_w)
    scale = HEAD_DIM ** -0.5 * math.log2(math.e)
    zv = proj(2 * qk_w, 3 * qk_w)
    for hd in range(qk_w // LANES):
        sl = slice(hd * LANES, (hd + 1) * LANES)
        rq = rope(zq[:, sl]) * scale
        qw = qt_ref.shape[4]
        for ck in range(tm // qw):
            qt_ref[0, hd, ck] = rq[ck * qw:(ck + 1) * qw, :].T.astype(BF16)
        k_ref[0, :, sl] = rope(zk[:, sl]).astype(BF16)
        for ck in range(tm // TK):
            vt_ref[0, hd, ck, 0:LANES, :] = zv[ck * TK:(ck + 1) * TK, sl].T.astype(BF16)
            vt_ref[0, hd, ck, LANES:VT_ROWS, :] = jnp.ones((VT_ROWS - LANES, TK), BF16)

    o = 3 * qk_w
    cb = proj(o, o + conv_w)
    u = proj(o + conv_w, o + 2 * conv_w) * proj(o + 2 * conv_w, o + 3 * conv_w)

    @pl.when(pl.program_id(1) == 0)
    def _():
        carry_ref[...] = jnp.zeros_like(carry_ref)

    prev = carry_ref[...]
    row = lax.broadcasted_iota(I32, (tm, 1), 0)
    last = SUBLANES - 1
    u1 = jnp.where(row == 0, prev[last:last + 1, :], pltpu.roll(u, 1, 0))
    u2 = jnp.where(row == 0, prev[last - 1:last, :],
                   jnp.where(row == 1, prev[last:last + 1, :], pltpu.roll(u, 2, 0)))
    carry_ref[...] = u[tm - SUBLANES:tm, :]
    cw = cw_ref[...]
    conv = cb * (cw[0:1, :] * u + cw[1:2, :] * u1 + cw[2:3, :] * u2)
    convp = jnp.dot(conv.astype(BF16), wupc_ref[...].astype(BF16), preferred_element_type=F32)

    o = o + 3 * conv_w
    sg_ref[0] = jax.nn.sigmoid(proj(o, o + d)).astype(BF16)
    mc_ref[0] = (jax.nn.sigmoid(proj(o + d, o + 2 * d)) * convp).astype(BF16)


def _inproj(x, pos3, mod3, norm_g, invf, w_in, conv_w, w_up_conv):
    b, s, d = x.shape
    qk_w = N_HEADS * 2 * HEAD_DIM
    cw = conv_w.shape[1]
    tm = TM_IN
    if tm >= TQ:
        qt_spec = pl.BlockSpec((1, N_HEADS, tm // TQ, LANES, TQ), lambda bi, si: (bi, 0, si, 0, 0))
    else:
        qt_spec = pl.BlockSpec((1, N_HEADS, 1, LANES, tm),
                               lambda bi, si: (bi, 0, si // (TQ // tm), 0, si % (TQ // tm)))
    tok = lambda w: pl.BlockSpec((1, tm, w), lambda bi, si: (bi, si, 0))
    full = lambda a: pl.BlockSpec(a.shape, lambda bi, si: (0,) * a.ndim)
    return pl.pallas_call(
        functools.partial(_inproj_kernel, d=d, qk_w=qk_w, conv_w=cw),
        grid=(b, s // tm),
        in_specs=[tok(d), tok(1),
                  pl.BlockSpec((1, 1, mod3.shape[2]), lambda bi, si: (bi, 0, 0)),
                  full(norm_g), full(invf), full(w_in), full(conv_w), full(w_up_conv)],
        out_specs=[qt_spec,
                   tok(qk_w),
                   pl.BlockSpec((1, N_HEADS, tm // TK, VT_ROWS, TK),
                                lambda bi, si: (bi, 0, si, 0, 0)),
                   tok(d), tok(d)],
        out_shape=[jax.ShapeDtypeStruct((b, N_HEADS, s // TQ, LANES, TQ), BF16),
                   jax.ShapeDtypeStruct((b, s, qk_w), BF16),
                   jax.ShapeDtypeStruct((b, N_HEADS, s // TK, VT_ROWS, TK), BF16),
                   jax.ShapeDtypeStruct((b, s, d), BF16),
                   jax.ShapeDtypeStruct((b, s, d), BF16)],
        scratch_shapes=[pltpu.VMEM((SUBLANES, cw), F32)],
        compiler_params=_cparams(2),
        name="inproj",
    )(x, pos3, mod3, norm_g, invf, w_in, conv_w, w_up_conv)


def _attn_kernel(lam_ref, qt_ref, k_ref, vt_ref, g_ref, o_ref, s_ref, p_ref, qq_ref, bias_ref, *,
                 out_scale):
    hps, n_q, tq = qt_ref.shape[1], qt_ref.shape[2], qt_ref.shape[4]
    feat = lax.broadcasted_iota(I32, (LANES, 1), 0)
    halves = [slice(h * tq, (h + 1) * tq) for h in range(2)]
    streams = [(h, half) for h in range(hps) for half in halves]

    k_sub = lax.broadcasted_iota(I32, (TK, 1), 0)
    q_lane = lax.broadcasted_iota(I32, (1, tq), 1)
    bias_ref[...] = jnp.where(k_sub <= q_lane, 0.0, MASK_VALUE)

    def load_queries(qi):
        for h in range(hps):
            qt = qt_ref[0, h, qi]
            zero = jnp.zeros_like(qt)
            qq_ref[h, :, halves[0]] = jnp.where(feat < HEAD_DIM, qt, zero)
            qq_ref[h, :, halves[1]] = jnp.where(feat >= HEAD_DIM, qt, zero)

    half = TK // 2

    def late(lanes):
        return slice(lanes.start + half, lanes.stop)

    def scores(j, diagonal=False):
        for h, lanes in streams:
            kt = k_ref[0, pl.ds(j * TK, TK), h * LANES:(h + 1) * LANES]
            if diagonal:
                s_ref[h, 0:half, lanes] = jnp.dot(kt[0:half], qq_ref[h, :, lanes],
                                                  preferred_element_type=F32)
                s_ref[h, half:TK, late(lanes)] = jnp.dot(kt[half:TK], qq_ref[h, :, late(lanes)],
                                                         preferred_element_type=F32)
            else:
                s_ref[h, :, lanes] = jnp.dot(kt, qq_ref[h, :, lanes],
                                             preferred_element_type=F32)

    def values(j, state, diagonal=False):
        out = []
        for (h, lanes), (m, acc, alpha) in zip(streams, state):
            if diagonal:
                early = slice(lanes.start, lanes.start + half)
                pv = jnp.concatenate(
                    [jnp.dot(vt_ref[0, h, j, :, 0:half], p_ref[h, 0:half, early],
                             preferred_element_type=F32),
                     jnp.dot(vt_ref[0, h, j], p_ref[h, :, late(lanes)],
                             preferred_element_type=F32)], axis=1)
            else:
                pv = jnp.dot(vt_ref[0, h, j], p_ref[h, :, lanes], preferred_element_type=F32)
            out.append((m, alpha * acc + pv, alpha))
        return tuple(out)

    def softmax(state, diagonal):
        out = []
        for (h, lanes), (m, acc, _) in zip(streams, state):
            if diagonal:
                top = lambda: s_ref[h, 0:half, lanes] + bias_ref[0:half, :]
                bot = lambda: s_ref[h, half:TK, late(lanes)] + bias_ref[half:TK, half:tq]
                cm = jnp.max(top(), axis=0, keepdims=True)
                cm = jnp.concatenate(
                    [cm[:, 0:half],
                     jnp.maximum(cm[:, half:tq], jnp.max(bot(), axis=0, keepdims=True))], axis=1)
                m_new = jnp.maximum(m, cm)
                p_ref[h, 0:half, lanes] = jnp.exp2(top() - m_new).astype(BF16)
                p_ref[h, half:TK, late(lanes)] = jnp.exp2(bot() - m_new[:, half:tq]).astype(BF16)
            else:
                load = lambda: s_ref[h, :, lanes]
                m_new = jnp.maximum(m, jnp.max(load(), axis=0, keepdims=True))
                p_ref[h, :, lanes] = jnp.exp2(load() - m_new).astype(BF16)
            out.append((m_new, acc, jnp.exp2(m - m_new)))
        return tuple(out)

    def finish(qi, last_tile, state):
        state = values(last_tile, state, diagonal=(qi == 0))
        row0 = qi * tq
        for h in range(hps):
            (_, acc1, _), (_, acc2, _) = state[2 * h], state[2 * h + 1]
            o1 = acc1[0:LANES] / acc1[LANES:LANES + 1]
            o2 = acc2[0:LANES] / acc2[LANES:LANES + 1]
            a = o1 - lam_ref[0:1, 0:1] * o2
            y = a * lax.rsqrt(jnp.mean(a * a, axis=0, keepdims=True) + EPS) * g_ref[...]
            o_ref[0, pl.ds(row0, tq), h * LANES:(h + 1) * LANES] = (
                (y * out_scale).T.astype(BF16))

    def next_diagonal(qi):
        if qi + 1 < n_q:
            load_queries(qi + 1)
            scores(qi + 1, diagonal=True)

    one = (jnp.full((1, tq), MASK_VALUE, F32), jnp.zeros((VT_ROWS, tq), F32),
           jnp.ones((1, tq), F32))
    init = tuple(one for _ in streams)

    load_queries(0)
    scores(0, diagonal=True)
    pending = softmax(init, True)
    next_diagonal(0)

    for qi in range(1, n_q):
        finish(qi - 1, max(qi - 2, 0), pending)
        state = softmax(init, True)
        scores(0)
        for t in range(1, qi + 1):
            state = values(qi if t == 1 else t - 2, state, diagonal=(t == 1))
            state = softmax(state, False)
            if t < qi:
                scores(t)
            else:
                next_diagonal(qi)
        pending = state
    finish(n_q - 1, max(n_q - 2, 0), pending)


def _attn(lam, qt, k, vt, subln_g_col, out_scale):
    b, s, w = k.shape
    hps = ATTN_HEADS_PER_STEP
    return pl.pallas_call(
        functools.partial(_attn_kernel, out_scale=out_scale),
        grid=(b, N_HEADS // hps),
        in_specs=[pl.BlockSpec((SUBLANES, LANES), lambda bi, hi: (0, 0)),
                  pl.BlockSpec((1, hps, s // TQ, LANES, TQ), lambda bi, hi: (bi, hi, 0, 0, 0)),
                  pl.BlockSpec((1, s, hps * LANES), lambda bi, hi: (bi, 0, hi)),
                  pl.BlockSpec((1, hps, s // TK, VT_ROWS, TK), lambda bi, hi: (bi, hi, 0, 0, 0)),
                  pl.BlockSpec((LANES, 1), lambda bi, hi: (0, 0))],
        out_specs=pl.BlockSpec((1, s, hps * LANES), lambda bi, hi: (bi, 0, hi)),
        out_shape=jax.ShapeDtypeStruct((b, s, w), BF16),
        scratch_shapes=[pltpu.VMEM((hps, TK, 2 * TQ), F32), pltpu.VMEM((hps, TK, 2 * TQ), BF16),
                        pltpu.VMEM((hps, LANES, 2 * TQ), BF16), pltpu.VMEM((TK, TQ), F32)],
        compiler_params=_cparams(2),
        name="attn",
    )(lam, qt, k, vt, subln_g_col)


def _mixout_kernel(att_ref, sg_ref, mc_ref, x_ref, mod_ref, wua_ref, wo_ref, g_ref, wrt_ref,
                   br_ref, x1_ref, pos_ref, cnt_ref, hs_ref,
                   h2c_ref, tri_ref, carry_ref, posv_ref, cntv_ref, poss_ref, cnts_ref,
                   sem_s, sem_r, *, d, cap, tail):
    tm = x_ref.shape[0]
    i = pl.program_id(0)
    last = pl.num_programs(0) - 1

    @pl.when(i == 0)
    def _():
        carry_ref[...] = jnp.zeros_like(carry_ref)
        earlier = (lax.broadcasted_iota(I32, (tm, 1), 0)
                   <= lax.broadcasted_iota(I32, (1, tm), 1))
        tri_ref[...] = jnp.where(earlier, 1.0, 0.0).astype(BF16)

    mod = mod_ref[0]
    g_m = mod[:, 2 * d:3 * d]
    sh_f = mod[:, 3 * d:4 * d]
    sc_f = mod[:, 4 * d:5 * d]

    attp = jnp.dot(att_ref[...], wua_ref[...].astype(BF16), preferred_element_type=F32)
    merged = sg_ref[...].astype(F32) * attp + mc_ref[...].astype(F32)
    mo = jnp.dot(merged.astype(BF16), wo_ref[...].astype(BF16), preferred_element_type=F32)
    x1 = x_ref[...] + g_m * mo
    x1_ref[...] = x1
    ms = jnp.mean(x1 * x1, axis=-1, keepdims=True)
    h2 = (x1 * lax.rsqrt(ms + EPS) * g_ref[...]) * (1.0 + sc_f) + sh_f

    def split(a):
        hi = a.astype(BF16)
        return hi, (a - hi.astype(F32)).astype(BF16)

    def dot_t(a, b):
        return lax.dot_general(a, b, (((1,), (1,)), ((), ())), preferred_element_type=F32)

    w_hi, w_lo = split(wrt_ref[...])
    h_hi, h_lo = split(h2)
    by_hi = dot_t(jnp.concatenate([w_hi, w_lo], axis=0), h_hi)
    lt = (by_hi[0:ROUTER_ROWS] + (dot_t(w_hi, h_lo) + by_hi[ROUTER_ROWS:2 * ROUTER_ROWS])
          + br_ref[...])
    n_sel = EXPERTS_PER_GROUP
    ridx = lax.broadcasted_iota(I32, (n_sel, 1), 0)
    n_e = N_GROUPS * EXPERTS_PER_GROUP
    is_grp = ridx < N_GROUPS
    gl = jnp.where(is_grp, lt[n_e:n_e + n_sel, :], MASK_VALUE)
    gmax = jnp.max(gl, axis=0, keepdims=True)
    gsum = jnp.sum(jnp.where(is_grp, jnp.exp(gl - gmax), 0.0), axis=0, keepdims=True)
    g_w = 1.0 / gsum
    gidx = jnp.min(jnp.where(gl == gmax, ridx, n_sel), axis=0, keepdims=True)
    e_in = jnp.zeros((n_sel, tm), F32)
    for g in range(N_GROUPS):
        e_in = jnp.where(gidx == g, lt[g * n_sel:(g + 1) * n_sel, :], e_in)
    t1 = jnp.max(e_in, axis=0, keepdims=True)
    i1 = jnp.min(jnp.where(e_in == t1, ridx, n_sel), axis=0, keepdims=True)
    e_in2 = jnp.where(ridx == i1, MASK_VALUE, e_in)
    t2 = jnp.max(e_in2, axis=0, keepdims=True)
    i2 = jnp.min(jnp.where(e_in2 == t2, ridx, n_sel), axis=0, keepdims=True)
    e2 = jnp.exp(t2 - t1)
    den = 1.0 + e2
    w1 = (1.0 / den) * g_w
    w2 = (e2 / den) * g_w
    cw_t = jnp.where(ridx == i1, w1, jnp.where(ridx == i2, w2, 0.0))
    cw_col = jnp.concatenate([cw_t, jnp.zeros((LANES - n_sel, tm), F32)], axis=0).T

    onehot = jnp.where(ridx == gidx, 1.0, 0.0)
    oh16 = jnp.concatenate([onehot, jnp.zeros_like(onehot)], axis=0).astype(BF16)
    incl = jnp.dot(oh16, tri_ref[...], preferred_element_type=F32)[0:n_sel, :]
    run = carry_ref[...][:, 0:1]
    base = ridx.astype(F32) * float(cap)
    pos = jnp.sum(onehot * (base + run + incl - onehot), axis=0, keepdims=True).astype(I32)
    new_run = run + incl[:, tm - 1:tm]
    carry_ref[...] = jnp.broadcast_to(new_run, carry_ref.shape)
    cnt_i = jnp.broadcast_to(new_run, cnt_ref.shape).astype(I32)
    cnt_ref[...] = cnt_i
    pos_ref[0] = pos
    posv_ref[...] = pos
    cp = pltpu.make_async_copy(posv_ref.at[0], poss_ref, sem_s)
    cp.start()

    def row_copy(r, p):
        return pltpu.make_async_copy(h2c_ref.at[pl.ds(r, 1), :], hs_ref.at[pl.ds(p, 1), :], sem_r)

    def wait_tile():
        pltpu.make_async_copy(h2c_ref, hs_ref.at[pl.ds(0, tm), :], sem_r).wait()

    @pl.when(i > 0)
    def _():
        wait_tile()

    h2c_ref[:, 0:d] = h2
    h2c_ref[:, d:d + LANES] = cw_col
    cp.wait()

    for r in range(tm):
        row_copy(r, poss_ref[r]).start(priority=r % 2)

    @pl.when(i == last)
    def _():
        wait_tile()
        cntv_ref[...] = cnt_i
        cc = pltpu.make_async_copy(cntv_ref, cnts_ref, sem_s)
        cc.start()
        cc.wait()
        h2c_ref[0:SUBLANES, :] = jnp.zeros((SUBLANES, h2c_ref.shape[1]), F32)

        def zissue(r, c):
            for g in range(N_GROUPS):
                row_copy(0, g * cap + cnts_ref[g, 0] + r).start()
            return c

        lax.fori_loop(0, tail, zissue, 0, unroll=2)
        for g in range(N_GROUPS):
            pltpu.make_async_copy(h2c_ref.at[pl.ds(0, tail), :], hs_ref.at[pl.ds(0, tail), :],
                                  sem_r).wait()


def _mixout(att, sg, mc, x, mod3, w_up_att, w_out, norm_g, wrt, br, tiles_per_batch, cap):
    t, d = x.shape
    tm = TM_MIX
    dc = d + LANES
    tok = lambda w: pl.BlockSpec((tm, w), lambda i: (i, 0))
    full = lambda a: pl.BlockSpec(a.shape, lambda i: (0,) * a.ndim)
    return pl.pallas_call(
        functools.partial(_mixout_kernel, d=d, cap=cap, tail=cap - t),
        grid=(t // tm,),
        in_specs=[tok(att.shape[1]), tok(d), tok(d), tok(d),
                  pl.BlockSpec((1, 1, mod3.shape[2]), lambda i: (i // tiles_per_batch, 0, 0)),
                  full(w_up_att), full(w_out), full(norm_g), full(wrt), full(br)],
        out_specs=[tok(d),
                   pl.BlockSpec((1, 1, tm), lambda i: (i, 0, 0)),
                   pl.BlockSpec((SUBLANES, LANES), lambda i: (0, 0)),
                   pl.BlockSpec(memory_space=pl.ANY)],
        out_shape=[jax.ShapeDtypeStruct((t, d), F32),
                   jax.ShapeDtypeStruct((t // tm, 1, tm), I32),
                   jax.ShapeDtypeStruct((SUBLANES, LANES), I32),
                   jax.ShapeDtypeStruct((N_GROUPS * cap, dc), F32)],
        scratch_shapes=[pltpu.VMEM((tm, dc), F32), pltpu.VMEM((tm, tm), BF16),
                        pltpu.VMEM((SUBLANES, LANES), F32),
                        pltpu.VMEM((1, tm), I32), pltpu.VMEM((SUBLANES, LANES), I32),
                        pltpu.SMEM((tm,), I32), pltpu.SMEM((SUBLANES, LANES), I32),
                        pltpu.SemaphoreType.DMA, pltpu.SemaphoreType.DMA],
        compiler_params=_cparams(1),
        name="mixout",
    )(att, sg, mc, x, mod3, w_up_att, w_out, norm_g, wrt, br)


def _ffn_tile(step, cnt_ref, tm):
    log_tm = tm.bit_length() - 1
    ends = []
    tot = jnp.int32(0)
    for g in range(N_GROUPS):
        tot = tot + lax.shift_right_logical(cnt_ref[g] + (tm - 1), log_tm)
        ends.append(tot)
    sc = jnp.maximum(jnp.minimum(step, tot - 1), 0)
    g = sum((sc >= e).astype(I32) for e in ends[:-1])
    start = jnp.where(g == 0, 0, jnp.where(g == 1, ends[0], jnp.where(g == 2, ends[1], ends[2])))
    return g, sc - start, step < tot


def _ffn_kernel(cnt_ref, hs_ref, wg_ref, wu_ref, wd_ref, y_ref, *, d):
    tm = hs_ref.shape[0]
    _, _, real = _ffn_tile(pl.program_id(0), cnt_ref, tm)

    @pl.when(real)
    def _():
        h = hs_ref[:, 0:d].astype(BF16)
        cw = hs_ref[:, d:d + LANES]
        acc = jnp.zeros((tm, d), F32)
        for e in range(EXPERTS_PER_GROUP):
            hg = jnp.dot(h, wg_ref[e].astype(BF16), preferred_element_type=F32)
            hu = jnp.dot(h, wu_ref[e].astype(BF16), preferred_element_type=F32)
            a = hg * jax.nn.sigmoid(hg) * hu * cw[:, e:e + 1]
            acc = acc + jnp.dot(a.astype(BF16), wd_ref[e].astype(BF16),
                                preferred_element_type=F32)
        y_ref[...] = acc


def _ffn(cnt, hs, w_gate, w_up, w_down, cap):
    rows, dc = hs.shape
    d = dc - LANES
    tm = TM_FFN
    blocks_per_group = cap // tm
    n_steps = (blocks_per_group - 1) + N_GROUPS

    def row_map(s, c):
        g, blk, _ = _ffn_tile(s, c, tm)
        return (g * blocks_per_group + blk, 0)

    def w_map(s, c):
        g, _, _ = _ffn_tile(s, c, tm)
        return (g, 0, 0)

    ff = w_gate.shape[2]
    once = pl.Buffered(1)
    gs = pltpu.PrefetchScalarGridSpec(
        num_scalar_prefetch=1,
        grid=(n_steps,),
        in_specs=[pl.BlockSpec((tm, dc), row_map),
                  pl.BlockSpec((EXPERTS_PER_GROUP, d, ff), w_map),
                  pl.BlockSpec((EXPERTS_PER_GROUP, d, ff), w_map),
                  pl.BlockSpec((EXPERTS_PER_GROUP, ff, d), w_map, pipeline_mode=once)],
        out_specs=pl.BlockSpec((tm, d), row_map),
    )
    return pl.pallas_call(
        functools.partial(_ffn_kernel, d=d),
        grid_spec=gs,
        out_shape=jax.ShapeDtypeStruct((rows, d), F32),
        compiler_params=_cparams(1),
        name="ffn",
    )(cnt, hs, w_gate, w_up, w_down)


def _final_kernel(pos_ref, ys_ref, x1_ref, mod_ref, g_ref, o_ref, ybuf_ref, sem, *, d, normalize):
    tm = x1_ref.shape[0]
    i = pl.program_id(0)
    n = pl.num_programs(0)
    n_buf = ybuf_ref.shape[0]

    def gather_rows(tile, rows):
        slot = lax.rem(tile, n_buf)
        base = jnp.minimum(tile, n - 1) * tm
        for r in rows:
            pltpu.make_async_copy(ys_ref.at[pl.ds(pos_ref[base + r], 1), :],
                                  ybuf_ref.at[slot, pl.ds(r, 1), :],
                                  sem.at[slot]).start(priority=r % 2)

    def wait_tile(tile):
        slot = lax.rem(tile, n_buf)
        pltpu.make_async_copy(ys_ref.at[pl.ds(0, tm), :], ybuf_ref.at[slot], sem.at[slot]).wait()

    @pl.when(i == 0)
    def _():
        gather_rows(i, range(tm))
        gather_rows(i + 1, range(tm))

    wait_tile(i)
    slot = lax.rem(i, n_buf)
    g_f = mod_ref[0][:, 5 * d:6 * d]
    for c in range(tm // FIN_ROWS):
        rows = slice(c * FIN_ROWS, (c + 1) * FIN_ROWS)
        x2 = x1_ref[rows, :] + g_f * ybuf_ref[slot, rows, :]
        if normalize:
            x2 = x2 * lax.rsqrt(jnp.mean(x2 * x2, axis=-1, keepdims=True) + EPS) * g_ref[...]
        o_ref[rows, :] = x2
        gather_rows(i + 2, range(c * FIN_ROWS, (c + 1) * FIN_ROWS))

    @pl.when(i == n - 1)
    def _():
        wait_tile(i + 1)
        wait_tile(i + 2)


def _final(pos, ys, x1, mod3, final_g, tiles_per_batch, normalize):
    t, d = x1.shape
    tm = TM_FIN
    gs = pltpu.PrefetchScalarGridSpec(
        num_scalar_prefetch=1,
        grid=(t // tm,),
        in_specs=[pl.BlockSpec(memory_space=pl.ANY),
                  pl.BlockSpec((tm, d), lambda i, p: (i, 0)),
                  pl.BlockSpec((1, 1, mod3.shape[2]), lambda i, p: (i // tiles_per_batch, 0, 0)),
                  pl.BlockSpec((1, d), lambda i, p: (0, 0))],
        out_specs=pl.BlockSpec((tm, d), lambda i, p: (i, 0)),
        scratch_shapes=[pltpu.VMEM((3, tm, d), F32), pltpu.SemaphoreType.DMA((3,))],
    )
    return pl.pallas_call(
        functools.partial(_final_kernel, d=d, normalize=normalize),
        grid_spec=gs,
        out_shape=jax.ShapeDtypeStruct((t, d), F32),
        compiler_params=_cparams(1),
        name="final",
    )(pos, ys, x1, mod3, final_g)


def kernel(x, c, positions, w_ada, b_ada, norm_mix_g, w_in, lambda_q1, lambda_k1, lambda_q2,
           lambda_k2, subln_g, conv_w, w_up_att, w_up_conv, w_out, norm_ffn_g, w_group_router,
           b_group_router, w_expert_router, b_expert_router, w_gate, w_up, w_down, final_norm_g):
    b, s, d = x.shape
    depth = w_ada.shape[0]
    t = b * s
    assert s % TQ == 0 and s % TM_IN == 0 and TQ == TK and TM_IN % TK == 0
    assert TQ % TM_IN == 0 or TM_IN % TQ == 0
    assert N_HEADS % ATTN_HEADS_PER_STEP == 0
    assert s % TM_MIX == 0 and s % TM_FIN == 0 and t % TM_FFN == 0
    cap = t + TM_FFN
    n_e = N_GROUPS * EXPERTS_PER_GROUP

    assert b <= SUBLANES and EXPERTS_PER_GROUP == SUBLANES and N_GROUPS <= SUBLANES
    c8 = jnp.concatenate([c, jnp.zeros((SUBLANES - b, d), F32)], axis=0)
    pos3 = positions.reshape(b, s, 1)
    inv_freq = ROPE_THETA ** (-jnp.arange(0, HEAD_DIM, 2, dtype=F32) / HEAD_DIM)
    invf = jnp.tile(inv_freq, LANES // (HEAD_DIM // 2)).reshape(1, LANES)

    xf = x
    for l in range(depth):
        lambda_init = 0.8 - 0.6 * math.exp(-0.3 * l)
        row = lambda a: a[l].reshape(1, -1)
        mod8, lam = _ada(c8, w_ada[l], row(b_ada), row(lambda_q1), row(lambda_k1),
                         row(lambda_q2), row(lambda_k2), lambda_init)
        mod3 = mod8[0:b].reshape(b, 1, 6 * d)

        qt, k, vt, sg, mc = _inproj(xf.reshape(b, s, d), pos3, mod3, row(norm_mix_g), invf,
                                    w_in[l], conv_w[l], w_up_conv[l])
        att = _attn(lam, qt, k, vt, subln_g[l].reshape(-1, 1), 1.0 - lambda_init)

        pad = ROUTER_ROWS - n_e - N_GROUPS
        wrt = jnp.concatenate([w_expert_router[l].T, w_group_router[l].T,
                               jnp.zeros((pad, d), F32)], axis=0)
        br = jnp.concatenate([b_expert_router[l], b_group_router[l],
                              jnp.zeros((pad,), F32)]).reshape(ROUTER_ROWS, 1)
        x1, pos, cnt, hs = _mixout(att.reshape(t, -1), sg.reshape(t, d), mc.reshape(t, d),
                                   xf.reshape(t, d), mod3, w_up_att[l], w_out[l],
                                   row(norm_ffn_g), wrt, br,
                                   s // TM_MIX, cap)
        ys = _ffn(cnt[0:N_GROUPS, 0], hs, w_gate[l], w_up[l], w_down[l], cap)
        xf = _final(pos.reshape(t), ys, x1, mod3, final_norm_g.reshape(1, d), s // TM_FIN,
                    l == depth - 1)
    return xf.reshape(b, s, d)
```

```python
import functools
import math

import jax
import jax.numpy as jnp
from jax import lax
from jax.experimental import pallas as pl
from jax.experimental.pallas import tpu as pltpu

F32 = jnp.float32
BF16 = jnp.bfloat16
I32 = jnp.int32

EPS = 1e-6
MASK_VALUE = -1e30
ROPE_THETA = 10000.0

N_HEADS = 4
HEAD_DIM = 64
LANES = 128
SUBLANES = 8
N_GROUPS = 4
EXPERTS_PER_GROUP = 8
ROUTER_ROWS = 48

VMEM_LIMIT = 56 * 1024 * 1024

TM_IN = 1024
TQ = 512
TK = 512
ATTN_HEADS_PER_STEP = 1
VT_ROWS = LANES + 16
TM_MIX = 1024
TM_FFN = 512
TM_FIN = 512
FIN_ROWS = 64
TN_ADA = 2048


def _cparams(n_axes):
    return pltpu.CompilerParams(
        dimension_semantics=("arbitrary",) * n_axes, vmem_limit_bytes=VMEM_LIMIT)


def _ada_kernel(c_ref, w_ref, b_ref, lq1_ref, lk1_ref, lq2_ref, lk2_ref, mod_ref, lam_ref, *,
                lambda_init):
    c = c_ref[...]
    a = c * jax.nn.sigmoid(c)
    mod_ref[...] = jnp.dot(a.astype(BF16), w_ref[...].astype(BF16),
                           preferred_element_type=F32) + b_ref[...]
    s1 = jnp.sum(lq1_ref[...] * lk1_ref[...], axis=-1, keepdims=True)
    s2 = jnp.sum(lq2_ref[...] * lk2_ref[...], axis=-1, keepdims=True)
    lam = jnp.exp(s1) - jnp.exp(s2) + lambda_init
    lam_ref[...] = jnp.broadcast_to(lam, lam_ref.shape)


def _ada(c8, w_ada, b_ada, lq1, lk1, lq2, lk2, lambda_init):
    d, n = w_ada.shape
    small = pl.BlockSpec((1, HEAD_DIM), lambda j: (0, 0))
    return pl.pallas_call(
        functools.partial(_ada_kernel, lambda_init=lambda_init),
        grid=(n // TN_ADA,),
        in_specs=[pl.BlockSpec((SUBLANES, d), lambda j: (0, 0)),
                  pl.BlockSpec((d, TN_ADA), lambda j: (0, j)),
                  pl.BlockSpec((1, TN_ADA), lambda j: (0, j)),
                  small, small, small, small],
        out_specs=[pl.BlockSpec((SUBLANES, TN_ADA), lambda j: (0, j)),
                   pl.BlockSpec((SUBLANES, LANES), lambda j: (0, 0))],
        out_shape=[jax.ShapeDtypeStruct((SUBLANES, n), F32),
                   jax.ShapeDtypeStruct((SUBLANES, LANES), F32)],
        compiler_params=_cparams(1),
        name="ada",
    )(c8, w_ada, b_ada, lq1, lk1, lq2, lk2)


def _inproj_kernel(x_ref, pos_ref, mod_ref, g_ref, invf_ref, w_ref, cw_ref, wupc_ref,
                   qt_ref, k_ref, vt_ref, sg_ref, mc_ref, carry_ref, *, d, qk_w, conv_w):
    tm = x_ref.shape[1]
    x = x_ref[0]
    ms = jnp.mean(x * x, axis=-1, keepdims=True)
    mod = mod_ref[0]
    sh = mod[:, 0:d]
    sc = mod[:, d:2 * d]
    h = (x * lax.rsqrt(ms + EPS) * g_ref[...]) * (1.0 + sc) + sh
    hb = h.astype(BF16)

    def proj(a, b):
        return jnp.dot(hb, w_ref[:, a:b].astype(BF16), preferred_element_type=F32)

    lane = lax.broadcasted_iota(I32, (1, LANES), 1)
    first_half = (lane % HEAD_DIM) < (HEAD_DIM // 2)
    ang = pos_ref[0].astype(F32) * invf_ref[...]
    cos = jnp.cos(ang)
    sin = jnp.sin(ang)
    sin_signed = jnp.where(first_half, -sin, sin)

    def rope(t):
        partner = jnp.where(first_half, pltpu.roll(t, LANES - HEAD_DIM // 2, 1),
                            pltpu.roll(t, HEAD_DIM // 2, 1))
        return t * cos + partner * sin_signed

    zq = proj(0, qk_w)
    zk = proj(qk_w, 2 * qk_w)
    scale = HEAD_DIM ** -0.5 * math.log2(math.e)
    zv = proj(2 * qk_w, 3 * qk_w)
    for hd in range(qk_w // LANES):
        sl = slice(hd * LANES, (hd + 1) * LANES)
        rq = rope(zq[:, sl]) * scale
        qw = qt_ref.shape[4]
        for ck in range(tm // qw):
            qt_ref[0, hd, ck] = rq[ck * qw:(ck + 1) * qw, :].T.astype(BF16)
        k_ref[0, :, sl] = rope(zk[:, sl]).astype(BF16)
        for ck in range(tm // TK):
            vt_ref[0, hd, ck, 0:LANES, :] = zv[ck * TK:(ck + 1) * TK, sl].T.astype(BF16)
            vt_ref[0, hd, ck, LANES:VT_ROWS, :] = jnp.ones((VT_ROWS - LANES, TK), BF16)

    o = 3 * qk_w
    cb = proj(o, o + conv_w)
    u = proj(o + conv_w, o + 2 * conv_w) * proj(o + 2 * conv_w, o + 3 * conv_w)

    @pl.when(pl.program_id(1) == 0)
    def _():
        carry_ref[...] = jnp.zeros_like(carry_ref)

    prev = carry_ref[...]
    row = lax.broadcasted_iota(I32, (tm, 1), 0)
    last = SUBLANES - 1
    u1 = jnp.where(row == 0, prev[last:last + 1, :], pltpu.roll(u, 1, 0))
    u2 = jnp.where(row == 0, prev[last - 1:last, :],
                   jnp.where(row == 1, prev[last:last + 1, :], pltpu.roll(u, 2, 0)))
    carry_ref[...] = u[tm - SUBLANES:tm, :]
    cw = cw_ref[...]
    conv = cb * (cw[0:1, :] * u + cw[1:2, :] * u1 + cw[2:3, :] * u2)
    convp = jnp.dot(conv.astype(BF16), wupc_ref[...].astype(BF16), preferred_element_type=F32)

    o = o + 3 * conv_w
    sg_ref[0] = jax.nn.sigmoid(proj(o, o + d)).astype(BF16)
    mc_ref[0] = (jax.nn.sigmoid(proj(o + d, o + 2 * d)) * convp).astype(BF16)


def _inproj(x, pos3, mod3, norm_g, invf, w_in, conv_w, w_up_conv):
    b, s, d = x.shape
    qk_w = N_HEADS * 2 * HEAD_DIM
    cw = conv_w.shape[1]
    tm = TM_IN
    if tm >= TQ:
        qt_spec = pl.BlockSpec((1, N_HEADS, tm // TQ, LANES, TQ), lambda bi, si: (bi, 0, si, 0, 0))
    else:
        qt_spec = pl.BlockSpec((1, N_HEADS, 1, LANES, tm),
                               lambda bi, si: (bi, 0, si // (TQ // tm), 0, si % (TQ // tm)))
    tok = lambda w: pl.BlockSpec((1, tm, w), lambda bi, si: (bi, si, 0))
    full = lambda a: pl.BlockSpec(a.shape, lambda bi, si: (0,) * a.ndim)
    return pl.pallas_call(
        functools.partial(_inproj_kernel, d=d, qk_w=qk_w, conv_w=cw),
        grid=(b, s // tm),
        in_specs=[tok(d), tok(1),
                  pl.BlockSpec((1, 1, mod3.shape[2]), lambda bi, si: (bi, 0, 0)),
                  full(norm_g), full(invf), full(w_in), full(conv_w), full(w_up_conv)],
        out_specs=[qt_spec,
                   tok(qk_w),
                   pl.BlockSpec((1, N_HEADS, tm // TK, VT_ROWS, TK),
                                lambda bi, si: (bi, 0, si, 0, 0)),
                   tok(d), tok(d)],
        out_shape=[jax.ShapeDtypeStruct((b, N_HEADS, s // TQ, LANES, TQ), BF16),
                   jax.ShapeDtypeStruct((b, s, qk_w), BF16),
                   jax.ShapeDtypeStruct((b, N_HEADS, s // TK, VT_ROWS, TK), BF16),
                   jax.ShapeDtypeStruct((b, s, d), BF16),
                   jax.ShapeDtypeStruct((b, s, d), BF16)],
        scratch_shapes=[pltpu.VMEM((SUBLANES, cw), F32)],
        compiler_params=_cparams(2),
        name="inproj",
    )(x, pos3, mod3, norm_g, invf, w_in, conv_w, w_up_conv)


def _attn_kernel(lam_ref, qt_ref, k_ref, vt_ref, g_ref, o_ref, s_ref, p_ref, qq_ref, bias_ref, *,
                 out_scale):
    hps, n_q, tq = qt_ref.shape[1], qt_ref.shape[2], qt_ref.shape[4]
    feat = lax.broadcasted_iota(I32, (LANES, 1), 0)
    halves = [slice(h * tq, (h + 1) * tq) for h in range(2)]
    streams = [(h, half) for h in range(hps) for half in halves]

    k_sub = lax.broadcasted_iota(I32, (TK, 1), 0)
    q_lane = lax.broadcasted_iota(I32, (1, tq), 1)
    bias_ref[...] = jnp.where(k_sub <= q_lane, 0.0, MASK_VALUE)

    def load_queries(qi):
        for h in range(hps):
            qt = qt_ref[0, h, qi]
            zero = jnp.zeros_like(qt)
            qq_ref[h, :, halves[0]] = jnp.where(feat < HEAD_DIM, qt, zero)
            qq_ref[h, :, halves[1]] = jnp.where(feat >= HEAD_DIM, qt, zero)

    half = TK // 2

    def late(lanes):
        return slice(lanes.start + half, lanes.stop)

    def scores(j, diagonal=False):
        for h, lanes in streams:
            kt = k_ref[0, pl.ds(j * TK, TK), h * LANES:(h + 1) * LANES]
            if diagonal:
                s_ref[h, 0:half, lanes] = jnp.dot(kt[0:half], qq_ref[h, :, lanes],
                                                  preferred_element_type=F32)
                s_ref[h, half:TK, late(lanes)] = jnp.dot(kt[half:TK], qq_ref[h, :, late(lanes)],
                                                         preferred_element_type=F32)
            else:
                s_ref[h, :, lanes] = jnp.dot(kt, qq_ref[h, :, lanes],
                                             preferred_element_type=F32)

    def values(j, state, diagonal=False):
        out = []
        for (h, lanes), (m, acc, alpha) in zip(streams, state):
            if diagonal:
                early = slice(lanes.start, lanes.start + half)
                pv = jnp.concatenate(
                    [jnp.dot(vt_ref[0, h, j, :, 0:half], p_ref[h, 0:half, early],
                             preferred_element_type=F32),
                     jnp.dot(vt_ref[0, h, j], p_ref[h, :, late(lanes)],
                             preferred_element_type=F32)], axis=1)
            else:
                pv = jnp.dot(vt_ref[0, h, j], p_ref[h, :, lanes], preferred_element_type=F32)
            out.append((m, pv if diagonal else alpha * acc + pv, alpha))
        return tuple(out)

    def softmax(state, diagonal):
        out = []
        for (h, lanes), (m, acc, _) in zip(streams, state):
            if diagonal:
                top = lambda: s_ref[h, 0:half, lanes] + bias_ref[0:half, :]
                bot = lambda: s_ref[h, half:TK, late(lanes)] + bias_ref[half:TK, half:tq]
                cm = jnp.max(top(), axis=0, keepdims=True)
                cm = jnp.concatenate(
                    [cm[:, 0:half],
                     jnp.maximum(cm[:, half:tq], jnp.max(bot(), axis=0, keepdims=True))], axis=1)
                m_new = jnp.maximum(m, cm)
                p_ref[h, 0:half, lanes] = jnp.exp2(top() - m_new).astype(BF16)
                p_ref[h, half:TK, late(lanes)] = jnp.exp2(bot() - m_new[:, half:tq]).astype(BF16)
            else:
                load = lambda: s_ref[h, :, lanes]
                m_new = jnp.maximum(m, jnp.max(load(), axis=0, keepdims=True))
                p_ref[h, :, lanes] = jnp.exp2(load() - m_new).astype(BF16)
            out.append((m_new, acc, jnp.exp2(m - m_new)))
        return tuple(out)

    def finish(qi, last_tile, state):
        state = values(last_tile, state, diagonal=(qi == 0))
        row0 = qi * tq
        for h in range(hps):
            (_, acc1, _), (_, acc2, _) = state[2 * h], state[2 * h + 1]
            o1 = acc1[0:LANES] / acc1[LANES:LANES + 1]
            o2 = acc2[0:LANES] / acc2[LANES:LANES + 1]
            a = o1 - lam_ref[0:1, 0:1] * o2
            y = a * lax.rsqrt(jnp.mean(a * a, axis=0, keepdims=True) + EPS) * g_ref[...]
            o_ref[0, pl.ds(row0, tq), h * LANES:(h + 1) * LANES] = (
                (y * out_scale).T.astype(BF16))

    def next_diagonal(qi):
        if qi + 1 < n_q:
            load_queries(qi + 1)
            scores(qi + 1, diagonal=True)

    one = (jnp.full((1, tq), MASK_VALUE, F32), jnp.zeros((VT_ROWS, tq), F32),
           jnp.ones((1, tq), F32))
    init = tuple(one for _ in streams)

    load_queries(0)
    scores(0, diagonal=True)
    pending = softmax(init, True)
    next_diagonal(0)

    for qi in range(1, n_q):
        finish(qi - 1, max(qi - 2, 0), pending)
        state = softmax(init, True)
        scores(0)
        for t in range(1, qi + 1):
            state = values(qi if t == 1 else t - 2, state, diagonal=(t == 1))
            state = softmax(state, False)
            if t < qi:
                scores(t)
            else:
                next_diagonal(qi)
        pending = state
    finish(n_q - 1, max(n_q - 2, 0), pending)


def _attn(lam, qt, k, vt, subln_g_col, out_scale):
    b, s, w = k.shape
    hps = ATTN_HEADS_PER_STEP
    return pl.pallas_call(
        functools.partial(_attn_kernel, out_scale=out_scale),
        grid=(b, N_HEADS // hps),
        in_specs=[pl.BlockSpec((SUBLANES, LANES), lambda bi, hi: (0, 0)),
                  pl.BlockSpec((1, hps, s // TQ, LANES, TQ), lambda bi, hi: (bi, hi, 0, 0, 0)),
                  pl.BlockSpec((1, s, hps * LANES), lambda bi, hi: (bi, 0, hi)),
                  pl.BlockSpec((1, hps, s // TK, VT_ROWS, TK), lambda bi, hi: (bi, hi, 0, 0, 0)),
                  pl.BlockSpec((LANES, 1), lambda bi, hi: (0, 0))],
        out_specs=pl.BlockSpec((1, s, hps * LANES), lambda bi, hi: (bi, 0, hi)),
        out_shape=jax.ShapeDtypeStruct((b, s, w), BF16),
        scratch_shapes=[pltpu.VMEM((hps, TK, 2 * TQ), F32), pltpu.VMEM((hps, TK, 2 * TQ), BF16),
                        pltpu.VMEM((hps, LANES, 2 * TQ), BF16), pltpu.VMEM((TK, TQ), F32)],
        compiler_params=_cparams(2),
        name="attn",
    )(lam, qt, k, vt, subln_g_col)


def _mixout_kernel(att_ref, sg_ref, mc_ref, x_ref, mod_ref, wua_ref, wo_ref, g_ref, wrt_ref,
                   br_ref, x1_ref, pos_ref, cnt_ref, hs_ref,
                   h2c_ref, tri_ref, carry_ref, posv_ref, cntv_ref, poss_ref, cnts_ref,
                   sem_s, sem_r, *, d, cap, tail):
    tm = x_ref.shape[0]
    i = pl.program_id(0)
    last = pl.num_programs(0) - 1

    @pl.when(i == 0)
    def _():
        carry_ref[...] = jnp.zeros_like(carry_ref)
        earlier = (lax.broadcasted_iota(I32, (tm, 1), 0)
                   <= lax.broadcasted_iota(I32, (1, tm), 1))
        tri_ref[...] = jnp.where(earlier, 1.0, 0.0).astype(BF16)

    mod = mod_ref[0]
    g_m = mod[:, 2 * d:3 * d]
    sh_f = mod[:, 3 * d:4 * d]
    sc_f = mod[:, 4 * d:5 * d]

    attp = jnp.dot(att_ref[...], wua_ref[...].astype(BF16), preferred_element_type=F32)
    merged = sg_ref[...].astype(F32) * attp + mc_ref[...].astype(F32)
    mo = jnp.dot(merged.astype(BF16), wo_ref[...].astype(BF16), preferred_element_type=F32)
    x1 = x_ref[...] + g_m * mo
    x1_ref[...] = x1
    ms = jnp.mean(x1 * x1, axis=-1, keepdims=True)
    h2 = (x1 * lax.rsqrt(ms + EPS) * g_ref[...]) * (1.0 + sc_f) + sh_f

    def split(a):
        hi = a.astype(BF16)
        return hi, (a - hi.astype(F32)).astype(BF16)

    def dot_t(a, b):
        return lax.dot_general(a, b, (((1,), (1,)), ((), ())), preferred_element_type=F32)

    w_hi, w_lo = split(wrt_ref[...])
    h_hi, h_lo = split(h2)
    by_hi = dot_t(jnp.concatenate([w_hi, w_lo], axis=0), h_hi)
    lt = (by_hi[0:ROUTER_ROWS] + (dot_t(w_hi, h_lo) + by_hi[ROUTER_ROWS:2 * ROUTER_ROWS])
          + br_ref[...])
    n_sel = EXPERTS_PER_GROUP
    ridx = lax.broadcasted_iota(I32, (n_sel, 1), 0)
    n_e = N_GROUPS * EXPERTS_PER_GROUP
    is_grp = ridx < N_GROUPS
    gl = jnp.where(is_grp, lt[n_e:n_e + n_sel, :], MASK_VALUE)
    gmax = jnp.max(gl, axis=0, keepdims=True)
    gsum = jnp.sum(jnp.where(is_grp, jnp.exp(gl - gmax), 0.0), axis=0, keepdims=True)
    g_w = 1.0 / gsum
    gidx = jnp.min(jnp.where(gl == gmax, ridx, n_sel), axis=0, keepdims=True)
    e_in = jnp.zeros((n_sel, tm), F32)
    for g in range(N_GROUPS):
        e_in = jnp.where(gidx == g, lt[g * n_sel:(g + 1) * n_sel, :], e_in)
    t1 = jnp.max(e_in, axis=0, keepdims=True)
    i1 = jnp.min(jnp.where(e_in == t1, ridx, n_sel), axis=0, keepdims=True)
    e_in2 = jnp.where(ridx == i1, MASK_VALUE, e_in)
    t2 = jnp.max(e_in2, axis=0, keepdims=True)
    i2 = jnp.min(jnp.where(e_in2 == t2, ridx, n_sel), axis=0, keepdims=True)
    e2 = jnp.exp(t2 - t1)
    den = 1.0 + e2
    w1 = (1.0 / den) * g_w
    w2 = (e2 / den) * g_w
    cw_t = jnp.where(ridx == i1, w1, jnp.where(ridx == i2, w2, 0.0))
    cw_col = jnp.concatenate([cw_t, jnp.zeros((LANES - n_sel, tm), F32)], axis=0).T

    onehot = jnp.where(ridx == gidx, 1.0, 0.0)
    oh16 = jnp.concatenate([onehot, jnp.zeros_like(onehot)], axis=0).astype(BF16)
    incl = jnp.dot(oh16, tri_ref[...], preferred_element_type=F32)[0:n_sel, :]
    run = carry_ref[...][:, 0:1]
    base = ridx.astype(F32) * float(cap)
    pos = jnp.sum(onehot * (base + run + incl - onehot), axis=0, keepdims=True).astype(I32)
    new_run = run + incl[:, tm - 1:tm]
    carry_ref[...] = jnp.broadcast_to(new_run, carry_ref.shape)
    cnt_i = jnp.broadcast_to(new_run, cnt_ref.shape).astype(I32)
    cnt_ref[...] = cnt_i
    pos_ref[0] = pos
    posv_ref[...] = pos
    cp = pltpu.make_async_copy(posv_ref.at[0], poss_ref, sem_s)
    cp.start()

    def row_copy(r, p):
        return pltpu.make_async_copy(h2c_ref.at[pl.ds(r, 1), :], hs_ref.at[pl.ds(p, 1), :], sem_r)

    def wait_tile():
        pltpu.make_async_copy(h2c_ref, hs_ref.at[pl.ds(0, tm), :], sem_r).wait()

    @pl.when(i > 0)
    def _():
        wait_tile()

    h2c_ref[:, 0:d] = h2
    h2c_ref[:, d:d + LANES] = cw_col
    cp.wait()

    for r in range(tm):
        row_copy(r, poss_ref[r]).start(priority=r % 2)

    @pl.when(i == last)
    def _():
        wait_tile()
        cntv_ref[...] = cnt_i
        cc = pltpu.make_async_copy(cntv_ref, cnts_ref, sem_s)
        cc.start()
        cc.wait()
        h2c_ref[0:SUBLANES, :] = jnp.zeros((SUBLANES, h2c_ref.shape[1]), F32)

        def zissue(r, c):
            for g in range(N_GROUPS):
                row_copy(0, g * cap + cnts_ref[g, 0] + r).start()
            return c

        lax.fori_loop(0, tail, zissue, 0, unroll=2)
        for g in range(N_GROUPS):
            pltpu.make_async_copy(h2c_ref.at[pl.ds(0, tail), :], hs_ref.at[pl.ds(0, tail), :],
                                  sem_r).wait()


def _mixout(att, sg, mc, x, mod3, w_up_att, w_out, norm_g, wrt, br, tiles_per_batch, cap):
    t, d = x.shape
    tm = TM_MIX
    dc = d + LANES
    tok = lambda w: pl.BlockSpec((tm, w), lambda i: (i, 0))
    full = lambda a: pl.BlockSpec(a.shape, lambda i: (0,) * a.ndim)
    return pl.pallas_call(
        functools.partial(_mixout_kernel, d=d, cap=cap, tail=cap - t),
        grid=(t // tm,),
        in_specs=[tok(att.shape[1]), tok(d), tok(d), tok(d),
                  pl.BlockSpec((1, 1, mod3.shape[2]), lambda i: (i // tiles_per_batch, 0, 0)),
                  full(w_up_att), full(w_out), full(norm_g), full(wrt), full(br)],
        out_specs=[tok(d),
                   pl.BlockSpec((1, 1, tm), lambda i: (i, 0, 0)),
                   pl.BlockSpec((SUBLANES, LANES), lambda i: (0, 0)),
                   pl.BlockSpec(memory_space=pl.ANY)],
        out_shape=[jax.ShapeDtypeStruct((t, d), F32),
                   jax.ShapeDtypeStruct((t // tm, 1, tm), I32),
                   jax.ShapeDtypeStruct((SUBLANES, LANES), I32),
                   jax.ShapeDtypeStruct((N_GROUPS * cap, dc), F32)],
        scratch_shapes=[pltpu.VMEM((tm, dc), F32), pltpu.VMEM((tm, tm), BF16),
                        pltpu.VMEM((SUBLANES, LANES), F32),
                        pltpu.VMEM((1, tm), I32), pltpu.VMEM((SUBLANES, LANES), I32),
                        pltpu.SMEM((tm,), I32), pltpu.SMEM((SUBLANES, LANES), I32),
                        pltpu.SemaphoreType.DMA, pltpu.SemaphoreType.DMA],
        compiler_params=_cparams(1),
        name="mixout",
    )(att, sg, mc, x, mod3, w_up_att, w_out, norm_g, wrt, br)


def _ffn_tile(step, cnt_ref, tm):
    log_tm = tm.bit_length() - 1
    ends = []
    tot = jnp.int32(0)
    for g in range(N_GROUPS):
        tot = tot + lax.shift_right_logical(cnt_ref[g] + (tm - 1), log_tm)
        ends.append(tot)
    sc = jnp.maximum(jnp.minimum(step, tot - 1), 0)
    g = sum((sc >= e).astype(I32) for e in ends[:-1])
    start = jnp.where(g == 0, 0, jnp.where(g == 1, ends[0], jnp.where(g == 2, ends[1], ends[2])))
    return g, sc - start, step < tot


def _ffn_kernel(cnt_ref, hs_ref, wg_ref, wu_ref, wd_ref, y_ref, *, d):
    tm = hs_ref.shape[0]
    _, _, real = _ffn_tile(pl.program_id(0), cnt_ref, tm)

    @pl.when(real)
    def _():
        h = hs_ref[:, 0:d].astype(BF16)
        cw = hs_ref[:, d:d + LANES]
        acc = jnp.zeros((tm, d), F32)
        for e in range(EXPERTS_PER_GROUP):
            hg = jnp.dot(h, wg_ref[e].astype(BF16), preferred_element_type=F32)
            hu = jnp.dot(h, wu_ref[e].astype(BF16), preferred_element_type=F32)
            a = hg * jax.nn.sigmoid(hg) * hu * cw[:, e:e + 1]
            acc = acc + jnp.dot(a.astype(BF16), wd_ref[e].astype(BF16),
                                preferred_element_type=F32)
        y_ref[...] = acc


def _ffn(cnt, hs, w_gate, w_up, w_down, cap):
    rows, dc = hs.shape
    d = dc - LANES
    tm = TM_FFN
    blocks_per_group = cap // tm
    n_steps = (blocks_per_group - 1) + N_GROUPS

    def row_map(s, c):
        g, blk, _ = _ffn_tile(s, c, tm)
        return (g * blocks_per_group + blk, 0)

    def w_map(s, c):
        g, _, _ = _ffn_tile(s, c, tm)
        return (g, 0, 0)

    ff = w_gate.shape[2]
    once = pl.Buffered(1)
    gs = pltpu.PrefetchScalarGridSpec(
        num_scalar_prefetch=1,
        grid=(n_steps,),
        in_specs=[pl.BlockSpec((tm, dc), row_map),
                  pl.BlockSpec((EXPERTS_PER_GROUP, d, ff), w_map),
                  pl.BlockSpec((EXPERTS_PER_GROUP, d, ff), w_map),
                  pl.BlockSpec((EXPERTS_PER_GROUP, ff, d), w_map, pipeline_mode=once)],
        out_specs=pl.BlockSpec((tm, d), row_map),
    )
    return pl.pallas_call(
        functools.partial(_ffn_kernel, d=d),
        grid_spec=gs,
        out_shape=jax.ShapeDtypeStruct((rows, d), F32),
        compiler_params=_cparams(1),
        name="ffn",
    )(cnt, hs, w_gate, w_up, w_down)


def _final_kernel(pos_ref, ys_ref, x1_ref, mod_ref, g_ref, o_ref, ybuf_ref, sem, *, d, normalize):
    tm = x1_ref.shape[0]
    i = pl.program_id(0)
    n = pl.num_programs(0)
    n_buf = ybuf_ref.shape[0]

    def gather_rows(tile, rows):
        slot = lax.rem(tile, n_buf)
        base = jnp.minimum(tile, n - 1) * tm
        for r in rows:
            pltpu.make_async_copy(ys_ref.at[pl.ds(pos_ref[base + r], 1), :],
                                  ybuf_ref.at[slot, pl.ds(r, 1), :],
                                  sem.at[slot]).start(priority=r % 2)

    def wait_tile(tile):
        slot = lax.rem(tile, n_buf)
        pltpu.make_async_copy(ys_ref.at[pl.ds(0, tm), :], ybuf_ref.at[slot], sem.at[slot]).wait()

    @pl.when(i == 0)
    def _():
        gather_rows(i, range(tm))
        gather_rows(i + 1, range(tm))

    wait_tile(i)
    slot = lax.rem(i, n_buf)
    g_f = mod_ref[0][:, 5 * d:6 * d]
    for c in range(tm // FIN_ROWS):
        rows = slice(c * FIN_ROWS, (c + 1) * FIN_ROWS)
        x2 = x1_ref[rows, :] + g_f * ybuf_ref[slot, rows, :]
        if normalize:
            x2 = x2 * lax.rsqrt(jnp.mean(x2 * x2, axis=-1, keepdims=True) + EPS) * g_ref[...]
        o_ref[rows, :] = x2
        gather_rows(i + 2, range(c * FIN_ROWS, (c + 1) * FIN_ROWS))

    @pl.when(i == n - 1)
    def _():
        wait_tile(i + 1)
        wait_tile(i + 2)


def _final(pos, ys, x1, mod3, final_g, tiles_per_batch, normalize):
    t, d = x1.shape
    tm = TM_FIN
    gs = pltpu.PrefetchScalarGridSpec(
        num_scalar_prefetch=1,
        grid=(t // tm,),
        in_specs=[pl.BlockSpec(memory_space=pl.ANY),
                  pl.BlockSpec((tm, d), lambda i, p: (i, 0)),
                  pl.BlockSpec((1, 1, mod3.shape[2]), lambda i, p: (i // tiles_per_batch, 0, 0)),
                  pl.BlockSpec((1, d), lambda i, p: (0, 0))],
        out_specs=pl.BlockSpec((tm, d), lambda i, p: (i, 0)),
        scratch_shapes=[pltpu.VMEM((3, tm, d), F32), pltpu.SemaphoreType.DMA((3,))],
    )
    return pl.pallas_call(
        functools.partial(_final_kernel, d=d, normalize=normalize),
        grid_spec=gs,
        out_shape=jax.ShapeDtypeStruct((t, d), F32),
        compiler_params=_cparams(1),
        name="final",
    )(pos, ys, x1, mod3, final_g)


def kernel(x, c, positions, w_ada, b_ada, norm_mix_g, w_in, lambda_q1, lambda_k1, lambda_q2,
           lambda_k2, subln_g, conv_w, w_up_att, w_up_conv, w_out, norm_ffn_g, w_group_router,
           b_group_router, w_expert_router, b_expert_router, w_gate, w_up, w_down, final_norm_g):
    b, s, d = x.shape
    depth = w_ada.shape[0]
    t = b * s
    assert s % TQ == 0 and s % TM_IN == 0 and TQ == TK and TM_IN % TK == 0
    assert TQ % TM_IN == 0 or TM_IN % TQ == 0
    assert N_HEADS % ATTN_HEADS_PER_STEP == 0
    assert s % TM_MIX == 0 and s % TM_FIN == 0 and t % TM_FFN == 0
    cap = t + TM_FFN
    n_e = N_GROUPS * EXPERTS_PER_GROUP

    assert b <= SUBLANES and EXPERTS_PER_GROUP == SUBLANES and N_GROUPS <= SUBLANES
    c8 = jnp.concatenate([c, jnp.zeros((SUBLANES - b, d), F32)], axis=0)
    pos3 = positions.reshape(b, s, 1)
    inv_freq = ROPE_THETA ** (-jnp.arange(0, HEAD_DIM, 2, dtype=F32) / HEAD_DIM)
    invf = jnp.tile(inv_freq, LANES // (HEAD_DIM // 2)).reshape(1, LANES)

    xf = x
    for l in range(depth):
        lambda_init = 0.8 - 0.6 * math.exp(-0.3 * l)
        row = lambda a: a[l].reshape(1, -1)
        mod8, lam = _ada(c8, w_ada[l], row(b_ada), row(lambda_q1), row(lambda_k1),
                         row(lambda_q2), row(lambda_k2), lambda_init)
        mod3 = mod8[0:b].reshape(b, 1, 6 * d)

        qt, k, vt, sg, mc = _inproj(xf.reshape(b, s, d), pos3, mod3, row(norm_mix_g), invf,
                                    w_in[l], conv_w[l], w_up_conv[l])
        att = _attn(lam, qt, k, vt, subln_g[l].reshape(-1, 1), 1.0 - lambda_init)

        pad = ROUTER_ROWS - n_e - N_GROUPS
        wrt = jnp.concatenate([w_expert_router[l].T, w_group_router[l].T,
                               jnp.zeros((pad, d), F32)], axis=0)
        br = jnp.concatenate([b_expert_router[l], b_group_router[l],
                              jnp.zeros((pad,), F32)]).reshape(ROUTER_ROWS, 1)
        x1, pos, cnt, hs = _mixout(att.reshape(t, -1), sg.reshape(t, d), mc.reshape(t, d),
                                   xf.reshape(t, d), mod3, w_up_att[l], w_out[l],
                                   row(norm_ffn_g), wrt, br,
                                   s // TM_MIX, cap)
        ys = _ffn(cnt[0:N_GROUPS, 0], hs, w_gate[l], w_up[l], w_down[l], cap)
        xf = _final(pos.reshape(t), ys, x1, mod3, final_norm_g.reshape(1, d), s // TM_FIN,
                    l == depth - 1)
    return xf.reshape(b, s, d)
```
